```python
import jax
import jax.numpy as jnp
from jax import lax
import numpy as np

D_MODEL = 2048
BATCH = 32
SEQ = 256
DEPTH = 2
DEC_BATCH = 4
DEC_SEQ = 1024
PAST_LEN = 256

GRID_W = 64
GROUP_W = D_MODEL // 4
FN_W = GROUP_W
FN_GROUPS = 4
FN_GW = FN_W // FN_GROUPS
HEAD_DIM = 128
N_HEADS = GROUP_W // HEAD_DIM
N_KV = 2
GQA_GROUP = N_HEADS // N_KV
Q_BLOCK = 128
ROPE_THETA = 10000.0
HG_W = GROUP_W
HG_DK = 128
HG_DV = 128
HG_HEADS = HG_W // HG_DK
HG_CHUNK = 32
CV_W = GROUP_W
CONV_K = 3
IN_SIZES = (FN_W, N_HEADS * HEAD_DIM, N_KV * HEAD_DIM, N_KV * HEAD_DIM, HG_W, HG_W, HG_W, HG_W, HG_W, CV_W, CV_W, CV_W)
IN_COLS = FN_W + N_HEADS * HEAD_DIM + 2 * N_KV * HEAD_DIM + 5 * HG_W + 3 * CV_W
PEER_HEADS = 8
N_KEYS = 128
N_EXPERTS = N_KEYS * N_KEYS
PEER_QDIM = 256
PEER_HALF = PEER_QDIM // 2
PEER_TOPK = 16
PEER_BLOCK = 128
EPS = 1e-6
F_FLOOR = 1e-30

kernel_name = 'hybrid_fourier_gqa_hgrn2_conv_peer_prefix_step'


def _rms_norm(x, g):
    xf = x.astype(jnp.float32)
    y = xf * lax.rsqrt(jnp.mean(xf * xf, axis=-1, keepdims=True) + EPS)
    return (y * g.astype(jnp.float32)).astype(x.dtype)


def _split_points():
    return [int(p) for p in np.cumsum(IN_SIZES)[:-1]]


def _fourier_mix(u):
    b, t, _ = u.shape
    ug = u.reshape(b, t, FN_GROUPS, FN_GW).astype(jnp.float32)
    y = jnp.fft.fft2(ug, axes=(1, 3), norm='ortho').real
    return y.reshape(b, t, FN_W).astype(u.dtype)


def _axial_rope(n_tok):
    rows = n_tok // GRID_W
    row = jnp.repeat(jnp.arange(rows, dtype=jnp.float32), GRID_W)
    col = jnp.tile(jnp.arange(GRID_W, dtype=jnp.float32), rows)
    n_freq = HEAD_DIM // 4
    inv = ROPE_THETA ** (-jnp.arange(n_freq, dtype=jnp.float32) / n_freq)
    ang = jnp.concatenate([row[:, None] * inv, col[:, None] * inv], axis=-1)
    return jnp.cos(ang)[None, :, None, :], jnp.sin(ang)[None, :, None, :]


def _apply_rope(x, cos, sin):
    x1, x2 = jnp.split(x.astype(jnp.float32), 2, axis=-1)
    return jnp.concatenate([x1 * cos - x2 * sin, x2 * cos + x1 * sin], axis=-1).astype(x.dtype)


def _attention(q, k, v):
    b, t, _, _ = q.shape
    scale = HEAD_DIM ** -0.5
    qb = jnp.moveaxis(q.reshape(b, t // Q_BLOCK, Q_BLOCK, N_KV, GQA_GROUP, HEAD_DIM), 1, 0)

    def one_block(qblk):
        s = jnp.einsum('bqhgd,bkhd->bhgqk', qblk, k).astype(jnp.float32) * scale
        p = jax.nn.softmax(s, axis=-1).astype(v.dtype)
        return jnp.einsum('bhgqk,bkhd->bqhgd', p, v)

    o = lax.map(one_block, qb)
    return jnp.moveaxis(o, 0, 1).reshape(b, t, N_HEADS * HEAD_DIM)


def _log_forget(z, lb):
    lb = lb.astype(jnp.float32)
    f = lb + (1.0 - lb) * jax.nn.sigmoid(z.astype(jnp.float32))
    return jnp.log(jnp.maximum(f, F_FLOOR))


def _hgrn_scan(q, k, v, log_f, s0):
    b, t, h, _ = q.shape
    dv = v.shape[-1]
    n_chunks = t // HG_CHUNK

    def chunks(a):
        return jnp.moveaxis(a.reshape(b, n_chunks, HG_CHUNK, h, a.shape[-1]), 1, 0)

    causal = jnp.tril(jnp.ones((HG_CHUNK, HG_CHUNK), dtype=bool))[None, :, :, None, None]

    def step(state, xs):
        qc, kc, vc, lc = xs
        cum = jnp.cumsum(lc, axis=1)
        o_inter = jnp.einsum('bthd,bhde->bthe', qc * jnp.exp(cum), state)
        diff = cum[:, :, None] - cum[:, None, :]
        decay = jnp.where(causal, jnp.exp(jnp.minimum(diff, 0.0)), 0.0)
        att = jnp.einsum('bthd,bshd,btshd->bhts', qc, kc, decay)
        o_intra = jnp.einsum('bhts,bshe->bthe', att, vc)
        last = cum[:, -1]
        new_state = jnp.exp(last)[..., None] * state + jnp.einsum('bshd,bshe->bhde', kc * jnp.exp(last[:, None] - cum), vc)
        return new_state, o_inter + o_intra

    s_fin, o = lax.scan(step, s0, (chunks(q), chunks(k), chunks(v), chunks(log_f)))
    return jnp.moveaxis(o, 0, 1).reshape(b, t, h, dv), s_fin


def _short_conv(x, w, bias):
    y = lax.conv_general_dilated(x, w[:, None, :].astype(x.dtype), window_strides=(1,), padding=((CONV_K // 2, CONV_K // 2),), dimension_numbers=('NWC', 'WIO', 'NWC'), feature_group_count=CV_W)
    return y + bias.astype(x.dtype)


def _token_mixers(z, lw, ctx):
    b, t, _ = z.shape
    u_fn, q, k, v, g_q, g_ff, g_fb, g_i, g_o, c_b, c_c, c_x = jnp.split(z, _split_points(), axis=-1)
    o_fn = _fourier_mix(u_fn)
    q = _rms_norm(q.reshape(b, t, N_HEADS, HEAD_DIM), lw['q_norm'])
    k = _rms_norm(k.reshape(b, t, N_KV, HEAD_DIM), lw['k_norm'])
    v = v.reshape(b, t, N_KV, HEAD_DIM)
    hq = g_q.reshape(b, t, HG_HEADS, HG_DK).astype(jnp.float32)
    hv = g_i.reshape(b, t, HG_HEADS, HG_DV).astype(jnp.float32)
    lf_f = _log_forget(g_ff.reshape(b, t, HG_HEADS, HG_DK), lw['lb'][0].reshape(HG_HEADS, HG_DK))
    lf_b = _log_forget(g_fb.reshape(b, t, HG_HEADS, HG_DK), lw['lb'][1].reshape(HG_HEADS, HG_DK))
    if ctx is None:
        o_at = _attention(q, k, v)
        s0_f = jnp.zeros((b, HG_HEADS, HG_DK, HG_DV), jnp.float32)
        s0_b = jnp.zeros((b, HG_HEADS, HG_DK, HG_DV), jnp.float32)
    else:
        k_ctx, v_ctx, s0_f, s0_b = ctx
        s0_f = s0_f.astype(jnp.float32)
        s0_b = s0_b.astype(jnp.float32)
        cos, sin = _axial_rope(t)
        q = _apply_rope(q, cos, sin)
        k = _apply_rope(k, cos, sin)
        k_all = jnp.concatenate([k_ctx.astype(k.dtype), k], axis=1)
        v_all = jnp.concatenate([v_ctx.astype(v.dtype), v], axis=1)
        o_at = _attention(q, k_all, v_all)
    o_f, s_f = _hgrn_scan(hq, -jnp.expm1(lf_f), hv, lf_f, s0_f)
    o_b, s_b = _hgrn_scan(jnp.flip(hq, 1), jnp.flip(-jnp.expm1(lf_b), 1), jnp.flip(hv, 1), jnp.flip(lf_b, 1), s0_b)
    o_hg = _rms_norm(o_f + jnp.flip(o_b, 1), lw['hg_onorm']) * jax.nn.silu(g_o.reshape(b, t, HG_HEADS, HG_DV).astype(jnp.float32))
    o_hg = o_hg.reshape(b, t, HG_W).astype(z.dtype)
    o_cv = c_b * _short_conv(c_c * c_x, lw['conv_w'], lw['conv_b'])
    mixed = jnp.concatenate([o_fn, o_at, o_hg, o_cv], axis=-1)
    return mixed @ lw['w_out'], (k, v, s_f, s_b)


def _peer(h, lw):
    b, t, d = h.shape
    wq, k1, k2, u, v = lw['peer_wq'], lw['peer_k1'], lw['peer_k2'], lw['peer_u'], lw['peer_v']
    hb = h.reshape(b * t // PEER_BLOCK, PEER_BLOCK, d)

    def one_block(hblk):
        qh = (hblk @ wq).reshape(PEER_BLOCK, PEER_HEADS, 2, PEER_HALF)
        s1 = jnp.einsum('thd,nd->thn', qh[:, :, 0], k1).astype(jnp.float32)
        s2 = jnp.einsum('thd,nd->thn', qh[:, :, 1], k2).astype(jnp.float32)
        v1, i1 = lax.top_k(s1, PEER_TOPK)
        v2, i2 = lax.top_k(s2, PEER_TOPK)
        cand = (v1[..., :, None] + v2[..., None, :]).reshape(PEER_BLOCK, PEER_HEADS, PEER_TOPK * PEER_TOPK)
        top_s, top_c = lax.top_k(cand, PEER_TOPK)
        e1 = jnp.take_along_axis(i1, top_c // PEER_TOPK, axis=-1)
        e2 = jnp.take_along_axis(i2, top_c % PEER_TOPK, axis=-1)
        experts = (e1 * N_KEYS + e2).reshape(PEER_BLOCK, PEER_HEADS * PEER_TOPK)
        gates = jax.nn.softmax(top_s, axis=-1).reshape(PEER_BLOCK, PEER_HEADS * PEER_TOPK)
        act = jax.nn.gelu(jnp.einsum('td,ted->te', hblk, u[experts]))
        return jnp.einsum('te,ted->td', gates.astype(h.dtype) * act, v[experts])

    return lax.map(one_block, hb).reshape(b, t, d)


def _layer(x, cvec, lw, ctx):
    m = jax.nn.silu(cvec) @ lw['ada_w'] + lw['ada_b']
    sh1, sc1, g1, sh2, sc2, g2 = [p[:, None, :] for p in jnp.split(m, 6, axis=-1)]
    h = _rms_norm(x, lw['norm1']) * (1.0 + sc1) + sh1
    mix, ctx_out = _token_mixers(h @ lw['w_in'], lw, ctx)
    x = x + g1 * mix
    h = _rms_norm(x, lw['norm2']) * (1.0 + sc2) + sh2
    x = x + g2 * _peer(h, lw)
    return x, ctx_out


def setup_inputs(seed: int = 0) -> dict:
    key = jax.random.key(seed)
    ks = jax.random.split(key, 26)

    def nrm(k, shape, s):
        return s * jax.random.normal(k, shape, jnp.float32)

    d = D_MODEL
    return {
        'x_prompt': nrm(ks[0], (BATCH, SEQ, d), 1.0),
        'x_sample': nrm(ks[1], (DEC_BATCH, DEC_SEQ, d), 1.0),
        'cache_k': nrm(ks[2], (DEC_BATCH, DEPTH, PAST_LEN, N_KV, HEAD_DIM), 1.0),
        'cache_v': nrm(ks[3], (DEC_BATCH, DEPTH, PAST_LEN, N_KV, HEAD_DIM), 1.0),
        'state_hgrn': nrm(ks[4], (DEC_BATCH, DEPTH, 2, HG_HEADS, HG_DK, HG_DV), 0.5),
        'c': nrm(ks[5], (DEC_BATCH, d), 1.0),
        'c_ctx': nrm(ks[6], (d,), 1.0),
        'ada_w': nrm(ks[7], (DEPTH, d, 6 * d), 0.5 * d ** -0.5),
        'ada_b': nrm(ks[8], (DEPTH, 6 * d), 0.01),
        'norm1_g': 1.0 + nrm(ks[9], (DEPTH, d), 0.05),
        'norm2_g': 1.0 + nrm(ks[10], (DEPTH, d), 0.05),
        'w_in': nrm(ks[11], (DEPTH, d, IN_COLS), d ** -0.5),
        'q_norm_g': 1.0 + nrm(ks[12], (DEPTH, HEAD_DIM), 0.05),
        'k_norm_g': 1.0 + nrm(ks[13], (DEPTH, HEAD_DIM), 0.05),
        'hg_lb': nrm(ks[14], (DEPTH, 2, HG_W), 1.0),
        'hg_onorm_g': 1.0 + nrm(ks[15], (DEPTH, HG_DV), 0.05),
        'conv_w': nrm(ks[16], (DEPTH, CONV_K, CV_W), 0.5),
        'conv_b': nrm(ks[17], (DEPTH, CV_W), 0.01),
        'w_out': nrm(ks[18], (DEPTH, d, d), d ** -0.5),
        'peer_wq': nrm(ks[19], (DEPTH, d, PEER_HEADS * PEER_QDIM), d ** -0.5),
        'peer_k1': nrm(ks[20], (DEPTH, N_KEYS, PEER_HALF), PEER_HALF ** -0.5),
        'peer_k2': nrm(ks[21], (DEPTH, N_KEYS, PEER_HALF), PEER_HALF ** -0.5),
        'peer_u': nrm(ks[22], (DEPTH, N_EXPERTS, d), d ** -0.5),
        'peer_v': nrm(ks[23], (DEPTH, N_EXPERTS, d), 0.5),
    }


def reference(x_prompt, x_sample, cache_k, cache_v, state_hgrn, c, c_ctx, ada_w, ada_b, norm1_g, norm2_g, w_in, q_norm_g, k_norm_g, hg_lb, hg_onorm_g, conv_w, conv_b, w_out, peer_wq, peer_k1, peer_k2, peer_u, peer_v):
    lb_soft = jax.nn.softmax(hg_lb.astype(jnp.float32), axis=0)
    lb_all = jnp.cumsum(lb_soft, axis=0) - lb_soft[0]
    ctx_vec = c_ctx[None, :]
    y_prompt = x_prompt
    y_sample = x_sample
    ks_out, vs_out, ss_out = [], [], []
    for l in range(DEPTH):
        lw = {
            'ada_w': ada_w[l], 'ada_b': ada_b[l], 'norm1': norm1_g[l], 'norm2': norm2_g[l],
            'w_in': w_in[l], 'q_norm': q_norm_g[l], 'k_norm': k_norm_g[l], 'lb': lb_all[l],
            'hg_onorm': hg_onorm_g[l], 'conv_w': conv_w[l], 'conv_b': conv_b[l], 'w_out': w_out[l],
            'peer_wq': peer_wq[l], 'peer_k1': peer_k1[l], 'peer_k2': peer_k2[l],
            'peer_u': peer_u[l], 'peer_v': peer_v[l],
        }
        y_prompt, (k_l, v_l, sf_l, sb_l) = _layer(y_prompt, ctx_vec, lw, None)
        y_sample, _ = _layer(y_sample, c, lw, (cache_k[:, l], cache_v[:, l], state_hgrn[:, l, 0], state_hgrn[:, l, 1]))
        ks_out.append(k_l)
        vs_out.append(v_l)
        ss_out.append(jnp.stack([sf_l, sb_l], axis=1))
    new_cache_k = jnp.stack(ks_out, axis=1)
    new_cache_v = jnp.stack(vs_out, axis=1)
    new_state_hgrn = jnp.stack(ss_out, axis=1)
    return (y_prompt, y_sample, new_cache_k, new_cache_v, new_state_hgrn)
```

```python
import functools

import numpy as np
import jax
import jax.numpy as jnp
from jax import lax
from jax.experimental import pallas as pl
from jax.experimental.pallas import tpu as pltpu

F32 = jnp.float32
BF16 = jnp.bfloat16

D_MODEL = 2048
BATCH = 32
SEQ = 256
DEPTH = 2
DEC_BATCH = 4
DEC_SEQ = 1024
PAST_LEN = 256
GRID_W = 64
GROUP_W = D_MODEL // 4
FN_GROUPS = 4
FN_GW = GROUP_W // FN_GROUPS
HEAD_DIM = 128
N_HEADS = GROUP_W // HEAD_DIM
N_KV = 2
GQA_GROUP = N_HEADS // N_KV
ROPE_THETA = 10000.0
HG_DK = 128
HG_DV = 128
HG_HEADS = GROUP_W // HG_DK
CONV_K = 3
IN_COLS = 11 * GROUP_W
PEER_HEADS = 8
N_KEYS = 128
N_EXPERTS = N_KEYS * N_KEYS
PEER_QDIM = 256
PEER_HALF = PEER_QDIM // 2
PEER_TOPK = 16
EPS = 1e-6
F_FLOOR = 1e-30

N_PROMPT = BATCH * SEQ
N_SAMPLE = DEC_BATCH * DEC_SEQ
N_TOK = N_PROMPT + N_SAMPLE
MOD_ROWS = 8
LANES = 128
HG_CHUNK = 32
HG_CUM_BLOCK = 256
MASK_PITCH = 136
VMEM_LIMIT = 56 * 1024 * 1024

COL_FN, COL_Q, COL_K, COL_V = 0, 4, 8, 10
COL_GQ, COL_GFF, COL_GFB, COL_GI, COL_GO = 12, 16, 20, 24, 28
COL_CB, COL_CC, COL_CX = 32, 36, 40

NT_DIMS = (((1,), (1,)), ((), ()))
TN_DIMS = (((0,), (0,)), ((), ()))


def _params(*sem):
    return pltpu.CompilerParams(dimension_semantics=sem, vmem_limit_bytes=VMEM_LIMIT)


def _dot(a, b):
    return jnp.dot(a, b, preferred_element_type=F32)


def _split3(x):
    h = x.astype(BF16)
    r = x - h.astype(F32)
    m = r.astype(BF16)
    l = (r - m.astype(F32)).astype(BF16)
    return h, m, l


def _split2(x):
    h = x.astype(BF16)
    return h, (x - h.astype(F32)).astype(BF16)


def _mod_row(tok0):
    return jnp.where(tok0 >= N_PROMPT, (tok0 - N_PROMPT) // DEC_SEQ + 1, 0)


def _rms(x, g):
    return x * lax.rsqrt(jnp.mean(x * x, axis=-1, keepdims=True) + EPS) * g


def _ada_kernel(c_ref, w_ref, b_ref, o_ref):
    c = c_ref[...]
    s = (c * jax.nn.sigmoid(c)).astype(BF16)
    o_ref[0] = _dot(s, w_ref[0].astype(BF16)) + b_ref[0]


def _ada(cvec, ada_w, ada_b):
    tn = 1024
    return pl.pallas_call(
        _ada_kernel,
        out_shape=jax.ShapeDtypeStruct((DEPTH, MOD_ROWS, 6 * D_MODEL), F32),
        grid=(DEPTH, 6 * D_MODEL // tn),
        in_specs=[
            pl.BlockSpec((MOD_ROWS, D_MODEL), lambda l, j: (0, 0)),
            pl.BlockSpec((1, D_MODEL, tn), lambda l, j: (l, 0, j)),
            pl.BlockSpec((1, 1, tn), lambda l, j: (l, 0, j)),
        ],
        out_specs=pl.BlockSpec((1, MOD_ROWS, tn), lambda l, j: (l, 0, j)),
        compiler_params=_params("arbitrary", "arbitrary"),
        name="ada",
    )(cvec, ada_w, ada_b.reshape(DEPTH, 1, 6 * D_MODEL))


def _inproj_kernel(x_ref, sh_ref, sc_ref, g_ref, w_ref, o_ref, h_ref, *, tm):
    @pl.when(pl.program_id(1) == 0)
    def _():
        r = _mod_row(pl.program_id(0) * tm)
        y = _rms(x_ref[...], g_ref[...])
        h_ref[...] = (y * (1.0 + sc_ref[pl.ds(r, 1), :]) + sh_ref[pl.ds(r, 1), :]).astype(BF16)

    o_ref[...] = _dot(h_ref[...], w_ref[...])


def _inproj(x, mod, norm_g, w_in_bf):
    tm, tn = 1024, 512
    return pl.pallas_call(
        functools.partial(_inproj_kernel, tm=tm),
        out_shape=jax.ShapeDtypeStruct((N_TOK, IN_COLS), F32),
        grid=(N_TOK // tm, IN_COLS // tn),
        in_specs=[
            pl.BlockSpec((tm, D_MODEL), lambda i, j: (i, 0)),
            pl.BlockSpec((MOD_ROWS, D_MODEL), lambda i, j: (0, 0)),
            pl.BlockSpec((MOD_ROWS, D_MODEL), lambda i, j: (0, 1)),
            pl.BlockSpec((1, D_MODEL), lambda i, j: (0, 0)),
            pl.BlockSpec((D_MODEL, tn), lambda i, j: (0, j)),
        ],
        out_specs=pl.BlockSpec((tm, tn), lambda i, j: (i, j)),
        scratch_shapes=[pltpu.VMEM((tm, D_MODEL), BF16)],
        compiler_params=_params("arbitrary", "arbitrary"),
        name="inproj",
    )(x, mod, mod, norm_g.reshape(1, D_MODEL), w_in_bf)


def _fourier_kernel(*refs, aliased):
    u_ref, cch_ref, ccl_ref, cth_ref, ctl_ref = refs[:5]
    o_ref = refs[-1]
    cch, ccl, cth, ctl = cch_ref[...], ccl_ref[...], cth_ref[...], ctl_ref[...]
    for g in range(FN_GROUPS):
        u = u_ref[:, g * FN_GW:(g + 1) * FN_GW]
        uh, um, ul = _split3(u)
        w = _dot(uh, cch) + (_dot(um, cch) + _dot(uh, ccl)) + (_dot(ul, cch) + _dot(um, ccl))
        st = jnp.concatenate([w[:, :FN_GW], w[:, FN_GW:]], axis=0)
        sh, sm, sl = _split3(st)
        y = _dot(cth, sh) + (_dot(cth, sm) + _dot(ctl, sh)) + (_dot(cth, sl) + _dot(ctl, sm))
        o_ref[:, g * FN_GW:(g + 1) * FN_GW] = y


def _dft_consts(t):
    def cs(n):
        k = np.arange(n, dtype=np.float64)
        ang = 2.0 * np.pi * np.outer(k, k) / n
        return np.cos(ang) / np.sqrt(n), np.sin(ang) / np.sqrt(n)

    cc, sc = cs(FN_GW)
    ct, st = cs(t)
    right = np.concatenate([cc, sc], axis=1).astype(np.float32)
    left = np.concatenate([ct, -st], axis=1).astype(np.float32)

    def hl(a):
        a = jnp.asarray(a)
        h = a.astype(BF16)
        return h, (a - h.astype(F32)).astype(BF16)

    return hl(right) + hl(left)


def _fourier(z, t, nb, row0, prev):
    cch, ccl, cth, ctl = _dft_consts(t)
    rb = row0 // t
    in_specs = [
        pl.BlockSpec((t, GROUP_W), lambda b: (rb + b, COL_FN // 4)),
        pl.BlockSpec((FN_GW, 2 * FN_GW), lambda b: (0, 0)),
        pl.BlockSpec((FN_GW, 2 * FN_GW), lambda b: (0, 0)),
        pl.BlockSpec((t, 2 * t), lambda b: (0, 0)),
        pl.BlockSpec((t, 2 * t), lambda b: (0, 0)),
    ]
    args = [z, cch, ccl, cth, ctl]
    aliases = {}
    if prev is not None:
        in_specs.append(pl.BlockSpec(memory_space=pl.ANY))
        args.append(prev)
        aliases = {5: 0}
    return pl.pallas_call(
        functools.partial(_fourier_kernel, aliased=prev is not None),
        out_shape=jax.ShapeDtypeStruct((N_TOK, GROUP_W), F32),
        grid=(nb,),
        in_specs=in_specs,
        out_specs=pl.BlockSpec((t, GROUP_W), lambda b: (rb + b, 0)),
        input_output_aliases=aliases,
        compiler_params=_params("arbitrary"),
        name=f"fourier_t{t}",
    )(*args)


def _softmax_pv(q_bf, k_bf, v_bf):
    s = lax.dot_general(q_bf, k_bf, NT_DIMS, preferred_element_type=F32) * (HEAD_DIM ** -0.5)
    p = jnp.exp(s - jnp.max(s, axis=-1, keepdims=True))
    return _dot(p.astype(BF16), v_bf) / jnp.sum(p, axis=-1, keepdims=True)


def _attn_prompt_kernel(q_ref, k_ref, v_ref, qg_ref, kg_ref, o_ref, kc_ref, vc_ref):
    kn = _rms(k_ref[...], kg_ref[...])
    v = v_ref[...]
    kc_ref[...] = kn
    vc_ref[...] = v
    kb, vb = kn.astype(BF16), v.astype(BF16)
    for g in range(GQA_GROUP):
        qn = _rms(q_ref[:, g * HEAD_DIM:(g + 1) * HEAD_DIM], qg_ref[...])
        o_ref[:, g * HEAD_DIM:(g + 1) * HEAD_DIM] = _softmax_pv(qn.astype(BF16), kb, vb)


def _attn_prompt(z, q_g, k_g):
    t = SEQ
    kv_shape = jax.ShapeDtypeStruct((N_PROMPT, N_KV * HEAD_DIM), F32)
    return pl.pallas_call(
        _attn_prompt_kernel,
        out_shape=(jax.ShapeDtypeStruct((N_TOK, GROUP_W), F32), kv_shape, kv_shape),
        grid=(BATCH, N_KV),
        in_specs=[
            pl.BlockSpec((t, GQA_GROUP * HEAD_DIM), lambda b, h: (b, COL_Q // 2 + h)),
            pl.BlockSpec((t, HEAD_DIM), lambda b, h: (b, COL_K + h)),
            pl.BlockSpec((t, HEAD_DIM), lambda b, h: (b, COL_V + h)),
            pl.BlockSpec((1, HEAD_DIM), lambda b, h: (0, 0)),
            pl.BlockSpec((1, HEAD_DIM), lambda b, h: (0, 0)),
        ],
        out_specs=(
            pl.BlockSpec((t, GQA_GROUP * HEAD_DIM), lambda b, h: (b, h)),
            pl.BlockSpec((t, HEAD_DIM), lambda b, h: (b, h)),
            pl.BlockSpec((t, HEAD_DIM), lambda b, h: (b, h)),
        ),
        compiler_params=_params("arbitrary", "arbitrary"),
        name="attn_prompt",
    )(z, z, z, q_g.reshape(1, HEAD_DIM), k_g.reshape(1, HEAD_DIM))


def _rope(x, cosf, sinf):
    return x * cosf + pltpu.roll(x, HEAD_DIM // 2, axis=1) * sinf


def _attn_sample_kernel(q_ref, k_ref, v_ref, kctx_ref, vctx_ref, qg_ref, kg_ref, cos_ref, sin_ref,
                        prev_ref, o_ref, kall_ref, vall_ref, *, qb):
    del prev_ref
    cosf, sinf = cos_ref[...], sin_ref[...]
    kall_ref[0:PAST_LEN, :] = kctx_ref[0].astype(BF16)
    vall_ref[0:PAST_LEN, :] = vctx_ref[0].astype(BF16)
    kall_ref[PAST_LEN:, :] = _rope(_rms(k_ref[...], kg_ref[...]), cosf, sinf).astype(BF16)
    vall_ref[PAST_LEN:, :] = v_ref[...].astype(BF16)
    kb, vb = kall_ref[...], vall_ref[...]
    for i in range(DEC_SEQ // qb):
        rows = slice(i * qb, (i + 1) * qb)
        for g in range(GQA_GROUP):
            cols = slice(g * HEAD_DIM, (g + 1) * HEAD_DIM)
            qn = _rope(_rms(q_ref[rows, cols], qg_ref[...]), cosf[rows], sinf[rows])
            o_ref[rows, cols] = _softmax_pv(qn.astype(BF16), kb, vb)


def _rope_tables():
    rows = DEC_SEQ // GRID_W
    row = np.repeat(np.arange(rows, dtype=np.float32), GRID_W)
    col = np.tile(np.arange(GRID_W, dtype=np.float32), rows)
    n_freq = HEAD_DIM // 4
    inv = (np.float32(ROPE_THETA) ** (-np.arange(n_freq, dtype=np.float32) / n_freq)).astype(np.float32)
    ang = np.concatenate([row[:, None] * inv, col[:, None] * inv], axis=-1).astype(np.float32)
    cos, sin = np.cos(ang.astype(np.float64)), np.sin(ang.astype(np.float64))
    cosf = np.concatenate([cos, cos], axis=-1).astype(np.float32)
    sinf = np.concatenate([-sin, sin], axis=-1).astype(np.float32)
    return jnp.asarray(cosf), jnp.asarray(sinf)


def _attn_sample(z, q_g, k_g, k_ctx, v_ctx, prev):
    t = DEC_SEQ
    rb = N_PROMPT // t
    cosf, sinf = _rope_tables()
    const = lambda b, h: (0, 0)
    return pl.pallas_call(
        functools.partial(_attn_sample_kernel, qb=256),
        out_shape=jax.ShapeDtypeStruct((N_TOK, GROUP_W), F32),
        grid=(DEC_BATCH, N_KV),
        in_specs=[
            pl.BlockSpec((t, GQA_GROUP * HEAD_DIM), lambda b, h: (rb + b, COL_Q // 2 + h)),
            pl.BlockSpec((t, HEAD_DIM), lambda b, h: (rb + b, COL_K + h)),
            pl.BlockSpec((t, HEAD_DIM), lambda b, h: (rb + b, COL_V + h)),
            pl.BlockSpec((1, PAST_LEN, HEAD_DIM), lambda b, h: (b, 0, h)),
            pl.BlockSpec((1, PAST_LEN, HEAD_DIM), lambda b, h: (b, 0, h)),
            pl.BlockSpec((1, HEAD_DIM), const),
            pl.BlockSpec((1, HEAD_DIM), const),
            pl.BlockSpec((t, HEAD_DIM), const),
            pl.BlockSpec((t, HEAD_DIM), const),
            pl.BlockSpec(memory_space=pl.ANY),
        ],
        out_specs=pl.BlockSpec((t, GQA_GROUP * HEAD_DIM), lambda b, h: (rb + b, h)),
        scratch_shapes=[pltpu.VMEM((PAST_LEN + t, HEAD_DIM), BF16), pltpu.VMEM((PAST_LEN + t, HEAD_DIM), BF16)],
        input_output_aliases={9: 0},
        compiler_params=_params("arbitrary", "arbitrary"),
        name="attn_sample",
    )(z, z, z, k_ctx, v_ctx, q_g.reshape(1, HEAD_DIM), k_g.reshape(1, HEAD_DIM), cosf, sinf, prev)


def _hgrn_chain(rev, r0, gq_ref, gi_ref, k_scr, cum_scr, st_scr, o_scr):
    c = HG_CHUNK
    d = 1 if rev else 0
    rows = pl.ds(r0, c)
    qc, vc = gq_ref[rows, :], gi_ref[rows, :]
    kc, cc = k_scr[d, rows, :], cum_scr[d, rows, :]
    st = st_scr[d]
    last = cc[0:1] if rev else cc[c - 1:c]
    qs = (qc * jnp.exp(cc)).astype(BF16)
    o_inter = lax.dot_general(qs, st.astype(BF16), NT_DIMS, preferred_element_type=F32)
    ks = (kc * jnp.exp(last - cc)).astype(BF16)
    upd = lax.dot_general(vc.astype(BF16), ks, TN_DIMS, preferred_element_type=F32)
    st_scr[d] = st * jnp.exp(last) + upd
    row = lax.broadcasted_iota(jnp.int32, (c, 1), 0)
    acc = jnp.zeros((c, HG_DV), F32)
    for delta in range(c):
        if delta == 0:
            k_sh, c_sh, v_sh = kc, cc, vc
        else:
            sh = (c - delta) if rev else delta
            k_sh, c_sh, v_sh = (pltpu.roll(a, sh, axis=0) for a in (kc, cc, vc))
        a = jnp.sum(qc * k_sh * jnp.exp(jnp.minimum(cc - c_sh, 0.0)), axis=-1, keepdims=True)
        valid = (row <= c - 1 - delta) if rev else (row >= delta)
        acc = acc + jnp.where(valid, a, 0.0) * v_sh
    o_scr[d, rows, :] = o_inter + acc


def _hgrn_kernel(*refs, layer, t, has_ctx):
    gq_ref, gff_ref, gfb_ref, gi_ref, go_ref, lb_ref, on_ref, bdf_ref, bdb_ref = refs[:9]
    n_in = 9
    s0_ref = None
    if has_ctx:
        s0_ref = refs[9]
        n_in = 11
    o_ref, sout_ref = refs[n_in], refs[n_in + 1]
    k_scr, cum_scr, st_scr, o_scr = refs[n_in + 2:]

    raw = lb_ref[...]
    e = jnp.exp(raw - jnp.max(raw, axis=0, keepdims=True))
    soft = e / jnp.sum(e, axis=0, keepdims=True)
    csum = soft[0]
    for i in range(1, layer + 1):
        csum = csum + soft[i]
    lb = csum - soft[0]

    for d, (g_ref, bd_ref) in enumerate(((gff_ref, bdf_ref), (gfb_ref, bdb_ref))):
        lbd = lb[d:d + 1]
        f = jnp.maximum(lbd + (1.0 - lbd) * jax.nn.sigmoid(g_ref[...]), F_FLOOR)
        k_scr[d] = 1.0 - f
        lf = jnp.log(f)
        bd = bd_ref[...]
        for p in range(t // HG_CUM_BLOCK):
            rows = slice(p * HG_CUM_BLOCK, (p + 1) * HG_CUM_BLOCK)
            h, m, l = _split3(lf[rows])
            cum_scr[d, rows, :] = _dot(bd, h) + _dot(bd, m) + _dot(bd, l)
        if has_ctx:
            st_scr[d] = s0_ref[0, d, 0].T
        else:
            st_scr[d] = jnp.zeros((HG_DV, HG_DK), F32)

    n = t // HG_CHUNK

    def body(i, carry):
        _hgrn_chain(False, pl.multiple_of(i * HG_CHUNK, HG_CHUNK), gq_ref, gi_ref, k_scr, cum_scr, st_scr, o_scr)
        _hgrn_chain(True, pl.multiple_of((n - 1 - i) * HG_CHUNK, HG_CHUNK), gq_ref, gi_ref, k_scr, cum_scr,
                    st_scr, o_scr)
        return carry

    lax.fori_loop(0, n, body, 0)

    o = _rms(o_scr[0] + o_scr[1], on_ref[...])
    go = go_ref[...]
    o_ref[...] = o * (go * jax.nn.sigmoid(go))
    for d in range(2):
        sout_ref[0, d, 0] = st_scr[d].T


def _cum_consts():
    r = np.arange(HG_CUM_BLOCK)
    same = (r[:, None] // HG_CHUNK) == (r[None, :] // HG_CHUNK)
    fwd = same & (r[None, :] <= r[:, None])
    bwd = same & (r[None, :] >= r[:, None])
    return jnp.asarray(fwd, BF16), jnp.asarray(bwd, BF16)


def _hgrn(z, hg_lb, onorm_g, layer, t, nb, row0, s0, prev):
    rb = row0 // t
    bdf, bdb = _cum_consts()
    has_ctx = s0 is not None
    col = lambda c0: (lambda b, h: (rb + b, c0 + h))
    in_specs = [
        pl.BlockSpec((t, HG_DK), col(COL_GQ)),
        pl.BlockSpec((t, HG_DK), col(COL_GFF)),
        pl.BlockSpec((t, HG_DK), col(COL_GFB)),
        pl.BlockSpec((t, HG_DV), col(COL_GI)),
        pl.BlockSpec((t, HG_DV), col(COL_GO)),
        pl.BlockSpec((DEPTH, 2, HG_DK), lambda b, h: (0, 0, h)),
        pl.BlockSpec((1, HG_DV), lambda b, h: (0, 0)),
        pl.BlockSpec((HG_CUM_BLOCK, HG_CUM_BLOCK), lambda b, h: (0, 0)),
        pl.BlockSpec((HG_CUM_BLOCK, HG_CUM_BLOCK), lambda b, h: (0, 0)),
    ]
    args = [z, z, z, z, z, hg_lb, onorm_g.reshape(1, HG_DV), bdf, bdb]
    aliases = {}
    if has_ctx:
        in_specs += [pl.BlockSpec((1, 2, 1, HG_DK, HG_DV), lambda b, h: (b, 0, h, 0, 0)),
                     pl.BlockSpec(memory_space=pl.ANY)]
        args += [s0, prev]
        aliases = {10: 0}
    return pl.pallas_call(
        functools.partial(_hgrn_kernel, layer=layer, t=t, has_ctx=has_ctx),
        out_shape=(jax.ShapeDtypeStruct((N_TOK, GROUP_W), F32),
                   jax.ShapeDtypeStruct((nb, 2, HG_HEADS, HG_DK, HG_DV), F32)),
        grid=(nb, HG_HEADS),
        in_specs=in_specs,
        out_specs=(pl.BlockSpec((t, HG_DV), lambda b, h: (rb + b, h)),
                   pl.BlockSpec((1, 2, 1, HG_DK, HG_DV), lambda b, h: (b, 0, h, 0, 0))),
        scratch_shapes=[pltpu.VMEM((2, t, HG_DK), F32), pltpu.VMEM((2, t, HG_DK), F32),
                        pltpu.VMEM((2, HG_DV, HG_DK), F32), pltpu.VMEM((2, t, HG_DV), F32)],
        input_output_aliases=aliases,
        compiler_params=_params("arbitrary", "arbitrary"),
        name=f"hgrn_t{t}",
    )(*args)


def _conv_kernel(*refs, t):
    cb_ref, cc_ref, cx_ref, w_ref, b_ref = refs[:5]
    o_ref = refs[-1]
    p = cc_ref[...] * cx_ref[...]
    row = lax.broadcasted_iota(jnp.int32, (t, 1), 0)
    prev = jnp.where(row >= 1, pltpu.roll(p, 1, axis=0), 0.0)
    nxt = jnp.where(row <= t - 2, pltpu.roll(p, t - 1, axis=0), 0.0)
    w = w_ref[...]
    y = prev * w[0:1] + p * w[1:2] + nxt * w[2:3] + b_ref[...]
    o_ref[...] = cb_ref[...] * y


def _conv(z, w, b, t, nb, row0, prev):
    rb = row0 // t
    in_specs = [
        pl.BlockSpec((t, GROUP_W), lambda i: (rb + i, COL_CB // 4)),
        pl.BlockSpec((t, GROUP_W), lambda i: (rb + i, COL_CC // 4)),
        pl.BlockSpec((t, GROUP_W), lambda i: (rb + i, COL_CX // 4)),
        pl.BlockSpec((CONV_K, GROUP_W), lambda i: (0, 0)),
        pl.BlockSpec((1, GROUP_W), lambda i: (0, 0)),
    ]
    args = [z, z, z, w, b.reshape(1, GROUP_W)]
    aliases = {}
    if prev is not None:
        in_specs.append(pl.BlockSpec(memory_space=pl.ANY))
        args.append(prev)
        aliases = {5: 0}
    return pl.pallas_call(
        functools.partial(_conv_kernel, t=t),
        out_shape=jax.ShapeDtypeStruct((N_TOK, GROUP_W), F32),
        grid=(nb,),
        in_specs=in_specs,
        out_specs=pl.BlockSpec((t, GROUP_W), lambda i: (rb + i, 0)),
        input_output_aliases=aliases,
        compiler_params=_params("arbitrary"),
        name=f"conv_t{t}",
    )(*args)


def _outproj_kernel(fn_ref, at_ref, hg_ref, cv_ref, w_ref, x_ref, g1_ref, sh_ref, sc_ref, n2_ref,
                    x1_ref, h2_ref, *, tm):
    r = _mod_row(pl.program_id(0) * tm)
    acc = None
    for g, ref in enumerate((fn_ref, at_ref, hg_ref, cv_ref)):
        part = _dot(ref[...].astype(BF16), w_ref[g * GROUP_W:(g + 1) * GROUP_W, :])
        acc = part if acc is None else acc + part
    x1 = x_ref[...] + g1_ref[pl.ds(r, 1), :] * acc
    x1_ref[...] = x1
    h2 = _rms(x1, n2_ref[...]) * (1.0 + sc_ref[pl.ds(r, 1), :]) + sh_ref[pl.ds(r, 1), :]
    h2_ref[...] = h2.astype(BF16)


def _outproj(o_fn, o_at, o_hg, o_cv, w_out_bf, x, mod, norm2_g):
    tm = 512
    grp = pl.BlockSpec((tm, GROUP_W), lambda i: (i, 0))
    modspec = lambda k: pl.BlockSpec((MOD_ROWS, D_MODEL), lambda i: (0, k))
    return pl.pallas_call(
        functools.partial(_outproj_kernel, tm=tm),
        out_shape=(jax.ShapeDtypeStruct((N_TOK, D_MODEL), F32), jax.ShapeDtypeStruct((N_TOK, D_MODEL), BF16)),
        grid=(N_TOK // tm,),
        in_specs=[grp, grp, grp, grp,
                  pl.BlockSpec((D_MODEL, D_MODEL), lambda i: (0, 0)),
                  pl.BlockSpec((tm, D_MODEL), lambda i: (i, 0)),
                  modspec(2), modspec(3), modspec(4),
                  pl.BlockSpec((1, D_MODEL), lambda i: (0, 0))],
        out_specs=(pl.BlockSpec((tm, D_MODEL), lambda i: (i, 0)), pl.BlockSpec((tm, D_MODEL), lambda i: (i, 0))),
        compiler_params=_params("arbitrary"),
        name="outproj",
    )(o_fn, o_at, o_hg, o_cv, w_out_bf, x, mod, mod, mod, norm2_g.reshape(1, D_MODEL))


def _topk_rows(x_scr, n_rows, v_out, i_out, tm):
    iota = lax.broadcasted_iota(jnp.int32, (n_rows, tm), 0)

    def body(j, carry):
        x = x_scr[0:n_rows, :]
        m = jnp.max(x, axis=0, keepdims=True)
        idx = jnp.min(jnp.where(x == m, iota, n_rows), axis=0, keepdims=True)
        v_out[pl.ds(j, 1), :] = m
        i_out[pl.ds(j, 1), :] = idx
        x_scr[0:n_rows, :] = jnp.where(iota == idx, -jnp.inf, x)
        return carry

    lax.fori_loop(0, PEER_TOPK, body, 0)


def _peer_topk_kernel(h_ref, wq_ref, k1_ref, k2_ref, e_ref, g_ref, x_scr, v1_scr, i1_scr, v2_scr, i2_scr,
                      ts_scr, tc_scr, *, tm):
    q = _dot(h_ref[...], wq_ref[...]).astype(BF16)
    for half, (k_ref, v_scr, i_scr) in enumerate(((k1_ref, v1_scr, i1_scr), (k2_ref, v2_scr, i2_scr))):
        qh = q[:, half * PEER_HALF:(half + 1) * PEER_HALF]
        x_scr[0:N_KEYS, :] = lax.dot_general(k_ref[...], qh, NT_DIMS, preferred_element_type=F32)
        _topk_rows(x_scr, N_KEYS, v_scr, i_scr, tm)
    v1, v2 = v1_scr[...], v2_scr[...]
    for a in range(PEER_TOPK):
        x_scr[a * PEER_TOPK:(a + 1) * PEER_TOPK, :] = v1[a:a + 1] + v2
    _topk_rows(x_scr, PEER_TOPK * PEER_TOPK, ts_scr, tc_scr, tm)
    tc = tc_scr[...]
    c1, c2 = tc // PEER_TOPK, tc % PEER_TOPK
    i1, i2 = i1_scr[...], i2_scr[...]
    e1 = jnp.zeros_like(tc)
    e2 = jnp.zeros_like(tc)
    for a in range(PEER_TOPK):
        e1 = jnp.where(c1 == a, i1[a:a + 1], e1)
        e2 = jnp.where(c2 == a, i2[a:a + 1], e2)
    e_ref[...] = e1 * N_KEYS + e2
    ts = ts_scr[...]
    p = jnp.exp(ts - jnp.max(ts, axis=0, keepdims=True))
    g_ref[...] = p / jnp.sum(p, axis=0, keepdims=True)


def _peer_topk(h2, wq_bf, k1_bf, k2_bf):
    tm = 256
    row_f = pltpu.VMEM((PEER_TOPK, tm), F32)
    row_i = pltpu.VMEM((PEER_TOPK, tm), jnp.int32)
    return pl.pallas_call(
        functools.partial(_peer_topk_kernel, tm=tm),
        out_shape=(jax.ShapeDtypeStruct((PEER_HEADS * PEER_TOPK, N_TOK), jnp.int32),
                   jax.ShapeDtypeStruct((PEER_HEADS * PEER_TOPK, N_TOK), F32)),
        grid=(N_TOK // tm, PEER_HEADS),
        in_specs=[
            pl.BlockSpec((tm, D_MODEL), lambda i, h: (i, 0)),
            pl.BlockSpec((D_MODEL, PEER_QDIM), lambda i, h: (0, h)),
            pl.BlockSpec((N_KEYS, PEER_HALF), lambda i, h: (0, 0)),
            pl.BlockSpec((N_KEYS, PEER_HALF), lambda i, h: (0, 0)),
        ],
        out_specs=(pl.BlockSpec((PEER_TOPK, tm), lambda i, h: (h, i)),
                   pl.BlockSpec((PEER_TOPK, tm), lambda i, h: (h, i))),
        scratch_shapes=[pltpu.VMEM((PEER_TOPK * PEER_TOPK, tm), F32), row_f, row_i, row_f, row_i, row_f, row_i],
        compiler_params=_params("arbitrary", "arbitrary"),
        name="peer_topk",
    )(h2, wq_bf, k1_bf, k2_bf)


def _peer_mask_kernel(e_ref, g_ref, m_ref, et_scr, gt_scr, slab_scr, *, tb):
    et_scr[...] = e_ref[...].T
    gt_scr[...] = g_ref[...].T
    key = lax.broadcasted_iota(jnp.int32, (N_KEYS, PEER_HEADS * PEER_TOPK), 0)

    def body(t, carry):
        er = et_scr[pl.ds(t, 1), :]
        gr = gt_scr[pl.ds(t, 1), :]
        p1 = jnp.where(key == er // N_KEYS, gr, 0.0).astype(BF16)
        p2 = jnp.where(key == er % N_KEYS, 1.0, 0.0).astype(BF16)
        mt = lax.dot_general(p1, p2, NT_DIMS, preferred_element_type=F32)
        slab_scr[pl.ds(pl.multiple_of(t * MASK_PITCH, 8), N_KEYS), :] = mt
        return carry

    lax.fori_loop(0, tb, body, 0)
    for k1 in range(N_KEYS):
        m_ref[:, k1 * N_KEYS:(k1 + 1) * N_KEYS] = slab_scr[pl.ds(k1, tb, stride=MASK_PITCH), :].astype(BF16)


def _peer_mask(experts_t, gates_t):
    tb = LANES
    pairs = PEER_HEADS * PEER_TOPK
    return pl.pallas_call(
        functools.partial(_peer_mask_kernel, tb=tb),
        out_shape=jax.ShapeDtypeStruct((N_TOK, N_EXPERTS), BF16),
        grid=(N_TOK // tb,),
        in_specs=[pl.BlockSpec((pairs, tb), lambda i: (0, i)), pl.BlockSpec((pairs, tb), lambda i: (0, i))],
        out_specs=pl.BlockSpec((tb, N_EXPERTS), lambda i: (i, 0)),
        scratch_shapes=[pltpu.VMEM((tb, pairs), jnp.int32), pltpu.VMEM((tb, pairs), F32),
                        pltpu.VMEM((tb * MASK_PITCH, N_KEYS), F32)],
        compiler_params=_params("arbitrary"),
        name="peer_mask",
    )(experts_t, gates_t)


def _peer_expert_kernel(h_ref, ut_ref, v_ref, m_ref, x_ref, g2_ref, o_ref, acc_ref, *, tm):
    j = pl.program_id(1)

    @pl.when(j == 0)
    def _():
        acc_ref[...] = jnp.zeros_like(acc_ref)

    s = _dot(h_ref[...], ut_ref[...])
    a = (m_ref[...].astype(F32) * jax.nn.gelu(s)).astype(BF16)
    acc_ref[...] += _dot(a, v_ref[...])

    @pl.when(j == pl.num_programs(1) - 1)
    def _():
        r = _mod_row(pl.program_id(0) * tm)
        o_ref[...] = x_ref[...] + g2_ref[pl.ds(r, 1), :] * acc_ref[...]


def _peer_expert(h2, ut_bf, v_bf, mask, x1, mod):
    tm, te = 512, 512
    return pl.pallas_call(
        functools.partial(_peer_expert_kernel, tm=tm),
        out_shape=jax.ShapeDtypeStruct((N_TOK, D_MODEL), F32),
        grid=(N_TOK // tm, N_EXPERTS // te),
        in_specs=[
            pl.BlockSpec((tm, D_MODEL), lambda i, j: (i, 0)),
            pl.BlockSpec((D_MODEL, te), lambda i, j: (0, j)),
            pl.BlockSpec((te, D_MODEL), lambda i, j: (j, 0)),
            pl.BlockSpec((tm, te), lambda i, j: (i, j)),
            pl.BlockSpec((tm, D_MODEL), lambda i, j: (i, 0)),
            pl.BlockSpec((MOD_ROWS, D_MODEL), lambda i, j: (0, 5)),
        ],
        out_specs=pl.BlockSpec((tm, D_MODEL), lambda i, j: (i, 0)),
        scratch_shapes=[pltpu.VMEM((tm, D_MODEL), F32)],
        compiler_params=_params("arbitrary", "arbitrary"),
        name="peer_expert",
    )(h2, ut_bf, v_bf, mask, x1, mod)


def kernel(x_prompt, x_sample, cache_k, cache_v, state_hgrn, c, c_ctx, ada_w, ada_b, norm1_g, norm2_g, w_in,
           q_norm_g, k_norm_g, hg_lb, hg_onorm_g, conv_w, conv_b, w_out, peer_wq, peer_k1, peer_k2, peer_u, peer_v):
    x = jnp.concatenate([x_prompt.reshape(N_PROMPT, D_MODEL), x_sample.reshape(N_SAMPLE, D_MODEL)], axis=0)
    cvec = jnp.concatenate([c_ctx[None, :], c, jnp.zeros((MOD_ROWS - 1 - DEC_BATCH, D_MODEL), F32)], axis=0)
    mod_all = _ada(cvec, ada_w, ada_b)

    ks_out, vs_out, ss_out = [], [], []
    for l in range(DEPTH):
        mod = mod_all[l]
        z = _inproj(x, mod, norm1_g[l], w_in[l].astype(BF16))

        o_fn = _fourier(z, SEQ, BATCH, 0, None)
        o_fn = _fourier(z, DEC_SEQ, DEC_BATCH, N_PROMPT, o_fn)

        o_at, k_l, v_l = _attn_prompt(z, q_norm_g[l], k_norm_g[l])
        k_ctx = cache_k[:, l].reshape(DEC_BATCH, PAST_LEN, N_KV * HEAD_DIM)
        v_ctx = cache_v[:, l].reshape(DEC_BATCH, PAST_LEN, N_KV * HEAD_DIM)
        o_at = _attn_sample(z, q_norm_g[l], k_norm_g[l], k_ctx, v_ctx, o_at)

        o_hg, s_l = _hgrn(z, hg_lb, hg_onorm_g[l], l, SEQ, BATCH, 0, None, None)
        o_hg, _ = _hgrn(z, hg_lb, hg_onorm_g[l], l, DEC_SEQ, DEC_BATCH, N_PROMPT, state_hgrn[:, l], o_hg)

        o_cv = _conv(z, conv_w[l], conv_b[l], SEQ, BATCH, 0, None)
        o_cv = _conv(z, conv_w[l], conv_b[l], DEC_SEQ, DEC_BATCH, N_PROMPT, o_cv)

        x1, h2 = _outproj(o_fn, o_at, o_hg, o_cv, w_out[l].astype(BF16), x, mod, norm2_g[l])

        experts_t, gates_t = _peer_topk(h2, peer_wq[l].astype(BF16), peer_k1[l].astype(BF16),
                                        peer_k2[l].astype(BF16))
        mask = _peer_mask(experts_t, gates_t)
        x = _peer_expert(h2, peer_u[l].T.astype(BF16), peer_v[l].astype(BF16), mask, x1, mod)

        ks_out.append(k_l.reshape(BATCH, SEQ, N_KV, HEAD_DIM))
        vs_out.append(v_l.reshape(BATCH, SEQ, N_KV, HEAD_DIM))
        ss_out.append(s_l)

    y_prompt = x[:N_PROMPT].reshape(BATCH, SEQ, D_MODEL)
    y_sample = x[N_PROMPT:].reshape(DEC_BATCH, DEC_SEQ, D_MODEL)
    return (y_prompt, y_sample, jnp.stack(ks_out, axis=1), jnp.stack(vs_out, axis=1), jnp.stack(ss_out, axis=1))
```

```python
import functools

import numpy as np
import jax
import jax.numpy as jnp
from jax import lax
from jax.experimental import pallas as pl
from jax.experimental.pallas import tpu as pltpu

F32 = jnp.float32
BF16 = jnp.bfloat16

D_MODEL = 2048
BATCH = 32
SEQ = 256
DEPTH = 2
DEC_BATCH = 4
DEC_SEQ = 1024
PAST_LEN = 256
GRID_W = 64
GROUP_W = D_MODEL // 4
FN_GROUPS = 4
FN_GW = GROUP_W // FN_GROUPS
HEAD_DIM = 128
N_HEADS = GROUP_W // HEAD_DIM
N_KV = 2
GQA_GROUP = N_HEADS // N_KV
ROPE_THETA = 10000.0
HG_DK = 128
HG_DV = 128
HG_HEADS = GROUP_W // HG_DK
CONV_K = 3
IN_COLS = 11 * GROUP_W
PEER_HEADS = 8
N_KEYS = 128
N_EXPERTS = N_KEYS * N_KEYS
PEER_QDIM = 256
PEER_HALF = PEER_QDIM // 2
PEER_TOPK = 16
EPS = 1e-6
F_FLOOR = 1e-30

N_PROMPT = BATCH * SEQ
N_SAMPLE = DEC_BATCH * DEC_SEQ
N_TOK = N_PROMPT + N_SAMPLE
MOD_ROWS = 8
LANES = 128
HG_CHUNK = 32
HG_CUM_BLOCK = 256
MASK_PITCH = 136
VMEM_LIMIT = 56 * 1024 * 1024

COL_FN, COL_Q, COL_K, COL_V = 0, 4, 8, 10
COL_GQ, COL_GFF, COL_GFB, COL_GI, COL_GO = 12, 16, 20, 24, 28
COL_CB, COL_CC, COL_CX = 32, 36, 40

NT_DIMS = (((1,), (1,)), ((), ()))
TN_DIMS = (((0,), (0,)), ((), ()))


def _params(*sem):
    return pltpu.CompilerParams(dimension_semantics=sem, vmem_limit_bytes=VMEM_LIMIT)


def _dot(a, b):
    return jnp.dot(a, b, preferred_element_type=F32)


def _split3(x):
    h = x.astype(BF16)
    r = x - h.astype(F32)
    m = r.astype(BF16)
    l = (r - m.astype(F32)).astype(BF16)
    return h, m, l


def _split2(x):
    h = x.astype(BF16)
    return h, (x - h.astype(F32)).astype(BF16)


def _mod_row(tok0):
    return jnp.where(tok0 >= N_PROMPT, (tok0 - N_PROMPT) // DEC_SEQ + 1, 0)


def _rms(x, g):
    return x * lax.rsqrt(jnp.mean(x * x, axis=-1, keepdims=True) + EPS) * g


def _ada_kernel(c_ref, w_ref, b_ref, o_ref):
    c = c_ref[...]
    s = (c * jax.nn.sigmoid(c)).astype(BF16)
    o_ref[0] = _dot(s, w_ref[0].astype(BF16)) + b_ref[0]


def _ada(cvec, ada_w, ada_b):
    tn = 1024
    return pl.pallas_call(
        _ada_kernel,
        out_shape=jax.ShapeDtypeStruct((DEPTH, MOD_ROWS, 6 * D_MODEL), F32),
        grid=(DEPTH, 6 * D_MODEL // tn),
        in_specs=[
            pl.BlockSpec((MOD_ROWS, D_MODEL), lambda l, j: (0, 0)),
            pl.BlockSpec((1, D_MODEL, tn), lambda l, j: (l, 0, j)),
            pl.BlockSpec((1, 1, tn), lambda l, j: (l, 0, j)),
        ],
        out_specs=pl.BlockSpec((1, MOD_ROWS, tn), lambda l, j: (l, 0, j)),
        compiler_params=_params("arbitrary", "arbitrary"),
        name="ada",
    )(cvec, ada_w, ada_b.reshape(DEPTH, 1, 6 * D_MODEL))


def _inproj_kernel(x_ref, sh_ref, sc_ref, g_ref, w_ref, o_ref, h_ref, *, tm):
    @pl.when(pl.program_id(1) == 0)
    def _():
        r = _mod_row(pl.program_id(0) * tm)
        y = _rms(x_ref[...], g_ref[...])
        h_ref[...] = (y * (1.0 + sc_ref[pl.ds(r, 1), :]) + sh_ref[pl.ds(r, 1), :]).astype(BF16)

    o_ref[...] = _dot(h_ref[...], w_ref[...])


def _inproj(x, mod, norm_g, w_in_bf):
    tm, tn = 1024, 512
    return pl.pallas_call(
        functools.partial(_inproj_kernel, tm=tm),
        out_shape=jax.ShapeDtypeStruct((N_TOK, IN_COLS), F32),
        grid=(N_TOK // tm, IN_COLS // tn),
        in_specs=[
            pl.BlockSpec((tm, D_MODEL), lambda i, j: (i, 0)),
            pl.BlockSpec((MOD_ROWS, D_MODEL), lambda i, j: (0, 0)),
            pl.BlockSpec((MOD_ROWS, D_MODEL), lambda i, j: (0, 1)),
            pl.BlockSpec((1, D_MODEL), lambda i, j: (0, 0)),
            pl.BlockSpec((D_MODEL, tn), lambda i, j: (0, j)),
        ],
        out_specs=pl.BlockSpec((tm, tn), lambda i, j: (i, j)),
        scratch_shapes=[pltpu.VMEM((tm, D_MODEL), BF16)],
        compiler_params=_params("arbitrary", "arbitrary"),
        name="inproj",
    )(x, mod, mod, norm_g.reshape(1, D_MODEL), w_in_bf)


def _fourier_kernel(*refs, aliased):
    u_ref, cch_ref, ccl_ref, cth_ref, ctl_ref = refs[:5]
    o_ref = refs[-1]
    cch, ccl, cth, ctl = cch_ref[...], ccl_ref[...], cth_ref[...], ctl_ref[...]
    for g in range(FN_GROUPS):
        u = u_ref[:, g * FN_GW:(g + 1) * FN_GW]
        uh, um, ul = _split3(u)
        w = _dot(uh, cch) + (_dot(um, cch) + _dot(uh, ccl)) + (_dot(ul, cch) + _dot(um, ccl))
        st = jnp.concatenate([w[:, :FN_GW], w[:, FN_GW:]], axis=0)
        sh, sm, sl = _split3(st)
        y = _dot(cth, sh) + (_dot(cth, sm) + _dot(ctl, sh)) + (_dot(cth, sl) + _dot(ctl, sm))
        o_ref[:, g * FN_GW:(g + 1) * FN_GW] = y


def _dft_consts(t):
    def cs(n):
        k = np.arange(n, dtype=np.float64)
        ang = 2.0 * np.pi * np.outer(k, k) / n
        return np.cos(ang) / np.sqrt(n), np.sin(ang) / np.sqrt(n)

    cc, sc = cs(FN_GW)
    ct, st = cs(t)
    right = np.concatenate([cc, sc], axis=1).astype(np.float32)
    left = np.concatenate([ct, -st], axis=1).astype(np.float32)

    def hl(a):
        a = jnp.asarray(a)
        h = a.astype(BF16)
        return h, (a - h.astype(F32)).astype(BF16)

    return hl(right) + hl(left)


def _fourier(z, t, nb, row0, prev):
    cch, ccl, cth, ctl = _dft_consts(t)
    rb = row0 // t
    in_specs = [
        pl.BlockSpec((t, GROUP_W), lambda b: (rb + b, COL_FN // 4)),
        pl.BlockSpec((FN_GW, 2 * FN_GW), lambda b: (0, 0)),
        pl.BlockSpec((FN_GW, 2 * FN_GW), lambda b: (0, 0)),
        pl.BlockSpec((t, 2 * t), lambda b: (0, 0)),
        pl.BlockSpec((t, 2 * t), lambda b: (0, 0)),
    ]
    args = [z, cch, ccl, cth, ctl]
    aliases = {}
    if prev is not None:
        in_specs.append(pl.BlockSpec(memory_space=pl.ANY))
        args.append(prev)
        aliases = {5: 0}
    return pl.pallas_call(
        functools.partial(_fourier_kernel, aliased=prev is not None),
        out_shape=jax.ShapeDtypeStruct((N_TOK, GROUP_W), F32),
        grid=(nb,),
        in_specs=in_specs,
        out_specs=pl.BlockSpec((t, GROUP_W), lambda b: (rb + b, 0)),
        input_output_aliases=aliases,
        compiler_params=_params("arbitrary"),
        name=f"fourier_t{t}",
    )(*args)


def _softmax_pv(q_bf, k_bf, v_bf):
    s = lax.dot_general(q_bf, k_bf, NT_DIMS, preferred_element_type=F32) * (HEAD_DIM ** -0.5)
    p = jnp.exp(s - jnp.max(s, axis=-1, keepdims=True))
    return _dot(p.astype(BF16), v_bf) / jnp.sum(p, axis=-1, keepdims=True)


def _attn_prompt_kernel(q_ref, k_ref, v_ref, qg_ref, kg_ref, o_ref, kc_ref, vc_ref):
    kn = _rms(k_ref[...], kg_ref[...])
    v = v_ref[...]
    kc_ref[...] = kn
    vc_ref[...] = v
    kb, vb = kn.astype(BF16), v.astype(BF16)
    for g in range(GQA_GROUP):
        qn = _rms(q_ref[:, g * HEAD_DIM:(g + 1) * HEAD_DIM], qg_ref[...])
        o_ref[:, g * HEAD_DIM:(g + 1) * HEAD_DIM] = _softmax_pv(qn.astype(BF16), kb, vb)


def _attn_prompt(z, q_g, k_g):
    t = SEQ
    kv_shape = jax.ShapeDtypeStruct((N_PROMPT, N_KV * HEAD_DIM), F32)
    return pl.pallas_call(
        _attn_prompt_kernel,
        out_shape=(jax.ShapeDtypeStruct((N_TOK, GROUP_W), F32), kv_shape, kv_shape),
        grid=(BATCH, N_KV),
        in_specs=[
            pl.BlockSpec((t, GQA_GROUP * HEAD_DIM), lambda b, h: (b, COL_Q // 2 + h)),
            pl.BlockSpec((t, HEAD_DIM), lambda b, h: (b, COL_K + h)),
            pl.BlockSpec((t, HEAD_DIM), lambda b, h: (b, COL_V + h)),
            pl.BlockSpec((1, HEAD_DIM), lambda b, h: (0, 0)),
            pl.BlockSpec((1, HEAD_DIM), lambda b, h: (0, 0)),
        ],
        out_specs=(
            pl.BlockSpec((t, GQA_GROUP * HEAD_DIM), lambda b, h: (b, h)),
            pl.BlockSpec((t, HEAD_DIM), lambda b, h: (b, h)),
            pl.BlockSpec((t, HEAD_DIM), lambda b, h: (b, h)),
        ),
        compiler_params=_params("arbitrary", "arbitrary"),
        name="attn_prompt",
    )(z, z, z, q_g.reshape(1, HEAD_DIM), k_g.reshape(1, HEAD_DIM))


def _rope(x, cosf, sinf):
    return x * cosf + pltpu.roll(x, HEAD_DIM // 2, axis=1) * sinf


def _attn_sample_kernel(q_ref, k_ref, v_ref, kctx_ref, vctx_ref, qg_ref, kg_ref, cos_ref, sin_ref,
                        prev_ref, o_ref, kall_ref, vall_ref, *, qb):
    del prev_ref
    cosf, sinf = cos_ref[...], sin_ref[...]
    kall_ref[0:PAST_LEN, :] = kctx_ref[0].astype(BF16)
    vall_ref[0:PAST_LEN, :] = vctx_ref[0].astype(BF16)
    kall_ref[PAST_LEN:, :] = _rope(_rms(k_ref[...], kg_ref[...]), cosf, sinf).astype(BF16)
    vall_ref[PAST_LEN:, :] = v_ref[...].astype(BF16)
    kb, vb = kall_ref[...], vall_ref[...]
    for i in range(DEC_SEQ // qb):
        rows = slice(i * qb, (i + 1) * qb)
        for g in range(GQA_GROUP):
            cols = slice(g * HEAD_DIM, (g + 1) * HEAD_DIM)
            qn = _rope(_rms(q_ref[rows, cols], qg_ref[...]), cosf[rows], sinf[rows])
            o_ref[rows, cols] = _softmax_pv(qn.astype(BF16), kb, vb)


def _rope_tables():
    rows = DEC_SEQ // GRID_W
    row = np.repeat(np.arange(rows, dtype=np.float32), GRID_W)
    col = np.tile(np.arange(GRID_W, dtype=np.float32), rows)
    n_freq = HEAD_DIM // 4
    inv = (np.float32(ROPE_THETA) ** (-np.arange(n_freq, dtype=np.float32) / n_freq)).astype(np.float32)
    ang = np.concatenate([row[:, None] * inv, col[:, None] * inv], axis=-1).astype(np.float32)
    cos, sin = np.cos(ang.astype(np.float64)), np.sin(ang.astype(np.float64))
    cosf = np.concatenate([cos, cos], axis=-1).astype(np.float32)
    sinf = np.concatenate([-sin, sin], axis=-1).astype(np.float32)
    return jnp.asarray(cosf), jnp.asarray(sinf)


def _attn_sample(z, q_g, k_g, k_ctx, v_ctx, prev):
    t = DEC_SEQ
    rb = N_PROMPT // t
    cosf, sinf = _rope_tables()
    const = lambda b, h: (0, 0)
    return pl.pallas_call(
        functools.partial(_attn_sample_kernel, qb=256),
        out_shape=jax.ShapeDtypeStruct((N_TOK, GROUP_W), F32),
        grid=(DEC_BATCH, N_KV),
        in_specs=[
            pl.BlockSpec((t, GQA_GROUP * HEAD_DIM), lambda b, h: (rb + b, COL_Q // 2 + h)),
            pl.BlockSpec((t, HEAD_DIM), lambda b, h: (rb + b, COL_K + h)),
            pl.BlockSpec((t, HEAD_DIM), lambda b, h: (rb + b, COL_V + h)),
            pl.BlockSpec((1, PAST_LEN, HEAD_DIM), lambda b, h: (b, 0, h)),
            pl.BlockSpec((1, PAST_LEN, HEAD_DIM), lambda b, h: (b, 0, h)),
            pl.BlockSpec((1, HEAD_DIM), const),
            pl.BlockSpec((1, HEAD_DIM), const),
            pl.BlockSpec((t, HEAD_DIM), const),
            pl.BlockSpec((t, HEAD_DIM), const),
            pl.BlockSpec(memory_space=pl.ANY),
        ],
        out_specs=pl.BlockSpec((t, GQA_GROUP * HEAD_DIM), lambda b, h: (rb + b, h)),
        scratch_shapes=[pltpu.VMEM((PAST_LEN + t, HEAD_DIM), BF16), pltpu.VMEM((PAST_LEN + t, HEAD_DIM), BF16)],
        input_output_aliases={9: 0},
        compiler_params=_params("arbitrary", "arbitrary"),
        name="attn_sample",
    )(z, z, z, k_ctx, v_ctx, q_g.reshape(1, HEAD_DIM), k_g.reshape(1, HEAD_DIM), cosf, sinf, prev)


def _hgrn_chain(rev, r0, gq_ref, gi_ref, k_scr, cum_scr, st_scr, o_scr):
    c = HG_CHUNK
    d = 1 if rev else 0
    rows = pl.ds(r0, c)
    qc, vc = gq_ref[rows, :], gi_ref[rows, :]
    kc, cc = k_scr[d, rows, :], cum_scr[d, rows, :]
    st = st_scr[d]
    last = cc[0:1] if rev else cc[c - 1:c]
    qs = (qc * jnp.exp(cc)).astype(BF16)
    o_inter = lax.dot_general(qs, st.astype(BF16), NT_DIMS, preferred_element_type=F32)
    ks = (kc * jnp.exp(last - cc)).astype(BF16)
    upd = lax.dot_general(vc.astype(BF16), ks, TN_DIMS, preferred_element_type=F32)
    st_scr[d] = st * jnp.exp(last) + upd
    row = lax.broadcasted_iota(jnp.int32, (c, 1), 0)
    acc = jnp.zeros((c, HG_DV), F32)
    for delta in range(c):
        if delta == 0:
            k_sh, c_sh, v_sh = kc, cc, vc
        else:
            sh = (c - delta) if rev else delta
            k_sh, c_sh, v_sh = (pltpu.roll(a, sh, axis=0) for a in (kc, cc, vc))
        a = jnp.sum(qc * k_sh * jnp.exp(jnp.minimum(cc - c_sh, 0.0)), axis=-1, keepdims=True)
        valid = (row <= c - 1 - delta) if rev else (row >= delta)
        acc = acc + jnp.where(valid, a, 0.0) * v_sh
    o_scr[d, rows, :] = o_inter + acc


def _hgrn_kernel(*refs, layer, t, has_ctx):
    gq_ref, gff_ref, gfb_ref, gi_ref, go_ref, lb_ref, on_ref, bdf_ref, bdb_ref = refs[:9]
    n_in = 9
    s0_ref = None
    if has_ctx:
        s0_ref = refs[9]
        n_in = 11
    o_ref, sout_ref = refs[n_in], refs[n_in + 1]
    k_scr, cum_scr, st_scr, o_scr = refs[n_in + 2:]

    raw = lb_ref[...]
    e = jnp.exp(raw - jnp.max(raw, axis=0, keepdims=True))
    soft = e / jnp.sum(e, axis=0, keepdims=True)
    csum = soft[0]
    for i in range(1, layer + 1):
        csum = csum + soft[i]
    lb = csum - soft[0]

    for d, (g_ref, bd_ref) in enumerate(((gff_ref, bdf_ref), (gfb_ref, bdb_ref))):
        lbd = lb[d:d + 1]
        f = jnp.maximum(lbd + (1.0 - lbd) * jax.nn.sigmoid(g_ref[...]), F_FLOOR)
        k_scr[d] = 1.0 - f
        lf = jnp.log(f)
        bd = bd_ref[...]
        for p in range(t // HG_CUM_BLOCK):
            rows = slice(p * HG_CUM_BLOCK, (p + 1) * HG_CUM_BLOCK)
            h, m, l = _split3(lf[rows])
            cum_scr[d, rows, :] = _dot(bd, h) + _dot(bd, m) + _dot(bd, l)
        if has_ctx:
            st_scr[d] = s0_ref[0, d, 0].T
        else:
            st_scr[d] = jnp.zeros((HG_DV, HG_DK), F32)

    n = t // HG_CHUNK

    def body(i, carry):
        _hgrn_chain(False, pl.multiple_of(i * HG_CHUNK, HG_CHUNK), gq_ref, gi_ref, k_scr, cum_scr, st_scr, o_scr)
        _hgrn_chain(True, pl.multiple_of((n - 1 - i) * HG_CHUNK, HG_CHUNK), gq_ref, gi_ref, k_scr, cum_scr,
                    st_scr, o_scr)
        return carry

    lax.fori_loop(0, n, body, 0)

    o = _rms(o_scr[0] + o_scr[1], on_ref[...])
    go = go_ref[...]
    o_ref[...] = o * (go * jax.nn.sigmoid(go))
    for d in range(2):
        sout_ref[0, d, 0] = st_scr[d].T


def _cum_consts():
    r = np.arange(HG_CUM_BLOCK)
    same = (r[:, None] // HG_CHUNK) == (r[None, :] // HG_CHUNK)
    fwd = same & (r[None, :] <= r[:, None])
    bwd = same & (r[None, :] >= r[:, None])
    return jnp.asarray(fwd, BF16), jnp.asarray(bwd, BF16)


def _hgrn(z, hg_lb, onorm_g, layer, t, nb, row0, s0, prev):
    rb = row0 // t
    bdf, bdb = _cum_consts()
    has_ctx = s0 is not None
    col = lambda c0: (lambda b, h: (rb + b, c0 + h))
    in_specs = [
        pl.BlockSpec((t, HG_DK), col(COL_GQ)),
        pl.BlockSpec((t, HG_DK), col(COL_GFF)),
        pl.BlockSpec((t, HG_DK), col(COL_GFB)),
        pl.BlockSpec((t, HG_DV), col(COL_GI)),
        pl.BlockSpec((t, HG_DV), col(COL_GO)),
        pl.BlockSpec((DEPTH, 2, HG_DK), lambda b, h: (0, 0, h)),
        pl.BlockSpec((1, HG_DV), lambda b, h: (0, 0)),
        pl.BlockSpec((HG_CUM_BLOCK, HG_CUM_BLOCK), lambda b, h: (0, 0)),
        pl.BlockSpec((HG_CUM_BLOCK, HG_CUM_BLOCK), lambda b, h: (0, 0)),
    ]
    args = [z, z, z, z, z, hg_lb, onorm_g.reshape(1, HG_DV), bdf, bdb]
    aliases = {}
    if has_ctx:
        in_specs += [pl.BlockSpec((1, 2, 1, HG_DK, HG_DV), lambda b, h: (b, 0, h, 0, 0)),
                     pl.BlockSpec(memory_space=pl.ANY)]
        args += [s0, prev]
        aliases = {10: 0}
    return pl.pallas_call(
        functools.partial(_hgrn_kernel, layer=layer, t=t, has_ctx=has_ctx),
        out_shape=(jax.ShapeDtypeStruct((N_TOK, GROUP_W), F32),
                   jax.ShapeDtypeStruct((nb, 2, HG_HEADS, HG_DK, HG_DV), F32)),
        grid=(nb, HG_HEADS),
        in_specs=in_specs,
        out_specs=(pl.BlockSpec((t, HG_DV), lambda b, h: (rb + b, h)),
                   pl.BlockSpec((1, 2, 1, HG_DK, HG_DV), lambda b, h: (b, 0, h, 0, 0))),
        scratch_shapes=[pltpu.VMEM((2, t, HG_DK), F32), pltpu.VMEM((2, t, HG_DK), F32),
                        pltpu.VMEM((2, HG_DV, HG_DK), F32), pltpu.VMEM((2, t, HG_DV), F32)],
        input_output_aliases=aliases,
        compiler_params=_params("arbitrary", "arbitrary"),
        name=f"hgrn_t{t}",
    )(*args)


def _conv_kernel(*refs, t):
    cb_ref, cc_ref, cx_ref, w_ref, b_ref = refs[:5]
    o_ref = refs[-1]
    p = cc_ref[...] * cx_ref[...]
    row = lax.broadcasted_iota(jnp.int32, (t, 1), 0)
    prev = jnp.where(row >= 1, pltpu.roll(p, 1, axis=0), 0.0)
    nxt = jnp.where(row <= t - 2, pltpu.roll(p, t - 1, axis=0), 0.0)
    w = w_ref[...]
    y = prev * w[0:1] + p * w[1:2] + nxt * w[2:3] + b_ref[...]
    o_ref[...] = cb_ref[...] * y


def _conv(z, w, b, t, nb, row0, prev):
    rb = row0 // t
    in_specs = [
        pl.BlockSpec((t, GROUP_W), lambda i: (rb + i, COL_CB // 4)),
        pl.BlockSpec((t, GROUP_W), lambda i: (rb + i, COL_CC // 4)),
        pl.BlockSpec((t, GROUP_W), lambda i: (rb + i, COL_CX // 4)),
        pl.BlockSpec((CONV_K, GROUP_W), lambda i: (0, 0)),
        pl.BlockSpec((1, GROUP_W), lambda i: (0, 0)),
    ]
    args = [z, z, z, w, b.reshape(1, GROUP_W)]
    aliases = {}
    if prev is not None:
        in_specs.append(pl.BlockSpec(memory_space=pl.ANY))
        args.append(prev)
        aliases = {5: 0}
    return pl.pallas_call(
        functools.partial(_conv_kernel, t=t),
        out_shape=jax.ShapeDtypeStruct((N_TOK, GROUP_W), F32),
        grid=(nb,),
        in_specs=in_specs,
        out_specs=pl.BlockSpec((t, GROUP_W), lambda i: (rb + i, 0)),
        input_output_aliases=aliases,
        compiler_params=_params("arbitrary"),
        name=f"conv_t{t}",
    )(*args)


def _outproj_kernel(fn_ref, at_ref, hg_ref, cv_ref, w_ref, x_ref, g1_ref, sh_ref, sc_ref, n2_ref,
                    x1_ref, h2_ref, *, tm):
    r = _mod_row(pl.program_id(0) * tm)
    acc = None
    for g, ref in enumerate((fn_ref, at_ref, hg_ref, cv_ref)):
        part = _dot(ref[...].astype(BF16), w_ref[g * GROUP_W:(g + 1) * GROUP_W, :])
        acc = part if acc is None else acc + part
    x1 = x_ref[...] + g1_ref[pl.ds(r, 1), :] * acc
    x1_ref[...] = x1
    h2 = _rms(x1, n2_ref[...]) * (1.0 + sc_ref[pl.ds(r, 1), :]) + sh_ref[pl.ds(r, 1), :]
    h2_ref[...] = h2.astype(BF16)


def _outproj(o_fn, o_at, o_hg, o_cv, w_out_bf, x, mod, norm2_g):
    tm = 512
    grp = pl.BlockSpec((tm, GROUP_W), lambda i: (i, 0))
    modspec = lambda k: pl.BlockSpec((MOD_ROWS, D_MODEL), lambda i: (0, k))
    return pl.pallas_call(
        functools.partial(_outproj_kernel, tm=tm),
        out_shape=(jax.ShapeDtypeStruct((N_TOK, D_MODEL), F32), jax.ShapeDtypeStruct((N_TOK, D_MODEL), BF16)),
        grid=(N_TOK // tm,),
        in_specs=[grp, grp, grp, grp,
                  pl.BlockSpec((D_MODEL, D_MODEL), lambda i: (0, 0)),
                  pl.BlockSpec((tm, D_MODEL), lambda i: (i, 0)),
                  modspec(2), modspec(3), modspec(4),
                  pl.BlockSpec((1, D_MODEL), lambda i: (0, 0))],
        out_specs=(pl.BlockSpec((tm, D_MODEL), lambda i: (i, 0)), pl.BlockSpec((tm, D_MODEL), lambda i: (i, 0))),
        compiler_params=_params("arbitrary"),
        name="outproj",
    )(o_fn, o_at, o_hg, o_cv, w_out_bf, x, mod, mod, mod, norm2_g.reshape(1, D_MODEL))


def _peer_query_kernel(h_ref, wq_ref, q_ref):
    q_ref[...] = _dot(h_ref[...], wq_ref[...]).astype(BF16)


def _peer_query(h2, wq_bf):
    tm = 512
    width = PEER_HEADS * PEER_QDIM
    return pl.pallas_call(
        _peer_query_kernel,
        out_shape=jax.ShapeDtypeStruct((N_TOK, width), BF16),
        grid=(N_TOK // tm,),
        in_specs=[pl.BlockSpec((tm, D_MODEL), lambda i: (i, 0)), pl.BlockSpec((D_MODEL, width), lambda i: (0, 0))],
        out_specs=pl.BlockSpec((tm, width), lambda i: (i, 0)),
        compiler_params=_params("arbitrary"),
        name="peer_query",
    )(h2, wq_bf)


_CAND_PIECES = ((0, 0, 8), (0, 8, 8), (1, 0, 8), (2, 0, 5), (3, 0, 4), (4, 0, 3), (5, 0, 2), (6, 0, 2), (7, 0, 2),
                (None, 0, 8))
_CAND_ROWS = 8 * len(_CAND_PIECES)
_NO_CAND = 1 << 20


def _topk_rows(x_scr, ids, n_rows, v_out, i_out):
    def body(j, carry):
        x = x_scr[0:n_rows, :]
        m = jnp.max(x, axis=0, keepdims=True)
        idx = jnp.min(jnp.where(x == m, ids, _NO_CAND), axis=0, keepdims=True)
        v_out[pl.ds(j, 1), :] = m
        i_out[pl.ds(j, 1), :] = idx
        x_scr[0:n_rows, :] = jnp.where(ids == idx, -jnp.inf, x)
        return carry

    lax.fori_loop(0, PEER_TOPK, body, 0)


def _peer_topk_kernel(q_ref, k1_ref, k2_ref, e_ref, g_ref, x_scr, v1_scr, i1_scr, v2_scr, i2_scr,
                      ts_scr, tc_scr, *, tm):
    key_ids = lax.broadcasted_iota(jnp.int32, (N_KEYS, tm), 0)
    for half, (k_ref, v_scr, i_scr) in enumerate(((k1_ref, v1_scr, i1_scr), (k2_ref, v2_scr, i2_scr))):
        qh = q_ref[:, half * PEER_HALF:(half + 1) * PEER_HALF]
        x_scr[...] = lax.dot_general(k_ref[...], qh, NT_DIMS, preferred_element_type=F32)
        _topk_rows(x_scr, key_ids, N_KEYS, v_scr, i_scr)
    v1, v2 = v1_scr[...], v2_scr[...]
    sub = lax.broadcasted_iota(jnp.int32, (8, tm), 0)
    cand_ids = []
    for p, (r1, r2, n_valid) in enumerate(_CAND_PIECES):
        if r1 is None:
            vals = v1[8:16] + v2[0:1]
            ids = (sub + 8) * PEER_TOPK
        else:
            vals = v1[r1:r1 + 1] + v2[r2:r2 + 8]
            ids = sub + (r1 * PEER_TOPK + r2)
            if n_valid < 8:
                vals = jnp.where(sub < n_valid, vals, -jnp.inf)
                ids = jnp.where(sub < n_valid, ids, _NO_CAND)
        x_scr[p * 8:(p + 1) * 8, :] = vals
        cand_ids.append(ids)
    _topk_rows(x_scr, jnp.concatenate(cand_ids, axis=0), _CAND_ROWS, ts_scr, tc_scr)
    tc = tc_scr[...]
    c1, c2 = tc // PEER_TOPK, tc % PEER_TOPK
    i1, i2 = i1_scr[...], i2_scr[...]
    e1 = jnp.zeros_like(tc)
    e2 = jnp.zeros_like(tc)
    for a in range(PEER_TOPK):
        e1 = jnp.where(c1 == a, i1[a:a + 1], e1)
        e2 = jnp.where(c2 == a, i2[a:a + 1], e2)
    e_ref[...] = e1 * N_KEYS + e2
    ts = ts_scr[...]
    p = jnp.exp(ts - jnp.max(ts, axis=0, keepdims=True))
    g_ref[...] = p / jnp.sum(p, axis=0, keepdims=True)


def _peer_topk(q, k1_bf, k2_bf):
    tm = 256
    row_f = pltpu.VMEM((PEER_TOPK, tm), F32)
    row_i = pltpu.VMEM((PEER_TOPK, tm), jnp.int32)
    return pl.pallas_call(
        functools.partial(_peer_topk_kernel, tm=tm),
        out_shape=(jax.ShapeDtypeStruct((PEER_HEADS * PEER_TOPK, N_TOK), jnp.int32),
                   jax.ShapeDtypeStruct((PEER_HEADS * PEER_TOPK, N_TOK), F32)),
        grid=(N_TOK // tm, PEER_HEADS),
        in_specs=[
            pl.BlockSpec((tm, PEER_QDIM), lambda i, h: (i, h)),
            pl.BlockSpec((N_KEYS, PEER_HALF), lambda i, h: (0, 0)),
            pl.BlockSpec((N_KEYS, PEER_HALF), lambda i, h: (0, 0)),
        ],
        out_specs=(pl.BlockSpec((PEER_TOPK, tm), lambda i, h: (h, i)),
                   pl.BlockSpec((PEER_TOPK, tm), lambda i, h: (h, i))),
        scratch_shapes=[pltpu.VMEM((N_KEYS, tm), F32), row_f, row_i, row_f, row_i, row_f, row_i],
        compiler_params=_params("arbitrary", "arbitrary"),
        name="peer_topk",
    )(q, k1_bf, k2_bf)


def _peer_mask_kernel(e_ref, g_ref, m_ref, et_scr, gt_scr, slab_scr, *, tb):
    et_scr[...] = e_ref[...].T
    gt_scr[...] = g_ref[...].T
    key = lax.broadcasted_iota(jnp.int32, (N_KEYS, PEER_HEADS * PEER_TOPK), 0)

    def body(t, carry):
        er = et_scr[pl.ds(t, 1), :]
        gr = gt_scr[pl.ds(t, 1), :]
        p1 = jnp.where(key == er // N_KEYS, gr, 0.0).astype(BF16)
        p2 = jnp.where(key == er % N_KEYS, 1.0, 0.0).astype(BF16)
        mt = lax.dot_general(p1, p2, NT_DIMS, preferred_element_type=F32)
        slab_scr[pl.ds(pl.multiple_of(t * MASK_PITCH, 8), N_KEYS), :] = mt
        return carry

    lax.fori_loop(0, tb, body, 0, unroll=8)
    for k1 in range(N_KEYS):
        m_ref[:, k1 * N_KEYS:(k1 + 1) * N_KEYS] = slab_scr[pl.ds(k1, tb, stride=MASK_PITCH), :].astype(BF16)


def _peer_mask(experts_t, gates_t):
    tb = LANES
    pairs = PEER_HEADS * PEER_TOPK
    return pl.pallas_call(
        functools.partial(_peer_mask_kernel, tb=tb),
        out_shape=jax.ShapeDtypeStruct((N_TOK, N_EXPERTS), BF16),
        grid=(N_TOK // tb,),
        in_specs=[pl.BlockSpec((pairs, tb), lambda i: (0, i)), pl.BlockSpec((pairs, tb), lambda i: (0, i))],
        out_specs=pl.BlockSpec((tb, N_EXPERTS), lambda i: (i, 0)),
        scratch_shapes=[pltpu.VMEM((tb, pairs), jnp.int32), pltpu.VMEM((tb, pairs), F32),
                        pltpu.VMEM((tb * MASK_PITCH, N_KEYS), F32)],
        compiler_params=_params("arbitrary"),
        name="peer_mask",
    )(experts_t, gates_t)


def _peer_expert_kernel(h_ref, ut_ref, v_ref, m_ref, x_ref, g2_ref, o_ref, acc_ref, a_ref, *, tm):
    j = pl.program_id(1)
    n = pl.num_programs(1)

    @pl.when(j == 0)
    def _():
        acc_ref[...] = jnp.zeros_like(acc_ref)
        a_ref[1] = jnp.zeros(a_ref.shape[1:], BF16)

    for parity in range(2):
        @pl.when(j % 2 == parity)
        def _():
            acc_ref[...] += _dot(a_ref[1 - parity], v_ref[...])
            s = _dot(h_ref[...], ut_ref[...])
            a_ref[parity] = (m_ref[...].astype(F32) * jax.nn.gelu(s)).astype(BF16)

    @pl.when(j == n - 1)
    def _():
        r = _mod_row(pl.program_id(0) * tm)
        o_ref[...] = x_ref[...] + g2_ref[pl.ds(r, 1), :] * acc_ref[...]


def _peer_expert(h2, ut_bf, v_bf, mask, x1, mod):
    tm, te = 512, 512
    nb = N_EXPERTS // te
    cur = lambda j: jnp.minimum(j, nb - 1)
    return pl.pallas_call(
        functools.partial(_peer_expert_kernel, tm=tm),
        out_shape=jax.ShapeDtypeStruct((N_TOK, D_MODEL), F32),
        grid=(N_TOK // tm, nb + 1),
        in_specs=[
            pl.BlockSpec((tm, D_MODEL), lambda i, j: (i, 0)),
            pl.BlockSpec((D_MODEL, te), lambda i, j: (0, cur(j))),
            pl.BlockSpec((te, D_MODEL), lambda i, j: (jnp.maximum(j - 1, 0), 0)),
            pl.BlockSpec((tm, te), lambda i, j: (i, cur(j))),
            pl.BlockSpec((tm, D_MODEL), lambda i, j: (i, 0)),
            pl.BlockSpec((MOD_ROWS, D_MODEL), lambda i, j: (0, 5)),
        ],
        out_specs=pl.BlockSpec((tm, D_MODEL), lambda i, j: (i, 0)),
        scratch_shapes=[pltpu.VMEM((tm, D_MODEL), F32), pltpu.VMEM((2, tm, te), BF16)],
        compiler_params=_params("arbitrary", "arbitrary"),
        name="peer_expert",
    )(h2, ut_bf, v_bf, mask, x1, mod)


def kernel(x_prompt, x_sample, cache_k, cache_v, state_hgrn, c, c_ctx, ada_w, ada_b, norm1_g, norm2_g, w_in,
           q_norm_g, k_norm_g, hg_lb, hg_onorm_g, conv_w, conv_b, w_out, peer_wq, peer_k1, peer_k2, peer_u, peer_v):
    x = jnp.concatenate([x_prompt.reshape(N_PROMPT, D_MODEL), x_sample.reshape(N_SAMPLE, D_MODEL)], axis=0)
    cvec = jnp.concatenate([c_ctx[None, :], c, jnp.zeros((MOD_ROWS - 1 - DEC_BATCH, D_MODEL), F32)], axis=0)
    mod_all = _ada(cvec, ada_w, ada_b)

    ks_out, vs_out, ss_out = [], [], []
    for l in range(DEPTH):
        mod = mod_all[l]
        z = _inproj(x, mod, norm1_g[l], w_in[l].astype(BF16))

        o_fn = _fourier(z, SEQ, BATCH, 0, None)
        o_fn = _fourier(z, DEC_SEQ, DEC_BATCH, N_PROMPT, o_fn)

        o_at, k_l, v_l = _attn_prompt(z, q_norm_g[l], k_norm_g[l])
        k_ctx = cache_k[:, l].reshape(DEC_BATCH, PAST_LEN, N_KV * HEAD_DIM)
        v_ctx = cache_v[:, l].reshape(DEC_BATCH, PAST_LEN, N_KV * HEAD_DIM)
        o_at = _attn_sample(z, q_norm_g[l], k_norm_g[l], k_ctx, v_ctx, o_at)

        o_hg, s_l = _hgrn(z, hg_lb, hg_onorm_g[l], l, SEQ, BATCH, 0, None, None)
        o_hg, _ = _hgrn(z, hg_lb, hg_onorm_g[l], l, DEC_SEQ, DEC_BATCH, N_PROMPT, state_hgrn[:, l], o_hg)

        o_cv = _conv(z, conv_w[l], conv_b[l], SEQ, BATCH, 0, None)
        o_cv = _conv(z, conv_w[l], conv_b[l], DEC_SEQ, DEC_BATCH, N_PROMPT, o_cv)

        x1, h2 = _outproj(o_fn, o_at, o_hg, o_cv, w_out[l].astype(BF16), x, mod, norm2_g[l])

        q_peer = _peer_query(h2, peer_wq[l].astype(BF16))
        experts_t, gates_t = _peer_topk(q_peer, peer_k1[l].astype(BF16), peer_k2[l].astype(BF16))
        mask = _peer_mask(experts_t, gates_t)
        x = _peer_expert(h2, peer_u[l].T.astype(BF16), peer_v[l].astype(BF16), mask, x1, mod)

        ks_out.append(k_l.reshape(BATCH, SEQ, N_KV, HEAD_DIM))
        vs_out.append(v_l.reshape(BATCH, SEQ, N_KV, HEAD_DIM))
        ss_out.append(s_l)

    y_prompt = x[:N_PROMPT].reshape(BATCH, SEQ, D_MODEL)
    y_sample = x[N_PROMPT:].reshape(DEC_BATCH, DEC_SEQ, D_MODEL)
    return (y_prompt, y_sample, jnp.stack(ks_out, axis=1), jnp.stack(vs_out, axis=1), jnp.stack(ss_out, axis=1))
```

```python
import functools

import numpy as np
import jax
import jax.numpy as jnp
from jax import lax
from jax.experimental import pallas as pl
from jax.experimental.pallas import tpu as pltpu

F32 = jnp.float32
BF16 = jnp.bfloat16

D_MODEL = 2048
BATCH = 32
SEQ = 256
DEPTH = 2
DEC_BATCH = 4
DEC_SEQ = 1024
PAST_LEN = 256
GRID_W = 64
GROUP_W = D_MODEL // 4
FN_GROUPS = 4
FN_GW = GROUP_W // FN_GROUPS
HEAD_DIM = 128
N_HEADS = GROUP_W // HEAD_DIM
N_KV = 2
GQA_GROUP = N_HEADS // N_KV
ROPE_THETA = 10000.0
HG_DK = 128
HG_DV = 128
HG_HEADS = GROUP_W // HG_DK
CONV_K = 3
IN_COLS = 11 * GROUP_W
PEER_HEADS = 8
N_KEYS = 128
N_EXPERTS = N_KEYS * N_KEYS
PEER_QDIM = 256
PEER_HALF = PEER_QDIM // 2
PEER_TOPK = 16
EPS = 1e-6
F_FLOOR = 1e-30

N_PROMPT = BATCH * SEQ
N_SAMPLE = DEC_BATCH * DEC_SEQ
N_TOK = N_PROMPT + N_SAMPLE
MOD_ROWS = 8
LANES = 128
HG_CHUNK = 32
HG_CUM_BLOCK = 256
MASK_PITCH = 136
VMEM_LIMIT = 56 * 1024 * 1024

COL_FN, COL_Q, COL_K, COL_V = 0, 4, 8, 10
COL_GQ, COL_GFF, COL_GFB, COL_GI, COL_GO = 12, 16, 20, 24, 28
COL_CB, COL_CC, COL_CX = 32, 36, 40

NT_DIMS = (((1,), (1,)), ((), ()))
TN_DIMS = (((0,), (0,)), ((), ()))


def _params(*sem):
    return pltpu.CompilerParams(dimension_semantics=sem, vmem_limit_bytes=VMEM_LIMIT)


def _dot(a, b):
    return jnp.dot(a, b, preferred_element_type=F32)


def _split3(x):
    h = x.astype(BF16)
    r = x - h.astype(F32)
    m = r.astype(BF16)
    l = (r - m.astype(F32)).astype(BF16)
    return h, m, l


def _split2(x):
    h = x.astype(BF16)
    return h, (x - h.astype(F32)).astype(BF16)


def _mod_row(tok0):
    return jnp.where(tok0 >= N_PROMPT, (tok0 - N_PROMPT) // DEC_SEQ + 1, 0)


def _rms(x, g):
    return x * lax.rsqrt(jnp.mean(x * x, axis=-1, keepdims=True) + EPS) * g


def _ada_kernel(c_ref, w_ref, b_ref, o_ref):
    c = c_ref[...]
    s = (c * jax.nn.sigmoid(c)).astype(BF16)
    o_ref[0] = _dot(s, w_ref[0].astype(BF16)) + b_ref[0]


def _ada(cvec, ada_w, ada_b):
    tn = 1024
    return pl.pallas_call(
        _ada_kernel,
        out_shape=jax.ShapeDtypeStruct((DEPTH, MOD_ROWS, 6 * D_MODEL), F32),
        grid=(DEPTH, 6 * D_MODEL // tn),
        in_specs=[
            pl.BlockSpec((MOD_ROWS, D_MODEL), lambda l, j: (0, 0)),
            pl.BlockSpec((1, D_MODEL, tn), lambda l, j: (l, 0, j)),
            pl.BlockSpec((1, 1, tn), lambda l, j: (l, 0, j)),
        ],
        out_specs=pl.BlockSpec((1, MOD_ROWS, tn), lambda l, j: (l, 0, j)),
        compiler_params=_params("arbitrary", "arbitrary"),
        name="ada",
    )(cvec, ada_w, ada_b.reshape(DEPTH, 1, 6 * D_MODEL))


def _inproj_kernel(x_ref, sh_ref, sc_ref, g_ref, w_ref, o_ref, h_ref, *, tm):
    @pl.when(pl.program_id(1) == 0)
    def _():
        r = _mod_row(pl.program_id(0) * tm)
        y = _rms(x_ref[...], g_ref[...])
        h_ref[...] = (y * (1.0 + sc_ref[pl.ds(r, 1), :]) + sh_ref[pl.ds(r, 1), :]).astype(BF16)

    o_ref[...] = _dot(h_ref[...], w_ref[...])


def _inproj(x, mod, norm_g, w_in_bf):
    tm, tn = 1024, 512
    return pl.pallas_call(
        functools.partial(_inproj_kernel, tm=tm),
        out_shape=jax.ShapeDtypeStruct((N_TOK, IN_COLS), F32),
        grid=(N_TOK // tm, IN_COLS // tn),
        in_specs=[
            pl.BlockSpec((tm, D_MODEL), lambda i, j: (i, 0)),
            pl.BlockSpec((MOD_ROWS, D_MODEL), lambda i, j: (0, 0)),
            pl.BlockSpec((MOD_ROWS, D_MODEL), lambda i, j: (0, 1)),
            pl.BlockSpec((1, D_MODEL), lambda i, j: (0, 0)),
            pl.BlockSpec((D_MODEL, tn), lambda i, j: (0, j)),
        ],
        out_specs=pl.BlockSpec((tm, tn), lambda i, j: (i, j)),
        scratch_shapes=[pltpu.VMEM((tm, D_MODEL), BF16)],
        compiler_params=_params("arbitrary", "arbitrary"),
        name="inproj",
    )(x, mod, mod, norm_g.reshape(1, D_MODEL), w_in_bf)


def _fourier_kernel(u_ref, cch_ref, ccl_ref, cth_ref, ctl_ref, o_ref):
    cch, ccl, cth, ctl = cch_ref[...], ccl_ref[...], cth_ref[...], ctl_ref[...]
    for g in range(FN_GROUPS):
        u = u_ref[:, g * FN_GW:(g + 1) * FN_GW]
        uh, um, ul = _split3(u)
        w = _dot(uh, cch) + (_dot(um, cch) + _dot(uh, ccl)) + (_dot(ul, cch) + _dot(um, ccl))
        st = jnp.concatenate([w[:, :FN_GW], w[:, FN_GW:]], axis=0)
        sh, sm, sl = _split3(st)
        y = _dot(cth, sh) + (_dot(cth, sm) + _dot(ctl, sh)) + (_dot(cth, sl) + _dot(ctl, sm))
        o_ref[:, g * FN_GW:(g + 1) * FN_GW] = y


def _dft_consts(t):
    def cs(n):
        k = np.arange(n, dtype=np.float64)
        ang = 2.0 * np.pi * np.outer(k, k) / n
        return np.cos(ang) / np.sqrt(n), np.sin(ang) / np.sqrt(n)

    cc, sc = cs(FN_GW)
    ct, st = cs(t)
    right = np.concatenate([cc, sc], axis=1).astype(np.float32)
    left = np.concatenate([ct, -st], axis=1).astype(np.float32)

    def hl(a):
        a = jnp.asarray(a)
        h = a.astype(BF16)
        return h, (a - h.astype(F32)).astype(BF16)

    return hl(right) + hl(left)


def _fourier(z, t, nb, row0):
    cch, ccl, cth, ctl = _dft_consts(t)
    rb = row0 // t
    return pl.pallas_call(
        _fourier_kernel,
        out_shape=jax.ShapeDtypeStruct((nb * t, GROUP_W), F32),
        grid=(nb,),
        in_specs=[
            pl.BlockSpec((t, GROUP_W), lambda b: (rb + b, COL_FN // 4)),
            pl.BlockSpec((FN_GW, 2 * FN_GW), lambda b: (0, 0)),
            pl.BlockSpec((FN_GW, 2 * FN_GW), lambda b: (0, 0)),
            pl.BlockSpec((t, 2 * t), lambda b: (0, 0)),
            pl.BlockSpec((t, 2 * t), lambda b: (0, 0)),
        ],
        out_specs=pl.BlockSpec((t, GROUP_W), lambda b: (b, 0)),
        compiler_params=_params("arbitrary"),
        name=f"fourier_t{t}",
    )(z, cch, ccl, cth, ctl)


def _softmax_pv(q_bf, k_bf, v_bf):
    s = lax.dot_general(q_bf, k_bf, NT_DIMS, preferred_element_type=F32) * (HEAD_DIM ** -0.5)
    p = jnp.exp(s - jnp.max(s, axis=-1, keepdims=True))
    return _dot(p.astype(BF16), v_bf) / jnp.sum(p, axis=-1, keepdims=True)


def _attn_prompt_kernel(q_ref, k_ref, v_ref, qg_ref, kg_ref, o_ref, kc_ref, vc_ref):
    kn = _rms(k_ref[...], kg_ref[...])
    v = v_ref[...]
    kc_ref[...] = kn
    vc_ref[...] = v
    kb, vb = kn.astype(BF16), v.astype(BF16)
    for g in range(GQA_GROUP):
        qn = _rms(q_ref[:, g * HEAD_DIM:(g + 1) * HEAD_DIM], qg_ref[...])
        o_ref[:, g * HEAD_DIM:(g + 1) * HEAD_DIM] = _softmax_pv(qn.astype(BF16), kb, vb)


def _attn_prompt(z, q_g, k_g):
    t = SEQ
    kv_shape = jax.ShapeDtypeStruct((N_PROMPT, N_KV * HEAD_DIM), F32)
    return pl.pallas_call(
        _attn_prompt_kernel,
        out_shape=(jax.ShapeDtypeStruct((N_PROMPT, GROUP_W), F32), kv_shape, kv_shape),
        grid=(BATCH, N_KV),
        in_specs=[
            pl.BlockSpec((t, GQA_GROUP * HEAD_DIM), lambda b, h: (b, COL_Q // 2 + h)),
            pl.BlockSpec((t, HEAD_DIM), lambda b, h: (b, COL_K + h)),
            pl.BlockSpec((t, HEAD_DIM), lambda b, h: (b, COL_V + h)),
            pl.BlockSpec((1, HEAD_DIM), lambda b, h: (0, 0)),
            pl.BlockSpec((1, HEAD_DIM), lambda b, h: (0, 0)),
        ],
        out_specs=(
            pl.BlockSpec((t, GQA_GROUP * HEAD_DIM), lambda b, h: (b, h)),
            pl.BlockSpec((t, HEAD_DIM), lambda b, h: (b, h)),
            pl.BlockSpec((t, HEAD_DIM), lambda b, h: (b, h)),
        ),
        compiler_params=_params("arbitrary", "arbitrary"),
        name="attn_prompt",
    )(z, z, z, q_g.reshape(1, HEAD_DIM), k_g.reshape(1, HEAD_DIM))


def _rope(x, cosf, sinf):
    return x * cosf + pltpu.roll(x, HEAD_DIM // 2, axis=1) * sinf


def _attn_sample_kernel(q_ref, k_ref, v_ref, kctx_ref, vctx_ref, qg_ref, kg_ref, cos_ref, sin_ref,
                        o_ref, kall_ref, vall_ref, *, qb):
    cosf, sinf = cos_ref[...], sin_ref[...]
    kall_ref[0:PAST_LEN, :] = kctx_ref[0].astype(BF16)
    vall_ref[0:PAST_LEN, :] = vctx_ref[0].astype(BF16)
    kall_ref[PAST_LEN:, :] = _rope(_rms(k_ref[...], kg_ref[...]), cosf, sinf).astype(BF16)
    vall_ref[PAST_LEN:, :] = v_ref[...].astype(BF16)
    kb, vb = kall_ref[...], vall_ref[...]
    for i in range(DEC_SEQ // qb):
        rows = slice(i * qb, (i + 1) * qb)
        for g in range(GQA_GROUP):
            cols = slice(g * HEAD_DIM, (g + 1) * HEAD_DIM)
            qn = _rope(_rms(q_ref[rows, cols], qg_ref[...]), cosf[rows], sinf[rows])
            o_ref[rows, cols] = _softmax_pv(qn.astype(BF16), kb, vb)


def _rope_tables():
    rows = DEC_SEQ // GRID_W
    row = np.repeat(np.arange(rows, dtype=np.float32), GRID_W)
    col = np.tile(np.arange(GRID_W, dtype=np.float32), rows)
    n_freq = HEAD_DIM // 4
    inv = (np.float32(ROPE_THETA) ** (-np.arange(n_freq, dtype=np.float32) / n_freq)).astype(np.float32)
    ang = np.concatenate([row[:, None] * inv, col[:, None] * inv], axis=-1).astype(np.float32)
    cos, sin = np.cos(ang.astype(np.float64)), np.sin(ang.astype(np.float64))
    cosf = np.concatenate([cos, cos], axis=-1).astype(np.float32)
    sinf = np.concatenate([-sin, sin], axis=-1).astype(np.float32)
    return jnp.asarray(cosf), jnp.asarray(sinf)


def _attn_sample(z, q_g, k_g, k_ctx, v_ctx):
    t = DEC_SEQ
    rb = N_PROMPT // t
    cosf, sinf = _rope_tables()
    const = lambda b, h: (0, 0)
    return pl.pallas_call(
        functools.partial(_attn_sample_kernel, qb=256),
        out_shape=jax.ShapeDtypeStruct((N_SAMPLE, GROUP_W), F32),
        grid=(DEC_BATCH, N_KV),
        in_specs=[
            pl.BlockSpec((t, GQA_GROUP * HEAD_DIM), lambda b, h: (rb + b, COL_Q // 2 + h)),
            pl.BlockSpec((t, HEAD_DIM), lambda b, h: (rb + b, COL_K + h)),
            pl.BlockSpec((t, HEAD_DIM), lambda b, h: (rb + b, COL_V + h)),
            pl.BlockSpec((1, PAST_LEN, HEAD_DIM), lambda b, h: (b, 0, h)),
            pl.BlockSpec((1, PAST_LEN, HEAD_DIM), lambda b, h: (b, 0, h)),
            pl.BlockSpec((1, HEAD_DIM), const),
            pl.BlockSpec((1, HEAD_DIM), const),
            pl.BlockSpec((t, HEAD_DIM), const),
            pl.BlockSpec((t, HEAD_DIM), const),
        ],
        out_specs=pl.BlockSpec((t, GQA_GROUP * HEAD_DIM), lambda b, h: (b, h)),
        scratch_shapes=[pltpu.VMEM((PAST_LEN + t, HEAD_DIM), BF16), pltpu.VMEM((PAST_LEN + t, HEAD_DIM), BF16)],
        compiler_params=_params("arbitrary", "arbitrary"),
        name="attn_sample",
    )(z, z, z, k_ctx, v_ctx, q_g.reshape(1, HEAD_DIM), k_g.reshape(1, HEAD_DIM), cosf, sinf)


def _hgrn_chain(rev, r0, gq_ref, gi_ref, k_scr, cum_scr, st_scr, o_scr):
    c = HG_CHUNK
    d = 1 if rev else 0
    rows = pl.ds(r0, c)
    qc, vc = gq_ref[rows, :], gi_ref[rows, :]
    kc, cc = k_scr[d, rows, :], cum_scr[d, rows, :]
    st = st_scr[d]
    last = cc[0:1] if rev else cc[c - 1:c]
    qs = (qc * jnp.exp(cc)).astype(BF16)
    o_inter = lax.dot_general(qs, st.astype(BF16), NT_DIMS, preferred_element_type=F32)
    ks = (kc * jnp.exp(last - cc)).astype(BF16)
    upd = lax.dot_general(vc.astype(BF16), ks, TN_DIMS, preferred_element_type=F32)
    st_scr[d] = st * jnp.exp(last) + upd
    row = lax.broadcasted_iota(jnp.int32, (c, 1), 0)
    acc = jnp.zeros((c, HG_DV), F32)
    for delta in range(c):
        if delta == 0:
            k_sh, c_sh, v_sh = kc, cc, vc
        else:
            sh = (c - delta) if rev else delta
            k_sh, c_sh, v_sh = (pltpu.roll(a, sh, axis=0) for a in (kc, cc, vc))
        a = jnp.sum(qc * k_sh * jnp.exp(jnp.minimum(cc - c_sh, 0.0)), axis=-1, keepdims=True)
        valid = (row <= c - 1 - delta) if rev else (row >= delta)
        acc = acc + jnp.where(valid, a, 0.0) * v_sh
    o_scr[d, rows, :] = o_inter + acc


def _hgrn_kernel(*refs, layer, t, has_ctx):
    gq_ref, gff_ref, gfb_ref, gi_ref, go_ref, lb_ref, on_ref, bdf_ref, bdb_ref = refs[:9]
    n_in = 9
    s0_ref = None
    if has_ctx:
        s0_ref = refs[9]
        n_in = 10
    o_ref, sout_ref = refs[n_in], refs[n_in + 1]
    k_scr, cum_scr, st_scr, o_scr = refs[n_in + 2:]

    raw = lb_ref[...]
    e = jnp.exp(raw - jnp.max(raw, axis=0, keepdims=True))
    soft = e / jnp.sum(e, axis=0, keepdims=True)
    csum = soft[0]
    for i in range(1, layer + 1):
        csum = csum + soft[i]
    lb = csum - soft[0]

    for d, (g_ref, bd_ref) in enumerate(((gff_ref, bdf_ref), (gfb_ref, bdb_ref))):
        lbd = lb[d:d + 1]
        f = jnp.maximum(lbd + (1.0 - lbd) * jax.nn.sigmoid(g_ref[...]), F_FLOOR)
        k_scr[d] = 1.0 - f
        lf = jnp.log(f)
        bd = bd_ref[...]
        for p in range(t // HG_CUM_BLOCK):
            rows = slice(p * HG_CUM_BLOCK, (p + 1) * HG_CUM_BLOCK)
            h, m, l = _split3(lf[rows])
            cum_scr[d, rows, :] = _dot(bd, h) + _dot(bd, m) + _dot(bd, l)
        if has_ctx:
            st_scr[d] = s0_ref[0, d, 0].T
        else:
            st_scr[d] = jnp.zeros((HG_DV, HG_DK), F32)

    n = t // HG_CHUNK

    def body(i, carry):
        _hgrn_chain(False, pl.multiple_of(i * HG_CHUNK, HG_CHUNK), gq_ref, gi_ref, k_scr, cum_scr, st_scr, o_scr)
        _hgrn_chain(True, pl.multiple_of((n - 1 - i) * HG_CHUNK, HG_CHUNK), gq_ref, gi_ref, k_scr, cum_scr,
                    st_scr, o_scr)
        return carry

    lax.fori_loop(0, n, body, 0)

    o = _rms(o_scr[0] + o_scr[1], on_ref[...])
    go = go_ref[...]
    o_ref[...] = o * (go * jax.nn.sigmoid(go))
    for d in range(2):
        sout_ref[0, d, 0] = st_scr[d].T


def _cum_consts():
    r = np.arange(HG_CUM_BLOCK)
    same = (r[:, None] // HG_CHUNK) == (r[None, :] // HG_CHUNK)
    fwd = same & (r[None, :] <= r[:, None])
    bwd = same & (r[None, :] >= r[:, None])
    return jnp.asarray(fwd, BF16), jnp.asarray(bwd, BF16)


def _hgrn(z, hg_lb, onorm_g, layer, t, nb, row0, s0):
    rb = row0 // t
    bdf, bdb = _cum_consts()
    has_ctx = s0 is not None
    col = lambda c0: (lambda b, h: (rb + b, c0 + h))
    in_specs = [
        pl.BlockSpec((t, HG_DK), col(COL_GQ)),
        pl.BlockSpec((t, HG_DK), col(COL_GFF)),
        pl.BlockSpec((t, HG_DK), col(COL_GFB)),
        pl.BlockSpec((t, HG_DV), col(COL_GI)),
        pl.BlockSpec((t, HG_DV), col(COL_GO)),
        pl.BlockSpec((DEPTH, 2, HG_DK), lambda b, h: (0, 0, h)),
        pl.BlockSpec((1, HG_DV), lambda b, h: (0, 0)),
        pl.BlockSpec((HG_CUM_BLOCK, HG_CUM_BLOCK), lambda b, h: (0, 0)),
        pl.BlockSpec((HG_CUM_BLOCK, HG_CUM_BLOCK), lambda b, h: (0, 0)),
    ]
    args = [z, z, z, z, z, hg_lb, onorm_g.reshape(1, HG_DV), bdf, bdb]
    if has_ctx:
        in_specs.append(pl.BlockSpec((1, 2, 1, HG_DK, HG_DV), lambda b, h: (b, 0, h, 0, 0)))
        args.append(s0)
    return pl.pallas_call(
        functools.partial(_hgrn_kernel, layer=layer, t=t, has_ctx=has_ctx),
        out_shape=(jax.ShapeDtypeStruct((nb * t, GROUP_W), F32),
                   jax.ShapeDtypeStruct((nb, 2, HG_HEADS, HG_DK, HG_DV), F32)),
        grid=(nb, HG_HEADS),
        in_specs=in_specs,
        out_specs=(pl.BlockSpec((t, HG_DV), lambda b, h: (b, h)),
                   pl.BlockSpec((1, 2, 1, HG_DK, HG_DV), lambda b, h: (b, 0, h, 0, 0))),
        scratch_shapes=[pltpu.VMEM((2, t, HG_DK), F32), pltpu.VMEM((2, t, HG_DK), F32),
                        pltpu.VMEM((2, HG_DV, HG_DK), F32), pltpu.VMEM((2, t, HG_DV), F32)],
        compiler_params=_params("arbitrary", "arbitrary"),
        name=f"hgrn_t{t}",
    )(*args)


def _conv_kernel(cb_ref, cc_ref, cx_ref, w_ref, b_ref, o_ref, *, t):
    p = cc_ref[...] * cx_ref[...]
    row = lax.broadcasted_iota(jnp.int32, (t, 1), 0)
    prev = jnp.where(row >= 1, pltpu.roll(p, 1, axis=0), 0.0)
    nxt = jnp.where(row <= t - 2, pltpu.roll(p, t - 1, axis=0), 0.0)
    w = w_ref[...]
    y = prev * w[0:1] + p * w[1:2] + nxt * w[2:3] + b_ref[...]
    o_ref[...] = cb_ref[...] * y


def _conv(z, w, b, t, nb, row0):
    rb = row0 // t
    return pl.pallas_call(
        functools.partial(_conv_kernel, t=t),
        out_shape=jax.ShapeDtypeStruct((nb * t, GROUP_W), F32),
        grid=(nb,),
        in_specs=[
            pl.BlockSpec((t, GROUP_W), lambda i: (rb + i, COL_CB // 4)),
            pl.BlockSpec((t, GROUP_W), lambda i: (rb + i, COL_CC // 4)),
            pl.BlockSpec((t, GROUP_W), lambda i: (rb + i, COL_CX // 4)),
            pl.BlockSpec((CONV_K, GROUP_W), lambda i: (0, 0)),
            pl.BlockSpec((1, GROUP_W), lambda i: (0, 0)),
        ],
        out_specs=pl.BlockSpec((t, GROUP_W), lambda i: (i, 0)),
        compiler_params=_params("arbitrary"),
        name=f"conv_t{t}",
    )(z, z, z, w, b.reshape(1, GROUP_W))


def _outproj_kernel(*refs, tm):
    prompt_refs, sample_refs = refs[0:4], refs[4:8]
    w_ref, x_ref, g1_ref, sh_ref, sc_ref, n2_ref, x1_ref, h2_ref = refs[8:]
    i = pl.program_id(0)
    r = _mod_row(i * tm)

    def run(group_refs):
        acc = None
        for g, ref in enumerate(group_refs):
            part = _dot(ref[...].astype(BF16), w_ref[g * GROUP_W:(g + 1) * GROUP_W, :])
            acc = part if acc is None else acc + part
        x1 = x_ref[...] + g1_ref[pl.ds(r, 1), :] * acc
        x1_ref[...] = x1
        h2 = _rms(x1, n2_ref[...]) * (1.0 + sc_ref[pl.ds(r, 1), :]) + sh_ref[pl.ds(r, 1), :]
        h2_ref[...] = h2.astype(BF16)

    pl.when(i < N_PROMPT // tm)(lambda: run(prompt_refs))
    pl.when(i >= N_PROMPT // tm)(lambda: run(sample_refs))


def _outproj(prompt_groups, sample_groups, w_out_bf, x, mod, norm2_g):
    tm = 512
    n_pb, n_sb = N_PROMPT // tm, N_SAMPLE // tm
    p_spec = pl.BlockSpec((tm, GROUP_W), lambda i: (jnp.minimum(i, n_pb - 1), 0))
    s_spec = pl.BlockSpec((tm, GROUP_W), lambda i: (jnp.clip(i - n_pb, 0, n_sb - 1), 0))
    modspec = lambda k: pl.BlockSpec((MOD_ROWS, D_MODEL), lambda i: (0, k))
    return pl.pallas_call(
        functools.partial(_outproj_kernel, tm=tm),
        out_shape=(jax.ShapeDtypeStruct((N_TOK, D_MODEL), F32), jax.ShapeDtypeStruct((N_TOK, D_MODEL), BF16)),
        grid=(N_TOK // tm,),
        in_specs=[p_spec] * 4 + [s_spec] * 4 + [
            pl.BlockSpec((D_MODEL, D_MODEL), lambda i: (0, 0)),
            pl.BlockSpec((tm, D_MODEL), lambda i: (i, 0)),
            modspec(2), modspec(3), modspec(4),
            pl.BlockSpec((1, D_MODEL), lambda i: (0, 0))],
        out_specs=(pl.BlockSpec((tm, D_MODEL), lambda i: (i, 0)), pl.BlockSpec((tm, D_MODEL), lambda i: (i, 0))),
        compiler_params=_params("arbitrary"),
        name="outproj",
    )(*prompt_groups, *sample_groups, w_out_bf, x, mod, mod, mod, norm2_g.reshape(1, D_MODEL))


def _peer_query_kernel(h_ref, wq_ref, q_ref):
    q_ref[...] = _dot(h_ref[...], wq_ref[...]).astype(BF16)


def _peer_query(h2, wq_bf):
    tm = 512
    width = PEER_HEADS * PEER_QDIM
    return pl.pallas_call(
        _peer_query_kernel,
        out_shape=jax.ShapeDtypeStruct((N_TOK, width), BF16),
        grid=(N_TOK // tm,),
        in_specs=[pl.BlockSpec((tm, D_MODEL), lambda i: (i, 0)), pl.BlockSpec((D_MODEL, width), lambda i: (0, 0))],
        out_specs=pl.BlockSpec((tm, width), lambda i: (i, 0)),
        compiler_params=_params("arbitrary"),
        name="peer_query",
    )(h2, wq_bf)


_CAND_PIECES = ((0, 0, 8), (0, 8, 8), (1, 0, 8), (2, 0, 5), (3, 0, 4), (4, 0, 3), (5, 0, 2), (6, 0, 2), (7, 0, 2),
                (None, 0, 8))
_CAND_ROWS = 8 * len(_CAND_PIECES)
_NO_CAND = 1 << 20


def _topk_rows(x_scr, ids, n_rows, v_out, i_out):
    def body(j, carry):
        x = x_scr[0:n_rows, :]
        m = jnp.max(x, axis=0, keepdims=True)
        idx = jnp.min(jnp.where(x == m, ids, _NO_CAND), axis=0, keepdims=True)
        v_out[pl.ds(j, 1), :] = m
        i_out[pl.ds(j, 1), :] = idx
        x_scr[0:n_rows, :] = jnp.where(ids == idx, -jnp.inf, x)
        return carry

    lax.fori_loop(0, PEER_TOPK, body, 0, unroll=2)


def _peer_topk_kernel(q_ref, k1_ref, k2_ref, e_ref, g_ref, x_scr, v1_scr, i1_scr, v2_scr, i2_scr,
                      ts_scr, tc_scr, *, tm):
    key_ids = lax.broadcasted_iota(jnp.int32, (N_KEYS, tm), 0)
    for half, (k_ref, v_scr, i_scr) in enumerate(((k1_ref, v1_scr, i1_scr), (k2_ref, v2_scr, i2_scr))):
        qh = q_ref[:, half * PEER_HALF:(half + 1) * PEER_HALF]
        x_scr[...] = lax.dot_general(k_ref[...], qh, NT_DIMS, preferred_element_type=F32)
        _topk_rows(x_scr, key_ids, N_KEYS, v_scr, i_scr)
    v1, v2 = v1_scr[...], v2_scr[...]
    sub = lax.broadcasted_iota(jnp.int32, (8, tm), 0)
    cand_ids = []
    for p, (r1, r2, n_valid) in enumerate(_CAND_PIECES):
        if r1 is None:
            vals = v1[8:16] + v2[0:1]
            ids = (sub + 8) * PEER_TOPK
        else:
            vals = v1[r1:r1 + 1] + v2[r2:r2 + 8]
            ids = sub + (r1 * PEER_TOPK + r2)
            if n_valid < 8:
                vals = jnp.where(sub < n_valid, vals, -jnp.inf)
                ids = jnp.where(sub < n_valid, ids, _NO_CAND)
        x_scr[p * 8:(p + 1) * 8, :] = vals
        cand_ids.append(ids)
    _topk_rows(x_scr, jnp.concatenate(cand_ids, axis=0), _CAND_ROWS, ts_scr, tc_scr)
    tc = tc_scr[...]
    c1, c2 = tc // PEER_TOPK, tc % PEER_TOPK
    i1, i2 = i1_scr[...], i2_scr[...]
    e1 = jnp.zeros_like(tc)
    e2 = jnp.zeros_like(tc)
    for a in range(PEER_TOPK):
        e1 = jnp.where(c1 == a, i1[a:a + 1], e1)
        e2 = jnp.where(c2 == a, i2[a:a + 1], e2)
    e_ref[...] = e1 * N_KEYS + e2
    ts = ts_scr[...]
    p = jnp.exp(ts - jnp.max(ts, axis=0, keepdims=True))
    g_ref[...] = p / jnp.sum(p, axis=0, keepdims=True)


def _peer_topk(q, k1_bf, k2_bf):
    tm = 512
    row_f = pltpu.VMEM((PEER_TOPK, tm), F32)
    row_i = pltpu.VMEM((PEER_TOPK, tm), jnp.int32)
    return pl.pallas_call(
        functools.partial(_peer_topk_kernel, tm=tm),
        out_shape=(jax.ShapeDtypeStruct((PEER_HEADS * PEER_TOPK, N_TOK), jnp.int32),
                   jax.ShapeDtypeStruct((PEER_HEADS * PEER_TOPK, N_TOK), F32)),
        grid=(N_TOK // tm, PEER_HEADS),
        in_specs=[
            pl.BlockSpec((tm, PEER_QDIM), lambda i, h: (i, h)),
            pl.BlockSpec((N_KEYS, PEER_HALF), lambda i, h: (0, 0)),
            pl.BlockSpec((N_KEYS, PEER_HALF), lambda i, h: (0, 0)),
        ],
        out_specs=(pl.BlockSpec((PEER_TOPK, tm), lambda i, h: (h, i)),
                   pl.BlockSpec((PEER_TOPK, tm), lambda i, h: (h, i))),
        scratch_shapes=[pltpu.VMEM((N_KEYS, tm), F32), row_f, row_i, row_f, row_i, row_f, row_i],
        compiler_params=_params("arbitrary", "arbitrary"),
        name="peer_topk",
    )(q, k1_bf, k2_bf)


def _peer_mask_kernel(e_ref, g_ref, m_ref, et_scr, gt_scr, slab_scr, *, tb):
    et_scr[...] = e_ref[...].T
    gt_scr[...] = g_ref[...].T
    key = lax.broadcasted_iota(jnp.int32, (N_KEYS, PEER_HEADS * PEER_TOPK), 0)

    def body(t, carry):
        er = et_scr[pl.ds(t, 1), :]
        gr = gt_scr[pl.ds(t, 1), :]
        p1 = jnp.where(key == er // N_KEYS, gr, 0.0).astype(BF16)
        p2 = jnp.where(key == er % N_KEYS, 1.0, 0.0).astype(BF16)
        mt = lax.dot_general(p1, p2, NT_DIMS, preferred_element_type=F32)
        slab_scr[pl.ds(pl.multiple_of(t * MASK_PITCH, 8), N_KEYS), :] = mt
        return carry

    lax.fori_loop(0, tb, body, 0, unroll=32)
    for k1 in range(N_KEYS):
        m_ref[:, k1 * N_KEYS:(k1 + 1) * N_KEYS] = slab_scr[pl.ds(k1, tb, stride=MASK_PITCH), :].astype(BF16)


def _peer_mask(experts_t, gates_t):
    tb = LANES
    pairs = PEER_HEADS * PEER_TOPK
    return pl.pallas_call(
        functools.partial(_peer_mask_kernel, tb=tb),
        out_shape=jax.ShapeDtypeStruct((N_TOK, N_EXPERTS), BF16),
        grid=(N_TOK // tb,),
        in_specs=[pl.BlockSpec((pairs, tb), lambda i: (0, i)), pl.BlockSpec((pairs, tb), lambda i: (0, i))],
        out_specs=pl.BlockSpec((tb, N_EXPERTS), lambda i: (i, 0)),
        scratch_shapes=[pltpu.VMEM((tb, pairs), jnp.int32), pltpu.VMEM((tb, pairs), F32),
                        pltpu.VMEM((tb * MASK_PITCH, N_KEYS), F32)],
        compiler_params=_params("arbitrary"),
        name="peer_mask",
    )(experts_t, gates_t)


def _peer_expert_kernel(h_ref, ut_ref, v_ref, m_ref, x_ref, g2_ref, o_ref, *, tm):
    j = pl.program_id(1)

    @pl.when(j == 0)
    def _():
        o_ref[...] = jnp.zeros_like(o_ref)

    s = _dot(h_ref[...], ut_ref[...])
    a = (m_ref[...].astype(F32) * jax.nn.gelu(s)).astype(BF16)
    o_ref[...] += _dot(a, v_ref[...])

    @pl.when(j == pl.num_programs(1) - 1)
    def _():
        r = _mod_row(pl.program_id(0) * tm)
        o_ref[...] = x_ref[...] + g2_ref[pl.ds(r, 1), :] * o_ref[...]


EXPERT_TM, EXPERT_TE = 1024, 512


def _peer_expert(h2, ut_bf, v_bf, mask, x1, mod):
    tm, te = EXPERT_TM, EXPERT_TE
    return pl.pallas_call(
        functools.partial(_peer_expert_kernel, tm=tm),
        out_shape=jax.ShapeDtypeStruct((N_TOK, D_MODEL), F32),
        grid=(N_TOK // tm, N_EXPERTS // te),
        in_specs=[
            pl.BlockSpec((tm, D_MODEL), lambda i, j: (i, 0)),
            pl.BlockSpec((D_MODEL, te), lambda i, j: (0, j)),
            pl.BlockSpec((te, D_MODEL), lambda i, j: (j, 0)),
            pl.BlockSpec((tm, te), lambda i, j: (i, j)),
            pl.BlockSpec((tm, D_MODEL), lambda i, j: (i, 0)),
            pl.BlockSpec((MOD_ROWS, D_MODEL), lambda i, j: (0, 5)),
        ],
        out_specs=pl.BlockSpec((tm, D_MODEL), lambda i, j: (i, 0)),
        compiler_params=_params("arbitrary", "arbitrary"),
        name="peer_expert",
    )(h2, ut_bf, v_bf, mask, x1, mod)


def kernel(x_prompt, x_sample, cache_k, cache_v, state_hgrn, c, c_ctx, ada_w, ada_b, norm1_g, norm2_g, w_in,
           q_norm_g, k_norm_g, hg_lb, hg_onorm_g, conv_w, conv_b, w_out, peer_wq, peer_k1, peer_k2, peer_u, peer_v):
    x = jnp.concatenate([x_prompt.reshape(N_PROMPT, D_MODEL), x_sample.reshape(N_SAMPLE, D_MODEL)], axis=0)
    cvec = jnp.concatenate([c_ctx[None, :], c, jnp.zeros((MOD_ROWS - 1 - DEC_BATCH, D_MODEL), F32)], axis=0)
    mod_all = _ada(cvec, ada_w, ada_b)

    ks_out, vs_out, ss_out = [], [], []
    for l in range(DEPTH):
        mod = mod_all[l]
        z = _inproj(x, mod, norm1_g[l], w_in[l].astype(BF16))

        fn_p = _fourier(z, SEQ, BATCH, 0)
        fn_s = _fourier(z, DEC_SEQ, DEC_BATCH, N_PROMPT)

        at_p, k_l, v_l = _attn_prompt(z, q_norm_g[l], k_norm_g[l])
        k_ctx = cache_k[:, l].reshape(DEC_BATCH, PAST_LEN, N_KV * HEAD_DIM)
        v_ctx = cache_v[:, l].reshape(DEC_BATCH, PAST_LEN, N_KV * HEAD_DIM)
        at_s = _attn_sample(z, q_norm_g[l], k_norm_g[l], k_ctx, v_ctx)

        hg_p, s_l = _hgrn(z, hg_lb, hg_onorm_g[l], l, SEQ, BATCH, 0, None)
        hg_s, _ = _hgrn(z, hg_lb, hg_onorm_g[l], l, DEC_SEQ, DEC_BATCH, N_PROMPT, state_hgrn[:, l])

        cv_p = _conv(z, conv_w[l], conv_b[l], SEQ, BATCH, 0)
        cv_s = _conv(z, conv_w[l], conv_b[l], DEC_SEQ, DEC_BATCH, N_PROMPT)

        x1, h2 = _outproj((fn_p, at_p, hg_p, cv_p), (fn_s, at_s, hg_s, cv_s), w_out[l].astype(BF16), x, mod,
                          norm2_g[l])

        q_peer = _peer_query(h2, peer_wq[l].astype(BF16))
        experts_t, gates_t = _peer_topk(q_peer, peer_k1[l].astype(BF16), peer_k2[l].astype(BF16))
        mask = _peer_mask(experts_t, gates_t)
        x = _peer_expert(h2, peer_u[l].T.astype(BF16), peer_v[l].astype(BF16), mask, x1, mod)

        ks_out.append(k_l.reshape(BATCH, SEQ, N_KV, HEAD_DIM))
        vs_out.append(v_l.reshape(BATCH, SEQ, N_KV, HEAD_DIM))
        ss_out.append(s_l)

    y_prompt = x[:N_PROMPT].reshape(BATCH, SEQ, D_MODEL)
    y_sample = x[N_PROMPT:].reshape(DEC_BATCH, DEC_SEQ, D_MODEL)
    return (y_prompt, y_sample, jnp.stack(ks_out, axis=1), jnp.stack(vs_out, axis=1), jnp.stack(ss_out, axis=1))
```

```python
import functools

import numpy as np
import jax
import jax.numpy as jnp
from jax import lax
from jax.experimental import pallas as pl
from jax.experimental.pallas import tpu as pltpu

F32 = jnp.float32
BF16 = jnp.bfloat16

D_MODEL = 2048
BATCH = 32
SEQ = 256
DEPTH = 2
DEC_BATCH = 4
DEC_SEQ = 1024
PAST_LEN = 256
GRID_W = 64
GROUP_W = D_MODEL // 4
FN_GROUPS = 4
FN_GW = GROUP_W // FN_GROUPS
HEAD_DIM = 128
N_HEADS = GROUP_W // HEAD_DIM
N_KV = 2
GQA_GROUP = N_HEADS // N_KV
ROPE_THETA = 10000.0
HG_DK = 128
HG_DV = 128
HG_HEADS = GROUP_W // HG_DK
CONV_K = 3
IN_COLS = 11 * GROUP_W
PEER_HEADS = 8
N_KEYS = 128
N_EXPERTS = N_KEYS * N_KEYS
PEER_QDIM = 256
PEER_HALF = PEER_QDIM // 2
PEER_TOPK = 16
EPS = 1e-6
F_FLOOR = 1e-30

N_PROMPT = BATCH * SEQ
N_SAMPLE = DEC_BATCH * DEC_SEQ
N_TOK = N_PROMPT + N_SAMPLE
MOD_ROWS = 8
LANES = 128
HG_CHUNK = 32
HG_CUM_BLOCK = 256
MASK_PITCH = 136
VMEM_LIMIT = 56 * 1024 * 1024

COL_FN, COL_Q, COL_K, COL_V = 0, 4, 8, 10
COL_GQ, COL_GFF, COL_GFB, COL_GI, COL_GO = 12, 16, 20, 24, 28
COL_CB, COL_CC, COL_CX = 32, 36, 40

NT_DIMS = (((1,), (1,)), ((), ()))
TN_DIMS = (((0,), (0,)), ((), ()))


def _params(*sem):
    return pltpu.CompilerParams(dimension_semantics=sem, vmem_limit_bytes=VMEM_LIMIT)


def _dot(a, b):
    return jnp.dot(a, b, preferred_element_type=F32)


def _split3(x):
    h = x.astype(BF16)
    r = x - h.astype(F32)
    m = r.astype(BF16)
    l = (r - m.astype(F32)).astype(BF16)
    return h, m, l


def _split2(x):
    h = x.astype(BF16)
    return h, (x - h.astype(F32)).astype(BF16)


def _mod_row(tok0):
    return jnp.where(tok0 >= N_PROMPT, (tok0 - N_PROMPT) // DEC_SEQ + 1, 0)


def _rms(x, g):
    return x * lax.rsqrt(jnp.mean(x * x, axis=-1, keepdims=True) + EPS) * g


def _ada_kernel(c_ref, w_ref, b_ref, o_ref):
    c = c_ref[...]
    s = (c * jax.nn.sigmoid(c)).astype(BF16)
    o_ref[0] = _dot(s, w_ref[0].astype(BF16)) + b_ref[0]


def _ada(cvec, ada_w, ada_b):
    tn = 1024
    return pl.pallas_call(
        _ada_kernel,
        out_shape=jax.ShapeDtypeStruct((DEPTH, MOD_ROWS, 6 * D_MODEL), F32),
        grid=(DEPTH, 6 * D_MODEL // tn),
        in_specs=[
            pl.BlockSpec((MOD_ROWS, D_MODEL), lambda l, j: (0, 0)),
            pl.BlockSpec((1, D_MODEL, tn), lambda l, j: (l, 0, j)),
            pl.BlockSpec((1, 1, tn), lambda l, j: (l, 0, j)),
        ],
        out_specs=pl.BlockSpec((1, MOD_ROWS, tn), lambda l, j: (l, 0, j)),
        compiler_params=_params("arbitrary", "arbitrary"),
        name="ada",
    )(cvec, ada_w, ada_b.reshape(DEPTH, 1, 6 * D_MODEL))


def _inproj_kernel(x_ref, sh_ref, sc_ref, g_ref, w_ref, o_ref, h_ref, *, tm):
    @pl.when(pl.program_id(1) == 0)
    def _():
        r = _mod_row(pl.program_id(0) * tm)
        y = _rms(x_ref[...], g_ref[...])
        h_ref[...] = (y * (1.0 + sc_ref[pl.ds(r, 1), :]) + sh_ref[pl.ds(r, 1), :]).astype(BF16)

    o_ref[...] = _dot(h_ref[...], w_ref[...])


def _inproj(x, mod, norm_g, w_in_bf, layer):
    tm, tn = 1024, 512
    return pl.pallas_call(
        functools.partial(_inproj_kernel, tm=tm),
        out_shape=jax.ShapeDtypeStruct((N_TOK, IN_COLS), F32),
        grid=(N_TOK // tm, IN_COLS // tn),
        in_specs=[
            pl.BlockSpec((tm, D_MODEL), lambda i, j: (i, 0)),
            pl.BlockSpec((MOD_ROWS, D_MODEL), lambda i, j: (0, 0)),
            pl.BlockSpec((MOD_ROWS, D_MODEL), lambda i, j: (0, 1)),
            pl.BlockSpec((1, D_MODEL), lambda i, j: (0, 0)),
            pl.BlockSpec((None, D_MODEL, tn), lambda i, j: (layer, 0, j)),
        ],
        out_specs=pl.BlockSpec((tm, tn), lambda i, j: (i, j)),
        scratch_shapes=[pltpu.VMEM((tm, D_MODEL), BF16)],
        compiler_params=_params("arbitrary", "arbitrary"),
        name="inproj",
    )(x, mod, mod, norm_g.reshape(1, D_MODEL), w_in_bf)


def _fourier_kernel(u_ref, cch_ref, ccl_ref, cth_ref, ctl_ref, o_ref):
    cch, ccl, cth, ctl = cch_ref[...], ccl_ref[...], cth_ref[...], ctl_ref[...]
    for g in range(FN_GROUPS):
        u = u_ref[:, g * FN_GW:(g + 1) * FN_GW]
        uh, ul = _split2(u)
        w = _dot(uh, cch) + (_dot(ul, cch) + _dot(uh, ccl))
        st = jnp.concatenate([w[:, :FN_GW], w[:, FN_GW:]], axis=0)
        sh, sl = _split2(st)
        y = _dot(cth, sh) + (_dot(cth, sl) + _dot(ctl, sh))
        o_ref[:, g * FN_GW:(g + 1) * FN_GW] = y


def _dft_consts(t):
    def cs(n):
        k = np.arange(n, dtype=np.float64)
        ang = 2.0 * np.pi * np.outer(k, k) / n
        return np.cos(ang) / np.sqrt(n), np.sin(ang) / np.sqrt(n)

    cc, sc = cs(FN_GW)
    ct, st = cs(t)
    right = np.concatenate([cc, sc], axis=1).astype(np.float32)
    left = np.concatenate([ct, -st], axis=1).astype(np.float32)

    def hl(a):
        a = jnp.asarray(a)
        h = a.astype(BF16)
        return h, (a - h.astype(F32)).astype(BF16)

    return hl(right) + hl(left)


def _fourier(z, t, nb, row0):
    cch, ccl, cth, ctl = _dft_consts(t)
    rb = row0 // t
    return pl.pallas_call(
        _fourier_kernel,
        out_shape=jax.ShapeDtypeStruct((nb * t, GROUP_W), F32),
        grid=(nb,),
        in_specs=[
            pl.BlockSpec((t, GROUP_W), lambda b: (rb + b, COL_FN // 4)),
            pl.BlockSpec((FN_GW, 2 * FN_GW), lambda b: (0, 0)),
            pl.BlockSpec((FN_GW, 2 * FN_GW), lambda b: (0, 0)),
            pl.BlockSpec((t, 2 * t), lambda b: (0, 0)),
            pl.BlockSpec((t, 2 * t), lambda b: (0, 0)),
        ],
        out_specs=pl.BlockSpec((t, GROUP_W), lambda b: (b, 0)),
        compiler_params=_params("arbitrary"),
        name=f"fourier_t{t}",
    )(z, cch, ccl, cth, ctl)


def _softmax_pv(q_bf, k_bf, v_bf):
    s = lax.dot_general(q_bf, k_bf, NT_DIMS, preferred_element_type=F32) * (HEAD_DIM ** -0.5)
    p = jnp.exp(s - jnp.max(s, axis=-1, keepdims=True))
    return _dot(p.astype(BF16), v_bf) / jnp.sum(p, axis=-1, keepdims=True)


def _attn_prompt_kernel(q_ref, k_ref, v_ref, qg_ref, kg_ref, o_ref, kc_ref, vc_ref):
    kn = _rms(k_ref[...], kg_ref[...])
    v = v_ref[...]
    kc_ref[...] = kn
    vc_ref[...] = v
    kb, vb = kn.astype(BF16), v.astype(BF16)
    for g in range(GQA_GROUP):
        qn = _rms(q_ref[:, g * HEAD_DIM:(g + 1) * HEAD_DIM], qg_ref[...])
        o_ref[:, g * HEAD_DIM:(g + 1) * HEAD_DIM] = _softmax_pv(qn.astype(BF16), kb, vb)


def _attn_prompt(z, q_g, k_g):
    t = SEQ
    kv_shape = jax.ShapeDtypeStruct((N_PROMPT, N_KV * HEAD_DIM), F32)
    return pl.pallas_call(
        _attn_prompt_kernel,
        out_shape=(jax.ShapeDtypeStruct((N_PROMPT, GROUP_W), F32), kv_shape, kv_shape),
        grid=(BATCH, N_KV),
        in_specs=[
            pl.BlockSpec((t, GQA_GROUP * HEAD_DIM), lambda b, h: (b, COL_Q // 2 + h)),
            pl.BlockSpec((t, HEAD_DIM), lambda b, h: (b, COL_K + h)),
            pl.BlockSpec((t, HEAD_DIM), lambda b, h: (b, COL_V + h)),
            pl.BlockSpec((1, HEAD_DIM), lambda b, h: (0, 0)),
            pl.BlockSpec((1, HEAD_DIM), lambda b, h: (0, 0)),
        ],
        out_specs=(
            pl.BlockSpec((t, GQA_GROUP * HEAD_DIM), lambda b, h: (b, h)),
            pl.BlockSpec((t, HEAD_DIM), lambda b, h: (b, h)),
            pl.BlockSpec((t, HEAD_DIM), lambda b, h: (b, h)),
        ),
        compiler_params=_params("arbitrary", "arbitrary"),
        name="attn_prompt",
    )(z, z, z, q_g.reshape(1, HEAD_DIM), k_g.reshape(1, HEAD_DIM))


def _rope(x, cosf, sinf):
    return x * cosf + pltpu.roll(x, HEAD_DIM // 2, axis=1) * sinf


def _attn_sample_kernel(q_ref, k_ref, v_ref, kctx_ref, vctx_ref, qg_ref, kg_ref, cos_ref, sin_ref,
                        o_ref, kall_ref, vall_ref, *, qb):
    cosf, sinf = cos_ref[...], sin_ref[...]
    kall_ref[0:PAST_LEN, :] = kctx_ref[0].astype(BF16)
    vall_ref[0:PAST_LEN, :] = vctx_ref[0].astype(BF16)
    kall_ref[PAST_LEN:, :] = _rope(_rms(k_ref[...], kg_ref[...]), cosf, sinf).astype(BF16)
    vall_ref[PAST_LEN:, :] = v_ref[...].astype(BF16)
    kb, vb = kall_ref[...], vall_ref[...]
    for i in range(DEC_SEQ // qb):
        rows = slice(i * qb, (i + 1) * qb)
        for g in range(GQA_GROUP):
            cols = slice(g * HEAD_DIM, (g + 1) * HEAD_DIM)
            qn = _rope(_rms(q_ref[rows, cols], qg_ref[...]), cosf[rows], sinf[rows])
            o_ref[rows, cols] = _softmax_pv(qn.astype(BF16), kb, vb)


def _rope_tables():
    rows = DEC_SEQ // GRID_W
    row = np.repeat(np.arange(rows, dtype=np.float32), GRID_W)
    col = np.tile(np.arange(GRID_W, dtype=np.float32), rows)
    n_freq = HEAD_DIM // 4
    inv = (np.float32(ROPE_THETA) ** (-np.arange(n_freq, dtype=np.float32) / n_freq)).astype(np.float32)
    ang = np.concatenate([row[:, None] * inv, col[:, None] * inv], axis=-1).astype(np.float32)
    cos, sin = np.cos(ang.astype(np.float64)), np.sin(ang.astype(np.float64))
    cosf = np.concatenate([cos, cos], axis=-1).astype(np.float32)
    sinf = np.concatenate([-sin, sin], axis=-1).astype(np.float32)
    return jnp.asarray(cosf), jnp.asarray(sinf)


def _attn_sample(z, q_g, k_g, k_ctx, v_ctx):
    t = DEC_SEQ
    rb = N_PROMPT // t
    cosf, sinf = _rope_tables()
    const = lambda b, h: (0, 0)
    return pl.pallas_call(
        functools.partial(_attn_sample_kernel, qb=256),
        out_shape=jax.ShapeDtypeStruct((N_SAMPLE, GROUP_W), F32),
        grid=(DEC_BATCH, N_KV),
        in_specs=[
            pl.BlockSpec((t, GQA_GROUP * HEAD_DIM), lambda b, h: (rb + b, COL_Q // 2 + h)),
            pl.BlockSpec((t, HEAD_DIM), lambda b, h: (rb + b, COL_K + h)),
            pl.BlockSpec((t, HEAD_DIM), lambda b, h: (rb + b, COL_V + h)),
            pl.BlockSpec((1, PAST_LEN, HEAD_DIM), lambda b, h: (b, 0, h)),
            pl.BlockSpec((1, PAST_LEN, HEAD_DIM), lambda b, h: (b, 0, h)),
            pl.BlockSpec((1, HEAD_DIM), const),
            pl.BlockSpec((1, HEAD_DIM), const),
            pl.BlockSpec((t, HEAD_DIM), const),
            pl.BlockSpec((t, HEAD_DIM), const),
        ],
        out_specs=pl.BlockSpec((t, GQA_GROUP * HEAD_DIM), lambda b, h: (b, h)),
        scratch_shapes=[pltpu.VMEM((PAST_LEN + t, HEAD_DIM), BF16), pltpu.VMEM((PAST_LEN + t, HEAD_DIM), BF16)],
        compiler_params=_params("arbitrary", "arbitrary"),
        name="attn_sample",
    )(z, z, z, k_ctx, v_ctx, q_g.reshape(1, HEAD_DIM), k_g.reshape(1, HEAD_DIM), cosf, sinf)


def _hgrn_chain(rev, r0, gq_ref, gi_ref, k_scr, cum_scr, st_scr, o_scr):
    c = HG_CHUNK
    d = 1 if rev else 0
    rows = pl.ds(r0, c)
    qc, vc = gq_ref[rows, :], gi_ref[rows, :]
    kc, cc = k_scr[d, rows, :], cum_scr[d, rows, :]
    st = st_scr[d]
    last = cc[0:1] if rev else cc[c - 1:c]
    qs = (qc * jnp.exp(cc)).astype(BF16)
    o_inter = lax.dot_general(qs, st.astype(BF16), NT_DIMS, preferred_element_type=F32)
    ks = (kc * jnp.exp(last - cc)).astype(BF16)
    upd = lax.dot_general(vc.astype(BF16), ks, TN_DIMS, preferred_element_type=F32)
    st_scr[d] = st * jnp.exp(last) + upd
    row = lax.broadcasted_iota(jnp.int32, (c, 1), 0)
    acc = jnp.zeros((c, HG_DV), F32)
    for delta in range(c):
        if delta == 0:
            k_sh, c_sh, v_sh = kc, cc, vc
        else:
            sh = (c - delta) if rev else delta
            k_sh, c_sh, v_sh = (pltpu.roll(a, sh, axis=0) for a in (kc, cc, vc))
        a = jnp.sum(qc * k_sh * jnp.exp(jnp.minimum(cc - c_sh, 0.0)), axis=-1, keepdims=True)
        valid = (row <= c - 1 - delta) if rev else (row >= delta)
        acc = acc + jnp.where(valid, a, 0.0) * v_sh
    o_scr[d, rows, :] = o_inter + acc


def _hgrn_kernel(*refs, layer, t, has_ctx):
    gq_ref, gff_ref, gfb_ref, gi_ref, go_ref, lb_ref, on_ref, bdf_ref, bdb_ref = refs[:9]
    n_in = 9
    s0_ref = None
    if has_ctx:
        s0_ref = refs[9]
        n_in = 10
    o_ref, sout_ref = refs[n_in], refs[n_in + 1]
    k_scr, cum_scr, st_scr, o_scr = refs[n_in + 2:]

    raw = lb_ref[...]
    e = jnp.exp(raw - jnp.max(raw, axis=0, keepdims=True))
    soft = e / jnp.sum(e, axis=0, keepdims=True)
    csum = soft[0]
    for i in range(1, layer + 1):
        csum = csum + soft[i]
    lb = csum - soft[0]

    for d, (g_ref, bd_ref) in enumerate(((gff_ref, bdf_ref), (gfb_ref, bdb_ref))):
        lbd = lb[d:d + 1]
        f = jnp.maximum(lbd + (1.0 - lbd) * jax.nn.sigmoid(g_ref[...]), F_FLOOR)
        k_scr[d] = 1.0 - f
        lf = jnp.log(f)
        bd = bd_ref[...]
        for p in range(t // HG_CUM_BLOCK):
            rows = slice(p * HG_CUM_BLOCK, (p + 1) * HG_CUM_BLOCK)
            h, m, l = _split3(lf[rows])
            cum_scr[d, rows, :] = _dot(bd, h) + _dot(bd, m) + _dot(bd, l)
        if has_ctx:
            st_scr[d] = s0_ref[0, d, 0].T
        else:
            st_scr[d] = jnp.zeros((HG_DV, HG_DK), F32)

    n = t // HG_CHUNK

    def body(i, carry):
        _hgrn_chain(False, pl.multiple_of(i * HG_CHUNK, HG_CHUNK), gq_ref, gi_ref, k_scr, cum_scr, st_scr, o_scr)
        _hgrn_chain(True, pl.multiple_of((n - 1 - i) * HG_CHUNK, HG_CHUNK), gq_ref, gi_ref, k_scr, cum_scr,
                    st_scr, o_scr)
        return carry

    lax.fori_loop(0, n, body, 0)

    o = _rms(o_scr[0] + o_scr[1], on_ref[...])
    go = go_ref[...]
    o_ref[...] = o * (go * jax.nn.sigmoid(go))
    for d in range(2):
        sout_ref[0, d, 0] = st_scr[d].T


def _cum_consts():
    r = np.arange(HG_CUM_BLOCK)
    same = (r[:, None] // HG_CHUNK) == (r[None, :] // HG_CHUNK)
    fwd = same & (r[None, :] <= r[:, None])
    bwd = same & (r[None, :] >= r[:, None])
    return jnp.asarray(fwd, BF16), jnp.asarray(bwd, BF16)


def _hgrn(z, hg_lb, onorm_g, layer, t, nb, row0, s0):
    rb = row0 // t
    bdf, bdb = _cum_consts()
    has_ctx = s0 is not None
    col = lambda c0: (lambda b, h: (rb + b, c0 + h))
    in_specs = [
        pl.BlockSpec((t, HG_DK), col(COL_GQ)),
        pl.BlockSpec((t, HG_DK), col(COL_GFF)),
        pl.BlockSpec((t, HG_DK), col(COL_GFB)),
        pl.BlockSpec((t, HG_DV), col(COL_GI)),
        pl.BlockSpec((t, HG_DV), col(COL_GO)),
        pl.BlockSpec((DEPTH, 2, HG_DK), lambda b, h: (0, 0, h)),
        pl.BlockSpec((1, HG_DV), lambda b, h: (0, 0)),
        pl.BlockSpec((HG_CUM_BLOCK, HG_CUM_BLOCK), lambda b, h: (0, 0)),
        pl.BlockSpec((HG_CUM_BLOCK, HG_CUM_BLOCK), lambda b, h: (0, 0)),
    ]
    args = [z, z, z, z, z, hg_lb, onorm_g.reshape(1, HG_DV), bdf, bdb]
    if has_ctx:
        in_specs.append(pl.BlockSpec((1, 2, 1, HG_DK, HG_DV), lambda b, h: (b, 0, h, 0, 0)))
        args.append(s0)
    return pl.pallas_call(
        functools.partial(_hgrn_kernel, layer=layer, t=t, has_ctx=has_ctx),
        out_shape=(jax.ShapeDtypeStruct((nb * t, GROUP_W), F32),
                   jax.ShapeDtypeStruct((nb, 2, HG_HEADS, HG_DK, HG_DV), F32)),
        grid=(nb, HG_HEADS),
        in_specs=in_specs,
        out_specs=(pl.BlockSpec((t, HG_DV), lambda b, h: (b, h)),
                   pl.BlockSpec((1, 2, 1, HG_DK, HG_DV), lambda b, h: (b, 0, h, 0, 0))),
        scratch_shapes=[pltpu.VMEM((2, t, HG_DK), F32), pltpu.VMEM((2, t, HG_DK), F32),
                        pltpu.VMEM((2, HG_DV, HG_DK), F32), pltpu.VMEM((2, t, HG_DV), F32)],
        compiler_params=_params("arbitrary", "arbitrary"),
        name=f"hgrn_t{t}",
    )(*args)


def _conv_kernel(cb_ref, cc_ref, cx_ref, w_ref, b_ref, o_ref, *, t):
    p = cc_ref[...] * cx_ref[...]
    row = lax.broadcasted_iota(jnp.int32, (t, 1), 0)
    prev = jnp.where(row >= 1, pltpu.roll(p, 1, axis=0), 0.0)
    nxt = jnp.where(row <= t - 2, pltpu.roll(p, t - 1, axis=0), 0.0)
    w = w_ref[...]
    y = prev * w[0:1] + p * w[1:2] + nxt * w[2:3] + b_ref[...]
    o_ref[...] = cb_ref[...] * y


def _conv(z, w, b, t, nb, row0):
    rb = row0 // t
    return pl.pallas_call(
        functools.partial(_conv_kernel, t=t),
        out_shape=jax.ShapeDtypeStruct((nb * t, GROUP_W), F32),
        grid=(nb,),
        in_specs=[
            pl.BlockSpec((t, GROUP_W), lambda i: (rb + i, COL_CB // 4)),
            pl.BlockSpec((t, GROUP_W), lambda i: (rb + i, COL_CC // 4)),
            pl.BlockSpec((t, GROUP_W), lambda i: (rb + i, COL_CX // 4)),
            pl.BlockSpec((CONV_K, GROUP_W), lambda i: (0, 0)),
            pl.BlockSpec((1, GROUP_W), lambda i: (0, 0)),
        ],
        out_specs=pl.BlockSpec((t, GROUP_W), lambda i: (i, 0)),
        compiler_params=_params("arbitrary"),
        name=f"conv_t{t}",
    )(z, z, z, w, b.reshape(1, GROUP_W))


def _outproj_kernel(*refs, tm):
    prompt_refs, sample_refs = refs[0:4], refs[4:8]
    w_ref, x_ref, g1_ref, sh_ref, sc_ref, n2_ref, x1_ref, h2_ref = refs[8:]
    i = pl.program_id(0)
    r = _mod_row(i * tm)

    def run(group_refs):
        acc = None
        for g, ref in enumerate(group_refs):
            part = _dot(ref[...].astype(BF16), w_ref[g * GROUP_W:(g + 1) * GROUP_W, :])
            acc = part if acc is None else acc + part
        x1 = x_ref[...] + g1_ref[pl.ds(r, 1), :] * acc
        x1_ref[...] = x1
        h2 = _rms(x1, n2_ref[...]) * (1.0 + sc_ref[pl.ds(r, 1), :]) + sh_ref[pl.ds(r, 1), :]
        h2_ref[...] = h2.astype(BF16)

    pl.when(i < N_PROMPT // tm)(lambda: run(prompt_refs))
    pl.when(i >= N_PROMPT // tm)(lambda: run(sample_refs))


def _outproj(prompt_groups, sample_groups, w_out_bf, layer, x, mod, norm2_g):
    tm = 512
    n_pb, n_sb = N_PROMPT // tm, N_SAMPLE // tm
    p_spec = pl.BlockSpec((tm, GROUP_W), lambda i: (jnp.minimum(i, n_pb - 1), 0))
    s_spec = pl.BlockSpec((tm, GROUP_W), lambda i: (jnp.clip(i - n_pb, 0, n_sb - 1), 0))
    modspec = lambda k: pl.BlockSpec((MOD_ROWS, D_MODEL), lambda i: (0, k))
    return pl.pallas_call(
        functools.partial(_outproj_kernel, tm=tm),
        out_shape=(jax.ShapeDtypeStruct((N_TOK, D_MODEL), F32), jax.ShapeDtypeStruct((N_TOK, D_MODEL), BF16)),
        grid=(N_TOK // tm,),
        in_specs=[p_spec] * 4 + [s_spec] * 4 + [
            pl.BlockSpec((None, D_MODEL, D_MODEL), lambda i: (layer, 0, 0)),
            pl.BlockSpec((tm, D_MODEL), lambda i: (i, 0)),
            modspec(2), modspec(3), modspec(4),
            pl.BlockSpec((1, D_MODEL), lambda i: (0, 0))],
        out_specs=(pl.BlockSpec((tm, D_MODEL), lambda i: (i, 0)), pl.BlockSpec((tm, D_MODEL), lambda i: (i, 0))),
        compiler_params=_params("arbitrary"),
        name="outproj",
    )(*prompt_groups, *sample_groups, w_out_bf, x, mod, mod, mod, norm2_g.reshape(1, D_MODEL))


def _peer_query_kernel(h_ref, wq_ref, q_ref):
    q_ref[...] = _dot(h_ref[...], wq_ref[...]).astype(BF16)


def _peer_query(h2, wq_bf, layer):
    tm = 512
    width = PEER_HEADS * PEER_QDIM
    return pl.pallas_call(
        _peer_query_kernel,
        out_shape=jax.ShapeDtypeStruct((N_TOK, width), BF16),
        grid=(N_TOK // tm,),
        in_specs=[pl.BlockSpec((tm, D_MODEL), lambda i: (i, 0)),
                  pl.BlockSpec((None, D_MODEL, width), lambda i: (layer, 0, 0))],
        out_specs=pl.BlockSpec((tm, width), lambda i: (i, 0)),
        compiler_params=_params("arbitrary"),
        name="peer_query",
    )(h2, wq_bf)


_CAND_PIECES = ((0, 0, 8), (0, 8, 8), (1, 0, 8), (2, 0, 5), (3, 0, 4), (4, 0, 3), (5, 0, 2), (6, 0, 2), (7, 0, 2),
                (None, 0, 8))
_CAND_ROWS = 8 * len(_CAND_PIECES)
_NO_CAND = 1 << 20


def _topk_rows(x_scr, ids, n_rows, v_out, i_out):
    def body(j, carry):
        x = x_scr[0:n_rows, :]
        m = jnp.max(x, axis=0, keepdims=True)
        idx = jnp.min(jnp.where(x == m, ids, _NO_CAND), axis=0, keepdims=True)
        v_out[pl.ds(j, 1), :] = m
        i_out[pl.ds(j, 1), :] = idx
        x_scr[0:n_rows, :] = jnp.where(ids == idx, -jnp.inf, x)
        return carry

    lax.fori_loop(0, PEER_TOPK, body, 0, unroll=2)


def _peer_topk_kernel(q_ref, k1_ref, k2_ref, e_ref, g_ref, x_scr, v1_scr, i1_scr, v2_scr, i2_scr,
                      ts_scr, tc_scr, *, tm):
    key_ids = lax.broadcasted_iota(jnp.int32, (N_KEYS, tm), 0)
    for half, (k_ref, v_scr, i_scr) in enumerate(((k1_ref, v1_scr, i1_scr), (k2_ref, v2_scr, i2_scr))):
        qh = q_ref[:, half * PEER_HALF:(half + 1) * PEER_HALF]
        x_scr[...] = lax.dot_general(k_ref[...], qh, NT_DIMS, preferred_element_type=F32)
        _topk_rows(x_scr, key_ids, N_KEYS, v_scr, i_scr)
    v1, v2 = v1_scr[...], v2_scr[...]
    sub = lax.broadcasted_iota(jnp.int32, (8, tm), 0)
    cand_ids = []
    for p, (r1, r2, n_valid) in enumerate(_CAND_PIECES):
        if r1 is None:
            vals = v1[8:16] + v2[0:1]
            ids = (sub + 8) * PEER_TOPK
        else:
            vals = v1[r1:r1 + 1] + v2[r2:r2 + 8]
            ids = sub + (r1 * PEER_TOPK + r2)
            if n_valid < 8:
                vals = jnp.where(sub < n_valid, vals, -jnp.inf)
                ids = jnp.where(sub < n_valid, ids, _NO_CAND)
        x_scr[p * 8:(p + 1) * 8, :] = vals
        cand_ids.append(ids)
    _topk_rows(x_scr, jnp.concatenate(cand_ids, axis=0), _CAND_ROWS, ts_scr, tc_scr)
    tc = tc_scr[...]
    c1, c2 = tc // PEER_TOPK, tc % PEER_TOPK
    i1, i2 = i1_scr[...], i2_scr[...]
    e1 = jnp.zeros_like(tc)
    e2 = jnp.zeros_like(tc)
    for a in range(PEER_TOPK):
        e1 = jnp.where(c1 == a, i1[a:a + 1], e1)
        e2 = jnp.where(c2 == a, i2[a:a + 1], e2)
    e_ref[...] = e1 * N_KEYS + e2
    ts = ts_scr[...]
    p = jnp.exp(ts - jnp.max(ts, axis=0, keepdims=True))
    g_ref[...] = p / jnp.sum(p, axis=0, keepdims=True)


def _peer_topk(q, k1_bf, k2_bf, layer):
    tm = 512
    row_f = pltpu.VMEM((PEER_TOPK, tm), F32)
    row_i = pltpu.VMEM((PEER_TOPK, tm), jnp.int32)
    return pl.pallas_call(
        functools.partial(_peer_topk_kernel, tm=tm),
        out_shape=(jax.ShapeDtypeStruct((PEER_HEADS * PEER_TOPK, N_TOK), jnp.int32),
                   jax.ShapeDtypeStruct((PEER_HEADS * PEER_TOPK, N_TOK), F32)),
        grid=(N_TOK // tm, PEER_HEADS),
        in_specs=[
            pl.BlockSpec((tm, PEER_QDIM), lambda i, h: (i, h)),
            pl.BlockSpec((None, N_KEYS, PEER_HALF), lambda i, h: (layer, 0, 0)),
            pl.BlockSpec((None, N_KEYS, PEER_HALF), lambda i, h: (layer, 0, 0)),
        ],
        out_specs=(pl.BlockSpec((PEER_TOPK, tm), lambda i, h: (h, i)),
                   pl.BlockSpec((PEER_TOPK, tm), lambda i, h: (h, i))),
        scratch_shapes=[pltpu.VMEM((N_KEYS, tm), F32), row_f, row_i, row_f, row_i, row_f, row_i],
        compiler_params=_params("arbitrary", "arbitrary"),
        name="peer_topk",
    )(q, k1_bf, k2_bf)


def _peer_mask_kernel(e_ref, g_ref, m_ref, et_scr, gt_scr, slab_scr, *, tb):
    et_scr[...] = e_ref[...].T
    gt_scr[...] = g_ref[...].T
    key = lax.broadcasted_iota(jnp.int32, (N_KEYS, PEER_HEADS * PEER_TOPK), 0)

    def body(t, carry):
        er = et_scr[pl.ds(t, 1), :]
        gr = gt_scr[pl.ds(t, 1), :]
        p1 = jnp.where(key == er // N_KEYS, gr, 0.0).astype(BF16)
        p2 = jnp.where(key == er % N_KEYS, 1.0, 0.0).astype(BF16)
        mt = lax.dot_general(p1, p2, NT_DIMS, preferred_element_type=F32)
        slab_scr[pl.ds(pl.multiple_of(t * MASK_PITCH, 8), N_KEYS), :] = mt
        return carry

    lax.fori_loop(0, tb, body, 0, unroll=32)
    for k1 in range(N_KEYS):
        m_ref[:, k1 * N_KEYS:(k1 + 1) * N_KEYS] = slab_scr[pl.ds(k1, tb, stride=MASK_PITCH), :].astype(BF16)


def _peer_mask(experts_t, gates_t):
    tb = LANES
    pairs = PEER_HEADS * PEER_TOPK
    return pl.pallas_call(
        functools.partial(_peer_mask_kernel, tb=tb),
        out_shape=jax.ShapeDtypeStruct((N_TOK, N_EXPERTS), BF16),
        grid=(N_TOK // tb,),
        in_specs=[pl.BlockSpec((pairs, tb), lambda i: (0, i)), pl.BlockSpec((pairs, tb), lambda i: (0, i))],
        out_specs=pl.BlockSpec((tb, N_EXPERTS), lambda i: (i, 0)),
        scratch_shapes=[pltpu.VMEM((tb, pairs), jnp.int32), pltpu.VMEM((tb, pairs), F32),
                        pltpu.VMEM((tb * MASK_PITCH, N_KEYS), F32)],
        compiler_params=_params("arbitrary"),
        name="peer_mask",
    )(experts_t, gates_t)


def _peer_expert_kernel(h_ref, u_ref, v_ref, m_ref, x_ref, g2_ref, o_ref, *, tm):
    j = pl.program_id(1)

    @pl.when(j == 0)
    def _():
        o_ref[...] = jnp.zeros_like(o_ref)

    s = lax.dot_general(h_ref[...], u_ref[...], NT_DIMS, preferred_element_type=F32)
    a = (m_ref[...].astype(F32) * jax.nn.gelu(s)).astype(BF16)
    o_ref[...] += _dot(a, v_ref[...])

    @pl.when(j == pl.num_programs(1) - 1)
    def _():
        r = _mod_row(pl.program_id(0) * tm)
        o_ref[...] = x_ref[...] + g2_ref[pl.ds(r, 1), :] * o_ref[...]


EXPERT_TM, EXPERT_TE = 1024, 512


def _peer_expert(h2, u_bf, v_bf, layer, mask, x1, mod):
    tm, te = EXPERT_TM, EXPERT_TE
    return pl.pallas_call(
        functools.partial(_peer_expert_kernel, tm=tm),
        out_shape=jax.ShapeDtypeStruct((N_TOK, D_MODEL), F32),
        grid=(N_TOK // tm, N_EXPERTS // te),
        in_specs=[
            pl.BlockSpec((tm, D_MODEL), lambda i, j: (i, 0)),
            pl.BlockSpec((None, te, D_MODEL), lambda i, j: (layer, j, 0)),
            pl.BlockSpec((None, te, D_MODEL), lambda i, j: (layer, j, 0)),
            pl.BlockSpec((tm, te), lambda i, j: (i, j)),
            pl.BlockSpec((tm, D_MODEL), lambda i, j: (i, 0)),
            pl.BlockSpec((MOD_ROWS, D_MODEL), lambda i, j: (0, 5)),
        ],
        out_specs=pl.BlockSpec((tm, D_MODEL), lambda i, j: (i, 0)),
        compiler_params=_params("arbitrary", "arbitrary"),
        name="peer_expert",
    )(h2, u_bf, v_bf, mask, x1, mod)


def kernel(x_prompt, x_sample, cache_k, cache_v, state_hgrn, c, c_ctx, ada_w, ada_b, norm1_g, norm2_g, w_in,
           q_norm_g, k_norm_g, hg_lb, hg_onorm_g, conv_w, conv_b, w_out, peer_wq, peer_k1, peer_k2, peer_u, peer_v):
    x = jnp.concatenate([x_prompt.reshape(N_PROMPT, D_MODEL), x_sample.reshape(N_SAMPLE, D_MODEL)], axis=0)
    cvec = jnp.concatenate([c_ctx[None, :], c, jnp.zeros((MOD_ROWS - 1 - DEC_BATCH, D_MODEL), F32)], axis=0)
    mod_all = _ada(cvec, ada_w, ada_b)

    w_in_bf, w_out_bf, wq_bf = w_in.astype(BF16), w_out.astype(BF16), peer_wq.astype(BF16)
    k1_bf, k2_bf = peer_k1.astype(BF16), peer_k2.astype(BF16)
    u_bf, v_bf = peer_u.astype(BF16), peer_v.astype(BF16)

    ks_out, vs_out, ss_out = [], [], []
    for l in range(DEPTH):
        mod = mod_all[l]
        z = _inproj(x, mod, norm1_g[l], w_in_bf, l)

        fn_p = _fourier(z, SEQ, BATCH, 0)
        fn_s = _fourier(z, DEC_SEQ, DEC_BATCH, N_PROMPT)

        at_p, k_l, v_l = _attn_prompt(z, q_norm_g[l], k_norm_g[l])
        k_ctx = cache_k[:, l].reshape(DEC_BATCH, PAST_LEN, N_KV * HEAD_DIM)
        v_ctx = cache_v[:, l].reshape(DEC_BATCH, PAST_LEN, N_KV * HEAD_DIM)
        at_s = _attn_sample(z, q_norm_g[l], k_norm_g[l], k_ctx, v_ctx)

        hg_p, s_l = _hgrn(z, hg_lb, hg_onorm_g[l], l, SEQ, BATCH, 0, None)
        hg_s, _ = _hgrn(z, hg_lb, hg_onorm_g[l], l, DEC_SEQ, DEC_BATCH, N_PROMPT, state_hgrn[:, l])

        cv_p = _conv(z, conv_w[l], conv_b[l], SEQ, BATCH, 0)
        cv_s = _conv(z, conv_w[l], conv_b[l], DEC_SEQ, DEC_BATCH, N_PROMPT)

        x1, h2 = _outproj((fn_p, at_p, hg_p, cv_p), (fn_s, at_s, hg_s, cv_s), w_out_bf, l, x, mod, norm2_g[l])

        q_peer = _peer_query(h2, wq_bf, l)
        experts_t, gates_t = _peer_topk(q_peer, k1_bf, k2_bf, l)
        mask = _peer_mask(experts_t, gates_t)
        x = _peer_expert(h2, u_bf, v_bf, l, mask, x1, mod)

        ks_out.append(k_l.reshape(BATCH, SEQ, N_KV, HEAD_DIM))
        vs_out.append(v_l.reshape(BATCH, SEQ, N_KV, HEAD_DIM))
        ss_out.append(s_l)

    y_prompt = x[:N_PROMPT].reshape(BATCH, SEQ, D_MODEL)
    y_sample = x[N_PROMPT:].reshape(DEC_BATCH, DEC_SEQ, D_MODEL)
    return (y_prompt, y_sample, jnp.stack(ks_out, axis=1), jnp.stack(vs_out, axis=1), jnp.stack(ss_out, axis=1))
```

```python
import functools

import numpy as np
import jax
import jax.numpy as jnp
from jax import lax
from jax.experimental import pallas as pl
from jax.experimental.pallas import tpu as pltpu

F32 = jnp.float32
BF16 = jnp.bfloat16

D_MODEL = 2048
BATCH = 32
SEQ = 256
DEPTH = 2
DEC_BATCH = 4
DEC_SEQ = 1024
PAST_LEN = 256
GRID_W = 64
GROUP_W = D_MODEL // 4
FN_GROUPS = 4
FN_GW = GROUP_W // FN_GROUPS
HEAD_DIM = 128
N_HEADS = GROUP_W // HEAD_DIM
N_KV = 2
GQA_GROUP = N_HEADS // N_KV
ROPE_THETA = 10000.0
HG_DK = 128
HG_DV = 128
HG_HEADS = GROUP_W // HG_DK
CONV_K = 3
IN_COLS = 11 * GROUP_W
PEER_HEADS = 8
N_KEYS = 128
N_EXPERTS = N_KEYS * N_KEYS
PEER_QDIM = 256
PEER_HALF = PEER_QDIM // 2
PEER_TOPK = 16
EPS = 1e-6
F_FLOOR = 1e-30

N_PROMPT = BATCH * SEQ
N_SAMPLE = DEC_BATCH * DEC_SEQ
N_TOK = N_PROMPT + N_SAMPLE
MOD_ROWS = 8
LANES = 128
HG_CHUNK = 32
HG_GROUP = 8
HG_CUM_BLOCK = 256
MASK_PITCH = 132
VMEM_LIMIT = 56 * 1024 * 1024

COL_FN, COL_Q, COL_K, COL_V = 0, 4, 8, 10
COL_GQ, COL_GFF, COL_GFB, COL_GI, COL_GO = 12, 16, 20, 24, 28
COL_CB, COL_CC, COL_CX = 32, 36, 40

NT_DIMS = (((1,), (1,)), ((), ()))
TN_DIMS = (((0,), (0,)), ((), ()))


def _params(*sem):
    return pltpu.CompilerParams(dimension_semantics=sem, vmem_limit_bytes=VMEM_LIMIT)


def _dot(a, b):
    return jnp.dot(a, b, preferred_element_type=F32)


def _split3(x):
    h = x.astype(BF16)
    r = x - h.astype(F32)
    m = r.astype(BF16)
    l = (r - m.astype(F32)).astype(BF16)
    return h, m, l


def _split2(x):
    h = x.astype(BF16)
    return h, (x - h.astype(F32)).astype(BF16)


def _mod_row(tok0):
    return jnp.where(tok0 >= N_PROMPT, (tok0 - N_PROMPT) // DEC_SEQ + 1, 0)


def _rms(x, g):
    return x * lax.rsqrt(jnp.mean(x * x, axis=-1, keepdims=True) + EPS) * g


def _ada_kernel(c_ref, w_ref, b_ref, o_ref):
    c = c_ref[...]
    s = (c * jax.nn.sigmoid(c)).astype(BF16)
    o_ref[0] = _dot(s, w_ref[0].astype(BF16)) + b_ref[0]


def _ada(cvec, ada_w, ada_b):
    tn = 1024
    return pl.pallas_call(
        _ada_kernel,
        out_shape=jax.ShapeDtypeStruct((DEPTH, MOD_ROWS, 6 * D_MODEL), F32),
        grid=(DEPTH, 6 * D_MODEL // tn),
        in_specs=[
            pl.BlockSpec((MOD_ROWS, D_MODEL), lambda l, j: (0, 0)),
            pl.BlockSpec((1, D_MODEL, tn), lambda l, j: (l, 0, j)),
            pl.BlockSpec((1, 1, tn), lambda l, j: (l, 0, j)),
        ],
        out_specs=pl.BlockSpec((1, MOD_ROWS, tn), lambda l, j: (l, 0, j)),
        compiler_params=_params("arbitrary", "arbitrary"),
        name="ada",
    )(cvec, ada_w, ada_b.reshape(DEPTH, 1, 6 * D_MODEL))


def _inproj_kernel(x_ref, sh_ref, sc_ref, g_ref, w_ref, o_ref, h_ref, *, tm):
    @pl.when(pl.program_id(1) == 0)
    def _():
        r = _mod_row(pl.program_id(0) * tm)
        y = _rms(x_ref[...], g_ref[...])
        h_ref[...] = (y * (1.0 + sc_ref[pl.ds(r, 1), :]) + sh_ref[pl.ds(r, 1), :]).astype(BF16)

    o_ref[...] = _dot(h_ref[...], w_ref[...])


def _inproj(x, mod, norm_g, w_in_bf, layer):
    tm, tn = 1024, 512
    return pl.pallas_call(
        functools.partial(_inproj_kernel, tm=tm),
        out_shape=jax.ShapeDtypeStruct((N_TOK, IN_COLS), F32),
        grid=(N_TOK // tm, IN_COLS // tn),
        in_specs=[
            pl.BlockSpec((tm, D_MODEL), lambda i, j: (i, 0)),
            pl.BlockSpec((MOD_ROWS, D_MODEL), lambda i, j: (0, 0)),
            pl.BlockSpec((MOD_ROWS, D_MODEL), lambda i, j: (0, 1)),
            pl.BlockSpec((1, D_MODEL), lambda i, j: (0, 0)),
            pl.BlockSpec((None, D_MODEL, tn), lambda i, j: (layer, 0, j)),
        ],
        out_specs=pl.BlockSpec((tm, tn), lambda i, j: (i, j)),
        scratch_shapes=[pltpu.VMEM((tm, D_MODEL), BF16)],
        compiler_params=_params("arbitrary", "arbitrary"),
        name="inproj",
    )(x, mod, mod, norm_g.reshape(1, D_MODEL), w_in_bf)


def _fourier_kernel(u_ref, cch_ref, ccl_ref, cth_ref, ctl_ref, o_ref):
    cch, ccl, cth, ctl = cch_ref[...], ccl_ref[...], cth_ref[...], ctl_ref[...]
    for g in range(FN_GROUPS):
        u = u_ref[:, g * FN_GW:(g + 1) * FN_GW]
        uh, ul = _split2(u)
        w = _dot(uh, cch) + (_dot(ul, cch) + _dot(uh, ccl))
        st = jnp.concatenate([w[:, :FN_GW], w[:, FN_GW:]], axis=0)
        sh, sl = _split2(st)
        y = _dot(cth, sh) + (_dot(cth, sl) + _dot(ctl, sh))
        o_ref[:, g * FN_GW:(g + 1) * FN_GW] = y


def _dft_consts(t):
    def cs(n):
        k = np.arange(n, dtype=np.float64)
        ang = 2.0 * np.pi * np.outer(k, k) / n
        return np.cos(ang) / np.sqrt(n), np.sin(ang) / np.sqrt(n)

    cc, sc = cs(FN_GW)
    ct, st = cs(t)
    right = np.concatenate([cc, sc], axis=1).astype(np.float32)
    left = np.concatenate([ct, -st], axis=1).astype(np.float32)

    def hl(a):
        a = jnp.asarray(a)
        h = a.astype(BF16)
        return h, (a - h.astype(F32)).astype(BF16)

    return hl(right) + hl(left)


def _fourier(z, t, nb, row0):
    cch, ccl, cth, ctl = _dft_consts(t)
    rb = row0 // t
    return pl.pallas_call(
        _fourier_kernel,
        out_shape=jax.ShapeDtypeStruct((nb * t, GROUP_W), F32),
        grid=(nb,),
        in_specs=[
            pl.BlockSpec((t, GROUP_W), lambda b: (rb + b, COL_FN // 4)),
            pl.BlockSpec((FN_GW, 2 * FN_GW), lambda b: (0, 0)),
            pl.BlockSpec((FN_GW, 2 * FN_GW), lambda b: (0, 0)),
            pl.BlockSpec((t, 2 * t), lambda b: (0, 0)),
            pl.BlockSpec((t, 2 * t), lambda b: (0, 0)),
        ],
        out_specs=pl.BlockSpec((t, GROUP_W), lambda b: (b, 0)),
        compiler_params=_params("arbitrary"),
        name=f"fourier_t{t}",
    )(z, cch, ccl, cth, ctl)


def _softmax_pv(q_bf, k_bf, v_bf):
    s = lax.dot_general(q_bf, k_bf, NT_DIMS, preferred_element_type=F32) * (HEAD_DIM ** -0.5)
    p = jnp.exp(s - jnp.max(s, axis=-1, keepdims=True))
    return _dot(p.astype(BF16), v_bf) / jnp.sum(p, axis=-1, keepdims=True)


def _attn_prompt_kernel(q_ref, k_ref, v_ref, qg_ref, kg_ref, o_ref, kc_ref, vc_ref):
    kn = _rms(k_ref[...], kg_ref[...])
    v = v_ref[...]
    kc_ref[...] = kn
    vc_ref[...] = v
    kb, vb = kn.astype(BF16), v.astype(BF16)
    for g in range(GQA_GROUP):
        qn = _rms(q_ref[:, g * HEAD_DIM:(g + 1) * HEAD_DIM], qg_ref[...])
        o_ref[:, g * HEAD_DIM:(g + 1) * HEAD_DIM] = _softmax_pv(qn.astype(BF16), kb, vb)


def _attn_prompt(z, q_g, k_g):
    t = SEQ
    kv_shape = jax.ShapeDtypeStruct((N_PROMPT, N_KV * HEAD_DIM), F32)
    return pl.pallas_call(
        _attn_prompt_kernel,
        out_shape=(jax.ShapeDtypeStruct((N_PROMPT, GROUP_W), F32), kv_shape, kv_shape),
        grid=(BATCH, N_KV),
        in_specs=[
            pl.BlockSpec((t, GQA_GROUP * HEAD_DIM), lambda b, h: (b, COL_Q // 2 + h)),
            pl.BlockSpec((t, HEAD_DIM), lambda b, h: (b, COL_K + h)),
            pl.BlockSpec((t, HEAD_DIM), lambda b, h: (b, COL_V + h)),
            pl.BlockSpec((1, HEAD_DIM), lambda b, h: (0, 0)),
            pl.BlockSpec((1, HEAD_DIM), lambda b, h: (0, 0)),
        ],
        out_specs=(
            pl.BlockSpec((t, GQA_GROUP * HEAD_DIM), lambda b, h: (b, h)),
            pl.BlockSpec((t, HEAD_DIM), lambda b, h: (b, h)),
            pl.BlockSpec((t, HEAD_DIM), lambda b, h: (b, h)),
        ),
        compiler_params=_params("arbitrary", "arbitrary"),
        name="attn_prompt",
    )(z, z, z, q_g.reshape(1, HEAD_DIM), k_g.reshape(1, HEAD_DIM))


def _rope(x, cosf, sinf):
    return x * cosf + pltpu.roll(x, HEAD_DIM // 2, axis=1) * sinf


def _attn_sample_kernel(q_ref, k_ref, v_ref, kctx_ref, vctx_ref, qg_ref, kg_ref, cos_ref, sin_ref,
                        o_ref, kall_ref, vall_ref, *, qb):
    cosf, sinf = cos_ref[...], sin_ref[...]
    kall_ref[0:PAST_LEN, :] = kctx_ref[0].astype(BF16)
    vall_ref[0:PAST_LEN, :] = vctx_ref[0].astype(BF16)
    kall_ref[PAST_LEN:, :] = _rope(_rms(k_ref[...], kg_ref[...]), cosf, sinf).astype(BF16)
    vall_ref[PAST_LEN:, :] = v_ref[...].astype(BF16)
    kb, vb = kall_ref[...], vall_ref[...]
    for i in range(DEC_SEQ // qb):
        rows = slice(i * qb, (i + 1) * qb)
        for g in range(GQA_GROUP):
            cols = slice(g * HEAD_DIM, (g + 1) * HEAD_DIM)
            qn = _rope(_rms(q_ref[rows, cols], qg_ref[...]), cosf[rows], sinf[rows])
            o_ref[rows, cols] = _softmax_pv(qn.astype(BF16), kb, vb)


def _rope_tables():
    rows = DEC_SEQ // GRID_W
    row = np.repeat(np.arange(rows, dtype=np.float32), GRID_W)
    col = np.tile(np.arange(GRID_W, dtype=np.float32), rows)
    n_freq = HEAD_DIM // 4
    inv = (np.float32(ROPE_THETA) ** (-np.arange(n_freq, dtype=np.float32) / n_freq)).astype(np.float32)
    ang = np.concatenate([row[:, None] * inv, col[:, None] * inv], axis=-1).astype(np.float32)
    cos, sin = np.cos(ang.astype(np.float64)), np.sin(ang.astype(np.float64))
    cosf = np.concatenate([cos, cos], axis=-1).astype(np.float32)
    sinf = np.concatenate([-sin, sin], axis=-1).astype(np.float32)
    return jnp.asarray(cosf), jnp.asarray(sinf)


def _attn_sample(z, q_g, k_g, k_ctx, v_ctx):
    t = DEC_SEQ
    rb = N_PROMPT // t
    cosf, sinf = _rope_tables()
    const = lambda b, h: (0, 0)
    return pl.pallas_call(
        functools.partial(_attn_sample_kernel, qb=256),
        out_shape=jax.ShapeDtypeStruct((N_SAMPLE, GROUP_W), F32),
        grid=(DEC_BATCH, N_KV),
        in_specs=[
            pl.BlockSpec((t, GQA_GROUP * HEAD_DIM), lambda b, h: (rb + b, COL_Q // 2 + h)),
            pl.BlockSpec((t, HEAD_DIM), lambda b, h: (rb + b, COL_K + h)),
            pl.BlockSpec((t, HEAD_DIM), lambda b, h: (rb + b, COL_V + h)),
            pl.BlockSpec((1, PAST_LEN, HEAD_DIM), lambda b, h: (b, 0, h)),
            pl.BlockSpec((1, PAST_LEN, HEAD_DIM), lambda b, h: (b, 0, h)),
            pl.BlockSpec((1, HEAD_DIM), const),
            pl.BlockSpec((1, HEAD_DIM), const),
            pl.BlockSpec((t, HEAD_DIM), const),
            pl.BlockSpec((t, HEAD_DIM), const),
        ],
        out_specs=pl.BlockSpec((t, GQA_GROUP * HEAD_DIM), lambda b, h: (b, h)),
        scratch_shapes=[pltpu.VMEM((PAST_LEN + t, HEAD_DIM), BF16), pltpu.VMEM((PAST_LEN + t, HEAD_DIM), BF16)],
        compiler_params=_params("arbitrary", "arbitrary"),
        name="attn_sample",
    )(z, z, z, k_ctx, v_ctx, q_g.reshape(1, HEAD_DIM), k_g.reshape(1, HEAD_DIM), cosf, sinf)


def _hgrn_chain(rev, r0, gq_ref, gi_ref, k_scr, cum_scr, st_scr, o_scr):
    c = HG_CHUNK
    d = 1 if rev else 0
    rows = pl.ds(r0, c)
    qc, vc = gq_ref[rows, :], gi_ref[rows, :]
    kc, cc = k_scr[d, rows, :], cum_scr[d, rows, :]
    st = st_scr[d]
    last = cc[0:1] if rev else cc[c - 1:c]
    qs = (qc * jnp.exp(cc)).astype(BF16)
    o_inter = lax.dot_general(qs, st.astype(BF16), NT_DIMS, preferred_element_type=F32)
    ks = (kc * jnp.exp(last - cc)).astype(BF16)
    upd = lax.dot_general(vc.astype(BF16), ks, TN_DIMS, preferred_element_type=F32)
    st_scr[d] = st * jnp.exp(last) + upd
    n_grp = c // HG_GROUP
    row = lax.broadcasted_iota(jnp.int32, (HG_GROUP, 1), 0)
    krow = lax.broadcasted_iota(jnp.int32, (c, 1), 0)
    diag, att = [], []
    for i in range(n_grp):
        lo = i * HG_GROUP
        grp = slice(lo, lo + HG_GROUP)
        q_i, k_i, v_i, c_i = qc[grp], kc[grp], vc[grp], cc[grp]
        acc = jnp.zeros((HG_GROUP, HG_DV), F32)
        for delta in range(HG_GROUP):
            if delta == 0:
                k_sh, c_sh, v_sh = k_i, c_i, v_i
            else:
                sh = (HG_GROUP - delta) if rev else delta
                k_sh, c_sh, v_sh = (pltpu.roll(a, sh, axis=0) for a in (k_i, c_i, v_i))
            a = jnp.sum(q_i * k_sh * jnp.exp(jnp.minimum(c_i - c_sh, 0.0)), axis=-1, keepdims=True)
            valid = (row <= HG_GROUP - 1 - delta) if rev else (row >= delta)
            acc = acc + jnp.where(valid, a, 0.0) * v_sh
        diag.append(acc)
        if (i == n_grp - 1) if rev else (i == 0):
            att.append(jnp.zeros((HG_GROUP, c), F32))
            continue
        if rev:
            b, other = cc[lo + HG_GROUP:lo + HG_GROUP + 1], krow >= lo + HG_GROUP
        else:
            b, other = cc[lo - 1:lo], krow < lo
        qs_i = (q_i * jnp.exp(c_i - b)).astype(BF16)
        ks_i = jnp.where(other, kc * jnp.exp(jnp.minimum(b - cc, 0.0)), 0.0).astype(BF16)
        att.append(lax.dot_general(qs_i, ks_i, NT_DIMS, preferred_element_type=F32))
    o_off = _dot(jnp.concatenate(att, axis=0).astype(BF16), vc.astype(BF16))
    o_scr[d, rows, :] = o_inter + jnp.concatenate(diag, axis=0) + o_off


def _hgrn_kernel(*refs, layer, t, has_ctx):
    gq_ref, gff_ref, gfb_ref, gi_ref, go_ref, lb_ref, on_ref, bdf_ref, bdb_ref = refs[:9]
    n_in = 9
    s0_ref = None
    if has_ctx:
        s0_ref = refs[9]
        n_in = 10
    o_ref, sout_ref = refs[n_in], refs[n_in + 1]
    k_scr, cum_scr, st_scr, o_scr = refs[n_in + 2:]

    raw = lb_ref[...]
    e = jnp.exp(raw - jnp.max(raw, axis=0, keepdims=True))
    soft = e / jnp.sum(e, axis=0, keepdims=True)
    csum = soft[0]
    for i in range(1, layer + 1):
        csum = csum + soft[i]
    lb = csum - soft[0]

    for d, (g_ref, bd_ref) in enumerate(((gff_ref, bdf_ref), (gfb_ref, bdb_ref))):
        lbd = lb[d:d + 1]
        f = jnp.maximum(lbd + (1.0 - lbd) * jax.nn.sigmoid(g_ref[...]), F_FLOOR)
        k_scr[d] = 1.0 - f
        lf = jnp.log(f)
        bd = bd_ref[...]
        for p in range(t // HG_CUM_BLOCK):
            rows = slice(p * HG_CUM_BLOCK, (p + 1) * HG_CUM_BLOCK)
            h, m, l = _split3(lf[rows])
            cum_scr[d, rows, :] = _dot(bd, h) + _dot(bd, m) + _dot(bd, l)
        if has_ctx:
            st_scr[d] = s0_ref[0, d, 0].T
        else:
            st_scr[d] = jnp.zeros((HG_DV, HG_DK), F32)

    n = t // HG_CHUNK

    def body(i, carry):
        _hgrn_chain(False, pl.multiple_of(i * HG_CHUNK, HG_CHUNK), gq_ref, gi_ref, k_scr, cum_scr, st_scr, o_scr)
        _hgrn_chain(True, pl.multiple_of((n - 1 - i) * HG_CHUNK, HG_CHUNK), gq_ref, gi_ref, k_scr, cum_scr,
                    st_scr, o_scr)
        return carry

    lax.fori_loop(0, n, body, 0, unroll=4)

    o = _rms(o_scr[0] + o_scr[1], on_ref[...])
    go = go_ref[...]
    o_ref[...] = o * (go * jax.nn.sigmoid(go))
    for d in range(2):
        sout_ref[0, d, 0] = st_scr[d].T


def _cum_consts():
    r = np.arange(HG_CUM_BLOCK)
    same = (r[:, None] // HG_CHUNK) == (r[None, :] // HG_CHUNK)
    fwd = same & (r[None, :] <= r[:, None])
    bwd = same & (r[None, :] >= r[:, None])
    return jnp.asarray(fwd, BF16), jnp.asarray(bwd, BF16)


def _hgrn(z, hg_lb, onorm_g, layer, t, nb, row0, s0):
    rb = row0 // t
    bdf, bdb = _cum_consts()
    has_ctx = s0 is not None
    col = lambda c0: (lambda b, h: (rb + b, c0 + h))
    in_specs = [
        pl.BlockSpec((t, HG_DK), col(COL_GQ)),
        pl.BlockSpec((t, HG_DK), col(COL_GFF)),
        pl.BlockSpec((t, HG_DK), col(COL_GFB)),
        pl.BlockSpec((t, HG_DV), col(COL_GI)),
        pl.BlockSpec((t, HG_DV), col(COL_GO)),
        pl.BlockSpec((DEPTH, 2, HG_DK), lambda b, h: (0, 0, h)),
        pl.BlockSpec((1, HG_DV), lambda b, h: (0, 0)),
        pl.BlockSpec((HG_CUM_BLOCK, HG_CUM_BLOCK), lambda b, h: (0, 0)),
        pl.BlockSpec((HG_CUM_BLOCK, HG_CUM_BLOCK), lambda b, h: (0, 0)),
    ]
    args = [z, z, z, z, z, hg_lb, onorm_g.reshape(1, HG_DV), bdf, bdb]
    if has_ctx:
        in_specs.append(pl.BlockSpec((1, 2, 1, HG_DK, HG_DV), lambda b, h: (b, 0, h, 0, 0)))
        args.append(s0)
    return pl.pallas_call(
        functools.partial(_hgrn_kernel, layer=layer, t=t, has_ctx=has_ctx),
        out_shape=(jax.ShapeDtypeStruct((nb * t, GROUP_W), F32),
                   jax.ShapeDtypeStruct((nb, 2, HG_HEADS, HG_DK, HG_DV), F32)),
        grid=(nb, HG_HEADS),
        in_specs=in_specs,
        out_specs=(pl.BlockSpec((t, HG_DV), lambda b, h: (b, h)),
                   pl.BlockSpec((1, 2, 1, HG_DK, HG_DV), lambda b, h: (b, 0, h, 0, 0))),
        scratch_shapes=[pltpu.VMEM((2, t, HG_DK), F32), pltpu.VMEM((2, t, HG_DK), F32),
                        pltpu.VMEM((2, HG_DV, HG_DK), F32), pltpu.VMEM((2, t, HG_DV), F32)],
        compiler_params=_params("arbitrary", "arbitrary"),
        name=f"hgrn_t{t}",
    )(*args)


def _conv_kernel(cb_ref, cc_ref, cx_ref, w_ref, b_ref, o_ref, *, t):
    p = cc_ref[...] * cx_ref[...]
    row = lax.broadcasted_iota(jnp.int32, (t, 1), 0)
    prev = jnp.where(row >= 1, pltpu.roll(p, 1, axis=0), 0.0)
    nxt = jnp.where(row <= t - 2, pltpu.roll(p, t - 1, axis=0), 0.0)
    w = w_ref[...]
    y = prev * w[0:1] + p * w[1:2] + nxt * w[2:3] + b_ref[...]
    o_ref[...] = cb_ref[...] * y


def _conv(z, w, b, t, nb, row0):
    rb = row0 // t
    return pl.pallas_call(
        functools.partial(_conv_kernel, t=t),
        out_shape=jax.ShapeDtypeStruct((nb * t, GROUP_W), F32),
        grid=(nb,),
        in_specs=[
            pl.BlockSpec((t, GROUP_W), lambda i: (rb + i, COL_CB // 4)),
            pl.BlockSpec((t, GROUP_W), lambda i: (rb + i, COL_CC // 4)),
            pl.BlockSpec((t, GROUP_W), lambda i: (rb + i, COL_CX // 4)),
            pl.BlockSpec((CONV_K, GROUP_W), lambda i: (0, 0)),
            pl.BlockSpec((1, GROUP_W), lambda i: (0, 0)),
        ],
        out_specs=pl.BlockSpec((t, GROUP_W), lambda i: (i, 0)),
        compiler_params=_params("arbitrary"),
        name=f"conv_t{t}",
    )(z, z, z, w, b.reshape(1, GROUP_W))


def _outproj_kernel(*refs, tm):
    prompt_refs, sample_refs = refs[0:4], refs[4:8]
    w_ref, x_ref, g1_ref, sh_ref, sc_ref, n2_ref, x1_ref, h2_ref = refs[8:]
    i = pl.program_id(0)
    r = _mod_row(i * tm)

    def run(group_refs):
        acc = None
        for g, ref in enumerate(group_refs):
            part = _dot(ref[...].astype(BF16), w_ref[g * GROUP_W:(g + 1) * GROUP_W, :])
            acc = part if acc is None else acc + part
        x1 = x_ref[...] + g1_ref[pl.ds(r, 1), :] * acc
        x1_ref[...] = x1
        h2 = _rms(x1, n2_ref[...]) * (1.0 + sc_ref[pl.ds(r, 1), :]) + sh_ref[pl.ds(r, 1), :]
        h2_ref[...] = h2.astype(BF16)

    pl.when(i < N_PROMPT // tm)(lambda: run(prompt_refs))
    pl.when(i >= N_PROMPT // tm)(lambda: run(sample_refs))


def _outproj(prompt_groups, sample_groups, w_out_bf, layer, x, mod, norm2_g):
    tm = 512
    n_pb, n_sb = N_PROMPT // tm, N_SAMPLE // tm
    p_spec = pl.BlockSpec((tm, GROUP_W), lambda i: (jnp.minimum(i, n_pb - 1), 0))
    s_spec = pl.BlockSpec((tm, GROUP_W), lambda i: (jnp.clip(i - n_pb, 0, n_sb - 1), 0))
    modspec = lambda k: pl.BlockSpec((MOD_ROWS, D_MODEL), lambda i: (0, k))
    return pl.pallas_call(
        functools.partial(_outproj_kernel, tm=tm),
        out_shape=(jax.ShapeDtypeStruct((N_TOK, D_MODEL), F32), jax.ShapeDtypeStruct((N_TOK, D_MODEL), BF16)),
        grid=(N_TOK // tm,),
        in_specs=[p_spec] * 4 + [s_spec] * 4 + [
            pl.BlockSpec((None, D_MODEL, D_MODEL), lambda i: (layer, 0, 0)),
            pl.BlockSpec((tm, D_MODEL), lambda i: (i, 0)),
            modspec(2), modspec(3), modspec(4),
            pl.BlockSpec((1, D_MODEL), lambda i: (0, 0))],
        out_specs=(pl.BlockSpec((tm, D_MODEL), lambda i: (i, 0)), pl.BlockSpec((tm, D_MODEL), lambda i: (i, 0))),
        compiler_params=_params("arbitrary"),
        name="outproj",
    )(*prompt_groups, *sample_groups, w_out_bf, x, mod, mod, mod, norm2_g.reshape(1, D_MODEL))


def _peer_query_kernel(h_ref, wq_ref, q_ref):
    q_ref[...] = _dot(h_ref[...], wq_ref[...]).astype(BF16)


def _peer_query(h2, wq_bf, layer):
    tm = 512
    width = PEER_HEADS * PEER_QDIM
    return pl.pallas_call(
        _peer_query_kernel,
        out_shape=jax.ShapeDtypeStruct((N_TOK, width), BF16),
        grid=(N_TOK // tm,),
        in_specs=[pl.BlockSpec((tm, D_MODEL), lambda i: (i, 0)),
                  pl.BlockSpec((None, D_MODEL, width), lambda i: (layer, 0, 0))],
        out_specs=pl.BlockSpec((tm, width), lambda i: (i, 0)),
        compiler_params=_params("arbitrary"),
        name="peer_query",
    )(h2, wq_bf)


_CAND_PIECES = ((0, 0, 8), (0, 8, 8), (1, 0, 8), (2, 0, 5), (3, 0, 4), (4, 0, 3), (5, 0, 2), (6, 0, 2), (7, 0, 2),
                (None, 0, 8))
_CAND_ROWS = 8 * len(_CAND_PIECES)
_NO_CAND = 1 << 20


def _topk_rows(x_scr, ids, n_rows, v_out, i_out):
    def body(j, carry):
        x = x_scr[0:n_rows, :]
        m = jnp.max(x, axis=0, keepdims=True)
        idx = jnp.min(jnp.where(x == m, ids, _NO_CAND), axis=0, keepdims=True)
        v_out[pl.ds(j, 1), :] = m
        i_out[pl.ds(j, 1), :] = idx
        x_scr[0:n_rows, :] = jnp.where(ids == idx, -jnp.inf, x)
        return carry

    lax.fori_loop(0, PEER_TOPK, body, 0, unroll=2)


def _peer_topk_kernel(q_ref, k1_ref, k2_ref, e_ref, g_ref, x_scr, v1_scr, i1_scr, v2_scr, i2_scr,
                      ts_scr, tc_scr, *, tm):
    key_ids = lax.broadcasted_iota(jnp.int32, (N_KEYS, tm), 0)
    for half, (k_ref, v_scr, i_scr) in enumerate(((k1_ref, v1_scr, i1_scr), (k2_ref, v2_scr, i2_scr))):
        qh = q_ref[:, half * PEER_HALF:(half + 1) * PEER_HALF]
        x_scr[...] = lax.dot_general(k_ref[...], qh, NT_DIMS, preferred_element_type=F32)
        _topk_rows(x_scr, key_ids, N_KEYS, v_scr, i_scr)
    v1, v2 = v1_scr[...], v2_scr[...]
    sub = lax.broadcasted_iota(jnp.int32, (8, tm), 0)
    cand_ids = []
    for p, (r1, r2, n_valid) in enumerate(_CAND_PIECES):
        if r1 is None:
            vals = v1[8:16] + v2[0:1]
            ids = (sub + 8) * PEER_TOPK
        else:
            vals = v1[r1:r1 + 1] + v2[r2:r2 + 8]
            ids = sub + (r1 * PEER_TOPK + r2)
            if n_valid < 8:
                vals = jnp.where(sub < n_valid, vals, -jnp.inf)
                ids = jnp.where(sub < n_valid, ids, _NO_CAND)
        x_scr[p * 8:(p + 1) * 8, :] = vals
        cand_ids.append(ids)
    _topk_rows(x_scr, jnp.concatenate(cand_ids, axis=0), _CAND_ROWS, ts_scr, tc_scr)
    tc = tc_scr[...]
    c1, c2 = tc // PEER_TOPK, tc % PEER_TOPK
    i1, i2 = i1_scr[...], i2_scr[...]
    e1 = jnp.zeros_like(tc)
    e2 = jnp.zeros_like(tc)
    for a in range(PEER_TOPK):
        e1 = jnp.where(c1 == a, i1[a:a + 1], e1)
        e2 = jnp.where(c2 == a, i2[a:a + 1], e2)
    e_ref[...] = e1 * N_KEYS + e2
    ts = ts_scr[...]
    p = jnp.exp(ts - jnp.max(ts, axis=0, keepdims=True))
    g_ref[...] = p / jnp.sum(p, axis=0, keepdims=True)


def _peer_topk(q, k1_bf, k2_bf, layer):
    tm = 512
    row_f = pltpu.VMEM((PEER_TOPK, tm), F32)
    row_i = pltpu.VMEM((PEER_TOPK, tm), jnp.int32)
    return pl.pallas_call(
        functools.partial(_peer_topk_kernel, tm=tm),
        out_shape=(jax.ShapeDtypeStruct((PEER_HEADS * PEER_TOPK, N_TOK), jnp.int32),
                   jax.ShapeDtypeStruct((PEER_HEADS * PEER_TOPK, N_TOK), F32)),
        grid=(N_TOK // tm, PEER_HEADS),
        in_specs=[
            pl.BlockSpec((tm, PEER_QDIM), lambda i, h: (i, h)),
            pl.BlockSpec((None, N_KEYS, PEER_HALF), lambda i, h: (layer, 0, 0)),
            pl.BlockSpec((None, N_KEYS, PEER_HALF), lambda i, h: (layer, 0, 0)),
        ],
        out_specs=(pl.BlockSpec((PEER_TOPK, tm), lambda i, h: (h, i)),
                   pl.BlockSpec((PEER_TOPK, tm), lambda i, h: (h, i))),
        scratch_shapes=[pltpu.VMEM((N_KEYS, tm), F32), row_f, row_i, row_f, row_i, row_f, row_i],
        compiler_params=_params("arbitrary", "arbitrary"),
        name="peer_topk",
    )(q, k1_bf, k2_bf)


def _peer_mask_kernel(e_ref, g_ref, m_ref, et_scr, gt_scr, slab_scr, *, tb):
    et_scr[...] = e_ref[...].T
    gt_scr[...] = g_ref[...].T
    key = lax.broadcasted_iota(jnp.int32, (N_KEYS, PEER_HEADS * PEER_TOPK), 0)

    def body(t, carry):
        er = et_scr[pl.ds(t, 1), :]
        gr = gt_scr[pl.ds(t, 1), :]
        p1 = jnp.where(key == er // N_KEYS, gr, 0.0).astype(BF16)
        p2 = jnp.where(key == er % N_KEYS, 1.0, 0.0).astype(BF16)
        mt = lax.dot_general(p1, p2, NT_DIMS, preferred_element_type=F32)
        slab_scr[pl.ds(pl.multiple_of(t * MASK_PITCH, 4), N_KEYS), :] = mt
        return carry

    lax.fori_loop(0, tb, body, 0, unroll=64)
    for k1 in range(N_KEYS):
        m_ref[:, k1 * N_KEYS:(k1 + 1) * N_KEYS] = slab_scr[pl.ds(k1, tb, stride=MASK_PITCH), :].astype(BF16)


def _peer_mask(experts_t, gates_t):
    tb = LANES
    pairs = PEER_HEADS * PEER_TOPK
    return pl.pallas_call(
        functools.partial(_peer_mask_kernel, tb=tb),
        out_shape=jax.ShapeDtypeStruct((N_TOK, N_EXPERTS), BF16),
        grid=(N_TOK // tb,),
        in_specs=[pl.BlockSpec((pairs, tb), lambda i: (0, i)), pl.BlockSpec((pairs, tb), lambda i: (0, i))],
        out_specs=pl.BlockSpec((tb, N_EXPERTS), lambda i: (i, 0)),
        scratch_shapes=[pltpu.VMEM((tb, pairs), jnp.int32), pltpu.VMEM((tb, pairs), F32),
                        pltpu.VMEM((tb * MASK_PITCH, N_KEYS), F32)],
        compiler_params=_params("arbitrary"),
        name="peer_mask",
    )(experts_t, gates_t)


def _peer_expert_kernel(h_ref, u_ref, v_ref, m_ref, x_ref, g2_ref, o_ref, *, tm):
    j = pl.program_id(1)

    @pl.when(j == 0)
    def _():
        o_ref[...] = jnp.zeros_like(o_ref)

    s = lax.dot_general(h_ref[...], u_ref[...], NT_DIMS, preferred_element_type=F32)
    a = (m_ref[...].astype(F32) * jax.nn.gelu(s)).astype(BF16)
    o_ref[...] += _dot(a, v_ref[...])

    @pl.when(j == pl.num_programs(1) - 1)
    def _():
        r = _mod_row(pl.program_id(0) * tm)
        o_ref[...] = x_ref[...] + g2_ref[pl.ds(r, 1), :] * o_ref[...]


EXPERT_TM, EXPERT_TE = 1024, 512


def _peer_expert(h2, u_bf, v_bf, layer, mask, x1, mod):
    tm, te = EXPERT_TM, EXPERT_TE
    return pl.pallas_call(
        functools.partial(_peer_expert_kernel, tm=tm),
        out_shape=jax.ShapeDtypeStruct((N_TOK, D_MODEL), F32),
        grid=(N_TOK // tm, N_EXPERTS // te),
        in_specs=[
            pl.BlockSpec((tm, D_MODEL), lambda i, j: (i, 0)),
            pl.BlockSpec((None, te, D_MODEL), lambda i, j: (layer, j, 0)),
            pl.BlockSpec((None, te, D_MODEL), lambda i, j: (layer, j, 0)),
            pl.BlockSpec((tm, te), lambda i, j: (i, j)),
            pl.BlockSpec((tm, D_MODEL), lambda i, j: (i, 0)),
            pl.BlockSpec((MOD_ROWS, D_MODEL), lambda i, j: (0, 5)),
        ],
        out_specs=pl.BlockSpec((tm, D_MODEL), lambda i, j: (i, 0)),
        compiler_params=_params("arbitrary", "arbitrary"),
        name="peer_expert",
    )(h2, u_bf, v_bf, mask, x1, mod)


def kernel(x_prompt, x_sample, cache_k, cache_v, state_hgrn, c, c_ctx, ada_w, ada_b, norm1_g, norm2_g, w_in,
           q_norm_g, k_norm_g, hg_lb, hg_onorm_g, conv_w, conv_b, w_out, peer_wq, peer_k1, peer_k2, peer_u, peer_v):
    x = jnp.concatenate([x_prompt.reshape(N_PROMPT, D_MODEL), x_sample.reshape(N_SAMPLE, D_MODEL)], axis=0)
    cvec = jnp.concatenate([c_ctx[None, :], c, jnp.zeros((MOD_ROWS - 1 - DEC_BATCH, D_MODEL), F32)], axis=0)
    mod_all = _ada(cvec, ada_w, ada_b)

    w_in_bf, w_out_bf, wq_bf = w_in.astype(BF16), w_out.astype(BF16), peer_wq.astype(BF16)
    k1_bf, k2_bf = peer_k1.astype(BF16), peer_k2.astype(BF16)
    u_bf, v_bf = peer_u.astype(BF16), peer_v.astype(BF16)

    ks_out, vs_out, ss_out = [], [], []
    for l in range(DEPTH):
        mod = mod_all[l]
        z = _inproj(x, mod, norm1_g[l], w_in_bf, l)

        fn_p = _fourier(z, SEQ, BATCH, 0)
        fn_s = _fourier(z, DEC_SEQ, DEC_BATCH, N_PROMPT)

        at_p, k_l, v_l = _attn_prompt(z, q_norm_g[l], k_norm_g[l])
        k_ctx = cache_k[:, l].reshape(DEC_BATCH, PAST_LEN, N_KV * HEAD_DIM)
        v_ctx = cache_v[:, l].reshape(DEC_BATCH, PAST_LEN, N_KV * HEAD_DIM)
        at_s = _attn_sample(z, q_norm_g[l], k_norm_g[l], k_ctx, v_ctx)

        hg_p, s_l = _hgrn(z, hg_lb, hg_onorm_g[l], l, SEQ, BATCH, 0, None)
        hg_s, _ = _hgrn(z, hg_lb, hg_onorm_g[l], l, DEC_SEQ, DEC_BATCH, N_PROMPT, state_hgrn[:, l])

        cv_p = _conv(z, conv_w[l], conv_b[l], SEQ, BATCH, 0)
        cv_s = _conv(z, conv_w[l], conv_b[l], DEC_SEQ, DEC_BATCH, N_PROMPT)

        x1, h2 = _outproj((fn_p, at_p, hg_p, cv_p), (fn_s, at_s, hg_s, cv_s), w_out_bf, l, x, mod, norm2_g[l])

        q_peer = _peer_query(h2, wq_bf, l)
        experts_t, gates_t = _peer_topk(q_peer, k1_bf, k2_bf, l)
        mask = _peer_mask(experts_t, gates_t)
        x = _peer_expert(h2, u_bf, v_bf, l, mask, x1, mod)

        ks_out.append(k_l.reshape(BATCH, SEQ, N_KV, HEAD_DIM))
        vs_out.append(v_l.reshape(BATCH, SEQ, N_KV, HEAD_DIM))
        ss_out.append(s_l)

    y_prompt = x[:N_PROMPT].reshape(BATCH, SEQ, D_MODEL)
    y_sample = x[N_PROMPT:].reshape(DEC_BATCH, DEC_SEQ, D_MODEL)
    return (y_prompt, y_sample, jnp.stack(ks_out, axis=1), jnp.stack(vs_out, axis=1), jnp.stack(ss_out, axis=1))
```

```python
import functools

import numpy as np
import jax
import jax.numpy as jnp
from jax import lax
from jax.experimental import pallas as pl
from jax.experimental.pallas import tpu as pltpu

F32 = jnp.float32
BF16 = jnp.bfloat16

D_MODEL = 2048
BATCH = 32
SEQ = 256
DEPTH = 2
DEC_BATCH = 4
DEC_SEQ = 1024
PAST_LEN = 256
GRID_W = 64
GROUP_W = D_MODEL // 4
FN_GROUPS = 4
FN_GW = GROUP_W // FN_GROUPS
HEAD_DIM = 128
N_HEADS = GROUP_W // HEAD_DIM
N_KV = 2
GQA_GROUP = N_HEADS // N_KV
ROPE_THETA = 10000.0
HG_DK = 128
HG_DV = 128
HG_HEADS = GROUP_W // HG_DK
CONV_K = 3
IN_COLS = 11 * GROUP_W
PEER_HEADS = 8
N_KEYS = 128
N_EXPERTS = N_KEYS * N_KEYS
PEER_QDIM = 256
PEER_HALF = PEER_QDIM // 2
PEER_TOPK = 16
EPS = 1e-6
F_FLOOR = 1e-30

N_PROMPT = BATCH * SEQ
N_SAMPLE = DEC_BATCH * DEC_SEQ
N_TOK = N_PROMPT + N_SAMPLE
MOD_ROWS = 8
LANES = 128
HG_CHUNK = 32
HG_GROUP = 8
HG_CUM_BLOCK = 256
MASK_PITCH = 132
VMEM_LIMIT = 56 * 1024 * 1024

COL_FN, COL_Q, COL_K, COL_V = 0, 4, 8, 10
COL_GQ, COL_GFF, COL_GFB, COL_GI, COL_GO = 12, 16, 20, 24, 28
COL_CB, COL_CC, COL_CX = 32, 36, 40

NT_DIMS = (((1,), (1,)), ((), ()))
TN_DIMS = (((0,), (0,)), ((), ()))


def _params(*sem):
    return pltpu.CompilerParams(dimension_semantics=sem, vmem_limit_bytes=VMEM_LIMIT)


def _dot(a, b):
    return jnp.dot(a, b, preferred_element_type=F32)


def _split3(x):
    h = x.astype(BF16)
    r = x - h.astype(F32)
    m = r.astype(BF16)
    l = (r - m.astype(F32)).astype(BF16)
    return h, m, l


def _split2(x):
    h = x.astype(BF16)
    return h, (x - h.astype(F32)).astype(BF16)


def _mod_row(tok0):
    return jnp.where(tok0 >= N_PROMPT, (tok0 - N_PROMPT) // DEC_SEQ + 1, 0)


def _rms(x, g):
    return x * lax.rsqrt(jnp.mean(x * x, axis=-1, keepdims=True) + EPS) * g


def _ada_kernel(c_ref, w_ref, b_ref, o_ref):
    c = c_ref[...]
    s = (c * jax.nn.sigmoid(c)).astype(BF16)
    o_ref[0] = _dot(s, w_ref[0].astype(BF16)) + b_ref[0]


def _ada(cvec, ada_w, ada_b):
    tn = 1024
    return pl.pallas_call(
        _ada_kernel,
        out_shape=jax.ShapeDtypeStruct((DEPTH, MOD_ROWS, 6 * D_MODEL), F32),
        grid=(DEPTH, 6 * D_MODEL // tn),
        in_specs=[
            pl.BlockSpec((MOD_ROWS, D_MODEL), lambda l, j: (0, 0)),
            pl.BlockSpec((1, D_MODEL, tn), lambda l, j: (l, 0, j)),
            pl.BlockSpec((1, 1, tn), lambda l, j: (l, 0, j)),
        ],
        out_specs=pl.BlockSpec((1, MOD_ROWS, tn), lambda l, j: (l, 0, j)),
        compiler_params=_params("arbitrary", "arbitrary"),
        name="ada",
    )(cvec, ada_w, ada_b.reshape(DEPTH, 1, 6 * D_MODEL))


def _inproj_kernel(x_ref, sh_ref, sc_ref, g_ref, w_ref, o_ref, h_ref, *, tm):
    @pl.when(pl.program_id(1) == 0)
    def _():
        r = _mod_row(pl.program_id(0) * tm)
        y = _rms(x_ref[...], g_ref[...])
        h_ref[...] = (y * (1.0 + sc_ref[pl.ds(r, 1), :]) + sh_ref[pl.ds(r, 1), :]).astype(BF16)

    o_ref[...] = _dot(h_ref[...], w_ref[...])


def _inproj(x, mod, norm_g, w_in_bf, layer):
    tm, tn = 1024, 512
    return pl.pallas_call(
        functools.partial(_inproj_kernel, tm=tm),
        out_shape=jax.ShapeDtypeStruct((N_TOK, IN_COLS), F32),
        grid=(N_TOK // tm, IN_COLS // tn),
        in_specs=[
            pl.BlockSpec((tm, D_MODEL), lambda i, j: (i, 0)),
            pl.BlockSpec((MOD_ROWS, D_MODEL), lambda i, j: (0, 0)),
            pl.BlockSpec((MOD_ROWS, D_MODEL), lambda i, j: (0, 1)),
            pl.BlockSpec((1, D_MODEL), lambda i, j: (0, 0)),
            pl.BlockSpec((None, D_MODEL, tn), lambda i, j: (layer, 0, j)),
        ],
        out_specs=pl.BlockSpec((tm, tn), lambda i, j: (i, j)),
        scratch_shapes=[pltpu.VMEM((tm, D_MODEL), BF16)],
        compiler_params=_params("arbitrary", "arbitrary"),
        name="inproj",
    )(x, mod, mod, norm_g.reshape(1, D_MODEL), w_in_bf)


def _fourier_kernel(u_ref, cch_ref, ccl_ref, cth_ref, ctl_ref, o_ref):
    cch, ccl, cth, ctl = cch_ref[...], ccl_ref[...], cth_ref[...], ctl_ref[...]
    for g in range(FN_GROUPS):
        u = u_ref[:, g * FN_GW:(g + 1) * FN_GW]
        uh, ul = _split2(u)
        w = _dot(uh, cch) + (_dot(ul, cch) + _dot(uh, ccl))
        st = jnp.concatenate([w[:, :FN_GW], w[:, FN_GW:]], axis=0)
        sh, sl = _split2(st)
        y = _dot(cth, sh) + (_dot(cth, sl) + _dot(ctl, sh))
        o_ref[:, g * FN_GW:(g + 1) * FN_GW] = y


def _dft_consts(t):
    def cs(n):
        k = np.arange(n, dtype=np.float64)
        ang = 2.0 * np.pi * np.outer(k, k) / n
        return np.cos(ang) / np.sqrt(n), np.sin(ang) / np.sqrt(n)

    cc, sc = cs(FN_GW)
    ct, st = cs(t)
    right = np.concatenate([cc, sc], axis=1).astype(np.float32)
    left = np.concatenate([ct, -st], axis=1).astype(np.float32)

    def hl(a):
        a = jnp.asarray(a)
        h = a.astype(BF16)
        return h, (a - h.astype(F32)).astype(BF16)

    return hl(right) + hl(left)


def _fourier(z, t, nb, row0):
    cch, ccl, cth, ctl = _dft_consts(t)
    rb = row0 // t
    return pl.pallas_call(
        _fourier_kernel,
        out_shape=jax.ShapeDtypeStruct((nb * t, GROUP_W), F32),
        grid=(nb,),
        in_specs=[
            pl.BlockSpec((t, GROUP_W), lambda b: (rb + b, COL_FN // 4)),
            pl.BlockSpec((FN_GW, 2 * FN_GW), lambda b: (0, 0)),
            pl.BlockSpec((FN_GW, 2 * FN_GW), lambda b: (0, 0)),
            pl.BlockSpec((t, 2 * t), lambda b: (0, 0)),
            pl.BlockSpec((t, 2 * t), lambda b: (0, 0)),
        ],
        out_specs=pl.BlockSpec((t, GROUP_W), lambda b: (b, 0)),
        compiler_params=_params("arbitrary"),
        name=f"fourier_t{t}",
    )(z, cch, ccl, cth, ctl)


def _softmax_pv(q_bf, k_bf, v_bf):
    s = lax.dot_general(q_bf, k_bf, NT_DIMS, preferred_element_type=F32) * (HEAD_DIM ** -0.5)
    p = jnp.exp(s - jnp.max(s, axis=-1, keepdims=True))
    return _dot(p.astype(BF16), v_bf) / jnp.sum(p, axis=-1, keepdims=True)


def _attn_prompt_kernel(q_ref, k_ref, v_ref, qg_ref, kg_ref, o_ref, kc_ref, vc_ref):
    kn = _rms(k_ref[...], kg_ref[...])
    v = v_ref[...]
    kc_ref[...] = kn
    vc_ref[...] = v
    kb, vb = kn.astype(BF16), v.astype(BF16)
    for g in range(GQA_GROUP):
        qn = _rms(q_ref[:, g * HEAD_DIM:(g + 1) * HEAD_DIM], qg_ref[...])
        o_ref[:, g * HEAD_DIM:(g + 1) * HEAD_DIM] = _softmax_pv(qn.astype(BF16), kb, vb)


def _attn_prompt(z, q_g, k_g):
    t = SEQ
    kv_shape = jax.ShapeDtypeStruct((N_PROMPT, N_KV * HEAD_DIM), F32)
    return pl.pallas_call(
        _attn_prompt_kernel,
        out_shape=(jax.ShapeDtypeStruct((N_PROMPT, GROUP_W), F32), kv_shape, kv_shape),
        grid=(BATCH, N_KV),
        in_specs=[
            pl.BlockSpec((t, GQA_GROUP * HEAD_DIM), lambda b, h: (b, COL_Q // 2 + h)),
            pl.BlockSpec((t, HEAD_DIM), lambda b, h: (b, COL_K + h)),
            pl.BlockSpec((t, HEAD_DIM), lambda b, h: (b, COL_V + h)),
            pl.BlockSpec((1, HEAD_DIM), lambda b, h: (0, 0)),
            pl.BlockSpec((1, HEAD_DIM), lambda b, h: (0, 0)),
        ],
        out_specs=(
            pl.BlockSpec((t, GQA_GROUP * HEAD_DIM), lambda b, h: (b, h)),
            pl.BlockSpec((t, HEAD_DIM), lambda b, h: (b, h)),
            pl.BlockSpec((t, HEAD_DIM), lambda b, h: (b, h)),
        ),
        compiler_params=_params("arbitrary", "arbitrary"),
        name="attn_prompt",
    )(z, z, z, q_g.reshape(1, HEAD_DIM), k_g.reshape(1, HEAD_DIM))


def _rope(x, cosf, sinf):
    return x * cosf + pltpu.roll(x, HEAD_DIM // 2, axis=1) * sinf


def _attn_sample_kernel(q_ref, k_ref, v_ref, kctx_ref, vctx_ref, qg_ref, kg_ref, cos_ref, sin_ref,
                        o_ref, kall_ref, vall_ref, *, qb):
    cosf, sinf = cos_ref[...], sin_ref[...]
    kall_ref[0:PAST_LEN, :] = kctx_ref[0].astype(BF16)
    vall_ref[0:PAST_LEN, :] = vctx_ref[0].astype(BF16)
    kall_ref[PAST_LEN:, :] = _rope(_rms(k_ref[...], kg_ref[...]), cosf, sinf).astype(BF16)
    vall_ref[PAST_LEN:, :] = v_ref[...].astype(BF16)
    kb, vb = kall_ref[...], vall_ref[...]
    for i in range(DEC_SEQ // qb):
        rows = slice(i * qb, (i + 1) * qb)
        for g in range(GQA_GROUP):
            cols = slice(g * HEAD_DIM, (g + 1) * HEAD_DIM)
            qn = _rope(_rms(q_ref[rows, cols], qg_ref[...]), cosf[rows], sinf[rows])
            o_ref[rows, cols] = _softmax_pv(qn.astype(BF16), kb, vb)


def _rope_tables():
    rows = DEC_SEQ // GRID_W
    row = np.repeat(np.arange(rows, dtype=np.float32), GRID_W)
    col = np.tile(np.arange(GRID_W, dtype=np.float32), rows)
    n_freq = HEAD_DIM // 4
    inv = (np.float32(ROPE_THETA) ** (-np.arange(n_freq, dtype=np.float32) / n_freq)).astype(np.float32)
    ang = np.concatenate([row[:, None] * inv, col[:, None] * inv], axis=-1).astype(np.float32)
    cos, sin = np.cos(ang.astype(np.float64)), np.sin(ang.astype(np.float64))
    cosf = np.concatenate([cos, cos], axis=-1).astype(np.float32)
    sinf = np.concatenate([-sin, sin], axis=-1).astype(np.float32)
    return jnp.asarray(cosf), jnp.asarray(sinf)


def _attn_sample(z, q_g, k_g, k_ctx, v_ctx):
    t = DEC_SEQ
    rb = N_PROMPT // t
    cosf, sinf = _rope_tables()
    const = lambda b, h: (0, 0)
    return pl.pallas_call(
        functools.partial(_attn_sample_kernel, qb=256),
        out_shape=jax.ShapeDtypeStruct((N_SAMPLE, GROUP_W), F32),
        grid=(DEC_BATCH, N_KV),
        in_specs=[
            pl.BlockSpec((t, GQA_GROUP * HEAD_DIM), lambda b, h: (rb + b, COL_Q // 2 + h)),
            pl.BlockSpec((t, HEAD_DIM), lambda b, h: (rb + b, COL_K + h)),
            pl.BlockSpec((t, HEAD_DIM), lambda b, h: (rb + b, COL_V + h)),
            pl.BlockSpec((1, PAST_LEN, HEAD_DIM), lambda b, h: (b, 0, h)),
            pl.BlockSpec((1, PAST_LEN, HEAD_DIM), lambda b, h: (b, 0, h)),
            pl.BlockSpec((1, HEAD_DIM), const),
            pl.BlockSpec((1, HEAD_DIM), const),
            pl.BlockSpec((t, HEAD_DIM), const),
            pl.BlockSpec((t, HEAD_DIM), const),
        ],
        out_specs=pl.BlockSpec((t, GQA_GROUP * HEAD_DIM), lambda b, h: (b, h)),
        scratch_shapes=[pltpu.VMEM((PAST_LEN + t, HEAD_DIM), BF16), pltpu.VMEM((PAST_LEN + t, HEAD_DIM), BF16)],
        compiler_params=_params("arbitrary", "arbitrary"),
        name="attn_sample",
    )(z, z, z, k_ctx, v_ctx, q_g.reshape(1, HEAD_DIM), k_g.reshape(1, HEAD_DIM), cosf, sinf)


def _hgrn_chain(rev, r0, gq_ref, gi_ref, k_scr, cum_scr, st_scr, o_scr):
    c = HG_CHUNK
    d = 1 if rev else 0
    rows = pl.ds(r0, c)
    qc, vc = gq_ref[rows, :], gi_ref[rows, :]
    kc, cc = k_scr[d, rows, :], cum_scr[d, rows, :]
    st = st_scr[d]
    last = cc[0:1] if rev else cc[c - 1:c]
    qs = (qc * jnp.exp(cc)).astype(BF16)
    o_inter = lax.dot_general(qs, st.astype(BF16), NT_DIMS, preferred_element_type=F32)
    ks = (kc * jnp.exp(last - cc)).astype(BF16)
    upd = lax.dot_general(vc.astype(BF16), ks, TN_DIMS, preferred_element_type=F32)
    st_scr[d] = st * jnp.exp(last) + upd
    n_grp = c // HG_GROUP
    row = lax.broadcasted_iota(jnp.int32, (HG_GROUP, 1), 0)
    krow = lax.broadcasted_iota(jnp.int32, (c, 1), 0)
    diag, att = [], []
    for i in range(n_grp):
        lo = i * HG_GROUP
        grp = slice(lo, lo + HG_GROUP)
        q_i, k_i, v_i, c_i = qc[grp], kc[grp], vc[grp], cc[grp]
        acc = jnp.zeros((HG_GROUP, HG_DV), F32)
        for delta in range(HG_GROUP):
            if delta == 0:
                k_sh, c_sh, v_sh = k_i, c_i, v_i
            else:
                sh = (HG_GROUP - delta) if rev else delta
                k_sh, c_sh, v_sh = (pltpu.roll(a, sh, axis=0) for a in (k_i, c_i, v_i))
            a = jnp.sum(q_i * k_sh * jnp.exp(jnp.minimum(c_i - c_sh, 0.0)), axis=-1, keepdims=True)
            valid = (row <= HG_GROUP - 1 - delta) if rev else (row >= delta)
            acc = acc + jnp.where(valid, a, 0.0) * v_sh
        diag.append(acc)
        if (i == n_grp - 1) if rev else (i == 0):
            att.append(jnp.zeros((HG_GROUP, c), F32))
            continue
        if rev:
            b, other = cc[lo + HG_GROUP:lo + HG_GROUP + 1], krow >= lo + HG_GROUP
        else:
            b, other = cc[lo - 1:lo], krow < lo
        qs_i = (q_i * jnp.exp(c_i - b)).astype(BF16)
        ks_i = jnp.where(other, kc * jnp.exp(jnp.minimum(b - cc, 0.0)), 0.0).astype(BF16)
        att.append(lax.dot_general(qs_i, ks_i, NT_DIMS, preferred_element_type=F32))
    o_off = _dot(jnp.concatenate(att, axis=0).astype(BF16), vc.astype(BF16))
    o_scr[d, rows, :] = o_inter + jnp.concatenate(diag, axis=0) + o_off


def _hgrn_kernel(*refs, layer, t, has_ctx):
    gq_ref, gff_ref, gfb_ref, gi_ref, go_ref, lb_ref, on_ref, bdf_ref, bdb_ref = refs[:9]
    n_in = 9
    s0_ref = None
    if has_ctx:
        s0_ref = refs[9]
        n_in = 10
    o_ref, sout_ref = refs[n_in], refs[n_in + 1]
    k_scr, cum_scr, st_scr, o_scr = refs[n_in + 2:]

    raw = lb_ref[...]
    e = jnp.exp(raw - jnp.max(raw, axis=0, keepdims=True))
    soft = e / jnp.sum(e, axis=0, keepdims=True)
    csum = soft[0]
    for i in range(1, layer + 1):
        csum = csum + soft[i]
    lb = csum - soft[0]

    for d, (g_ref, bd_ref) in enumerate(((gff_ref, bdf_ref), (gfb_ref, bdb_ref))):
        lbd = lb[d:d + 1]
        f = jnp.maximum(lbd + (1.0 - lbd) * jax.nn.sigmoid(g_ref[...]), F_FLOOR)
        k_scr[d] = 1.0 - f
        lf = jnp.log(f)
        bd = bd_ref[...]
        for p in range(t // HG_CUM_BLOCK):
            rows = slice(p * HG_CUM_BLOCK, (p + 1) * HG_CUM_BLOCK)
            h, m, l = _split3(lf[rows])
            cum_scr[d, rows, :] = _dot(bd, h) + _dot(bd, m) + _dot(bd, l)
        if has_ctx:
            st_scr[d] = s0_ref[0, d, 0].T
        else:
            st_scr[d] = jnp.zeros((HG_DV, HG_DK), F32)

    n = t // HG_CHUNK

    def body(i, carry):
        _hgrn_chain(False, pl.multiple_of(i * HG_CHUNK, HG_CHUNK), gq_ref, gi_ref, k_scr, cum_scr, st_scr, o_scr)
        _hgrn_chain(True, pl.multiple_of((n - 1 - i) * HG_CHUNK, HG_CHUNK), gq_ref, gi_ref, k_scr, cum_scr,
                    st_scr, o_scr)
        return carry

    lax.fori_loop(0, n, body, 0, unroll=4)

    o = _rms(o_scr[0] + o_scr[1], on_ref[...])
    go = go_ref[...]
    o_ref[...] = o * (go * jax.nn.sigmoid(go))
    for d in range(2):
        sout_ref[0, d, 0] = st_scr[d].T


def _cum_consts():
    r = np.arange(HG_CUM_BLOCK)
    same = (r[:, None] // HG_CHUNK) == (r[None, :] // HG_CHUNK)
    fwd = same & (r[None, :] <= r[:, None])
    bwd = same & (r[None, :] >= r[:, None])
    return jnp.asarray(fwd, BF16), jnp.asarray(bwd, BF16)


def _hgrn(z, hg_lb, onorm_g, layer, t, nb, row0, s0):
    rb = row0 // t
    bdf, bdb = _cum_consts()
    has_ctx = s0 is not None
    col = lambda c0: (lambda b, h: (rb + b, c0 + h))
    in_specs = [
        pl.BlockSpec((t, HG_DK), col(COL_GQ)),
        pl.BlockSpec((t, HG_DK), col(COL_GFF)),
        pl.BlockSpec((t, HG_DK), col(COL_GFB)),
        pl.BlockSpec((t, HG_DV), col(COL_GI)),
        pl.BlockSpec((t, HG_DV), col(COL_GO)),
        pl.BlockSpec((DEPTH, 2, HG_DK), lambda b, h: (0, 0, h)),
        pl.BlockSpec((1, HG_DV), lambda b, h: (0, 0)),
        pl.BlockSpec((HG_CUM_BLOCK, HG_CUM_BLOCK), lambda b, h: (0, 0)),
        pl.BlockSpec((HG_CUM_BLOCK, HG_CUM_BLOCK), lambda b, h: (0, 0)),
    ]
    args = [z, z, z, z, z, hg_lb, onorm_g.reshape(1, HG_DV), bdf, bdb]
    if has_ctx:
        in_specs.append(pl.BlockSpec((1, 2, 1, HG_DK, HG_DV), lambda b, h: (b, 0, h, 0, 0)))
        args.append(s0)
    return pl.pallas_call(
        functools.partial(_hgrn_kernel, layer=layer, t=t, has_ctx=has_ctx),
        out_shape=(jax.ShapeDtypeStruct((nb * t, GROUP_W), F32),
                   jax.ShapeDtypeStruct((nb, 2, HG_HEADS, HG_DK, HG_DV), F32)),
        grid=(nb, HG_HEADS),
        in_specs=in_specs,
        out_specs=(pl.BlockSpec((t, HG_DV), lambda b, h: (b, h)),
                   pl.BlockSpec((1, 2, 1, HG_DK, HG_DV), lambda b, h: (b, 0, h, 0, 0))),
        scratch_shapes=[pltpu.VMEM((2, t, HG_DK), F32), pltpu.VMEM((2, t, HG_DK), F32),
                        pltpu.VMEM((2, HG_DV, HG_DK), F32), pltpu.VMEM((2, t, HG_DV), F32)],
        compiler_params=_params("arbitrary", "arbitrary"),
        name=f"hgrn_t{t}",
    )(*args)


def _conv_kernel(cb_ref, cc_ref, cx_ref, w_ref, b_ref, o_ref, *, t):
    p = cc_ref[...] * cx_ref[...]
    row = lax.broadcasted_iota(jnp.int32, (t, 1), 0)
    prev = jnp.where(row >= 1, pltpu.roll(p, 1, axis=0), 0.0)
    nxt = jnp.where(row <= t - 2, pltpu.roll(p, t - 1, axis=0), 0.0)
    w = w_ref[...]
    y = prev * w[0:1] + p * w[1:2] + nxt * w[2:3] + b_ref[...]
    o_ref[...] = cb_ref[...] * y


def _conv(z, w, b, t, nb, row0):
    rb = row0 // t
    return pl.pallas_call(
        functools.partial(_conv_kernel, t=t),
        out_shape=jax.ShapeDtypeStruct((nb * t, GROUP_W), F32),
        grid=(nb,),
        in_specs=[
            pl.BlockSpec((t, GROUP_W), lambda i: (rb + i, COL_CB // 4)),
            pl.BlockSpec((t, GROUP_W), lambda i: (rb + i, COL_CC // 4)),
            pl.BlockSpec((t, GROUP_W), lambda i: (rb + i, COL_CX // 4)),
            pl.BlockSpec((CONV_K, GROUP_W), lambda i: (0, 0)),
            pl.BlockSpec((1, GROUP_W), lambda i: (0, 0)),
        ],
        out_specs=pl.BlockSpec((t, GROUP_W), lambda i: (i, 0)),
        compiler_params=_params("arbitrary"),
        name=f"conv_t{t}",
    )(z, z, z, w, b.reshape(1, GROUP_W))


def _outproj_kernel(*refs, tm):
    prompt_refs, sample_refs = refs[0:4], refs[4:8]
    w_ref, x_ref, g1_ref, sh_ref, sc_ref, n2_ref, x1_ref, h2_ref = refs[8:]
    i = pl.program_id(0)
    r = _mod_row(i * tm)

    def run(group_refs):
        acc = None
        for g, ref in enumerate(group_refs):
            part = _dot(ref[...].astype(BF16), w_ref[g * GROUP_W:(g + 1) * GROUP_W, :])
            acc = part if acc is None else acc + part
        x1 = x_ref[...] + g1_ref[pl.ds(r, 1), :] * acc
        x1_ref[...] = x1
        h2 = _rms(x1, n2_ref[...]) * (1.0 + sc_ref[pl.ds(r, 1), :]) + sh_ref[pl.ds(r, 1), :]
        h2_ref[...] = h2.astype(BF16)

    pl.when(i < N_PROMPT // tm)(lambda: run(prompt_refs))
    pl.when(i >= N_PROMPT // tm)(lambda: run(sample_refs))


def _outproj(prompt_groups, sample_groups, w_out_bf, layer, x, mod, norm2_g):
    tm = 512
    n_pb, n_sb = N_PROMPT // tm, N_SAMPLE // tm
    p_spec = pl.BlockSpec((tm, GROUP_W), lambda i: (jnp.minimum(i, n_pb - 1), 0))
    s_spec = pl.BlockSpec((tm, GROUP_W), lambda i: (jnp.clip(i - n_pb, 0, n_sb - 1), 0))
    modspec = lambda k: pl.BlockSpec((MOD_ROWS, D_MODEL), lambda i: (0, k))
    return pl.pallas_call(
        functools.partial(_outproj_kernel, tm=tm),
        out_shape=(jax.ShapeDtypeStruct((N_TOK, D_MODEL), F32), jax.ShapeDtypeStruct((N_TOK, D_MODEL), BF16)),
        grid=(N_TOK // tm,),
        in_specs=[p_spec] * 4 + [s_spec] * 4 + [
            pl.BlockSpec((None, D_MODEL, D_MODEL), lambda i: (layer, 0, 0)),
            pl.BlockSpec((tm, D_MODEL), lambda i: (i, 0)),
            modspec(2), modspec(3), modspec(4),
            pl.BlockSpec((1, D_MODEL), lambda i: (0, 0))],
        out_specs=(pl.BlockSpec((tm, D_MODEL), lambda i: (i, 0)), pl.BlockSpec((tm, D_MODEL), lambda i: (i, 0))),
        compiler_params=_params("arbitrary"),
        name="outproj",
    )(*prompt_groups, *sample_groups, w_out_bf, x, mod, mod, mod, norm2_g.reshape(1, D_MODEL))


def _peer_query_kernel(h_ref, wq_ref, q_ref):
    q_ref[...] = _dot(h_ref[...], wq_ref[...]).astype(BF16)


def _peer_query(h2, wq_bf, layer):
    tm = 512
    width = PEER_HEADS * PEER_QDIM
    return pl.pallas_call(
        _peer_query_kernel,
        out_shape=jax.ShapeDtypeStruct((N_TOK, width), BF16),
        grid=(N_TOK // tm,),
        in_specs=[pl.BlockSpec((tm, D_MODEL), lambda i: (i, 0)),
                  pl.BlockSpec((None, D_MODEL, width), lambda i: (layer, 0, 0))],
        out_specs=pl.BlockSpec((tm, width), lambda i: (i, 0)),
        compiler_params=_params("arbitrary"),
        name="peer_query",
    )(h2, wq_bf)


_CAND_PIECES = ((0, 0, 8), (0, 8, 8), (1, 0, 8), (2, 0, 5), (3, 0, 4), (4, 0, 3), (5, 0, 2), (6, 0, 2), (7, 0, 2),
                (None, 0, 8))
_CAND_ROWS = 8 * len(_CAND_PIECES)
_NO_CAND = 1 << 20


def _topk_rows(x_scr, ids, n_rows, v_out, i_out):
    def body(j, carry):
        x = x_scr[0:n_rows, :]
        m = jnp.max(x, axis=0, keepdims=True)
        idx = jnp.min(jnp.where(x == m, ids, _NO_CAND), axis=0, keepdims=True)
        v_out[pl.ds(j, 1), :] = m
        i_out[pl.ds(j, 1), :] = idx
        x_scr[0:n_rows, :] = jnp.where(ids == idx, -jnp.inf, x)
        return carry

    lax.fori_loop(0, PEER_TOPK, body, 0, unroll=2)


def _peer_topk_kernel(q_ref, k1_ref, k2_ref, e_ref, g_ref, x_scr, v1_scr, i1_scr, v2_scr, i2_scr,
                      ts_scr, tc_scr, *, tm):
    key_ids = lax.broadcasted_iota(jnp.int32, (N_KEYS, tm), 0)
    for half, (k_ref, v_scr, i_scr) in enumerate(((k1_ref, v1_scr, i1_scr), (k2_ref, v2_scr, i2_scr))):
        qh = q_ref[:, half * PEER_HALF:(half + 1) * PEER_HALF]
        x_scr[...] = lax.dot_general(k_ref[...], qh, NT_DIMS, preferred_element_type=F32)
        _topk_rows(x_scr, key_ids, N_KEYS, v_scr, i_scr)
    v1, v2 = v1_scr[...], v2_scr[...]
    sub = lax.broadcasted_iota(jnp.int32, (8, tm), 0)
    cand_ids = []
    for p, (r1, r2, n_valid) in enumerate(_CAND_PIECES):
        if r1 is None:
            vals = v1[8:16] + v2[0:1]
            ids = (sub + 8) * PEER_TOPK
        else:
            vals = v1[r1:r1 + 1] + v2[r2:r2 + 8]
            ids = sub + (r1 * PEER_TOPK + r2)
            if n_valid < 8:
                vals = jnp.where(sub < n_valid, vals, -jnp.inf)
                ids = jnp.where(sub < n_valid, ids, _NO_CAND)
        x_scr[p * 8:(p + 1) * 8, :] = vals
        cand_ids.append(ids)
    _topk_rows(x_scr, jnp.concatenate(cand_ids, axis=0), _CAND_ROWS, ts_scr, tc_scr)
    tc = tc_scr[...]
    c1, c2 = tc // PEER_TOPK, tc % PEER_TOPK
    i1, i2 = i1_scr[...], i2_scr[...]
    e1 = jnp.zeros_like(tc)
    e2 = jnp.zeros_like(tc)
    for a in range(PEER_TOPK):
        e1 = jnp.where(c1 == a, i1[a:a + 1], e1)
        e2 = jnp.where(c2 == a, i2[a:a + 1], e2)
    e_ref[...] = e1 * N_KEYS + e2
    ts = ts_scr[...]
    p = jnp.exp(ts - jnp.max(ts, axis=0, keepdims=True))
    g_ref[...] = p / jnp.sum(p, axis=0, keepdims=True)


def _peer_topk(q, k1_bf, k2_bf, layer):
    tm = 512
    row_f = pltpu.VMEM((PEER_TOPK, tm), F32)
    row_i = pltpu.VMEM((PEER_TOPK, tm), jnp.int32)
    return pl.pallas_call(
        functools.partial(_peer_topk_kernel, tm=tm),
        out_shape=(jax.ShapeDtypeStruct((PEER_HEADS * PEER_TOPK, N_TOK), jnp.int32),
                   jax.ShapeDtypeStruct((PEER_HEADS * PEER_TOPK, N_TOK), F32)),
        grid=(N_TOK // tm, PEER_HEADS),
        in_specs=[
            pl.BlockSpec((tm, PEER_QDIM), lambda i, h: (i, h)),
            pl.BlockSpec((None, N_KEYS, PEER_HALF), lambda i, h: (layer, 0, 0)),
            pl.BlockSpec((None, N_KEYS, PEER_HALF), lambda i, h: (layer, 0, 0)),
        ],
        out_specs=(pl.BlockSpec((PEER_TOPK, tm), lambda i, h: (h, i)),
                   pl.BlockSpec((PEER_TOPK, tm), lambda i, h: (h, i))),
        scratch_shapes=[pltpu.VMEM((N_KEYS, tm), F32), row_f, row_i, row_f, row_i, row_f, row_i],
        compiler_params=_params("arbitrary", "arbitrary"),
        name="peer_topk",
    )(q, k1_bf, k2_bf)


def _peer_mask_kernel(e_ref, g_ref, m_ref, et_scr, gt_scr, slab_scr, *, tb):
    et_scr[...] = e_ref[...].T
    gt_scr[...] = g_ref[...].T
    key = lax.broadcasted_iota(jnp.int32, (N_KEYS, PEER_HEADS * PEER_TOPK), 0)

    def body(t, carry):
        er = et_scr[pl.ds(t, 1), :]
        gr = gt_scr[pl.ds(t, 1), :]
        p1 = jnp.where(key == er // N_KEYS, gr, 0.0).astype(BF16)
        p2 = jnp.where(key == er % N_KEYS, 1.0, 0.0).astype(BF16)
        mt = lax.dot_general(p1, p2, NT_DIMS, preferred_element_type=F32)
        slab_scr[pl.ds(pl.multiple_of(t * MASK_PITCH, 4), N_KEYS), :] = mt
        return carry

    lax.fori_loop(0, tb, body, 0, unroll=64)
    for k1 in range(N_KEYS):
        m_ref[:, k1 * N_KEYS:(k1 + 1) * N_KEYS] = slab_scr[pl.ds(k1, tb, stride=MASK_PITCH), :].astype(BF16)


def _peer_mask(experts_t, gates_t):
    tb = LANES
    pairs = PEER_HEADS * PEER_TOPK
    return pl.pallas_call(
        functools.partial(_peer_mask_kernel, tb=tb),
        out_shape=jax.ShapeDtypeStruct((N_TOK, N_EXPERTS), BF16),
        grid=(N_TOK // tb,),
        in_specs=[pl.BlockSpec((pairs, tb), lambda i: (0, i)), pl.BlockSpec((pairs, tb), lambda i: (0, i))],
        out_specs=pl.BlockSpec((tb, N_EXPERTS), lambda i: (i, 0)),
        scratch_shapes=[pltpu.VMEM((tb, pairs), jnp.int32), pltpu.VMEM((tb, pairs), F32),
                        pltpu.VMEM((tb * MASK_PITCH, N_KEYS), F32)],
        compiler_params=_params("arbitrary"),
        name="peer_mask",
    )(experts_t, gates_t)


def _peer_expert_kernel(h_ref, u_ref, v_ref, m_ref, x_ref, g2_ref, o_ref, *, tm):
    j = pl.program_id(1)

    @pl.when(j == 0)
    def _():
        o_ref[...] = jnp.zeros_like(o_ref)

    s = lax.dot_general(h_ref[...], u_ref[...].astype(BF16), NT_DIMS, preferred_element_type=F32)
    a = (m_ref[...].astype(F32) * jax.nn.gelu(s)).astype(BF16)
    o_ref[...] += _dot(a, v_ref[...].astype(BF16))

    @pl.when(j == pl.num_programs(1) - 1)
    def _():
        r = _mod_row(pl.program_id(0) * tm)
        o_ref[...] = x_ref[...] + g2_ref[pl.ds(r, 1), :] * o_ref[...]


EXPERT_TM, EXPERT_TE = 1024, 512


def _peer_expert(h2, peer_u, peer_v, layer, mask, x1, mod):
    tm, te = EXPERT_TM, EXPERT_TE
    return pl.pallas_call(
        functools.partial(_peer_expert_kernel, tm=tm),
        out_shape=jax.ShapeDtypeStruct((N_TOK, D_MODEL), F32),
        grid=(N_TOK // tm, N_EXPERTS // te),
        in_specs=[
            pl.BlockSpec((tm, D_MODEL), lambda i, j: (i, 0)),
            pl.BlockSpec((None, te, D_MODEL), lambda i, j: (layer, j, 0)),
            pl.BlockSpec((None, te, D_MODEL), lambda i, j: (layer, j, 0)),
            pl.BlockSpec((tm, te), lambda i, j: (i, j)),
            pl.BlockSpec((tm, D_MODEL), lambda i, j: (i, 0), pipeline_mode=pl.Buffered(1)),
            pl.BlockSpec((MOD_ROWS, D_MODEL), lambda i, j: (0, 5)),
        ],
        out_specs=pl.BlockSpec((tm, D_MODEL), lambda i, j: (i, 0)),
        compiler_params=_params("arbitrary", "arbitrary"),
        name="peer_expert",
    )(h2, peer_u, peer_v, mask, x1, mod)


def kernel(x_prompt, x_sample, cache_k, cache_v, state_hgrn, c, c_ctx, ada_w, ada_b, norm1_g, norm2_g, w_in,
           q_norm_g, k_norm_g, hg_lb, hg_onorm_g, conv_w, conv_b, w_out, peer_wq, peer_k1, peer_k2, peer_u, peer_v):
    x = jnp.concatenate([x_prompt.reshape(N_PROMPT, D_MODEL), x_sample.reshape(N_SAMPLE, D_MODEL)], axis=0)
    cvec = jnp.concatenate([c_ctx[None, :], c, jnp.zeros((MOD_ROWS - 1 - DEC_BATCH, D_MODEL), F32)], axis=0)
    mod_all = _ada(cvec, ada_w, ada_b)

    w_in_bf, w_out_bf, wq_bf = w_in.astype(BF16), w_out.astype(BF16), peer_wq.astype(BF16)
    k1_bf, k2_bf = peer_k1.astype(BF16), peer_k2.astype(BF16)

    ks_out, vs_out, ss_out = [], [], []
    for l in range(DEPTH):
        mod = mod_all[l]
        z = _inproj(x, mod, norm1_g[l], w_in_bf, l)

        fn_p = _fourier(z, SEQ, BATCH, 0)
        fn_s = _fourier(z, DEC_SEQ, DEC_BATCH, N_PROMPT)

        at_p, k_l, v_l = _attn_prompt(z, q_norm_g[l], k_norm_g[l])
        k_ctx = cache_k[:, l].reshape(DEC_BATCH, PAST_LEN, N_KV * HEAD_DIM)
        v_ctx = cache_v[:, l].reshape(DEC_BATCH, PAST_LEN, N_KV * HEAD_DIM)
        at_s = _attn_sample(z, q_norm_g[l], k_norm_g[l], k_ctx, v_ctx)

        hg_p, s_l = _hgrn(z, hg_lb, hg_onorm_g[l], l, SEQ, BATCH, 0, None)
        hg_s, _ = _hgrn(z, hg_lb, hg_onorm_g[l], l, DEC_SEQ, DEC_BATCH, N_PROMPT, state_hgrn[:, l])

        cv_p = _conv(z, conv_w[l], conv_b[l], SEQ, BATCH, 0)
        cv_s = _conv(z, conv_w[l], conv_b[l], DEC_SEQ, DEC_BATCH, N_PROMPT)

        x1, h2 = _outproj((fn_p, at_p, hg_p, cv_p), (fn_s, at_s, hg_s, cv_s), w_out_bf, l, x, mod, norm2_g[l])

        q_peer = _peer_query(h2, wq_bf, l)
        experts_t, gates_t = _peer_topk(q_peer, k1_bf, k2_bf, l)
        mask = _peer_mask(experts_t, gates_t)
        x = _peer_expert(h2, peer_u, peer_v, l, mask, x1, mod)

        ks_out.append(k_l.reshape(BATCH, SEQ, N_KV, HEAD_DIM))
        vs_out.append(v_l.reshape(BATCH, SEQ, N_KV, HEAD_DIM))
        ss_out.append(s_l)

    y_prompt = x[:N_PROMPT].reshape(BATCH, SEQ, D_MODEL)
    y_sample = x[N_PROMPT:].reshape(DEC_BATCH, DEC_SEQ, D_MODEL)
    return (y_prompt, y_sample, jnp.stack(ks_out, axis=1), jnp.stack(vs_out, axis=1), jnp.stack(ss_out, axis=1))
```

```python
import functools

import numpy as np
import jax
import jax.numpy as jnp
from jax import lax
from jax.experimental import pallas as pl
from jax.experimental.pallas import tpu as pltpu

F32 = jnp.float32
BF16 = jnp.bfloat16

D_MODEL = 2048
BATCH = 32
SEQ = 256
DEPTH = 2
DEC_BATCH = 4
DEC_SEQ = 1024
PAST_LEN = 256
GRID_W = 64
GROUP_W = D_MODEL // 4
FN_GROUPS = 4
FN_GW = GROUP_W // FN_GROUPS
HEAD_DIM = 128
N_HEADS = GROUP_W // HEAD_DIM
N_KV = 2
GQA_GROUP = N_HEADS // N_KV
ROPE_THETA = 10000.0
HG_DK = 128
HG_DV = 128
HG_HEADS = GROUP_W // HG_DK
CONV_K = 3
IN_COLS = 11 * GROUP_W
PEER_HEADS = 8
N_KEYS = 128
N_EXPERTS = N_KEYS * N_KEYS
PEER_QDIM = 256
PEER_HALF = PEER_QDIM // 2
PEER_TOPK = 16
EPS = 1e-6
F_FLOOR = 1e-30

N_PROMPT = BATCH * SEQ
N_SAMPLE = DEC_BATCH * DEC_SEQ
N_TOK = N_PROMPT + N_SAMPLE
MOD_ROWS = 8
LANES = 128
NORM_ROWS = 128
HG_CHUNK = 32
HG_GROUP = 8
HG_UNROLL = 16
HG_CUM_BLOCK = 256
MASK_PITCH = 132
VMEM_LIMIT = 56 * 1024 * 1024

COL_FN, COL_Q, COL_K, COL_V = 0, 4, 8, 10
COL_GQ, COL_GFF, COL_GFB, COL_GI, COL_GO = 12, 16, 20, 24, 28
COL_CB, COL_CC, COL_CX = 32, 36, 40

NT_DIMS = (((1,), (1,)), ((), ()))
TN_DIMS = (((0,), (0,)), ((), ()))


def _params(*sem):
    return pltpu.CompilerParams(dimension_semantics=sem, vmem_limit_bytes=VMEM_LIMIT)


def _dot(a, b):
    return jnp.dot(a, b, preferred_element_type=F32)


def _split3(x):
    h = x.astype(BF16)
    r = x - h.astype(F32)
    m = r.astype(BF16)
    l = (r - m.astype(F32)).astype(BF16)
    return h, m, l


def _split2(x):
    h = x.astype(BF16)
    return h, (x - h.astype(F32)).astype(BF16)


def _mod_row(tok0):
    return jnp.where(tok0 >= N_PROMPT, (tok0 - N_PROMPT) // DEC_SEQ + 1, 0)


def _rms(x, g):
    return x * lax.rsqrt(jnp.mean(x * x, axis=-1, keepdims=True) + EPS) * g


def _ada_kernel(c_ref, w_ref, b_ref, o_ref):
    c = c_ref[...]
    s = (c * jax.nn.sigmoid(c)).astype(BF16)
    o_ref[0] = _dot(s, w_ref[0].astype(BF16)) + b_ref[0]


def _ada(cvec, ada_w, ada_b):
    tn = 1024
    return pl.pallas_call(
        _ada_kernel,
        out_shape=jax.ShapeDtypeStruct((DEPTH, MOD_ROWS, 6 * D_MODEL), F32),
        grid=(DEPTH, 6 * D_MODEL // tn),
        in_specs=[
            pl.BlockSpec((MOD_ROWS, D_MODEL), lambda l, j: (0, 0)),
            pl.BlockSpec((1, D_MODEL, tn), lambda l, j: (l, 0, j)),
            pl.BlockSpec((1, 1, tn), lambda l, j: (l, 0, j)),
        ],
        out_specs=pl.BlockSpec((1, MOD_ROWS, tn), lambda l, j: (l, 0, j)),
        compiler_params=_params("arbitrary", "arbitrary"),
        name="ada",
    )(cvec, ada_w, ada_b.reshape(DEPTH, 1, 6 * D_MODEL))


def _inproj_kernel(x_ref, sh_ref, sc_ref, g_ref, w_ref, o_ref, h_ref, *, tm):
    @pl.when(pl.program_id(1) == 0)
    def _():
        r = _mod_row(pl.program_id(0) * tm)
        gain = g_ref[...] * (1.0 + sc_ref[pl.ds(r, 1), :])
        shift = sh_ref[pl.ds(r, 1), :]

        def norm_rows(c, carry):
            rows = pl.ds(pl.multiple_of(c * NORM_ROWS, NORM_ROWS), NORM_ROWS)
            x = x_ref[rows, :]
            y = x * lax.rsqrt(jnp.mean(x * x, axis=-1, keepdims=True) + EPS)
            h_ref[rows, :] = (y * gain + shift).astype(BF16)
            return carry

        lax.fori_loop(0, tm // NORM_ROWS, norm_rows, 0)

    o_ref[...] = _dot(h_ref[...], w_ref[...])


def _inproj(x, mod, norm_g, w_in_bf, layer):
    tm, tn = 1024, 512
    return pl.pallas_call(
        functools.partial(_inproj_kernel, tm=tm),
        out_shape=jax.ShapeDtypeStruct((N_TOK, IN_COLS), F32),
        grid=(N_TOK // tm, IN_COLS // tn),
        in_specs=[
            pl.BlockSpec((tm, D_MODEL), lambda i, j: (i, 0)),
            pl.BlockSpec((MOD_ROWS, D_MODEL), lambda i, j: (0, 0)),
            pl.BlockSpec((MOD_ROWS, D_MODEL), lambda i, j: (0, 1)),
            pl.BlockSpec((1, D_MODEL), lambda i, j: (0, 0)),
            pl.BlockSpec((None, D_MODEL, tn), lambda i, j: (layer, 0, j)),
        ],
        out_specs=pl.BlockSpec((tm, tn), lambda i, j: (i, j)),
        scratch_shapes=[pltpu.VMEM((tm, D_MODEL), BF16)],
        compiler_params=_params("arbitrary", "arbitrary"),
        name="inproj",
    )(x, mod, mod, norm_g.reshape(1, D_MODEL), w_in_bf)


def _fourier_kernel(u_ref, cch_ref, ccl_ref, cth_ref, ctl_ref, o_ref):
    cch, ccl, cth, ctl = cch_ref[...], ccl_ref[...], cth_ref[...], ctl_ref[...]
    for g in range(0, FN_GROUPS, 2):
        ws = []
        for gg in (g, g + 1):
            uh, ul = _split2(u_ref[:, gg * FN_GW:(gg + 1) * FN_GW])
            ws.append(_dot(uh, cch) + (_dot(ul, cch) + _dot(uh, ccl)))
        st = jnp.concatenate([jnp.concatenate([w[:, :FN_GW] for w in ws], axis=1),
                              jnp.concatenate([w[:, FN_GW:] for w in ws], axis=1)], axis=0)
        sh, sl = _split2(st)
        o_ref[:, g * FN_GW:(g + 2) * FN_GW] = _dot(cth, sh) + (_dot(cth, sl) + _dot(ctl, sh))


def _dft_consts(t):
    def cs(n):
        k = np.arange(n, dtype=np.float64)
        ang = 2.0 * np.pi * np.outer(k, k) / n
        return np.cos(ang) / np.sqrt(n), np.sin(ang) / np.sqrt(n)

    cc, sc = cs(FN_GW)
    ct, st = cs(t)
    right = np.concatenate([cc, sc], axis=1).astype(np.float32)
    left = np.concatenate([ct, -st], axis=1).astype(np.float32)

    def hl(a):
        a = jnp.asarray(a)
        h = a.astype(BF16)
        return h, (a - h.astype(F32)).astype(BF16)

    return hl(right) + hl(left)


def _fourier(z, t, nb, row0):
    cch, ccl, cth, ctl = _dft_consts(t)
    rb = row0 // t
    return pl.pallas_call(
        _fourier_kernel,
        out_shape=jax.ShapeDtypeStruct((nb * t, GROUP_W), F32),
        grid=(nb,),
        in_specs=[
            pl.BlockSpec((t, GROUP_W), lambda b: (rb + b, COL_FN // 4)),
            pl.BlockSpec((FN_GW, 2 * FN_GW), lambda b: (0, 0)),
            pl.BlockSpec((FN_GW, 2 * FN_GW), lambda b: (0, 0)),
            pl.BlockSpec((t, 2 * t), lambda b: (0, 0)),
            pl.BlockSpec((t, 2 * t), lambda b: (0, 0)),
        ],
        out_specs=pl.BlockSpec((t, GROUP_W), lambda b: (b, 0)),
        compiler_params=_params("arbitrary"),
        name=f"fourier_t{t}",
    )(z, cch, ccl, cth, ctl)


def _softmax_pv(q_bf, k_bf, v_bf):
    s = lax.dot_general(q_bf, k_bf, NT_DIMS, preferred_element_type=F32) * (HEAD_DIM ** -0.5)
    p = jnp.exp(s - jnp.max(s, axis=-1, keepdims=True))
    return _dot(p.astype(BF16), v_bf) / jnp.sum(p, axis=-1, keepdims=True)


def _attn_prompt_kernel(q_ref, k_ref, v_ref, qg_ref, kg_ref, o_ref, kc_ref, vc_ref):
    kn = _rms(k_ref[...], kg_ref[...])
    v = v_ref[...]
    kc_ref[...] = kn
    vc_ref[...] = v
    kb, vb = kn.astype(BF16), v.astype(BF16)
    for g in range(GQA_GROUP):
        qn = _rms(q_ref[:, g * HEAD_DIM:(g + 1) * HEAD_DIM], qg_ref[...])
        o_ref[:, g * HEAD_DIM:(g + 1) * HEAD_DIM] = _softmax_pv(qn.astype(BF16), kb, vb)


def _attn_prompt(z, q_g, k_g):
    t = SEQ
    kv_shape = jax.ShapeDtypeStruct((N_PROMPT, N_KV * HEAD_DIM), F32)
    return pl.pallas_call(
        _attn_prompt_kernel,
        out_shape=(jax.ShapeDtypeStruct((N_PROMPT, GROUP_W), F32), kv_shape, kv_shape),
        grid=(BATCH, N_KV),
        in_specs=[
            pl.BlockSpec((t, GQA_GROUP * HEAD_DIM), lambda b, h: (b, COL_Q // 2 + h)),
            pl.BlockSpec((t, HEAD_DIM), lambda b, h: (b, COL_K + h)),
            pl.BlockSpec((t, HEAD_DIM), lambda b, h: (b, COL_V + h)),
            pl.BlockSpec((1, HEAD_DIM), lambda b, h: (0, 0)),
            pl.BlockSpec((1, HEAD_DIM), lambda b, h: (0, 0)),
        ],
        out_specs=(
            pl.BlockSpec((t, GQA_GROUP * HEAD_DIM), lambda b, h: (b, h)),
            pl.BlockSpec((t, HEAD_DIM), lambda b, h: (b, h)),
            pl.BlockSpec((t, HEAD_DIM), lambda b, h: (b, h)),
        ),
        compiler_params=_params("arbitrary", "arbitrary"),
        name="attn_prompt",
    )(z, z, z, q_g.reshape(1, HEAD_DIM), k_g.reshape(1, HEAD_DIM))


def _rope(x, cosf, sinf):
    return x * cosf + pltpu.roll(x, HEAD_DIM // 2, axis=1) * sinf


def _attn_sample_kernel(q_ref, k_ref, v_ref, kctx_ref, vctx_ref, qg_ref, kg_ref, cos_ref, sin_ref,
                        o_ref, kall_ref, vall_ref, *, qb):
    cosf, sinf = cos_ref[...], sin_ref[...]
    kall_ref[0:PAST_LEN, :] = kctx_ref[0].astype(BF16)
    vall_ref[0:PAST_LEN, :] = vctx_ref[0].astype(BF16)
    kall_ref[PAST_LEN:, :] = _rope(_rms(k_ref[...], kg_ref[...]), cosf, sinf).astype(BF16)
    vall_ref[PAST_LEN:, :] = v_ref[...].astype(BF16)
    kb, vb = kall_ref[...], vall_ref[...]
    for i in range(DEC_SEQ // qb):
        rows = slice(i * qb, (i + 1) * qb)
        for g in range(GQA_GROUP):
            cols = slice(g * HEAD_DIM, (g + 1) * HEAD_DIM)
            qn = _rope(_rms(q_ref[rows, cols], qg_ref[...]), cosf[rows], sinf[rows])
            o_ref[rows, cols] = _softmax_pv(qn.astype(BF16), kb, vb)


def _rope_tables():
    rows = DEC_SEQ // GRID_W
    row = np.repeat(np.arange(rows, dtype=np.float32), GRID_W)
    col = np.tile(np.arange(GRID_W, dtype=np.float32), rows)
    n_freq = HEAD_DIM // 4
    inv = (np.float32(ROPE_THETA) ** (-np.arange(n_freq, dtype=np.float32) / n_freq)).astype(np.float32)
    ang = np.concatenate([row[:, None] * inv, col[:, None] * inv], axis=-1).astype(np.float32)
    cos, sin = np.cos(ang.astype(np.float64)), np.sin(ang.astype(np.float64))
    cosf = np.concatenate([cos, cos], axis=-1).astype(np.float32)
    sinf = np.concatenate([-sin, sin], axis=-1).astype(np.float32)
    return jnp.asarray(cosf), jnp.asarray(sinf)


def _attn_sample(z, q_g, k_g, k_ctx, v_ctx):
    t = DEC_SEQ
    rb = N_PROMPT // t
    cosf, sinf = _rope_tables()
    const = lambda b, h: (0, 0)
    return pl.pallas_call(
        functools.partial(_attn_sample_kernel, qb=256),
        out_shape=jax.ShapeDtypeStruct((N_SAMPLE, GROUP_W), F32),
        grid=(DEC_BATCH, N_KV),
        in_specs=[
            pl.BlockSpec((t, GQA_GROUP * HEAD_DIM), lambda b, h: (rb + b, COL_Q // 2 + h)),
            pl.BlockSpec((t, HEAD_DIM), lambda b, h: (rb + b, COL_K + h)),
            pl.BlockSpec((t, HEAD_DIM), lambda b, h: (rb + b, COL_V + h)),
            pl.BlockSpec((1, PAST_LEN, HEAD_DIM), lambda b, h: (b, 0, h)),
            pl.BlockSpec((1, PAST_LEN, HEAD_DIM), lambda b, h: (b, 0, h)),
            pl.BlockSpec((1, HEAD_DIM), const),
            pl.BlockSpec((1, HEAD_DIM), const),
            pl.BlockSpec((t, HEAD_DIM), const),
            pl.BlockSpec((t, HEAD_DIM), const),
        ],
        out_specs=pl.BlockSpec((t, GQA_GROUP * HEAD_DIM), lambda b, h: (b, h)),
        scratch_shapes=[pltpu.VMEM((PAST_LEN + t, HEAD_DIM), BF16), pltpu.VMEM((PAST_LEN + t, HEAD_DIM), BF16)],
        compiler_params=_params("arbitrary", "arbitrary"),
        name="attn_sample",
    )(z, z, z, k_ctx, v_ctx, q_g.reshape(1, HEAD_DIM), k_g.reshape(1, HEAD_DIM), cosf, sinf)


def _hgrn_chain(rev, r0, gq_ref, gi_ref, k_scr, cum_scr, st_scr, o_scr):
    c = HG_CHUNK
    d = 1 if rev else 0
    rows = pl.ds(r0, c)
    qc, vc = gq_ref[rows, :], gi_ref[rows, :]
    kc, cc = k_scr[d, rows, :], cum_scr[d, rows, :]
    st = st_scr[d]
    last = cc[0:1] if rev else cc[c - 1:c]
    qs = (qc * jnp.exp(cc)).astype(BF16)
    o_inter = lax.dot_general(qs, st.astype(BF16), NT_DIMS, preferred_element_type=F32)
    ks = (kc * jnp.exp(last - cc)).astype(BF16)
    upd = lax.dot_general(vc.astype(BF16), ks, TN_DIMS, preferred_element_type=F32)
    st_scr[d] = st * jnp.exp(last) + upd
    n_grp = c // HG_GROUP
    row = lax.broadcasted_iota(jnp.int32, (HG_GROUP, 1), 0)
    krow = lax.broadcasted_iota(jnp.int32, (c, 1), 0)
    diag, att = [], []
    for i in range(n_grp):
        lo = i * HG_GROUP
        grp = slice(lo, lo + HG_GROUP)
        q_i, k_i, v_i, c_i = qc[grp], kc[grp], vc[grp], cc[grp]
        acc = jnp.zeros((HG_GROUP, HG_DV), F32)
        for delta in range(HG_GROUP):
            if delta == 0:
                k_sh, c_sh, v_sh = k_i, c_i, v_i
            else:
                sh = (HG_GROUP - delta) if rev else delta
                k_sh, c_sh, v_sh = (pltpu.roll(a, sh, axis=0) for a in (k_i, c_i, v_i))
            a = jnp.sum(q_i * k_sh * jnp.exp(jnp.minimum(c_i - c_sh, 0.0)), axis=-1, keepdims=True)
            valid = (row <= HG_GROUP - 1 - delta) if rev else (row >= delta)
            acc = acc + jnp.where(valid, a, 0.0) * v_sh
        diag.append(acc)
        if (i == n_grp - 1) if rev else (i == 0):
            att.append(jnp.zeros((HG_GROUP, c), F32))
            continue
        if rev:
            b, other = cc[lo + HG_GROUP:lo + HG_GROUP + 1], krow >= lo + HG_GROUP
        else:
            b, other = cc[lo - 1:lo], krow < lo
        qs_i = (q_i * jnp.exp(c_i - b)).astype(BF16)
        ks_i = jnp.where(other, kc * jnp.exp(jnp.minimum(b - cc, 0.0)), 0.0).astype(BF16)
        att.append(lax.dot_general(qs_i, ks_i, NT_DIMS, preferred_element_type=F32))
    o_off = _dot(jnp.concatenate(att, axis=0).astype(BF16), vc.astype(BF16))
    o_scr[d, rows, :] = o_inter + jnp.concatenate(diag, axis=0) + o_off


def _hgrn_kernel(*refs, layer, t, has_ctx):
    gq_ref, gff_ref, gfb_ref, gi_ref, go_ref, lb_ref, on_ref, bdf_ref, bdb_ref = refs[:9]
    n_in = 9
    s0_ref = None
    if has_ctx:
        s0_ref = refs[9]
        n_in = 10
    o_ref, sout_ref = refs[n_in], refs[n_in + 1]
    k_scr, cum_scr, st_scr, o_scr = refs[n_in + 2:]

    raw = lb_ref[...]
    e = jnp.exp(raw - jnp.max(raw, axis=0, keepdims=True))
    soft = e / jnp.sum(e, axis=0, keepdims=True)
    csum = soft[0]
    for i in range(1, layer + 1):
        csum = csum + soft[i]
    lb = csum - soft[0]

    for d, (g_ref, bd_ref) in enumerate(((gff_ref, bdf_ref), (gfb_ref, bdb_ref))):
        lbd = lb[d:d + 1]
        f = jnp.maximum(lbd + (1.0 - lbd) * jax.nn.sigmoid(g_ref[...]), F_FLOOR)
        k_scr[d] = 1.0 - f
        lf = jnp.log(f)
        bd = bd_ref[...]
        for p in range(t // HG_CUM_BLOCK):
            rows = slice(p * HG_CUM_BLOCK, (p + 1) * HG_CUM_BLOCK)
            h, m, l = _split3(lf[rows])
            cum_scr[d, rows, :] = _dot(bd, h) + _dot(bd, m) + _dot(bd, l)
        if has_ctx:
            st_scr[d] = s0_ref[0, d, 0].T
        else:
            st_scr[d] = jnp.zeros((HG_DV, HG_DK), F32)

    n = t // HG_CHUNK

    def body(i, carry):
        _hgrn_chain(False, pl.multiple_of(i * HG_CHUNK, HG_CHUNK), gq_ref, gi_ref, k_scr, cum_scr, st_scr, o_scr)
        _hgrn_chain(True, pl.multiple_of((n - 1 - i) * HG_CHUNK, HG_CHUNK), gq_ref, gi_ref, k_scr, cum_scr,
                    st_scr, o_scr)
        return carry

    lax.fori_loop(0, n, body, 0, unroll=min(n, HG_UNROLL))

    o = _rms(o_scr[0] + o_scr[1], on_ref[...])
    go = go_ref[...]
    o_ref[...] = o * (go * jax.nn.sigmoid(go))
    for d in range(2):
        sout_ref[0, d, 0] = st_scr[d].T


def _cum_consts():
    r = np.arange(HG_CUM_BLOCK)
    same = (r[:, None] // HG_CHUNK) == (r[None, :] // HG_CHUNK)
    fwd = same & (r[None, :] <= r[:, None])
    bwd = same & (r[None, :] >= r[:, None])
    return jnp.asarray(fwd, BF16), jnp.asarray(bwd, BF16)


def _hgrn(z, hg_lb, onorm_g, layer, t, nb, row0, s0):
    rb = row0 // t
    bdf, bdb = _cum_consts()
    has_ctx = s0 is not None
    col = lambda c0: (lambda b, h: (rb + b, c0 + h))
    in_specs = [
        pl.BlockSpec((t, HG_DK), col(COL_GQ)),
        pl.BlockSpec((t, HG_DK), col(COL_GFF)),
        pl.BlockSpec((t, HG_DK), col(COL_GFB)),
        pl.BlockSpec((t, HG_DV), col(COL_GI)),
        pl.BlockSpec((t, HG_DV), col(COL_GO)),
        pl.BlockSpec((DEPTH, 2, HG_DK), lambda b, h: (0, 0, h)),
        pl.BlockSpec((1, HG_DV), lambda b, h: (0, 0)),
        pl.BlockSpec((HG_CUM_BLOCK, HG_CUM_BLOCK), lambda b, h: (0, 0)),
        pl.BlockSpec((HG_CUM_BLOCK, HG_CUM_BLOCK), lambda b, h: (0, 0)),
    ]
    args = [z, z, z, z, z, hg_lb, onorm_g.reshape(1, HG_DV), bdf, bdb]
    if has_ctx:
        in_specs.append(pl.BlockSpec((1, 2, 1, HG_DK, HG_DV), lambda b, h: (b, 0, h, 0, 0)))
        args.append(s0)
    return pl.pallas_call(
        functools.partial(_hgrn_kernel, layer=layer, t=t, has_ctx=has_ctx),
        out_shape=(jax.ShapeDtypeStruct((nb * t, GROUP_W), F32),
                   jax.ShapeDtypeStruct((nb, 2, HG_HEADS, HG_DK, HG_DV), F32)),
        grid=(nb, HG_HEADS),
        in_specs=in_specs,
        out_specs=(pl.BlockSpec((t, HG_DV), lambda b, h: (b, h)),
                   pl.BlockSpec((1, 2, 1, HG_DK, HG_DV), lambda b, h: (b, 0, h, 0, 0))),
        scratch_shapes=[pltpu.VMEM((2, t, HG_DK), F32), pltpu.VMEM((2, t, HG_DK), F32),
                        pltpu.VMEM((2, HG_DV, HG_DK), F32), pltpu.VMEM((2, t, HG_DV), F32)],
        compiler_params=_params("arbitrary", "arbitrary"),
        name=f"hgrn_t{t}",
    )(*args)


def _conv_kernel(cb_ref, cc_ref, cx_ref, w_ref, b_ref, o_ref, *, t):
    p = cc_ref[...] * cx_ref[...]
    row = lax.broadcasted_iota(jnp.int32, (t, 1), 0)
    prev = jnp.where(row >= 1, pltpu.roll(p, 1, axis=0), 0.0)
    nxt = jnp.where(row <= t - 2, pltpu.roll(p, t - 1, axis=0), 0.0)
    w = w_ref[...]
    y = prev * w[0:1] + p * w[1:2] + nxt * w[2:3] + b_ref[...]
    o_ref[...] = cb_ref[...] * y


def _conv(z, w, b, t, nb, row0):
    rb = row0 // t
    return pl.pallas_call(
        functools.partial(_conv_kernel, t=t),
        out_shape=jax.ShapeDtypeStruct((nb * t, GROUP_W), F32),
        grid=(nb,),
        in_specs=[
            pl.BlockSpec((t, GROUP_W), lambda i: (rb + i, COL_CB // 4)),
            pl.BlockSpec((t, GROUP_W), lambda i: (rb + i, COL_CC // 4)),
            pl.BlockSpec((t, GROUP_W), lambda i: (rb + i, COL_CX // 4)),
            pl.BlockSpec((CONV_K, GROUP_W), lambda i: (0, 0)),
            pl.BlockSpec((1, GROUP_W), lambda i: (0, 0)),
        ],
        out_specs=pl.BlockSpec((t, GROUP_W), lambda i: (i, 0)),
        compiler_params=_params("arbitrary"),
        name=f"conv_t{t}",
    )(z, z, z, w, b.reshape(1, GROUP_W))


def _outproj_kernel(*refs, tm):
    prompt_refs, sample_refs = refs[0:4], refs[4:8]
    w_ref, x_ref, g1_ref, sh_ref, sc_ref, n2_ref, x1_ref, h2_ref = refs[8:]
    i = pl.program_id(0)
    r = _mod_row(i * tm)

    def run(group_refs):
        acc = None
        for g, ref in enumerate(group_refs):
            part = _dot(ref[...].astype(BF16), w_ref[g * GROUP_W:(g + 1) * GROUP_W, :])
            acc = part if acc is None else acc + part
        x1 = x_ref[...] + g1_ref[pl.ds(r, 1), :] * acc
        x1_ref[...] = x1
        h2 = _rms(x1, n2_ref[...]) * (1.0 + sc_ref[pl.ds(r, 1), :]) + sh_ref[pl.ds(r, 1), :]
        h2_ref[...] = h2.astype(BF16)

    pl.when(i < N_PROMPT // tm)(lambda: run(prompt_refs))
    pl.when(i >= N_PROMPT // tm)(lambda: run(sample_refs))


def _outproj(prompt_groups, sample_groups, w_out_bf, layer, x, mod, norm2_g):
    tm = 512
    n_pb, n_sb = N_PROMPT // tm, N_SAMPLE // tm
    p_spec = pl.BlockSpec((tm, GROUP_W), lambda i: (jnp.minimum(i, n_pb - 1), 0))
    s_spec = pl.BlockSpec((tm, GROUP_W), lambda i: (jnp.clip(i - n_pb, 0, n_sb - 1), 0))
    modspec = lambda k: pl.BlockSpec((MOD_ROWS, D_MODEL), lambda i: (0, k))
    return pl.pallas_call(
        functools.partial(_outproj_kernel, tm=tm),
        out_shape=(jax.ShapeDtypeStruct((N_TOK, D_MODEL), F32), jax.ShapeDtypeStruct((N_TOK, D_MODEL), BF16)),
        grid=(N_TOK // tm,),
        in_specs=[p_spec] * 4 + [s_spec] * 4 + [
            pl.BlockSpec((None, D_MODEL, D_MODEL), lambda i: (layer, 0, 0)),
            pl.BlockSpec((tm, D_MODEL), lambda i: (i, 0)),
            modspec(2), modspec(3), modspec(4),
            pl.BlockSpec((1, D_MODEL), lambda i: (0, 0))],
        out_specs=(pl.BlockSpec((tm, D_MODEL), lambda i: (i, 0)), pl.BlockSpec((tm, D_MODEL), lambda i: (i, 0))),
        compiler_params=_params("arbitrary"),
        name="outproj",
    )(*prompt_groups, *sample_groups, w_out_bf, x, mod, mod, mod, norm2_g.reshape(1, D_MODEL))


def _peer_query_kernel(h_ref, wq_ref, q_ref):
    q_ref[...] = _dot(h_ref[...], wq_ref[...]).astype(BF16)


def _peer_query(h2, wq_bf, layer):
    tm = 512
    width = PEER_HEADS * PEER_QDIM
    return pl.pallas_call(
        _peer_query_kernel,
        out_shape=jax.ShapeDtypeStruct((N_TOK, width), BF16),
        grid=(N_TOK // tm,),
        in_specs=[pl.BlockSpec((tm, D_MODEL), lambda i: (i, 0)),
                  pl.BlockSpec((None, D_MODEL, width), lambda i: (layer, 0, 0))],
        out_specs=pl.BlockSpec((tm, width), lambda i: (i, 0)),
        compiler_params=_params("arbitrary"),
        name="peer_query",
    )(h2, wq_bf)


_CAND_PIECES = ((0, 0, 8), (0, 8, 8), (1, 0, 8), (2, 0, 5), (3, 0, 4), (4, 0, 3), (5, 0, 2), (6, 0, 2), (7, 0, 2),
                (None, 0, 8))
_CAND_ROWS = 8 * len(_CAND_PIECES)
_NO_CAND = 1 << 20


def _topk_rows(x_scr, ids, n_rows, v_out, i_out):
    def body(j, carry):
        x = x_scr[0:n_rows, :]
        m = jnp.max(x, axis=0, keepdims=True)
        idx = jnp.min(jnp.where(x == m, ids, _NO_CAND), axis=0, keepdims=True)
        v_out[pl.ds(j, 1), :] = m
        i_out[pl.ds(j, 1), :] = idx
        x_scr[0:n_rows, :] = jnp.where(ids == idx, -jnp.inf, x)
        return carry

    lax.fori_loop(0, PEER_TOPK, body, 0, unroll=2)


def _peer_topk_kernel(q_ref, k1_ref, k2_ref, e_ref, g_ref, x_scr, v1_scr, i1_scr, v2_scr, i2_scr,
                      ts_scr, tc_scr, *, tm):
    key_ids = lax.broadcasted_iota(jnp.int32, (N_KEYS, tm), 0)
    for half, (k_ref, v_scr, i_scr) in enumerate(((k1_ref, v1_scr, i1_scr), (k2_ref, v2_scr, i2_scr))):
        qh = q_ref[:, half * PEER_HALF:(half + 1) * PEER_HALF]
        x_scr[...] = lax.dot_general(k_ref[...], qh, NT_DIMS, preferred_element_type=F32)
        _topk_rows(x_scr, key_ids, N_KEYS, v_scr, i_scr)
    v1, v2 = v1_scr[...], v2_scr[...]
    sub = lax.broadcasted_iota(jnp.int32, (8, tm), 0)
    cand_ids = []
    for p, (r1, r2, n_valid) in enumerate(_CAND_PIECES):
        if r1 is None:
            vals = v1[8:16] + v2[0:1]
            ids = (sub + 8) * PEER_TOPK
        else:
            vals = v1[r1:r1 + 1] + v2[r2:r2 + 8]
            ids = sub + (r1 * PEER_TOPK + r2)
            if n_valid < 8:
                vals = jnp.where(sub < n_valid, vals, -jnp.inf)
                ids = jnp.where(sub < n_valid, ids, _NO_CAND)
        x_scr[p * 8:(p + 1) * 8, :] = vals
        cand_ids.append(ids)
    _topk_rows(x_scr, jnp.concatenate(cand_ids, axis=0), _CAND_ROWS, ts_scr, tc_scr)
    tc = tc_scr[...]
    c1, c2 = tc // PEER_TOPK, tc % PEER_TOPK
    i1, i2 = i1_scr[...], i2_scr[...]
    e1 = jnp.zeros_like(tc)
    e2 = jnp.zeros_like(tc)
    for a in range(PEER_TOPK):
        e1 = jnp.where(c1 == a, i1[a:a + 1], e1)
        e2 = jnp.where(c2 == a, i2[a:a + 1], e2)
    e_ref[...] = e1 * N_KEYS + e2
    ts = ts_scr[...]
    p = jnp.exp(ts - jnp.max(ts, axis=0, keepdims=True))
    g_ref[...] = p / jnp.sum(p, axis=0, keepdims=True)


def _peer_topk(q, k1_bf, k2_bf, layer):
    tm = 512
    row_f = pltpu.VMEM((PEER_TOPK, tm), F32)
    row_i = pltpu.VMEM((PEER_TOPK, tm), jnp.int32)
    return pl.pallas_call(
        functools.partial(_peer_topk_kernel, tm=tm),
        out_shape=(jax.ShapeDtypeStruct((PEER_HEADS * PEER_TOPK, N_TOK), jnp.int32),
                   jax.ShapeDtypeStruct((PEER_HEADS * PEER_TOPK, N_TOK), F32)),
        grid=(N_TOK // tm, PEER_HEADS),
        in_specs=[
            pl.BlockSpec((tm, PEER_QDIM), lambda i, h: (i, h)),
            pl.BlockSpec((None, N_KEYS, PEER_HALF), lambda i, h: (layer, 0, 0)),
            pl.BlockSpec((None, N_KEYS, PEER_HALF), lambda i, h: (layer, 0, 0)),
        ],
        out_specs=(pl.BlockSpec((PEER_TOPK, tm), lambda i, h: (h, i)),
                   pl.BlockSpec((PEER_TOPK, tm), lambda i, h: (h, i))),
        scratch_shapes=[pltpu.VMEM((N_KEYS, tm), F32), row_f, row_i, row_f, row_i, row_f, row_i],
        compiler_params=_params("arbitrary", "arbitrary"),
        name="peer_topk",
    )(q, k1_bf, k2_bf)


def _peer_mask_kernel(e_ref, g_ref, m_ref, et_scr, gt_scr, slab_scr, *, tb):
    et_scr[...] = e_ref[...].T
    gt_scr[...] = g_ref[...].T
    key = lax.broadcasted_iota(jnp.int32, (N_KEYS, PEER_HEADS * PEER_TOPK), 0)

    def body(t, carry):
        er = et_scr[pl.ds(t, 1), :]
        gr = gt_scr[pl.ds(t, 1), :]
        p1 = jnp.where(key == er // N_KEYS, gr, 0.0).astype(BF16)
        p2 = jnp.where(key == er % N_KEYS, 1.0, 0.0).astype(BF16)
        mt = lax.dot_general(p1, p2, NT_DIMS, preferred_element_type=F32)
        slab_scr[pl.ds(pl.multiple_of(t * MASK_PITCH, 4), N_KEYS), :] = mt
        return carry

    lax.fori_loop(0, tb, body, 0, unroll=64)
    for k1 in range(N_KEYS):
        m_ref[:, k1 * N_KEYS:(k1 + 1) * N_KEYS] = slab_scr[pl.ds(k1, tb, stride=MASK_PITCH), :].astype(BF16)


def _peer_mask(experts_t, gates_t):
    tb = LANES
    pairs = PEER_HEADS * PEER_TOPK
    return pl.pallas_call(
        functools.partial(_peer_mask_kernel, tb=tb),
        out_shape=jax.ShapeDtypeStruct((N_TOK, N_EXPERTS), BF16),
        grid=(N_TOK // tb,),
        in_specs=[pl.BlockSpec((pairs, tb), lambda i: (0, i)), pl.BlockSpec((pairs, tb), lambda i: (0, i))],
        out_specs=pl.BlockSpec((tb, N_EXPERTS), lambda i: (i, 0)),
        scratch_shapes=[pltpu.VMEM((tb, pairs), jnp.int32), pltpu.VMEM((tb, pairs), F32),
                        pltpu.VMEM((tb * MASK_PITCH, N_KEYS), F32)],
        compiler_params=_params("arbitrary"),
        name="peer_mask",
    )(experts_t, gates_t)


def _peer_expert_kernel(h_ref, u_ref, v_ref, m_ref, x_ref, g2_ref, o_ref, *, tm):
    j = pl.program_id(1)

    @pl.when(j == 0)
    def _():
        o_ref[...] = jnp.zeros_like(o_ref)

    s = lax.dot_general(h_ref[...], u_ref[...].astype(BF16), NT_DIMS, preferred_element_type=F32)
    a = (m_ref[...].astype(F32) * jax.nn.gelu(s)).astype(BF16)
    o_ref[...] += _dot(a, v_ref[...].astype(BF16))

    @pl.when(j == pl.num_programs(1) - 1)
    def _():
        r = _mod_row(pl.program_id(0) * tm)
        o_ref[...] = x_ref[...] + g2_ref[pl.ds(r, 1), :] * o_ref[...]


EXPERT_TM, EXPERT_TE = 1024, 512


def _peer_expert(h2, peer_u, peer_v, layer, mask, x1, mod):
    tm, te = EXPERT_TM, EXPERT_TE
    return pl.pallas_call(
        functools.partial(_peer_expert_kernel, tm=tm),
        out_shape=jax.ShapeDtypeStruct((N_TOK, D_MODEL), F32),
        grid=(N_TOK // tm, N_EXPERTS // te),
        in_specs=[
            pl.BlockSpec((tm, D_MODEL), lambda i, j: (i, 0)),
            pl.BlockSpec((None, te, D_MODEL), lambda i, j: (layer, j, 0)),
            pl.BlockSpec((None, te, D_MODEL), lambda i, j: (layer, j, 0)),
            pl.BlockSpec((tm, te), lambda i, j: (i, j)),
            pl.BlockSpec((tm, D_MODEL), lambda i, j: (i, 0), pipeline_mode=pl.Buffered(1)),
            pl.BlockSpec((MOD_ROWS, D_MODEL), lambda i, j: (0, 5)),
        ],
        out_specs=pl.BlockSpec((tm, D_MODEL), lambda i, j: (i, 0)),
        compiler_params=_params("arbitrary", "arbitrary"),
        name="peer_expert",
    )(h2, peer_u, peer_v, mask, x1, mod)


def kernel(x_prompt, x_sample, cache_k, cache_v, state_hgrn, c, c_ctx, ada_w, ada_b, norm1_g, norm2_g, w_in,
           q_norm_g, k_norm_g, hg_lb, hg_onorm_g, conv_w, conv_b, w_out, peer_wq, peer_k1, peer_k2, peer_u, peer_v):
    x = jnp.concatenate([x_prompt.reshape(N_PROMPT, D_MODEL), x_sample.reshape(N_SAMPLE, D_MODEL)], axis=0)
    cvec = jnp.concatenate([c_ctx[None, :], c, jnp.zeros((MOD_ROWS - 1 - DEC_BATCH, D_MODEL), F32)], axis=0)
    mod_all = _ada(cvec, ada_w, ada_b)

    w_in_bf, w_out_bf, wq_bf = w_in.astype(BF16), w_out.astype(BF16), peer_wq.astype(BF16)
    k1_bf, k2_bf = peer_k1.astype(BF16), peer_k2.astype(BF16)

    ks_out, vs_out, ss_out = [], [], []
    for l in range(DEPTH):
        mod = mod_all[l]
        z = _inproj(x, mod, norm1_g[l], w_in_bf, l)

        fn_p = _fourier(z, SEQ, BATCH, 0)
        fn_s = _fourier(z, DEC_SEQ, DEC_BATCH, N_PROMPT)

        at_p, k_l, v_l = _attn_prompt(z, q_norm_g[l], k_norm_g[l])
        k_ctx = cache_k[:, l].reshape(DEC_BATCH, PAST_LEN, N_KV * HEAD_DIM)
        v_ctx = cache_v[:, l].reshape(DEC_BATCH, PAST_LEN, N_KV * HEAD_DIM)
        at_s = _attn_sample(z, q_norm_g[l], k_norm_g[l], k_ctx, v_ctx)

        hg_p, s_l = _hgrn(z, hg_lb, hg_onorm_g[l], l, SEQ, BATCH, 0, None)
        hg_s, _ = _hgrn(z, hg_lb, hg_onorm_g[l], l, DEC_SEQ, DEC_BATCH, N_PROMPT, state_hgrn[:, l])

        cv_p = _conv(z, conv_w[l], conv_b[l], SEQ, BATCH, 0)
        cv_s = _conv(z, conv_w[l], conv_b[l], DEC_SEQ, DEC_BATCH, N_PROMPT)

        x1, h2 = _outproj((fn_p, at_p, hg_p, cv_p), (fn_s, at_s, hg_s, cv_s), w_out_bf, l, x, mod, norm2_g[l])

        q_peer = _peer_query(h2, wq_bf, l)
        experts_t, gates_t = _peer_topk(q_peer, k1_bf, k2_bf, l)
        mask = _peer_mask(experts_t, gates_t)
        x = _peer_expert(h2, peer_u, peer_v, l, mask, x1, mod)

        ks_out.append(k_l.reshape(BATCH, SEQ, N_KV, HEAD_DIM))
        vs_out.append(v_l.reshape(BATCH, SEQ, N_KV, HEAD_DIM))
        ss_out.append(s_l)

    y_prompt = x[:N_PROMPT].reshape(BATCH, SEQ, D_MODEL)
    y_sample = x[N_PROMPT:].reshape(DEC_BATCH, DEC_SEQ, D_MODEL)
    return (y_prompt, y_sample, jnp.stack(ks_out, axis=1), jnp.stack(vs_out, axis=1), jnp.stack(ss_out, axis=1))
```

```python
import functools

import numpy as np
import jax
import jax.numpy as jnp
from jax import lax
from jax.experimental import pallas as pl
from jax.experimental.pallas import tpu as pltpu

F32 = jnp.float32
BF16 = jnp.bfloat16

D_MODEL = 2048
BATCH = 32
SEQ = 256
DEPTH = 2
DEC_BATCH = 4
DEC_SEQ = 1024
PAST_LEN = 256
GRID_W = 64
GROUP_W = D_MODEL // 4
FN_GROUPS = 4
FN_GW = GROUP_W // FN_GROUPS
HEAD_DIM = 128
N_HEADS = GROUP_W // HEAD_DIM
N_KV = 2
GQA_GROUP = N_HEADS // N_KV
ROPE_THETA = 10000.0
HG_DK = 128
HG_DV = 128
HG_HEADS = GROUP_W // HG_DK
CONV_K = 3
IN_COLS = 11 * GROUP_W
PEER_HEADS = 8
N_KEYS = 128
N_EXPERTS = N_KEYS * N_KEYS
PEER_QDIM = 256
PEER_HALF = PEER_QDIM // 2
PEER_TOPK = 16
EPS = 1e-6
F_FLOOR = 1e-30

N_PROMPT = BATCH * SEQ
N_SAMPLE = DEC_BATCH * DEC_SEQ
N_TOK = N_PROMPT + N_SAMPLE
MOD_ROWS = 8
LANES = 128
NORM_ROWS = 128
HG_CHUNK = 32
HG_GROUP = 8
HG_UNROLL = 16
HG_CUM_BLOCK = 256
MASK_PITCH = 132
VMEM_LIMIT = 56 * 1024 * 1024

COL_FN, COL_Q, COL_K, COL_V = 0, 4, 8, 10
COL_GQ, COL_GFF, COL_GFB, COL_GI, COL_GO = 12, 16, 20, 24, 28
COL_CB, COL_CC, COL_CX = 32, 36, 40

NT_DIMS = (((1,), (1,)), ((), ()))
TN_DIMS = (((0,), (0,)), ((), ()))


def _params(*sem):
    return pltpu.CompilerParams(dimension_semantics=sem, vmem_limit_bytes=VMEM_LIMIT)


def _dot(a, b):
    return jnp.dot(a, b, preferred_element_type=F32)


def _split3(x):
    h = x.astype(BF16)
    r = x - h.astype(F32)
    m = r.astype(BF16)
    l = (r - m.astype(F32)).astype(BF16)
    return h, m, l


def _split2(x):
    h = x.astype(BF16)
    return h, (x - h.astype(F32)).astype(BF16)


def _mod_row(tok0):
    return jnp.where(tok0 >= N_PROMPT, (tok0 - N_PROMPT) // DEC_SEQ + 1, 0)


def _rms(x, g):
    return x * lax.rsqrt(jnp.mean(x * x, axis=-1, keepdims=True) + EPS) * g


def _ada_kernel(c_ref, w_ref, b_ref, o_ref):
    c = c_ref[...]
    s = (c * jax.nn.sigmoid(c)).astype(BF16)
    o_ref[0] = _dot(s, w_ref[0].astype(BF16)) + b_ref[0]


def _ada(cvec, ada_w, ada_b):
    tn = 1024
    return pl.pallas_call(
        _ada_kernel,
        out_shape=jax.ShapeDtypeStruct((DEPTH, MOD_ROWS, 6 * D_MODEL), F32),
        grid=(DEPTH, 6 * D_MODEL // tn),
        in_specs=[
            pl.BlockSpec((MOD_ROWS, D_MODEL), lambda l, j: (0, 0)),
            pl.BlockSpec((1, D_MODEL, tn), lambda l, j: (l, 0, j)),
            pl.BlockSpec((1, 1, tn), lambda l, j: (l, 0, j)),
        ],
        out_specs=pl.BlockSpec((1, MOD_ROWS, tn), lambda l, j: (l, 0, j)),
        compiler_params=_params("arbitrary", "arbitrary"),
        name="ada",
    )(cvec, ada_w, ada_b.reshape(DEPTH, 1, 6 * D_MODEL))


def _inproj_kernel(x_ref, sh_ref, sc_ref, g_ref, w_ref, o_ref, h_ref, *, tm):
    @pl.when(pl.program_id(1) == 0)
    def _():
        r = _mod_row(pl.program_id(0) * tm)
        gain = g_ref[...] * (1.0 + sc_ref[pl.ds(r, 1), :])
        shift = sh_ref[pl.ds(r, 1), :]

        def norm_rows(c, carry):
            rows = pl.ds(pl.multiple_of(c * NORM_ROWS, NORM_ROWS), NORM_ROWS)
            x = x_ref[rows, :]
            y = x * lax.rsqrt(jnp.mean(x * x, axis=-1, keepdims=True) + EPS)
            h_ref[rows, :] = (y * gain + shift).astype(BF16)
            return carry

        lax.fori_loop(0, tm // NORM_ROWS, norm_rows, 0)

    o_ref[...] = _dot(h_ref[...], w_ref[...])


def _inproj(x, mod, norm_g, w_in_bf, layer):
    tm, tn = 1024, 512
    return pl.pallas_call(
        functools.partial(_inproj_kernel, tm=tm),
        out_shape=jax.ShapeDtypeStruct((N_TOK, IN_COLS), F32),
        grid=(N_TOK // tm, IN_COLS // tn),
        in_specs=[
            pl.BlockSpec((tm, D_MODEL), lambda i, j: (i, 0)),
            pl.BlockSpec((MOD_ROWS, D_MODEL), lambda i, j: (0, 0)),
            pl.BlockSpec((MOD_ROWS, D_MODEL), lambda i, j: (0, 1)),
            pl.BlockSpec((1, D_MODEL), lambda i, j: (0, 0)),
            pl.BlockSpec((None, D_MODEL, tn), lambda i, j: (layer, 0, j)),
        ],
        out_specs=pl.BlockSpec((tm, tn), lambda i, j: (i, j)),
        scratch_shapes=[pltpu.VMEM((tm, D_MODEL), BF16)],
        compiler_params=_params("arbitrary", "arbitrary"),
        name="inproj",
    )(x, mod, mod, norm_g.reshape(1, D_MODEL), w_in_bf)


def _fourier_kernel(u_ref, cch_ref, ccl_ref, cth_ref, ctl_ref, o_ref):
    cch, ccl, cth, ctl = cch_ref[...], ccl_ref[...], cth_ref[...], ctl_ref[...]
    for g in range(0, FN_GROUPS, 2):
        ws = []
        for gg in (g, g + 1):
            uh, ul = _split2(u_ref[:, gg * FN_GW:(gg + 1) * FN_GW])
            ws.append(_dot(uh, cch) + (_dot(ul, cch) + _dot(uh, ccl)))
        st = jnp.concatenate([jnp.concatenate([w[:, :FN_GW] for w in ws], axis=1),
                              jnp.concatenate([w[:, FN_GW:] for w in ws], axis=1)], axis=0)
        sh, sl = _split2(st)
        o_ref[:, g * FN_GW:(g + 2) * FN_GW] = _dot(cth, sh) + (_dot(cth, sl) + _dot(ctl, sh))


def _dft_consts(t):
    def cs(n):
        k = np.arange(n, dtype=np.float64)
        ang = 2.0 * np.pi * np.outer(k, k) / n
        return np.cos(ang) / np.sqrt(n), np.sin(ang) / np.sqrt(n)

    cc, sc = cs(FN_GW)
    ct, st = cs(t)
    right = np.concatenate([cc, sc], axis=1).astype(np.float32)
    left = np.concatenate([ct, -st], axis=1).astype(np.float32)

    def hl(a):
        a = jnp.asarray(a)
        h = a.astype(BF16)
        return h, (a - h.astype(F32)).astype(BF16)

    return hl(right) + hl(left)


def _fourier(z, t, nb, row0):
    cch, ccl, cth, ctl = _dft_consts(t)
    rb = row0 // t
    return pl.pallas_call(
        _fourier_kernel,
        out_shape=jax.ShapeDtypeStruct((nb * t, GROUP_W), F32),
        grid=(nb,),
        in_specs=[
            pl.BlockSpec((t, GROUP_W), lambda b: (rb + b, COL_FN // 4)),
            pl.BlockSpec((FN_GW, 2 * FN_GW), lambda b: (0, 0)),
            pl.BlockSpec((FN_GW, 2 * FN_GW), lambda b: (0, 0)),
            pl.BlockSpec((t, 2 * t), lambda b: (0, 0)),
            pl.BlockSpec((t, 2 * t), lambda b: (0, 0)),
        ],
        out_specs=pl.BlockSpec((t, GROUP_W), lambda b: (b, 0)),
        compiler_params=_params("arbitrary"),
        name=f"fourier_t{t}",
    )(z, cch, ccl, cth, ctl)


def _softmax_pv(q_bf, k_bf, v_bf):
    s = lax.dot_general(q_bf, k_bf, NT_DIMS, preferred_element_type=F32) * (HEAD_DIM ** -0.5)
    p = jnp.exp(s - jnp.max(s, axis=-1, keepdims=True))
    return _dot(p.astype(BF16), v_bf) / jnp.sum(p, axis=-1, keepdims=True)


def _attn_prompt_kernel(q_ref, k_ref, v_ref, qg_ref, kg_ref, o_ref, kc_ref, vc_ref):
    kn = _rms(k_ref[...], kg_ref[...])
    v = v_ref[...]
    kc_ref[...] = kn
    vc_ref[...] = v
    kb, vb = kn.astype(BF16), v.astype(BF16)
    for g in range(GQA_GROUP):
        qn = _rms(q_ref[:, g * HEAD_DIM:(g + 1) * HEAD_DIM], qg_ref[...])
        o_ref[:, g * HEAD_DIM:(g + 1) * HEAD_DIM] = _softmax_pv(qn.astype(BF16), kb, vb)


def _attn_prompt(z, q_g, k_g):
    t = SEQ
    kv_shape = jax.ShapeDtypeStruct((N_PROMPT, N_KV * HEAD_DIM), F32)
    return pl.pallas_call(
        _attn_prompt_kernel,
        out_shape=(jax.ShapeDtypeStruct((N_PROMPT, GROUP_W), F32), kv_shape, kv_shape),
        grid=(BATCH, N_KV),
        in_specs=[
            pl.BlockSpec((t, GQA_GROUP * HEAD_DIM), lambda b, h: (b, COL_Q // 2 + h)),
            pl.BlockSpec((t, HEAD_DIM), lambda b, h: (b, COL_K + h)),
            pl.BlockSpec((t, HEAD_DIM), lambda b, h: (b, COL_V + h)),
            pl.BlockSpec((1, HEAD_DIM), lambda b, h: (0, 0)),
            pl.BlockSpec((1, HEAD_DIM), lambda b, h: (0, 0)),
        ],
        out_specs=(
            pl.BlockSpec((t, GQA_GROUP * HEAD_DIM), lambda b, h: (b, h)),
            pl.BlockSpec((t, HEAD_DIM), lambda b, h: (b, h)),
            pl.BlockSpec((t, HEAD_DIM), lambda b, h: (b, h)),
        ),
        compiler_params=_params("arbitrary", "arbitrary"),
        name="attn_prompt",
    )(z, z, z, q_g.reshape(1, HEAD_DIM), k_g.reshape(1, HEAD_DIM))


def _rope(x, cosf, sinf):
    return x * cosf + pltpu.roll(x, HEAD_DIM // 2, axis=1) * sinf


def _attn_sample_kernel(q_ref, k_ref, v_ref, kctx_ref, vctx_ref, qg_ref, kg_ref, cos_ref, sin_ref,
                        o_ref, kall_ref, vall_ref, *, qb):
    cosf, sinf = cos_ref[...], sin_ref[...]
    kall_ref[0:PAST_LEN, :] = kctx_ref[0].astype(BF16)
    vall_ref[0:PAST_LEN, :] = vctx_ref[0].astype(BF16)
    kall_ref[PAST_LEN:, :] = _rope(_rms(k_ref[...], kg_ref[...]), cosf, sinf).astype(BF16)
    vall_ref[PAST_LEN:, :] = v_ref[...].astype(BF16)
    kb, vb = kall_ref[...], vall_ref[...]
    for i in range(DEC_SEQ // qb):
        rows = slice(i * qb, (i + 1) * qb)
        for g in range(GQA_GROUP):
            cols = slice(g * HEAD_DIM, (g + 1) * HEAD_DIM)
            qn = _rope(_rms(q_ref[rows, cols], qg_ref[...]), cosf[rows], sinf[rows])
            o_ref[rows, cols] = _softmax_pv(qn.astype(BF16), kb, vb)


def _rope_tables():
    rows = DEC_SEQ // GRID_W
    row = np.repeat(np.arange(rows, dtype=np.float32), GRID_W)
    col = np.tile(np.arange(GRID_W, dtype=np.float32), rows)
    n_freq = HEAD_DIM // 4
    inv = (np.float32(ROPE_THETA) ** (-np.arange(n_freq, dtype=np.float32) / n_freq)).astype(np.float32)
    ang = np.concatenate([row[:, None] * inv, col[:, None] * inv], axis=-1).astype(np.float32)
    cos, sin = np.cos(ang.astype(np.float64)), np.sin(ang.astype(np.float64))
    cosf = np.concatenate([cos, cos], axis=-1).astype(np.float32)
    sinf = np.concatenate([-sin, sin], axis=-1).astype(np.float32)
    return jnp.asarray(cosf), jnp.asarray(sinf)


def _attn_sample(z, q_g, k_g, k_ctx, v_ctx):
    t = DEC_SEQ
    rb = N_PROMPT // t
    cosf, sinf = _rope_tables()
    const = lambda b, h: (0, 0)
    return pl.pallas_call(
        functools.partial(_attn_sample_kernel, qb=256),
        out_shape=jax.ShapeDtypeStruct((N_SAMPLE, GROUP_W), F32),
        grid=(DEC_BATCH, N_KV),
        in_specs=[
            pl.BlockSpec((t, GQA_GROUP * HEAD_DIM), lambda b, h: (rb + b, COL_Q // 2 + h)),
            pl.BlockSpec((t, HEAD_DIM), lambda b, h: (rb + b, COL_K + h)),
            pl.BlockSpec((t, HEAD_DIM), lambda b, h: (rb + b, COL_V + h)),
            pl.BlockSpec((1, PAST_LEN, HEAD_DIM), lambda b, h: (b, 0, h)),
            pl.BlockSpec((1, PAST_LEN, HEAD_DIM), lambda b, h: (b, 0, h)),
            pl.BlockSpec((1, HEAD_DIM), const),
            pl.BlockSpec((1, HEAD_DIM), const),
            pl.BlockSpec((t, HEAD_DIM), const),
            pl.BlockSpec((t, HEAD_DIM), const),
        ],
        out_specs=pl.BlockSpec((t, GQA_GROUP * HEAD_DIM), lambda b, h: (b, h)),
        scratch_shapes=[pltpu.VMEM((PAST_LEN + t, HEAD_DIM), BF16), pltpu.VMEM((PAST_LEN + t, HEAD_DIM), BF16)],
        compiler_params=_params("arbitrary", "arbitrary"),
        name="attn_sample",
    )(z, z, z, k_ctx, v_ctx, q_g.reshape(1, HEAD_DIM), k_g.reshape(1, HEAD_DIM), cosf, sinf)


def _hgrn_chain(rev, r0, gq_ref, gi_ref, k_scr, cum_scr, st_scr, o_scr):
    c = HG_CHUNK
    d = 1 if rev else 0
    rows = pl.ds(r0, c)
    qc, vc = gq_ref[rows, :], gi_ref[rows, :]
    kc, cc = k_scr[d, rows, :], cum_scr[d, rows, :]
    st = st_scr[d]
    last = cc[0:1] if rev else cc[c - 1:c]
    qs = (qc * jnp.exp(cc)).astype(BF16)
    o_inter = lax.dot_general(qs, st.astype(BF16), NT_DIMS, preferred_element_type=F32)
    ks = (kc * jnp.exp(last - cc)).astype(BF16)
    upd = lax.dot_general(vc.astype(BF16), ks, TN_DIMS, preferred_element_type=F32)
    st_scr[d] = st * jnp.exp(last) + upd
    n_grp = c // HG_GROUP
    row = lax.broadcasted_iota(jnp.int32, (HG_GROUP, 1), 0)
    krow = lax.broadcasted_iota(jnp.int32, (c, 1), 0)
    diag, att = [], []
    for i in range(n_grp):
        lo = i * HG_GROUP
        grp = slice(lo, lo + HG_GROUP)
        q_i, k_i, v_i, c_i = qc[grp], kc[grp], vc[grp], cc[grp]
        acc = jnp.zeros((HG_GROUP, HG_DV), F32)
        for delta in range(HG_GROUP):
            if delta == 0:
                k_sh, c_sh, v_sh = k_i, c_i, v_i
            else:
                sh = (HG_GROUP - delta) if rev else delta
                k_sh, c_sh, v_sh = (pltpu.roll(a, sh, axis=0) for a in (k_i, c_i, v_i))
            a = jnp.sum(q_i * k_sh * jnp.exp(jnp.minimum(c_i - c_sh, 0.0)), axis=-1, keepdims=True)
            valid = (row <= HG_GROUP - 1 - delta) if rev else (row >= delta)
            acc = acc + jnp.where(valid, a, 0.0) * v_sh
        diag.append(acc)
        if (i == n_grp - 1) if rev else (i == 0):
            att.append(jnp.zeros((HG_GROUP, c), F32))
            continue
        if rev:
            b, other = cc[lo + HG_GROUP:lo + HG_GROUP + 1], krow >= lo + HG_GROUP
        else:
            b, other = cc[lo - 1:lo], krow < lo
        qs_i = (q_i * jnp.exp(c_i - b)).astype(BF16)
        ks_i = jnp.where(other, kc * jnp.exp(jnp.minimum(b - cc, 0.0)), 0.0).astype(BF16)
        att.append(lax.dot_general(qs_i, ks_i, NT_DIMS, preferred_element_type=F32))
    o_off = _dot(jnp.concatenate(att, axis=0).astype(BF16), vc.astype(BF16))
    o_scr[d, rows, :] = o_inter + jnp.concatenate(diag, axis=0) + o_off


def _hgrn_kernel(*refs, layer, t, has_ctx):
    gq_ref, gff_ref, gfb_ref, gi_ref, go_ref, lb_ref, on_ref, bdf_ref, bdb_ref = refs[:9]
    n_in = 9
    s0_ref = None
    if has_ctx:
        s0_ref = refs[9]
        n_in = 10
    o_ref, sout_ref = refs[n_in], refs[n_in + 1]
    k_scr, cum_scr, st_scr, o_scr = refs[n_in + 2:]

    raw = lb_ref[...]
    e = jnp.exp(raw - jnp.max(raw, axis=0, keepdims=True))
    soft = e / jnp.sum(e, axis=0, keepdims=True)
    csum = soft[0]
    for i in range(1, layer + 1):
        csum = csum + soft[i]
    lb = csum - soft[0]

    for d, (g_ref, bd_ref) in enumerate(((gff_ref, bdf_ref), (gfb_ref, bdb_ref))):
        lbd = lb[d:d + 1]
        f = jnp.maximum(lbd + (1.0 - lbd) * jax.nn.sigmoid(g_ref[...]), F_FLOOR)
        k_scr[d] = 1.0 - f
        lf = jnp.log(f)
        bd = bd_ref[...]
        for p in range(t // HG_CUM_BLOCK):
            rows = slice(p * HG_CUM_BLOCK, (p + 1) * HG_CUM_BLOCK)
            h, m, l = _split3(lf[rows])
            cum_scr[d, rows, :] = _dot(bd, h) + _dot(bd, m) + _dot(bd, l)
        if has_ctx:
            st_scr[d] = s0_ref[0, d, 0].T
        else:
            st_scr[d] = jnp.zeros((HG_DV, HG_DK), F32)

    n = t // HG_CHUNK

    def body(i, carry):
        _hgrn_chain(False, pl.multiple_of(i * HG_CHUNK, HG_CHUNK), gq_ref, gi_ref, k_scr, cum_scr, st_scr, o_scr)
        _hgrn_chain(True, pl.multiple_of((n - 1 - i) * HG_CHUNK, HG_CHUNK), gq_ref, gi_ref, k_scr, cum_scr,
                    st_scr, o_scr)
        return carry

    lax.fori_loop(0, n, body, 0, unroll=min(n, HG_UNROLL))

    o = _rms(o_scr[0] + o_scr[1], on_ref[...])
    go = go_ref[...]
    o_ref[...] = o * (go * jax.nn.sigmoid(go))
    for d in range(2):
        sout_ref[0, d, 0] = st_scr[d].T


def _cum_consts():
    r = np.arange(HG_CUM_BLOCK)
    same = (r[:, None] // HG_CHUNK) == (r[None, :] // HG_CHUNK)
    fwd = same & (r[None, :] <= r[:, None])
    bwd = same & (r[None, :] >= r[:, None])
    return jnp.asarray(fwd, BF16), jnp.asarray(bwd, BF16)


def _hgrn(z, hg_lb, onorm_g, layer, t, nb, row0, s0):
    rb = row0 // t
    bdf, bdb = _cum_consts()
    has_ctx = s0 is not None
    col = lambda c0: (lambda b, h: (rb + b, c0 + h))
    in_specs = [
        pl.BlockSpec((t, HG_DK), col(COL_GQ)),
        pl.BlockSpec((t, HG_DK), col(COL_GFF)),
        pl.BlockSpec((t, HG_DK), col(COL_GFB)),
        pl.BlockSpec((t, HG_DV), col(COL_GI)),
        pl.BlockSpec((t, HG_DV), col(COL_GO)),
        pl.BlockSpec((DEPTH, 2, HG_DK), lambda b, h: (0, 0, h)),
        pl.BlockSpec((1, HG_DV), lambda b, h: (0, 0)),
        pl.BlockSpec((HG_CUM_BLOCK, HG_CUM_BLOCK), lambda b, h: (0, 0)),
        pl.BlockSpec((HG_CUM_BLOCK, HG_CUM_BLOCK), lambda b, h: (0, 0)),
    ]
    args = [z, z, z, z, z, hg_lb, onorm_g.reshape(1, HG_DV), bdf, bdb]
    if has_ctx:
        in_specs.append(pl.BlockSpec((1, 2, 1, HG_DK, HG_DV), lambda b, h: (b, 0, h, 0, 0)))
        args.append(s0)
    return pl.pallas_call(
        functools.partial(_hgrn_kernel, layer=layer, t=t, has_ctx=has_ctx),
        out_shape=(jax.ShapeDtypeStruct((nb * t, GROUP_W), F32),
                   jax.ShapeDtypeStruct((nb, 2, HG_HEADS, HG_DK, HG_DV), F32)),
        grid=(nb, HG_HEADS),
        in_specs=in_specs,
        out_specs=(pl.BlockSpec((t, HG_DV), lambda b, h: (b, h)),
                   pl.BlockSpec((1, 2, 1, HG_DK, HG_DV), lambda b, h: (b, 0, h, 0, 0))),
        scratch_shapes=[pltpu.VMEM((2, t, HG_DK), F32), pltpu.VMEM((2, t, HG_DK), F32),
                        pltpu.VMEM((2, HG_DV, HG_DK), F32), pltpu.VMEM((2, t, HG_DV), F32)],
        compiler_params=_params("arbitrary", "arbitrary"),
        name=f"hgrn_t{t}",
    )(*args)


def _conv_kernel(cb_ref, cc_ref, cx_ref, w_ref, b_ref, o_ref, *, t):
    p = cc_ref[...] * cx_ref[...]
    row = lax.broadcasted_iota(jnp.int32, (t, 1), 0)
    prev = jnp.where(row >= 1, pltpu.roll(p, 1, axis=0), 0.0)
    nxt = jnp.where(row <= t - 2, pltpu.roll(p, t - 1, axis=0), 0.0)
    w = w_ref[...]
    y = prev * w[0:1] + p * w[1:2] + nxt * w[2:3] + b_ref[...]
    o_ref[...] = cb_ref[...] * y


def _conv(z, w, b, t, nb, row0):
    rb = row0 // t
    return pl.pallas_call(
        functools.partial(_conv_kernel, t=t),
        out_shape=jax.ShapeDtypeStruct((nb * t, GROUP_W), F32),
        grid=(nb,),
        in_specs=[
            pl.BlockSpec((t, GROUP_W), lambda i: (rb + i, COL_CB // 4)),
            pl.BlockSpec((t, GROUP_W), lambda i: (rb + i, COL_CC // 4)),
            pl.BlockSpec((t, GROUP_W), lambda i: (rb + i, COL_CX // 4)),
            pl.BlockSpec((CONV_K, GROUP_W), lambda i: (0, 0)),
            pl.BlockSpec((1, GROUP_W), lambda i: (0, 0)),
        ],
        out_specs=pl.BlockSpec((t, GROUP_W), lambda i: (i, 0)),
        compiler_params=_params("arbitrary"),
        name=f"conv_t{t}",
    )(z, z, z, w, b.reshape(1, GROUP_W))


def _outproj_kernel(*refs, tm):
    prompt_refs, sample_refs = refs[0:4], refs[4:8]
    w_ref, x_ref, g1_ref, sh_ref, sc_ref, n2_ref, x1_ref, h2_ref = refs[8:]
    i = pl.program_id(0)
    r = _mod_row(i * tm)

    def run(group_refs):
        acc = None
        for g, ref in enumerate(group_refs):
            part = _dot(ref[...].astype(BF16), w_ref[g * GROUP_W:(g + 1) * GROUP_W, :])
            acc = part if acc is None else acc + part
        x1 = x_ref[...] + g1_ref[pl.ds(r, 1), :] * acc
        x1_ref[...] = x1
        h2 = _rms(x1, n2_ref[...]) * (1.0 + sc_ref[pl.ds(r, 1), :]) + sh_ref[pl.ds(r, 1), :]
        h2_ref[...] = h2.astype(BF16)

    pl.when(i < N_PROMPT // tm)(lambda: run(prompt_refs))
    pl.when(i >= N_PROMPT // tm)(lambda: run(sample_refs))


def _outproj(prompt_groups, sample_groups, w_out_bf, layer, x, mod, norm2_g):
    tm = 512
    n_pb, n_sb = N_PROMPT // tm, N_SAMPLE // tm
    p_spec = pl.BlockSpec((tm, GROUP_W), lambda i: (jnp.minimum(i, n_pb - 1), 0))
    s_spec = pl.BlockSpec((tm, GROUP_W), lambda i: (jnp.clip(i - n_pb, 0, n_sb - 1), 0))
    modspec = lambda k: pl.BlockSpec((MOD_ROWS, D_MODEL), lambda i: (0, k))
    return pl.pallas_call(
        functools.partial(_outproj_kernel, tm=tm),
        out_shape=(jax.ShapeDtypeStruct((N_TOK, D_MODEL), F32), jax.ShapeDtypeStruct((N_TOK, D_MODEL), BF16)),
        grid=(N_TOK // tm,),
        in_specs=[p_spec] * 4 + [s_spec] * 4 + [
            pl.BlockSpec((None, D_MODEL, D_MODEL), lambda i: (layer, 0, 0)),
            pl.BlockSpec((tm, D_MODEL), lambda i: (i, 0)),
            modspec(2), modspec(3), modspec(4),
            pl.BlockSpec((1, D_MODEL), lambda i: (0, 0))],
        out_specs=(pl.BlockSpec((tm, D_MODEL), lambda i: (i, 0)), pl.BlockSpec((tm, D_MODEL), lambda i: (i, 0))),
        compiler_params=_params("arbitrary"),
        name="outproj",
    )(*prompt_groups, *sample_groups, w_out_bf, x, mod, mod, mod, norm2_g.reshape(1, D_MODEL))


def _peer_query_kernel(h_ref, wq_ref, q_ref):
    q_ref[...] = _dot(h_ref[...], wq_ref[...]).astype(BF16)


def _peer_query(h2, wq_bf, layer):
    tm = 512
    width = PEER_HEADS * PEER_QDIM
    return pl.pallas_call(
        _peer_query_kernel,
        out_shape=jax.ShapeDtypeStruct((N_TOK, width), BF16),
        grid=(N_TOK // tm,),
        in_specs=[pl.BlockSpec((tm, D_MODEL), lambda i: (i, 0)),
                  pl.BlockSpec((None, D_MODEL, width), lambda i: (layer, 0, 0))],
        out_specs=pl.BlockSpec((tm, width), lambda i: (i, 0)),
        compiler_params=_params("arbitrary"),
        name="peer_query",
    )(h2, wq_bf)


_CAND_PIECES = ((0, 0, 8), (0, 8, 8), (1, 0, 8), (2, 0, 5), (3, 0, 4), (4, 0, 3), (5, 0, 2), (6, 0, 2), (7, 0, 2),
                (None, 0, 8))
_CAND_ROWS = 8 * len(_CAND_PIECES)
_NO_CAND = 1 << 20


def _topk_rows(x_scr, ids, n_rows, v_out, i_out):
    def body(j, carry):
        x = x_scr[0:n_rows, :]
        m = jnp.max(x, axis=0, keepdims=True)
        idx = jnp.min(jnp.where(x == m, ids, _NO_CAND), axis=0, keepdims=True)
        v_out[pl.ds(j, 1), :] = m
        i_out[pl.ds(j, 1), :] = idx
        x_scr[0:n_rows, :] = jnp.where(ids == idx, -jnp.inf, x)
        return carry

    lax.fori_loop(0, PEER_TOPK, body, 0, unroll=2)


def _peer_topk_kernel(q_ref, k1_ref, k2_ref, u_ref, v_ref, e_ref, g_ref, ub_ref, vb_ref, x_scr, v1_scr, i1_scr,
                      v2_scr, i2_scr, ts_scr, tc_scr, *, tm):
    ub_ref[...] = u_ref[...].astype(BF16)
    vb_ref[...] = v_ref[...].astype(BF16)
    key_ids = lax.broadcasted_iota(jnp.int32, (N_KEYS, tm), 0)
    for half, (k_ref, v_scr, i_scr) in enumerate(((k1_ref, v1_scr, i1_scr), (k2_ref, v2_scr, i2_scr))):
        qh = q_ref[:, half * PEER_HALF:(half + 1) * PEER_HALF]
        x_scr[...] = lax.dot_general(k_ref[...], qh, NT_DIMS, preferred_element_type=F32)
        _topk_rows(x_scr, key_ids, N_KEYS, v_scr, i_scr)
    v1, v2 = v1_scr[...], v2_scr[...]
    sub = lax.broadcasted_iota(jnp.int32, (8, tm), 0)
    cand_ids = []
    for p, (r1, r2, n_valid) in enumerate(_CAND_PIECES):
        if r1 is None:
            vals = v1[8:16] + v2[0:1]
            ids = (sub + 8) * PEER_TOPK
        else:
            vals = v1[r1:r1 + 1] + v2[r2:r2 + 8]
            ids = sub + (r1 * PEER_TOPK + r2)
            if n_valid < 8:
                vals = jnp.where(sub < n_valid, vals, -jnp.inf)
                ids = jnp.where(sub < n_valid, ids, _NO_CAND)
        x_scr[p * 8:(p + 1) * 8, :] = vals
        cand_ids.append(ids)
    _topk_rows(x_scr, jnp.concatenate(cand_ids, axis=0), _CAND_ROWS, ts_scr, tc_scr)
    tc = tc_scr[...]
    c1, c2 = tc // PEER_TOPK, tc % PEER_TOPK
    i1, i2 = i1_scr[...], i2_scr[...]
    e1 = jnp.zeros_like(tc)
    e2 = jnp.zeros_like(tc)
    for a in range(PEER_TOPK):
        e1 = jnp.where(c1 == a, i1[a:a + 1], e1)
        e2 = jnp.where(c2 == a, i2[a:a + 1], e2)
    e_ref[...] = e1 * N_KEYS + e2
    ts = ts_scr[...]
    p = jnp.exp(ts - jnp.max(ts, axis=0, keepdims=True))
    g_ref[...] = p / jnp.sum(p, axis=0, keepdims=True)


def _peer_topk(q, k1_bf, k2_bf, peer_u, peer_v, layer):
    tm = 512
    slab = LANES
    n_slabs = N_EXPERTS // slab
    assert (N_TOK // tm) * PEER_HEADS >= n_slabs
    row_f = pltpu.VMEM((PEER_TOPK, tm), F32)
    row_i = pltpu.VMEM((PEER_TOPK, tm), jnp.int32)
    slab_of = lambda i, h: jnp.minimum(i * PEER_HEADS + h, n_slabs - 1)
    table_in = pl.BlockSpec((None, slab, D_MODEL), lambda i, h: (layer, slab_of(i, h), 0))
    table_out = pl.BlockSpec((slab, D_MODEL), lambda i, h: (slab_of(i, h), 0))
    table_shape = jax.ShapeDtypeStruct((N_EXPERTS, D_MODEL), BF16)
    return pl.pallas_call(
        functools.partial(_peer_topk_kernel, tm=tm),
        out_shape=(jax.ShapeDtypeStruct((PEER_HEADS * PEER_TOPK, N_TOK), jnp.int32),
                   jax.ShapeDtypeStruct((PEER_HEADS * PEER_TOPK, N_TOK), F32), table_shape, table_shape),
        grid=(N_TOK // tm, PEER_HEADS),
        in_specs=[
            pl.BlockSpec((tm, PEER_QDIM), lambda i, h: (i, h)),
            pl.BlockSpec((None, N_KEYS, PEER_HALF), lambda i, h: (layer, 0, 0)),
            pl.BlockSpec((None, N_KEYS, PEER_HALF), lambda i, h: (layer, 0, 0)),
            table_in, table_in,
        ],
        out_specs=(pl.BlockSpec((PEER_TOPK, tm), lambda i, h: (h, i)),
                   pl.BlockSpec((PEER_TOPK, tm), lambda i, h: (h, i)), table_out, table_out),
        scratch_shapes=[pltpu.VMEM((N_KEYS, tm), F32), row_f, row_i, row_f, row_i, row_f, row_i],
        compiler_params=_params("arbitrary", "arbitrary"),
        name="peer_topk",
    )(q, k1_bf, k2_bf, peer_u, peer_v)


def _peer_mask_kernel(e_ref, g_ref, m_ref, et_scr, gt_scr, slab_scr, *, tb):
    et_scr[...] = e_ref[...].T
    gt_scr[...] = g_ref[...].T
    key = lax.broadcasted_iota(jnp.int32, (N_KEYS, PEER_HEADS * PEER_TOPK), 0)

    def body(t, carry):
        er = et_scr[pl.ds(t, 1), :]
        gr = gt_scr[pl.ds(t, 1), :]
        p1 = jnp.where(key == er // N_KEYS, gr, 0.0).astype(BF16)
        p2 = jnp.where(key == er % N_KEYS, 1.0, 0.0).astype(BF16)
        mt = lax.dot_general(p1, p2, NT_DIMS, preferred_element_type=F32)
        slab_scr[pl.ds(pl.multiple_of(t * MASK_PITCH, 4), N_KEYS), :] = mt
        return carry

    lax.fori_loop(0, tb, body, 0, unroll=64)
    for k1 in range(N_KEYS):
        m_ref[:, k1 * N_KEYS:(k1 + 1) * N_KEYS] = slab_scr[pl.ds(k1, tb, stride=MASK_PITCH), :].astype(BF16)


def _peer_mask(experts_t, gates_t):
    tb = LANES
    pairs = PEER_HEADS * PEER_TOPK
    return pl.pallas_call(
        functools.partial(_peer_mask_kernel, tb=tb),
        out_shape=jax.ShapeDtypeStruct((N_TOK, N_EXPERTS), BF16),
        grid=(N_TOK // tb,),
        in_specs=[pl.BlockSpec((pairs, tb), lambda i: (0, i)), pl.BlockSpec((pairs, tb), lambda i: (0, i))],
        out_specs=pl.BlockSpec((tb, N_EXPERTS), lambda i: (i, 0)),
        scratch_shapes=[pltpu.VMEM((tb, pairs), jnp.int32), pltpu.VMEM((tb, pairs), F32),
                        pltpu.VMEM((tb * MASK_PITCH, N_KEYS), F32)],
        compiler_params=_params("arbitrary"),
        name="peer_mask",
    )(experts_t, gates_t)


def _peer_expert_kernel(h_ref, u_ref, v_ref, m_ref, x_ref, g2_ref, o_ref, *, tm):
    j = pl.program_id(1)

    @pl.when(j == 0)
    def _():
        o_ref[...] = jnp.zeros_like(o_ref)

    s = lax.dot_general(h_ref[...], u_ref[...], NT_DIMS, preferred_element_type=F32)
    a = (m_ref[...].astype(F32) * jax.nn.gelu(s)).astype(BF16)
    o_ref[...] += _dot(a, v_ref[...])

    @pl.when(j == pl.num_programs(1) - 1)
    def _():
        r = _mod_row(pl.program_id(0) * tm)
        o_ref[...] = x_ref[...] + g2_ref[pl.ds(r, 1), :] * o_ref[...]


EXPERT_TM, EXPERT_TE = 1024, 512


def _peer_expert(h2, u_bf, v_bf, mask, x1, mod):
    tm, te = EXPERT_TM, EXPERT_TE
    return pl.pallas_call(
        functools.partial(_peer_expert_kernel, tm=tm),
        out_shape=jax.ShapeDtypeStruct((N_TOK, D_MODEL), F32),
        grid=(N_TOK // tm, N_EXPERTS // te),
        in_specs=[
            pl.BlockSpec((tm, D_MODEL), lambda i, j: (i, 0)),
            pl.BlockSpec((te, D_MODEL), lambda i, j: (j, 0)),
            pl.BlockSpec((te, D_MODEL), lambda i, j: (j, 0)),
            pl.BlockSpec((tm, te), lambda i, j: (i, j)),
            pl.BlockSpec((tm, D_MODEL), lambda i, j: (i, 0), pipeline_mode=pl.Buffered(1)),
            pl.BlockSpec((MOD_ROWS, D_MODEL), lambda i, j: (0, 5)),
        ],
        out_specs=pl.BlockSpec((tm, D_MODEL), lambda i, j: (i, 0)),
        compiler_params=_params("arbitrary", "arbitrary"),
        name="peer_expert",
    )(h2, u_bf, v_bf, mask, x1, mod)


def kernel(x_prompt, x_sample, cache_k, cache_v, state_hgrn, c, c_ctx, ada_w, ada_b, norm1_g, norm2_g, w_in,
           q_norm_g, k_norm_g, hg_lb, hg_onorm_g, conv_w, conv_b, w_out, peer_wq, peer_k1, peer_k2, peer_u, peer_v):
    x = jnp.concatenate([x_prompt.reshape(N_PROMPT, D_MODEL), x_sample.reshape(N_SAMPLE, D_MODEL)], axis=0)
    cvec = jnp.concatenate([c_ctx[None, :], c, jnp.zeros((MOD_ROWS - 1 - DEC_BATCH, D_MODEL), F32)], axis=0)
    mod_all = _ada(cvec, ada_w, ada_b)

    w_in_bf, w_out_bf, wq_bf = w_in.astype(BF16), w_out.astype(BF16), peer_wq.astype(BF16)
    k1_bf, k2_bf = peer_k1.astype(BF16), peer_k2.astype(BF16)

    ks_out, vs_out, ss_out = [], [], []
    for l in range(DEPTH):
        mod = mod_all[l]
        z = _inproj(x, mod, norm1_g[l], w_in_bf, l)

        fn_p = _fourier(z, SEQ, BATCH, 0)
        fn_s = _fourier(z, DEC_SEQ, DEC_BATCH, N_PROMPT)

        at_p, k_l, v_l = _attn_prompt(z, q_norm_g[l], k_norm_g[l])
        k_ctx = cache_k[:, l].reshape(DEC_BATCH, PAST_LEN, N_KV * HEAD_DIM)
        v_ctx = cache_v[:, l].reshape(DEC_BATCH, PAST_LEN, N_KV * HEAD_DIM)
        at_s = _attn_sample(z, q_norm_g[l], k_norm_g[l], k_ctx, v_ctx)

        hg_p, s_l = _hgrn(z, hg_lb, hg_onorm_g[l], l, SEQ, BATCH, 0, None)
        hg_s, _ = _hgrn(z, hg_lb, hg_onorm_g[l], l, DEC_SEQ, DEC_BATCH, N_PROMPT, state_hgrn[:, l])

        cv_p = _conv(z, conv_w[l], conv_b[l], SEQ, BATCH, 0)
        cv_s = _conv(z, conv_w[l], conv_b[l], DEC_SEQ, DEC_BATCH, N_PROMPT)

        x1, h2 = _outproj((fn_p, at_p, hg_p, cv_p), (fn_s, at_s, hg_s, cv_s), w_out_bf, l, x, mod, norm2_g[l])

        q_peer = _peer_query(h2, wq_bf, l)
        experts_t, gates_t, u_bf, v_bf = _peer_topk(q_peer, k1_bf, k2_bf, peer_u, peer_v, l)
        mask = _peer_mask(experts_t, gates_t)
        x = _peer_expert(h2, u_bf, v_bf, mask, x1, mod)

        ks_out.append(k_l.reshape(BATCH, SEQ, N_KV, HEAD_DIM))
        vs_out.append(v_l.reshape(BATCH, SEQ, N_KV, HEAD_DIM))
        ss_out.append(s_l)

    y_prompt = x[:N_PROMPT].reshape(BATCH, SEQ, D_MODEL)
    y_sample = x[N_PROMPT:].reshape(DEC_BATCH, DEC_SEQ, D_MODEL)
    return (y_prompt, y_sample, jnp.stack(ks_out, axis=1), jnp.stack(vs_out, axis=1), jnp.stack(ss_out, axis=1))
```

```python
import functools

import numpy as np
import jax
import jax.numpy as jnp
from jax import lax
from jax.experimental import pallas as pl
from jax.experimental.pallas import tpu as pltpu

F32 = jnp.float32
BF16 = jnp.bfloat16

D_MODEL = 2048
BATCH = 32
SEQ = 256
DEPTH = 2
DEC_BATCH = 4
DEC_SEQ = 1024
PAST_LEN = 256
GRID_W = 64
GROUP_W = D_MODEL // 4
FN_GROUPS = 4
FN_GW = GROUP_W // FN_GROUPS
HEAD_DIM = 128
N_HEADS = GROUP_W // HEAD_DIM
N_KV = 2
GQA_GROUP = N_HEADS // N_KV
ROPE_THETA = 10000.0
HG_DK = 128
HG_DV = 128
HG_HEADS = GROUP_W // HG_DK
CONV_K = 3
IN_COLS = 11 * GROUP_W
PEER_HEADS = 8
N_KEYS = 128
N_EXPERTS = N_KEYS * N_KEYS
PEER_QDIM = 256
PEER_HALF = PEER_QDIM // 2
PEER_TOPK = 16
EPS = 1e-6
F_FLOOR = 1e-30

N_PROMPT = BATCH * SEQ
N_SAMPLE = DEC_BATCH * DEC_SEQ
N_TOK = N_PROMPT + N_SAMPLE
MOD_ROWS = 8
LANES = 128
NORM_ROWS = 128
HG_CHUNK = 32
HG_GROUP = 8
HG_UNROLL = 16
HG_CUM_BLOCK = 256
MASK_PITCH = 132
VMEM_LIMIT = 56 * 1024 * 1024

COL_FN, COL_Q, COL_K, COL_V = 0, 4, 8, 10
COL_GQ, COL_GFF, COL_GFB, COL_GI, COL_GO = 12, 16, 20, 24, 28
COL_CB, COL_CC, COL_CX = 32, 36, 40

NT_DIMS = (((1,), (1,)), ((), ()))
TN_DIMS = (((0,), (0,)), ((), ()))


def _params(*sem):
    return pltpu.CompilerParams(dimension_semantics=sem, vmem_limit_bytes=VMEM_LIMIT)


def _dot(a, b):
    return jnp.dot(a, b, preferred_element_type=F32)


def _split3(x):
    h = x.astype(BF16)
    r = x - h.astype(F32)
    m = r.astype(BF16)
    l = (r - m.astype(F32)).astype(BF16)
    return h, m, l


def _split2(x):
    h = x.astype(BF16)
    return h, (x - h.astype(F32)).astype(BF16)


def _mod_row(tok0):
    return jnp.where(tok0 >= N_PROMPT, (tok0 - N_PROMPT) // DEC_SEQ + 1, 0)


def _rms(x, g):
    return x * lax.rsqrt(jnp.mean(x * x, axis=-1, keepdims=True) + EPS) * g


def _ada_kernel(c_ref, w_ref, b_ref, o_ref):
    c = c_ref[...]
    s = (c * jax.nn.sigmoid(c)).astype(BF16)
    o_ref[0] = _dot(s, w_ref[0].astype(BF16)) + b_ref[0]


def _ada(cvec, ada_w, ada_b):
    tn = 1024
    return pl.pallas_call(
        _ada_kernel,
        out_shape=jax.ShapeDtypeStruct((DEPTH, MOD_ROWS, 6 * D_MODEL), F32),
        grid=(DEPTH, 6 * D_MODEL // tn),
        in_specs=[
            pl.BlockSpec((MOD_ROWS, D_MODEL), lambda l, j: (0, 0)),
            pl.BlockSpec((1, D_MODEL, tn), lambda l, j: (l, 0, j)),
            pl.BlockSpec((1, 1, tn), lambda l, j: (l, 0, j)),
        ],
        out_specs=pl.BlockSpec((1, MOD_ROWS, tn), lambda l, j: (l, 0, j)),
        compiler_params=_params("arbitrary", "arbitrary"),
        name="ada",
    )(cvec, ada_w, ada_b.reshape(DEPTH, 1, 6 * D_MODEL))


def _inproj_kernel(x_ref, sh_ref, sc_ref, g_ref, w_ref, o_ref, h_ref, *, tm):
    @pl.when(pl.program_id(1) == 0)
    def _():
        r = _mod_row(pl.program_id(0) * tm)
        gain = g_ref[...] * (1.0 + sc_ref[pl.ds(r, 1), :])
        shift = sh_ref[pl.ds(r, 1), :]

        def norm_rows(c, carry):
            rows = pl.ds(pl.multiple_of(c * NORM_ROWS, NORM_ROWS), NORM_ROWS)
            x = x_ref[rows, :]
            y = x * lax.rsqrt(jnp.mean(x * x, axis=-1, keepdims=True) + EPS)
            h_ref[rows, :] = (y * gain + shift).astype(BF16)
            return carry

        lax.fori_loop(0, tm // NORM_ROWS, norm_rows, 0)

    o_ref[...] = _dot(h_ref[...], w_ref[...])


def _inproj(x, mod, norm_g, w_in_bf, layer):
    tm, tn = 1024, 512
    return pl.pallas_call(
        functools.partial(_inproj_kernel, tm=tm),
        out_shape=jax.ShapeDtypeStruct((N_TOK, IN_COLS), F32),
        grid=(N_TOK // tm, IN_COLS // tn),
        in_specs=[
            pl.BlockSpec((tm, D_MODEL), lambda i, j: (i, 0)),
            pl.BlockSpec((MOD_ROWS, D_MODEL), lambda i, j: (0, 0)),
            pl.BlockSpec((MOD_ROWS, D_MODEL), lambda i, j: (0, 1)),
            pl.BlockSpec((1, D_MODEL), lambda i, j: (0, 0)),
            pl.BlockSpec((None, D_MODEL, tn), lambda i, j: (layer, 0, j)),
        ],
        out_specs=pl.BlockSpec((tm, tn), lambda i, j: (i, j)),
        scratch_shapes=[pltpu.VMEM((tm, D_MODEL), BF16)],
        compiler_params=_params("arbitrary", "arbitrary"),
        name="inproj",
    )(x, mod, mod, norm_g.reshape(1, D_MODEL), w_in_bf)


def _fourier_kernel(u_ref, cch_ref, ccl_ref, cth_ref, ctl_ref, o_ref):
    cch, ccl, cth, ctl = cch_ref[...], ccl_ref[...], cth_ref[...], ctl_ref[...]
    for g in range(0, FN_GROUPS, 2):
        ws = []
        for gg in (g, g + 1):
            uh, ul = _split2(u_ref[:, gg * FN_GW:(gg + 1) * FN_GW])
            ws.append(_dot(uh, cch) + (_dot(ul, cch) + _dot(uh, ccl)))
        st = jnp.concatenate([jnp.concatenate([w[:, :FN_GW] for w in ws], axis=1),
                              jnp.concatenate([w[:, FN_GW:] for w in ws], axis=1)], axis=0)
        sh, sl = _split2(st)
        o_ref[:, g * FN_GW:(g + 2) * FN_GW] = _dot(cth, sh) + (_dot(cth, sl) + _dot(ctl, sh))


def _dft_consts(t):
    def cs(n):
        k = np.arange(n, dtype=np.float64)
        ang = 2.0 * np.pi * np.outer(k, k) / n
        return np.cos(ang) / np.sqrt(n), np.sin(ang) / np.sqrt(n)

    cc, sc = cs(FN_GW)
    ct, st = cs(t)
    right = np.concatenate([cc, sc], axis=1).astype(np.float32)
    left = np.concatenate([ct, -st], axis=1).astype(np.float32)

    def hl(a):
        a = jnp.asarray(a)
        h = a.astype(BF16)
        return h, (a - h.astype(F32)).astype(BF16)

    return hl(right) + hl(left)


def _fourier(z, t, nb, row0):
    cch, ccl, cth, ctl = _dft_consts(t)
    rb = row0 // t
    return pl.pallas_call(
        _fourier_kernel,
        out_shape=jax.ShapeDtypeStruct((nb * t, GROUP_W), F32),
        grid=(nb,),
        in_specs=[
            pl.BlockSpec((t, GROUP_W), lambda b: (rb + b, COL_FN // 4)),
            pl.BlockSpec((FN_GW, 2 * FN_GW), lambda b: (0, 0)),
            pl.BlockSpec((FN_GW, 2 * FN_GW), lambda b: (0, 0)),
            pl.BlockSpec((t, 2 * t), lambda b: (0, 0)),
            pl.BlockSpec((t, 2 * t), lambda b: (0, 0)),
        ],
        out_specs=pl.BlockSpec((t, GROUP_W), lambda b: (b, 0)),
        compiler_params=_params("arbitrary"),
        name=f"fourier_t{t}",
    )(z, cch, ccl, cth, ctl)


def _softmax_pv(q_bf, k_bf, v_bf):
    s = lax.dot_general(q_bf, k_bf, NT_DIMS, preferred_element_type=F32) * (HEAD_DIM ** -0.5)
    p = jnp.exp(s - jnp.max(s, axis=-1, keepdims=True))
    return _dot(p.astype(BF16), v_bf) / jnp.sum(p, axis=-1, keepdims=True)


def _attn_prompt_kernel(q_ref, k_ref, v_ref, qg_ref, kg_ref, o_ref, kc_ref, vc_ref):
    kn = _rms(k_ref[...], kg_ref[...])
    v = v_ref[...]
    kc_ref[...] = kn
    vc_ref[...] = v
    kb, vb = kn.astype(BF16), v.astype(BF16)
    for g in range(GQA_GROUP):
        qn = _rms(q_ref[:, g * HEAD_DIM:(g + 1) * HEAD_DIM], qg_ref[...])
        o_ref[:, g * HEAD_DIM:(g + 1) * HEAD_DIM] = _softmax_pv(qn.astype(BF16), kb, vb)


def _attn_prompt(z, q_g, k_g):
    t = SEQ
    kv_shape = jax.ShapeDtypeStruct((N_PROMPT, N_KV * HEAD_DIM), F32)
    return pl.pallas_call(
        _attn_prompt_kernel,
        out_shape=(jax.ShapeDtypeStruct((N_PROMPT, GROUP_W), F32), kv_shape, kv_shape),
        grid=(BATCH, N_KV),
        in_specs=[
            pl.BlockSpec((t, GQA_GROUP * HEAD_DIM), lambda b, h: (b, COL_Q // 2 + h)),
            pl.BlockSpec((t, HEAD_DIM), lambda b, h: (b, COL_K + h)),
            pl.BlockSpec((t, HEAD_DIM), lambda b, h: (b, COL_V + h)),
            pl.BlockSpec((1, HEAD_DIM), lambda b, h: (0, 0)),
            pl.BlockSpec((1, HEAD_DIM), lambda b, h: (0, 0)),
        ],
        out_specs=(
            pl.BlockSpec((t, GQA_GROUP * HEAD_DIM), lambda b, h: (b, h)),
            pl.BlockSpec((t, HEAD_DIM), lambda b, h: (b, h)),
            pl.BlockSpec((t, HEAD_DIM), lambda b, h: (b, h)),
        ),
        compiler_params=_params("arbitrary", "arbitrary"),
        name="attn_prompt",
    )(z, z, z, q_g.reshape(1, HEAD_DIM), k_g.reshape(1, HEAD_DIM))


def _rope(x, cosf, sinf):
    return x * cosf + pltpu.roll(x, HEAD_DIM // 2, axis=1) * sinf


def _attn_sample_kernel(q_ref, k_ref, v_ref, kctx_ref, vctx_ref, qg_ref, kg_ref, cos_ref, sin_ref,
                        o_ref, kall_ref, vall_ref, *, qb):
    cosf, sinf = cos_ref[...], sin_ref[...]
    kall_ref[0:PAST_LEN, :] = kctx_ref[0].astype(BF16)
    vall_ref[0:PAST_LEN, :] = vctx_ref[0].astype(BF16)
    kall_ref[PAST_LEN:, :] = _rope(_rms(k_ref[...], kg_ref[...]), cosf, sinf).astype(BF16)
    vall_ref[PAST_LEN:, :] = v_ref[...].astype(BF16)
    kb, vb = kall_ref[...], vall_ref[...]
    for i in range(DEC_SEQ // qb):
        rows = slice(i * qb, (i + 1) * qb)
        for g in range(GQA_GROUP):
            cols = slice(g * HEAD_DIM, (g + 1) * HEAD_DIM)
            qn = _rope(_rms(q_ref[rows, cols], qg_ref[...]), cosf[rows], sinf[rows])
            o_ref[rows, cols] = _softmax_pv(qn.astype(BF16), kb, vb)


def _rope_tables():
    rows = DEC_SEQ // GRID_W
    row = np.repeat(np.arange(rows, dtype=np.float32), GRID_W)
    col = np.tile(np.arange(GRID_W, dtype=np.float32), rows)
    n_freq = HEAD_DIM // 4
    inv = (np.float32(ROPE_THETA) ** (-np.arange(n_freq, dtype=np.float32) / n_freq)).astype(np.float32)
    ang = np.concatenate([row[:, None] * inv, col[:, None] * inv], axis=-1).astype(np.float32)
    cos, sin = np.cos(ang.astype(np.float64)), np.sin(ang.astype(np.float64))
    cosf = np.concatenate([cos, cos], axis=-1).astype(np.float32)
    sinf = np.concatenate([-sin, sin], axis=-1).astype(np.float32)
    return jnp.asarray(cosf), jnp.asarray(sinf)


def _attn_sample(z, q_g, k_g, k_ctx, v_ctx):
    t = DEC_SEQ
    rb = N_PROMPT // t
    cosf, sinf = _rope_tables()
    const = lambda b, h: (0, 0)
    return pl.pallas_call(
        functools.partial(_attn_sample_kernel, qb=256),
        out_shape=jax.ShapeDtypeStruct((N_SAMPLE, GROUP_W), F32),
        grid=(DEC_BATCH, N_KV),
        in_specs=[
            pl.BlockSpec((t, GQA_GROUP * HEAD_DIM), lambda b, h: (rb + b, COL_Q // 2 + h)),
            pl.BlockSpec((t, HEAD_DIM), lambda b, h: (rb + b, COL_K + h)),
            pl.BlockSpec((t, HEAD_DIM), lambda b, h: (rb + b, COL_V + h)),
            pl.BlockSpec((1, PAST_LEN, HEAD_DIM), lambda b, h: (b, 0, h)),
            pl.BlockSpec((1, PAST_LEN, HEAD_DIM), lambda b, h: (b, 0, h)),
            pl.BlockSpec((1, HEAD_DIM), const),
            pl.BlockSpec((1, HEAD_DIM), const),
            pl.BlockSpec((t, HEAD_DIM), const),
            pl.BlockSpec((t, HEAD_DIM), const),
        ],
        out_specs=pl.BlockSpec((t, GQA_GROUP * HEAD_DIM), lambda b, h: (b, h)),
        scratch_shapes=[pltpu.VMEM((PAST_LEN + t, HEAD_DIM), BF16), pltpu.VMEM((PAST_LEN + t, HEAD_DIM), BF16)],
        compiler_params=_params("arbitrary", "arbitrary"),
        name="attn_sample",
    )(z, z, z, k_ctx, v_ctx, q_g.reshape(1, HEAD_DIM), k_g.reshape(1, HEAD_DIM), cosf, sinf)


def _hgrn_chain(rev, r0, gq_ref, gi_ref, k_scr, cum_scr, st_scr, o_scr):
    c = HG_CHUNK
    d = 1 if rev else 0
    rows = pl.ds(r0, c)
    qc, vc = gq_ref[rows, :], gi_ref[rows, :]
    kc, cc = k_scr[d, rows, :], cum_scr[d, rows, :]
    st = st_scr[d]
    last = cc[0:1] if rev else cc[c - 1:c]
    qs = (qc * jnp.exp(cc)).astype(BF16)
    o_inter = lax.dot_general(qs, st.astype(BF16), NT_DIMS, preferred_element_type=F32)
    ks = (kc * jnp.exp(last - cc)).astype(BF16)
    upd = lax.dot_general(vc.astype(BF16), ks, TN_DIMS, preferred_element_type=F32)
    st_scr[d] = st * jnp.exp(last) + upd
    n_grp = c // HG_GROUP
    row = lax.broadcasted_iota(jnp.int32, (HG_GROUP, 1), 0)
    krow = lax.broadcasted_iota(jnp.int32, (c, 1), 0)
    diag, att = [], []
    for i in range(n_grp):
        lo = i * HG_GROUP
        grp = slice(lo, lo + HG_GROUP)
        q_i, k_i, v_i, c_i = qc[grp], kc[grp], vc[grp], cc[grp]
        acc = jnp.zeros((HG_GROUP, HG_DV), F32)
        for delta in range(HG_GROUP):
            if delta == 0:
                k_sh, c_sh, v_sh = k_i, c_i, v_i
            else:
                sh = (HG_GROUP - delta) if rev else delta
                k_sh, c_sh, v_sh = (pltpu.roll(a, sh, axis=0) for a in (k_i, c_i, v_i))
            a = jnp.sum(q_i * k_sh * jnp.exp(jnp.minimum(c_i - c_sh, 0.0)), axis=-1, keepdims=True)
            valid = (row <= HG_GROUP - 1 - delta) if rev else (row >= delta)
            acc = acc + jnp.where(valid, a, 0.0) * v_sh
        diag.append(acc)
        if (i == n_grp - 1) if rev else (i == 0):
            att.append(jnp.zeros((HG_GROUP, c), F32))
            continue
        if rev:
            b, other = cc[lo + HG_GROUP:lo + HG_GROUP + 1], krow >= lo + HG_GROUP
        else:
            b, other = cc[lo - 1:lo], krow < lo
        qs_i = (q_i * jnp.exp(c_i - b)).astype(BF16)
        ks_i = jnp.where(other, kc * jnp.exp(jnp.minimum(b - cc, 0.0)), 0.0).astype(BF16)
        att.append(lax.dot_general(qs_i, ks_i, NT_DIMS, preferred_element_type=F32))
    o_off = _dot(jnp.concatenate(att, axis=0).astype(BF16), vc.astype(BF16))
    o_scr[d, rows, :] = o_inter + jnp.concatenate(diag, axis=0) + o_off


def _hgrn_kernel(*refs, layer, t, has_ctx):
    gq_ref, gff_ref, gfb_ref, gi_ref, go_ref, lb_ref, on_ref, bdf_ref, bdb_ref = refs[:9]
    n_in = 9
    s0_ref = None
    if has_ctx:
        s0_ref = refs[9]
        n_in = 10
    o_ref, sout_ref = refs[n_in], refs[n_in + 1]
    k_scr, cum_scr, st_scr, o_scr = refs[n_in + 2:]

    raw = lb_ref[...]
    e = jnp.exp(raw - jnp.max(raw, axis=0, keepdims=True))
    soft = e / jnp.sum(e, axis=0, keepdims=True)
    csum = soft[0]
    for i in range(1, layer + 1):
        csum = csum + soft[i]
    lb = csum - soft[0]

    for d, (g_ref, bd_ref) in enumerate(((gff_ref, bdf_ref), (gfb_ref, bdb_ref))):
        lbd = lb[d:d + 1]
        f = jnp.maximum(lbd + (1.0 - lbd) * jax.nn.sigmoid(g_ref[...]), F_FLOOR)
        k_scr[d] = 1.0 - f
        lf = jnp.log(f)
        bd = bd_ref[...]
        for p in range(t // HG_CUM_BLOCK):
            rows = slice(p * HG_CUM_BLOCK, (p + 1) * HG_CUM_BLOCK)
            h, m, l = _split3(lf[rows])
            cum_scr[d, rows, :] = _dot(bd, h) + _dot(bd, m) + _dot(bd, l)
        if has_ctx:
            st_scr[d] = s0_ref[0, d, 0].T
        else:
            st_scr[d] = jnp.zeros((HG_DV, HG_DK), F32)

    n = t // HG_CHUNK

    def body(i, carry):
        _hgrn_chain(False, pl.multiple_of(i * HG_CHUNK, HG_CHUNK), gq_ref, gi_ref, k_scr, cum_scr, st_scr, o_scr)
        _hgrn_chain(True, pl.multiple_of((n - 1 - i) * HG_CHUNK, HG_CHUNK), gq_ref, gi_ref, k_scr, cum_scr,
                    st_scr, o_scr)
        return carry

    lax.fori_loop(0, n, body, 0, unroll=min(n, HG_UNROLL))

    o = _rms(o_scr[0] + o_scr[1], on_ref[...])
    go = go_ref[...]
    o_ref[...] = o * (go * jax.nn.sigmoid(go))
    for d in range(2):
        sout_ref[0, d, 0] = st_scr[d].T


def _cum_consts():
    r = np.arange(HG_CUM_BLOCK)
    same = (r[:, None] // HG_CHUNK) == (r[None, :] // HG_CHUNK)
    fwd = same & (r[None, :] <= r[:, None])
    bwd = same & (r[None, :] >= r[:, None])
    return jnp.asarray(fwd, BF16), jnp.asarray(bwd, BF16)


def _hgrn(z, hg_lb, onorm_g, layer, t, nb, row0, s0):
    rb = row0 // t
    bdf, bdb = _cum_consts()
    has_ctx = s0 is not None
    col = lambda c0: (lambda b, h: (rb + b, c0 + h))
    in_specs = [
        pl.BlockSpec((t, HG_DK), col(COL_GQ)),
        pl.BlockSpec((t, HG_DK), col(COL_GFF)),
        pl.BlockSpec((t, HG_DK), col(COL_GFB)),
        pl.BlockSpec((t, HG_DV), col(COL_GI)),
        pl.BlockSpec((t, HG_DV), col(COL_GO)),
        pl.BlockSpec((DEPTH, 2, HG_DK), lambda b, h: (0, 0, h)),
        pl.BlockSpec((1, HG_DV), lambda b, h: (0, 0)),
        pl.BlockSpec((HG_CUM_BLOCK, HG_CUM_BLOCK), lambda b, h: (0, 0)),
        pl.BlockSpec((HG_CUM_BLOCK, HG_CUM_BLOCK), lambda b, h: (0, 0)),
    ]
    args = [z, z, z, z, z, hg_lb, onorm_g.reshape(1, HG_DV), bdf, bdb]
    if has_ctx:
        in_specs.append(pl.BlockSpec((1, 2, 1, HG_DK, HG_DV), lambda b, h: (b, 0, h, 0, 0)))
        args.append(s0)
    return pl.pallas_call(
        functools.partial(_hgrn_kernel, layer=layer, t=t, has_ctx=has_ctx),
        out_shape=(jax.ShapeDtypeStruct((nb * t, GROUP_W), F32),
                   jax.ShapeDtypeStruct((nb, 2, HG_HEADS, HG_DK, HG_DV), F32)),
        grid=(nb, HG_HEADS),
        in_specs=in_specs,
        out_specs=(pl.BlockSpec((t, HG_DV), lambda b, h: (b, h)),
                   pl.BlockSpec((1, 2, 1, HG_DK, HG_DV), lambda b, h: (b, 0, h, 0, 0))),
        scratch_shapes=[pltpu.VMEM((2, t, HG_DK), F32), pltpu.VMEM((2, t, HG_DK), F32),
                        pltpu.VMEM((2, HG_DV, HG_DK), F32), pltpu.VMEM((2, t, HG_DV), F32)],
        compiler_params=_params("arbitrary", "arbitrary"),
        name=f"hgrn_t{t}",
    )(*args)


def _conv_kernel(cb_ref, cc_ref, cx_ref, w_ref, b_ref, o_ref, *, t):
    p = cc_ref[...] * cx_ref[...]
    row = lax.broadcasted_iota(jnp.int32, (t, 1), 0)
    prev = jnp.where(row >= 1, pltpu.roll(p, 1, axis=0), 0.0)
    nxt = jnp.where(row <= t - 2, pltpu.roll(p, t - 1, axis=0), 0.0)
    w = w_ref[...]
    y = prev * w[0:1] + p * w[1:2] + nxt * w[2:3] + b_ref[...]
    o_ref[...] = cb_ref[...] * y


def _conv(z, w, b, t, nb, row0):
    rb = row0 // t
    return pl.pallas_call(
        functools.partial(_conv_kernel, t=t),
        out_shape=jax.ShapeDtypeStruct((nb * t, GROUP_W), F32),
        grid=(nb,),
        in_specs=[
            pl.BlockSpec((t, GROUP_W), lambda i: (rb + i, COL_CB // 4)),
            pl.BlockSpec((t, GROUP_W), lambda i: (rb + i, COL_CC // 4)),
            pl.BlockSpec((t, GROUP_W), lambda i: (rb + i, COL_CX // 4)),
            pl.BlockSpec((CONV_K, GROUP_W), lambda i: (0, 0)),
            pl.BlockSpec((1, GROUP_W), lambda i: (0, 0)),
        ],
        out_specs=pl.BlockSpec((t, GROUP_W), lambda i: (i, 0)),
        compiler_params=_params("arbitrary"),
        name=f"conv_t{t}",
    )(z, z, z, w, b.reshape(1, GROUP_W))


def _outproj_kernel(*refs, tm):
    prompt_refs, sample_refs = refs[0:4], refs[4:8]
    w_ref, x_ref, g1_ref, sh_ref, sc_ref, n2_ref, x1_ref, h2_ref = refs[8:]
    i = pl.program_id(0)
    r = _mod_row(i * tm)

    def run(group_refs):
        acc = None
        for g, ref in enumerate(group_refs):
            part = _dot(ref[...].astype(BF16), w_ref[g * GROUP_W:(g + 1) * GROUP_W, :])
            acc = part if acc is None else acc + part
        x1 = x_ref[...] + g1_ref[pl.ds(r, 1), :] * acc
        x1_ref[...] = x1
        h2 = _rms(x1, n2_ref[...]) * (1.0 + sc_ref[pl.ds(r, 1), :]) + sh_ref[pl.ds(r, 1), :]
        h2_ref[...] = h2.astype(BF16)

    pl.when(i < N_PROMPT // tm)(lambda: run(prompt_refs))
    pl.when(i >= N_PROMPT // tm)(lambda: run(sample_refs))


def _outproj(prompt_groups, sample_groups, w_out_bf, layer, x, mod, norm2_g):
    tm = 512
    n_pb, n_sb = N_PROMPT // tm, N_SAMPLE // tm
    p_spec = pl.BlockSpec((tm, GROUP_W), lambda i: (jnp.minimum(i, n_pb - 1), 0))
    s_spec = pl.BlockSpec((tm, GROUP_W), lambda i: (jnp.clip(i - n_pb, 0, n_sb - 1), 0))
    modspec = lambda k: pl.BlockSpec((MOD_ROWS, D_MODEL), lambda i: (0, k))
    return pl.pallas_call(
        functools.partial(_outproj_kernel, tm=tm),
        out_shape=(jax.ShapeDtypeStruct((N_TOK, D_MODEL), F32), jax.ShapeDtypeStruct((N_TOK, D_MODEL), BF16)),
        grid=(N_TOK // tm,),
        in_specs=[p_spec] * 4 + [s_spec] * 4 + [
            pl.BlockSpec((None, D_MODEL, D_MODEL), lambda i: (layer, 0, 0)),
            pl.BlockSpec((tm, D_MODEL), lambda i: (i, 0)),
            modspec(2), modspec(3), modspec(4),
            pl.BlockSpec((1, D_MODEL), lambda i: (0, 0))],
        out_specs=(pl.BlockSpec((tm, D_MODEL), lambda i: (i, 0)), pl.BlockSpec((tm, D_MODEL), lambda i: (i, 0))),
        compiler_params=_params("arbitrary"),
        name="outproj",
    )(*prompt_groups, *sample_groups, w_out_bf, x, mod, mod, mod, norm2_g.reshape(1, D_MODEL))


def _peer_query_kernel(h_ref, wq_ref, q_ref):
    q_ref[...] = _dot(h_ref[...], wq_ref[...]).astype(BF16)


def _peer_query(h2, wq_bf, layer):
    tm = 512
    width = PEER_HEADS * PEER_QDIM
    return pl.pallas_call(
        _peer_query_kernel,
        out_shape=jax.ShapeDtypeStruct((N_TOK, width), BF16),
        grid=(N_TOK // tm,),
        in_specs=[pl.BlockSpec((tm, D_MODEL), lambda i: (i, 0)),
                  pl.BlockSpec((None, D_MODEL, width), lambda i: (layer, 0, 0))],
        out_specs=pl.BlockSpec((tm, width), lambda i: (i, 0)),
        compiler_params=_params("arbitrary"),
        name="peer_query",
    )(h2, wq_bf)


_CAND_PIECES = ((0, 0, 8), (0, 8, 8), (1, 0, 8), (2, 0, 5), (3, 0, 4), (4, 0, 3), (5, 0, 2), (6, 0, 2), (7, 0, 2),
                (None, 0, 8))
_CAND_ROWS = 8 * len(_CAND_PIECES)
_NO_CAND = 1 << 20


def _topk_rows(x_scr, ids, n_rows, v_out, i_out):
    def body(j, carry):
        x = x_scr[0:n_rows, :]
        m = jnp.max(x, axis=0, keepdims=True)
        idx = jnp.min(jnp.where(x == m, ids, _NO_CAND), axis=0, keepdims=True)
        v_out[pl.ds(j, 1), :] = m
        i_out[pl.ds(j, 1), :] = idx
        x_scr[0:n_rows, :] = jnp.where(ids == idx, -jnp.inf, x)
        return carry

    lax.fori_loop(0, PEER_TOPK, body, 0, unroll=2)


def _peer_topk_kernel(q_ref, k1_ref, k2_ref, u_ref, v_ref, e_ref, g_ref, ub_ref, vb_ref, x_scr, v1_scr, i1_scr,
                      v2_scr, i2_scr, ts_scr, tc_scr, *, tm):
    ub_ref[...] = u_ref[...].astype(BF16)
    vb_ref[...] = v_ref[...].astype(BF16)
    key_ids = lax.broadcasted_iota(jnp.int32, (N_KEYS, tm), 0)
    for half, (k_ref, v_scr, i_scr) in enumerate(((k1_ref, v1_scr, i1_scr), (k2_ref, v2_scr, i2_scr))):
        qh = q_ref[:, half * PEER_HALF:(half + 1) * PEER_HALF]
        x_scr[...] = lax.dot_general(k_ref[...], qh, NT_DIMS, preferred_element_type=F32)
        _topk_rows(x_scr, key_ids, N_KEYS, v_scr, i_scr)
    v1, v2 = v1_scr[...], v2_scr[...]
    sub = lax.broadcasted_iota(jnp.int32, (8, tm), 0)
    cand_ids = []
    for p, (r1, r2, n_valid) in enumerate(_CAND_PIECES):
        if r1 is None:
            vals = v1[8:16] + v2[0:1]
            ids = (sub + 8) * PEER_TOPK
        else:
            vals = v1[r1:r1 + 1] + v2[r2:r2 + 8]
            ids = sub + (r1 * PEER_TOPK + r2)
            if n_valid < 8:
                vals = jnp.where(sub < n_valid, vals, -jnp.inf)
                ids = jnp.where(sub < n_valid, ids, _NO_CAND)
        x_scr[p * 8:(p + 1) * 8, :] = vals
        cand_ids.append(ids)
    _topk_rows(x_scr, jnp.concatenate(cand_ids, axis=0), _CAND_ROWS, ts_scr, tc_scr)
    tc = tc_scr[...]
    c1, c2 = tc // PEER_TOPK, tc % PEER_TOPK
    i1, i2 = i1_scr[...], i2_scr[...]
    e1 = jnp.zeros_like(tc)
    e2 = jnp.zeros_like(tc)
    for a in range(PEER_TOPK):
        e1 = jnp.where(c1 == a, i1[a:a + 1], e1)
        e2 = jnp.where(c2 == a, i2[a:a + 1], e2)
    e_ref[...] = e1 * N_KEYS + e2
    ts = ts_scr[...]
    p = jnp.exp(ts - jnp.max(ts, axis=0, keepdims=True))
    g_ref[...] = p / jnp.sum(p, axis=0, keepdims=True)


def _peer_topk(q, k1_bf, k2_bf, peer_u, peer_v, layer):
    tm = 512
    slab = LANES
    n_slabs = N_EXPERTS // slab
    assert (N_TOK // tm) * PEER_HEADS >= n_slabs
    row_f = pltpu.VMEM((PEER_TOPK, tm), F32)
    row_i = pltpu.VMEM((PEER_TOPK, tm), jnp.int32)
    slab_of = lambda i, h: jnp.minimum(i * PEER_HEADS + h, n_slabs - 1)
    table_in = pl.BlockSpec((None, slab, D_MODEL), lambda i, h: (layer, slab_of(i, h), 0))
    table_out = pl.BlockSpec((slab, D_MODEL), lambda i, h: (slab_of(i, h), 0))
    table_shape = jax.ShapeDtypeStruct((N_EXPERTS, D_MODEL), BF16)
    return pl.pallas_call(
        functools.partial(_peer_topk_kernel, tm=tm),
        out_shape=(jax.ShapeDtypeStruct((PEER_HEADS * PEER_TOPK, N_TOK), jnp.int32),
                   jax.ShapeDtypeStruct((PEER_HEADS * PEER_TOPK, N_TOK), F32), table_shape, table_shape),
        grid=(N_TOK // tm, PEER_HEADS),
        in_specs=[
            pl.BlockSpec((tm, PEER_QDIM), lambda i, h: (i, h)),
            pl.BlockSpec((None, N_KEYS, PEER_HALF), lambda i, h: (layer, 0, 0)),
            pl.BlockSpec((None, N_KEYS, PEER_HALF), lambda i, h: (layer, 0, 0)),
            table_in, table_in,
        ],
        out_specs=(pl.BlockSpec((PEER_TOPK, tm), lambda i, h: (h, i)),
                   pl.BlockSpec((PEER_TOPK, tm), lambda i, h: (h, i)), table_out, table_out),
        scratch_shapes=[pltpu.VMEM((N_KEYS, tm), F32), row_f, row_i, row_f, row_i, row_f, row_i],
        compiler_params=_params("arbitrary", "arbitrary"),
        name="peer_topk",
    )(q, k1_bf, k2_bf, peer_u, peer_v)


def _peer_mask_kernel(e_ref, g_ref, m_ref, et_scr, gt_scr, slab_scr, *, tb):
    et_scr[...] = e_ref[...].T
    gt_scr[...] = g_ref[...].T
    key = lax.broadcasted_iota(jnp.int32, (N_KEYS, PEER_HEADS * PEER_TOPK), 0)

    def body(t, carry):
        er = et_scr[pl.ds(t, 1), :]
        gr = gt_scr[pl.ds(t, 1), :]
        p1 = jnp.where(key == er // N_KEYS, gr, 0.0).astype(BF16)
        p2 = jnp.where(key == er % N_KEYS, 1.0, 0.0).astype(BF16)
        mt = lax.dot_general(p1, p2, NT_DIMS, preferred_element_type=F32)
        slab_scr[pl.ds(pl.multiple_of(t * MASK_PITCH, 4), N_KEYS), :] = mt
        return carry

    lax.fori_loop(0, tb, body, 0, unroll=64)
    for k1 in range(N_KEYS):
        m_ref[:, k1 * N_KEYS:(k1 + 1) * N_KEYS] = slab_scr[pl.ds(k1, tb, stride=MASK_PITCH), :].astype(BF16)


def _peer_mask(experts_t, gates_t):
    tb = LANES
    pairs = PEER_HEADS * PEER_TOPK
    return pl.pallas_call(
        functools.partial(_peer_mask_kernel, tb=tb),
        out_shape=jax.ShapeDtypeStruct((N_TOK, N_EXPERTS), BF16),
        grid=(N_TOK // tb,),
        in_specs=[pl.BlockSpec((pairs, tb), lambda i: (0, i)), pl.BlockSpec((pairs, tb), lambda i: (0, i))],
        out_specs=pl.BlockSpec((tb, N_EXPERTS), lambda i: (i, 0)),
        scratch_shapes=[pltpu.VMEM((tb, pairs), jnp.int32), pltpu.VMEM((tb, pairs), F32),
                        pltpu.VMEM((tb * MASK_PITCH, N_KEYS), F32)],
        compiler_params=_params("arbitrary"),
        name="peer_mask",
    )(experts_t, gates_t)


def _peer_expert_kernel(h_ref, u_ref, v_ref, m_ref, x_ref, g2_ref, o_ref, *, tm):
    j = pl.program_id(1)

    @pl.when(j == 0)
    def _():
        o_ref[...] = jnp.zeros_like(o_ref)

    s = lax.dot_general(h_ref[...], u_ref[...], NT_DIMS, preferred_element_type=F32)
    a = (m_ref[...].astype(F32) * jax.nn.gelu(s)).astype(BF16)
    o_ref[...] += _dot(a, v_ref[...])

    @pl.when(j == pl.num_programs(1) - 1)
    def _():
        r = _mod_row(pl.program_id(0) * tm)
        o_ref[...] = x_ref[...] + g2_ref[pl.ds(r, 1), :] * o_ref[...]


EXPERT_TM, EXPERT_TE = 1024, 512


def _peer_expert(h2, u_bf, v_bf, mask, x1, mod):
    tm, te = EXPERT_TM, EXPERT_TE
    return pl.pallas_call(
        functools.partial(_peer_expert_kernel, tm=tm),
        out_shape=jax.ShapeDtypeStruct((N_TOK, D_MODEL), F32),
        grid=(N_TOK // tm, N_EXPERTS // te),
        in_specs=[
            pl.BlockSpec((tm, D_MODEL), lambda i, j: (i, 0)),
            pl.BlockSpec((te, D_MODEL), lambda i, j: (j, 0)),
            pl.BlockSpec((te, D_MODEL), lambda i, j: (j, 0)),
            pl.BlockSpec((tm, te), lambda i, j: (i, j)),
            pl.BlockSpec((tm, D_MODEL), lambda i, j: (i, 0)),
            pl.BlockSpec((MOD_ROWS, D_MODEL), lambda i, j: (0, 5)),
        ],
        out_specs=pl.BlockSpec((tm, D_MODEL), lambda i, j: (i, 0)),
        compiler_params=_params("arbitrary", "arbitrary"),
        name="peer_expert",
    )(h2, u_bf, v_bf, mask, x1, mod)


def kernel(x_prompt, x_sample, cache_k, cache_v, state_hgrn, c, c_ctx, ada_w, ada_b, norm1_g, norm2_g, w_in,
           q_norm_g, k_norm_g, hg_lb, hg_onorm_g, conv_w, conv_b, w_out, peer_wq, peer_k1, peer_k2, peer_u, peer_v):
    x = jnp.concatenate([x_prompt.reshape(N_PROMPT, D_MODEL), x_sample.reshape(N_SAMPLE, D_MODEL)], axis=0)
    cvec = jnp.concatenate([c_ctx[None, :], c, jnp.zeros((MOD_ROWS - 1 - DEC_BATCH, D_MODEL), F32)], axis=0)
    mod_all = _ada(cvec, ada_w, ada_b)

    w_in_bf, w_out_bf, wq_bf = w_in.astype(BF16), w_out.astype(BF16), peer_wq.astype(BF16)
    k1_bf, k2_bf = peer_k1.astype(BF16), peer_k2.astype(BF16)

    ks_out, vs_out, ss_out = [], [], []
    for l in range(DEPTH):
        mod = mod_all[l]
        z = _inproj(x, mod, norm1_g[l], w_in_bf, l)

        fn_p = _fourier(z, SEQ, BATCH, 0)
        fn_s = _fourier(z, DEC_SEQ, DEC_BATCH, N_PROMPT)

        at_p, k_l, v_l = _attn_prompt(z, q_norm_g[l], k_norm_g[l])
        k_ctx = cache_k[:, l].reshape(DEC_BATCH, PAST_LEN, N_KV * HEAD_DIM)
        v_ctx = cache_v[:, l].reshape(DEC_BATCH, PAST_LEN, N_KV * HEAD_DIM)
        at_s = _attn_sample(z, q_norm_g[l], k_norm_g[l], k_ctx, v_ctx)

        hg_p, s_l = _hgrn(z, hg_lb, hg_onorm_g[l], l, SEQ, BATCH, 0, None)
        hg_s, _ = _hgrn(z, hg_lb, hg_onorm_g[l], l, DEC_SEQ, DEC_BATCH, N_PROMPT, state_hgrn[:, l])

        cv_p = _conv(z, conv_w[l], conv_b[l], SEQ, BATCH, 0)
        cv_s = _conv(z, conv_w[l], conv_b[l], DEC_SEQ, DEC_BATCH, N_PROMPT)

        x1, h2 = _outproj((fn_p, at_p, hg_p, cv_p), (fn_s, at_s, hg_s, cv_s), w_out_bf, l, x, mod, norm2_g[l])

        q_peer = _peer_query(h2, wq_bf, l)
        experts_t, gates_t, u_bf, v_bf = _peer_topk(q_peer, k1_bf, k2_bf, peer_u, peer_v, l)
        mask = _peer_mask(experts_t, gates_t)
        x = _peer_expert(h2, u_bf, v_bf, mask, x1, mod)

        ks_out.append(k_l.reshape(BATCH, SEQ, N_KV, HEAD_DIM))
        vs_out.append(v_l.reshape(BATCH, SEQ, N_KV, HEAD_DIM))
        ss_out.append(s_l)

    y_prompt = x[:N_PROMPT].reshape(BATCH, SEQ, D_MODEL)
    y_sample = x[N_PROMPT:].reshape(DEC_BATCH, DEC_SEQ, D_MODEL)
    return (y_prompt, y_sample, jnp.stack(ks_out, axis=1), jnp.stack(vs_out, axis=1), jnp.stack(ss_out, axis=1))
```

```python
import functools

import numpy as np
import jax
import jax.numpy as jnp
from jax import lax
from jax.experimental import pallas as pl
from jax.experimental.pallas import tpu as pltpu

F32 = jnp.float32
BF16 = jnp.bfloat16
MIX_DTYPE = BF16

D_MODEL = 2048
BATCH = 32
SEQ = 256
DEPTH = 2
DEC_BATCH = 4
DEC_SEQ = 1024
PAST_LEN = 256
GRID_W = 64
GROUP_W = D_MODEL // 4
FN_GROUPS = 4
FN_GW = GROUP_W // FN_GROUPS
HEAD_DIM = 128
N_HEADS = GROUP_W // HEAD_DIM
N_KV = 2
GQA_GROUP = N_HEADS // N_KV
ROPE_THETA = 10000.0
HG_DK = 128
HG_DV = 128
HG_HEADS = GROUP_W // HG_DK
CONV_K = 3
IN_COLS = 11 * GROUP_W
PEER_HEADS = 8
N_KEYS = 128
N_EXPERTS = N_KEYS * N_KEYS
PEER_QDIM = 256
PEER_HALF = PEER_QDIM // 2
PEER_TOPK = 16
EPS = 1e-6
F_FLOOR = 1e-30

N_PROMPT = BATCH * SEQ
N_SAMPLE = DEC_BATCH * DEC_SEQ
N_TOK = N_PROMPT + N_SAMPLE
MOD_ROWS = 8
LANES = 128
NORM_ROWS = 128
HG_CHUNK = 32
HG_GROUP = 8
HG_UNROLL = 16
HG_CUM_BLOCK = 256
MASK_PITCH = 132
VMEM_LIMIT = 56 * 1024 * 1024

COL_FN, COL_Q, COL_K, COL_V = 0, 4, 8, 10
COL_GQ, COL_GFF, COL_GFB, COL_GI, COL_GO = 12, 16, 20, 24, 28
COL_CB, COL_CC, COL_CX = 32, 36, 40

NT_DIMS = (((1,), (1,)), ((), ()))
TN_DIMS = (((0,), (0,)), ((), ()))


def _params(*sem):
    return pltpu.CompilerParams(dimension_semantics=sem, vmem_limit_bytes=VMEM_LIMIT)


def _dot(a, b):
    return jnp.dot(a, b, preferred_element_type=F32)


def _split3(x):
    h = x.astype(BF16)
    r = x - h.astype(F32)
    m = r.astype(BF16)
    l = (r - m.astype(F32)).astype(BF16)
    return h, m, l


def _split2(x):
    h = x.astype(BF16)
    return h, (x - h.astype(F32)).astype(BF16)


def _mod_row(tok0):
    return jnp.where(tok0 >= N_PROMPT, (tok0 - N_PROMPT) // DEC_SEQ + 1, 0)


def _rms(x, g):
    return x * lax.rsqrt(jnp.mean(x * x, axis=-1, keepdims=True) + EPS) * g


def _ada_kernel(c_ref, w_ref, b_ref, o_ref):
    c = c_ref[...]
    s = (c * jax.nn.sigmoid(c)).astype(BF16)
    o_ref[0] = _dot(s, w_ref[0].astype(BF16)) + b_ref[0]


def _ada(cvec, ada_w, ada_b):
    tn = 1024
    return pl.pallas_call(
        _ada_kernel,
        out_shape=jax.ShapeDtypeStruct((DEPTH, MOD_ROWS, 6 * D_MODEL), F32),
        grid=(DEPTH, 6 * D_MODEL // tn),
        in_specs=[
            pl.BlockSpec((MOD_ROWS, D_MODEL), lambda l, j: (0, 0)),
            pl.BlockSpec((1, D_MODEL, tn), lambda l, j: (l, 0, j)),
            pl.BlockSpec((1, 1, tn), lambda l, j: (l, 0, j)),
        ],
        out_specs=pl.BlockSpec((1, MOD_ROWS, tn), lambda l, j: (l, 0, j)),
        compiler_params=_params("arbitrary", "arbitrary"),
        name="ada",
    )(cvec, ada_w, ada_b.reshape(DEPTH, 1, 6 * D_MODEL))


def _token_source_specs(tm, x_prompt, x_sample, n_grid_axes):
    n_pb, n_sb = N_PROMPT // tm, N_SAMPLE // tm
    s_off = x_sample.shape[0] // tm - n_sb
    if n_grid_axes == 1:
        return (pl.BlockSpec((tm, D_MODEL), lambda i: (jnp.minimum(i, n_pb - 1), 0)),
                pl.BlockSpec((tm, D_MODEL), lambda i: (s_off + jnp.clip(i - n_pb, 0, n_sb - 1), 0)))
    return (pl.BlockSpec((tm, D_MODEL), lambda i, j: (jnp.minimum(i, n_pb - 1), 0)),
            pl.BlockSpec((tm, D_MODEL), lambda i, j: (s_off + jnp.clip(i - n_pb, 0, n_sb - 1), 0)))


def _inproj_kernel(xp_ref, xs_ref, sh_ref, sc_ref, g_ref, w_ref, o_ref, h_ref, *, tm):
    i, j = pl.program_id(0), pl.program_id(1)

    def prologue(x_ref):
        r = _mod_row(i * tm)
        gain = g_ref[...] * (1.0 + sc_ref[pl.ds(r, 1), :])
        shift = sh_ref[pl.ds(r, 1), :]

        def norm_rows(c, carry):
            rows = pl.ds(pl.multiple_of(c * NORM_ROWS, NORM_ROWS), NORM_ROWS)
            x = x_ref[rows, :]
            y = x * lax.rsqrt(jnp.mean(x * x, axis=-1, keepdims=True) + EPS)
            h_ref[rows, :] = (y * gain + shift).astype(BF16)
            return carry

        lax.fori_loop(0, tm // NORM_ROWS, norm_rows, 0)

    pl.when((j == 0) & (i < N_PROMPT // tm))(lambda: prologue(xp_ref))
    pl.when((j == 0) & (i >= N_PROMPT // tm))(lambda: prologue(xs_ref))
    o_ref[...] = _dot(h_ref[...], w_ref[...])


def _inproj(x_prompt, x_sample, mod, norm_g, w_in_bf, layer):
    tm, tn = 1024, 512
    return pl.pallas_call(
        functools.partial(_inproj_kernel, tm=tm),
        out_shape=jax.ShapeDtypeStruct((N_TOK, IN_COLS), F32),
        grid=(N_TOK // tm, IN_COLS // tn),
        in_specs=[
            *_token_source_specs(tm, x_prompt, x_sample, 2),
            pl.BlockSpec((MOD_ROWS, D_MODEL), lambda i, j: (0, 0)),
            pl.BlockSpec((MOD_ROWS, D_MODEL), lambda i, j: (0, 1)),
            pl.BlockSpec((1, D_MODEL), lambda i, j: (0, 0)),
            pl.BlockSpec((None, D_MODEL, tn), lambda i, j: (layer, 0, j)),
        ],
        out_specs=pl.BlockSpec((tm, tn), lambda i, j: (i, j)),
        scratch_shapes=[pltpu.VMEM((tm, D_MODEL), BF16)],
        compiler_params=_params("arbitrary", "arbitrary"),
        name="inproj",
    )(x_prompt, x_sample, mod, mod, norm_g.reshape(1, D_MODEL), w_in_bf)


def _fourier_kernel(u_ref, cch_ref, ccl_ref, cth_ref, ctl_ref, o_ref):
    cch, ccl, cth, ctl = cch_ref[...], ccl_ref[...], cth_ref[...], ctl_ref[...]
    for g in range(0, FN_GROUPS, 2):
        ws = []
        for gg in (g, g + 1):
            uh, ul = _split2(u_ref[:, gg * FN_GW:(gg + 1) * FN_GW])
            ws.append(_dot(uh, cch) + (_dot(ul, cch) + _dot(uh, ccl)))
        st = jnp.concatenate([jnp.concatenate([w[:, :FN_GW] for w in ws], axis=1),
                              jnp.concatenate([w[:, FN_GW:] for w in ws], axis=1)], axis=0)
        sh, sl = _split2(st)
        o_ref[:, g * FN_GW:(g + 2) * FN_GW] = (_dot(cth, sh) + (_dot(cth, sl) + _dot(ctl, sh))).astype(MIX_DTYPE)


def _dft_consts(t):
    def cs(n):
        k = np.arange(n, dtype=np.float64)
        ang = 2.0 * np.pi * np.outer(k, k) / n
        return np.cos(ang) / np.sqrt(n), np.sin(ang) / np.sqrt(n)

    cc, sc = cs(FN_GW)
    ct, st = cs(t)
    right = np.concatenate([cc, sc], axis=1).astype(np.float32)
    left = np.concatenate([ct, -st], axis=1).astype(np.float32)

    def hl(a):
        a = jnp.asarray(a)
        h = a.astype(BF16)
        return h, (a - h.astype(F32)).astype(BF16)

    return hl(right) + hl(left)


def _fourier(z, t, nb, row0):
    cch, ccl, cth, ctl = _dft_consts(t)
    rb = row0 // t
    return pl.pallas_call(
        _fourier_kernel,
        out_shape=jax.ShapeDtypeStruct((nb * t, GROUP_W), MIX_DTYPE),
        grid=(nb,),
        in_specs=[
            pl.BlockSpec((t, GROUP_W), lambda b: (rb + b, COL_FN // 4)),
            pl.BlockSpec((FN_GW, 2 * FN_GW), lambda b: (0, 0)),
            pl.BlockSpec((FN_GW, 2 * FN_GW), lambda b: (0, 0)),
            pl.BlockSpec((t, 2 * t), lambda b: (0, 0)),
            pl.BlockSpec((t, 2 * t), lambda b: (0, 0)),
        ],
        out_specs=pl.BlockSpec((t, GROUP_W), lambda b: (b, 0)),
        compiler_params=_params("arbitrary"),
        name=f"fourier_t{t}",
    )(z, cch, ccl, cth, ctl)


def _softmax_pv(q_bf, k_bf, v_bf):
    s = lax.dot_general(q_bf, k_bf, NT_DIMS, preferred_element_type=F32) * (HEAD_DIM ** -0.5)
    p = jnp.exp(s - jnp.max(s, axis=-1, keepdims=True))
    return _dot(p.astype(BF16), v_bf) / jnp.sum(p, axis=-1, keepdims=True)


def _attn_prompt_kernel(q_ref, k_ref, v_ref, qg_ref, kg_ref, o_ref, kc_ref, vc_ref):
    kn = _rms(k_ref[...], kg_ref[...])
    v = v_ref[...]
    kc_ref[...] = kn
    vc_ref[...] = v
    kb, vb = kn.astype(BF16), v.astype(BF16)
    for g in range(GQA_GROUP):
        qn = _rms(q_ref[:, g * HEAD_DIM:(g + 1) * HEAD_DIM], qg_ref[...])
        o_ref[:, g * HEAD_DIM:(g + 1) * HEAD_DIM] = _softmax_pv(qn.astype(BF16), kb, vb).astype(MIX_DTYPE)


def _attn_prompt(z, q_g, k_g):
    t = SEQ
    kv_shape = jax.ShapeDtypeStruct((N_PROMPT, N_KV * HEAD_DIM), F32)
    return pl.pallas_call(
        _attn_prompt_kernel,
        out_shape=(jax.ShapeDtypeStruct((N_PROMPT, GROUP_W), MIX_DTYPE), kv_shape, kv_shape),
        grid=(BATCH, N_KV),
        in_specs=[
            pl.BlockSpec((t, GQA_GROUP * HEAD_DIM), lambda b, h: (b, COL_Q // 2 + h)),
            pl.BlockSpec((t, HEAD_DIM), lambda b, h: (b, COL_K + h)),
            pl.BlockSpec((t, HEAD_DIM), lambda b, h: (b, COL_V + h)),
            pl.BlockSpec((1, HEAD_DIM), lambda b, h: (0, 0)),
            pl.BlockSpec((1, HEAD_DIM), lambda b, h: (0, 0)),
        ],
        out_specs=(
            pl.BlockSpec((t, GQA_GROUP * HEAD_DIM), lambda b, h: (b, h)),
            pl.BlockSpec((t, HEAD_DIM), lambda b, h: (b, h)),
            pl.BlockSpec((t, HEAD_DIM), lambda b, h: (b, h)),
        ),
        compiler_params=_params("arbitrary", "arbitrary"),
        name="attn_prompt",
    )(z, z, z, q_g.reshape(1, HEAD_DIM), k_g.reshape(1, HEAD_DIM))


def _rope(x, cosf, sinf):
    return x * cosf + pltpu.roll(x, HEAD_DIM // 2, axis=1) * sinf


def _attn_sample_kernel(q_ref, k_ref, v_ref, kctx_ref, vctx_ref, qg_ref, kg_ref, cos_ref, sin_ref,
                        o_ref, kall_ref, vall_ref, *, qb):
    cosf, sinf = cos_ref[...], sin_ref[...]
    kall_ref[0:PAST_LEN, :] = kctx_ref[0].astype(BF16)
    vall_ref[0:PAST_LEN, :] = vctx_ref[0].astype(BF16)
    kall_ref[PAST_LEN:, :] = _rope(_rms(k_ref[...], kg_ref[...]), cosf, sinf).astype(BF16)
    vall_ref[PAST_LEN:, :] = v_ref[...].astype(BF16)
    kb, vb = kall_ref[...], vall_ref[...]
    for i in range(DEC_SEQ // qb):
        rows = slice(i * qb, (i + 1) * qb)
        for g in range(GQA_GROUP):
            cols = slice(g * HEAD_DIM, (g + 1) * HEAD_DIM)
            qn = _rope(_rms(q_ref[rows, cols], qg_ref[...]), cosf[rows], sinf[rows])
            o_ref[rows, cols] = _softmax_pv(qn.astype(BF16), kb, vb).astype(MIX_DTYPE)


def _rope_tables():
    rows = DEC_SEQ // GRID_W
    row = np.repeat(np.arange(rows, dtype=np.float32), GRID_W)
    col = np.tile(np.arange(GRID_W, dtype=np.float32), rows)
    n_freq = HEAD_DIM // 4
    inv = (np.float32(ROPE_THETA) ** (-np.arange(n_freq, dtype=np.float32) / n_freq)).astype(np.float32)
    ang = np.concatenate([row[:, None] * inv, col[:, None] * inv], axis=-1).astype(np.float32)
    cos, sin = np.cos(ang.astype(np.float64)), np.sin(ang.astype(np.float64))
    cosf = np.concatenate([cos, cos], axis=-1).astype(np.float32)
    sinf = np.concatenate([-sin, sin], axis=-1).astype(np.float32)
    return jnp.asarray(cosf), jnp.asarray(sinf)


def _attn_sample(z, q_g, k_g, k_ctx, v_ctx):
    t = DEC_SEQ
    rb = N_PROMPT // t
    cosf, sinf = _rope_tables()
    const = lambda b, h: (0, 0)
    return pl.pallas_call(
        functools.partial(_attn_sample_kernel, qb=256),
        out_shape=jax.ShapeDtypeStruct((N_SAMPLE, GROUP_W), MIX_DTYPE),
        grid=(DEC_BATCH, N_KV),
        in_specs=[
            pl.BlockSpec((t, GQA_GROUP * HEAD_DIM), lambda b, h: (rb + b, COL_Q // 2 + h)),
            pl.BlockSpec((t, HEAD_DIM), lambda b, h: (rb + b, COL_K + h)),
            pl.BlockSpec((t, HEAD_DIM), lambda b, h: (rb + b, COL_V + h)),
            pl.BlockSpec((1, PAST_LEN, HEAD_DIM), lambda b, h: (b, 0, h)),
            pl.BlockSpec((1, PAST_LEN, HEAD_DIM), lambda b, h: (b, 0, h)),
            pl.BlockSpec((1, HEAD_DIM), const),
            pl.BlockSpec((1, HEAD_DIM), const),
            pl.BlockSpec((t, HEAD_DIM), const),
            pl.BlockSpec((t, HEAD_DIM), const),
        ],
        out_specs=pl.BlockSpec((t, GQA_GROUP * HEAD_DIM), lambda b, h: (b, h)),
        scratch_shapes=[pltpu.VMEM((PAST_LEN + t, HEAD_DIM), BF16), pltpu.VMEM((PAST_LEN + t, HEAD_DIM), BF16)],
        compiler_params=_params("arbitrary", "arbitrary"),
        name="attn_sample",
    )(z, z, z, k_ctx, v_ctx, q_g.reshape(1, HEAD_DIM), k_g.reshape(1, HEAD_DIM), cosf, sinf)


def _hgrn_chain(rev, r0, gq_ref, gi_ref, k_scr, cum_scr, st_scr, o_scr):
    c = HG_CHUNK
    d = 1 if rev else 0
    rows = pl.ds(r0, c)
    qc, vc = gq_ref[rows, :], gi_ref[rows, :]
    kc, cc = k_scr[d, rows, :], cum_scr[d, rows, :]
    st = st_scr[d]
    last = cc[0:1] if rev else cc[c - 1:c]
    qs = (qc * jnp.exp(cc)).astype(BF16)
    o_inter = lax.dot_general(qs, st.astype(BF16), NT_DIMS, preferred_element_type=F32)
    ks = (kc * jnp.exp(last - cc)).astype(BF16)
    upd = lax.dot_general(vc.astype(BF16), ks, TN_DIMS, preferred_element_type=F32)
    st_scr[d] = st * jnp.exp(last) + upd
    n_grp = c // HG_GROUP
    row = lax.broadcasted_iota(jnp.int32, (HG_GROUP, 1), 0)
    krow = lax.broadcasted_iota(jnp.int32, (c, 1), 0)
    diag, att = [], []
    for i in range(n_grp):
        lo = i * HG_GROUP
        grp = slice(lo, lo + HG_GROUP)
        q_i, k_i, v_i, c_i = qc[grp], kc[grp], vc[grp], cc[grp]
        acc = jnp.zeros((HG_GROUP, HG_DV), F32)
        for delta in range(HG_GROUP):
            if delta == 0:
                k_sh, c_sh, v_sh = k_i, c_i, v_i
            else:
                sh = (HG_GROUP - delta) if rev else delta
                k_sh, c_sh, v_sh = (pltpu.roll(a, sh, axis=0) for a in (k_i, c_i, v_i))
            a = jnp.sum(q_i * k_sh * jnp.exp(jnp.minimum(c_i - c_sh, 0.0)), axis=-1, keepdims=True)
            valid = (row <= HG_GROUP - 1 - delta) if rev else (row >= delta)
            acc = acc + jnp.where(valid, a, 0.0) * v_sh
        diag.append(acc)
        if (i == n_grp - 1) if rev else (i == 0):
            att.append(jnp.zeros((HG_GROUP, c), F32))
            continue
        if rev:
            b, other = cc[lo + HG_GROUP:lo + HG_GROUP + 1], krow >= lo + HG_GROUP
        else:
            b, other = cc[lo - 1:lo], krow < lo
        qs_i = (q_i * jnp.exp(c_i - b)).astype(BF16)
        ks_i = jnp.where(other, kc * jnp.exp(jnp.minimum(b - cc, 0.0)), 0.0).astype(BF16)
        att.append(lax.dot_general(qs_i, ks_i, NT_DIMS, preferred_element_type=F32))
    o_off = _dot(jnp.concatenate(att, axis=0).astype(BF16), vc.astype(BF16))
    o_scr[d, rows, :] = o_inter + jnp.concatenate(diag, axis=0) + o_off


def _hgrn_kernel(*refs, layer, t, has_ctx):
    gq_ref, gff_ref, gfb_ref, gi_ref, go_ref, lb_ref, on_ref, bdf_ref, bdb_ref = refs[:9]
    n_in = 9
    s0_ref = None
    if has_ctx:
        s0_ref = refs[9]
        n_in = 10
    o_ref, sout_ref = refs[n_in], refs[n_in + 1]
    k_scr, cum_scr, st_scr, o_scr = refs[n_in + 2:]

    raw = lb_ref[...]
    e = jnp.exp(raw - jnp.max(raw, axis=0, keepdims=True))
    soft = e / jnp.sum(e, axis=0, keepdims=True)
    csum = soft[0]
    for i in range(1, layer + 1):
        csum = csum + soft[i]
    lb = csum - soft[0]

    for d, (g_ref, bd_ref) in enumerate(((gff_ref, bdf_ref), (gfb_ref, bdb_ref))):
        lbd = lb[d:d + 1]
        f = jnp.maximum(lbd + (1.0 - lbd) * jax.nn.sigmoid(g_ref[...]), F_FLOOR)
        k_scr[d] = 1.0 - f
        lf = jnp.log(f)
        bd = bd_ref[...]
        for p in range(t // HG_CUM_BLOCK):
            rows = slice(p * HG_CUM_BLOCK, (p + 1) * HG_CUM_BLOCK)
            h, m, l = _split3(lf[rows])
            cum_scr[d, rows, :] = _dot(bd, h) + _dot(bd, m) + _dot(bd, l)
        if has_ctx:
            st_scr[d] = s0_ref[0, d, 0].T
        else:
            st_scr[d] = jnp.zeros((HG_DV, HG_DK), F32)

    n = t // HG_CHUNK

    def body(i, carry):
        _hgrn_chain(False, pl.multiple_of(i * HG_CHUNK, HG_CHUNK), gq_ref, gi_ref, k_scr, cum_scr, st_scr, o_scr)
        _hgrn_chain(True, pl.multiple_of((n - 1 - i) * HG_CHUNK, HG_CHUNK), gq_ref, gi_ref, k_scr, cum_scr,
                    st_scr, o_scr)
        return carry

    lax.fori_loop(0, n, body, 0, unroll=min(n, HG_UNROLL))

    o = _rms(o_scr[0] + o_scr[1], on_ref[...])
    go = go_ref[...]
    o_ref[...] = (o * (go * jax.nn.sigmoid(go))).astype(MIX_DTYPE)
    for d in range(2):
        sout_ref[0, d, 0] = st_scr[d].T


def _cum_consts():
    r = np.arange(HG_CUM_BLOCK)
    same = (r[:, None] // HG_CHUNK) == (r[None, :] // HG_CHUNK)
    fwd = same & (r[None, :] <= r[:, None])
    bwd = same & (r[None, :] >= r[:, None])
    return jnp.asarray(fwd, BF16), jnp.asarray(bwd, BF16)


def _hgrn(z, hg_lb, onorm_g, layer, t, nb, row0, s0):
    rb = row0 // t
    bdf, bdb = _cum_consts()
    has_ctx = s0 is not None
    col = lambda c0: (lambda b, h: (rb + b, c0 + h))
    in_specs = [
        pl.BlockSpec((t, HG_DK), col(COL_GQ)),
        pl.BlockSpec((t, HG_DK), col(COL_GFF)),
        pl.BlockSpec((t, HG_DK), col(COL_GFB)),
        pl.BlockSpec((t, HG_DV), col(COL_GI)),
        pl.BlockSpec((t, HG_DV), col(COL_GO)),
        pl.BlockSpec((DEPTH, 2, HG_DK), lambda b, h: (0, 0, h)),
        pl.BlockSpec((1, HG_DV), lambda b, h: (0, 0)),
        pl.BlockSpec((HG_CUM_BLOCK, HG_CUM_BLOCK), lambda b, h: (0, 0)),
        pl.BlockSpec((HG_CUM_BLOCK, HG_CUM_BLOCK), lambda b, h: (0, 0)),
    ]
    args = [z, z, z, z, z, hg_lb, onorm_g.reshape(1, HG_DV), bdf, bdb]
    if has_ctx:
        in_specs.append(pl.BlockSpec((1, 2, 1, HG_DK, HG_DV), lambda b, h: (b, 0, h, 0, 0)))
        args.append(s0)
    return pl.pallas_call(
        functools.partial(_hgrn_kernel, layer=layer, t=t, has_ctx=has_ctx),
        out_shape=(jax.ShapeDtypeStruct((nb * t, GROUP_W), MIX_DTYPE),
                   jax.ShapeDtypeStruct((nb, 2, HG_HEADS, HG_DK, HG_DV), F32)),
        grid=(nb, HG_HEADS),
        in_specs=in_specs,
        out_specs=(pl.BlockSpec((t, HG_DV), lambda b, h: (b, h)),
                   pl.BlockSpec((1, 2, 1, HG_DK, HG_DV), lambda b, h: (b, 0, h, 0, 0))),
        scratch_shapes=[pltpu.VMEM((2, t, HG_DK), F32), pltpu.VMEM((2, t, HG_DK), F32),
                        pltpu.VMEM((2, HG_DV, HG_DK), F32), pltpu.VMEM((2, t, HG_DV), F32)],
        compiler_params=_params("arbitrary", "arbitrary"),
        name=f"hgrn_t{t}",
    )(*args)


def _conv_kernel(cb_ref, cc_ref, cx_ref, w_ref, b_ref, o_ref, *, t):
    p = cc_ref[...] * cx_ref[...]
    row = lax.broadcasted_iota(jnp.int32, (t, 1), 0)
    prev = jnp.where(row >= 1, pltpu.roll(p, 1, axis=0), 0.0)
    nxt = jnp.where(row <= t - 2, pltpu.roll(p, t - 1, axis=0), 0.0)
    w = w_ref[...]
    y = prev * w[0:1] + p * w[1:2] + nxt * w[2:3] + b_ref[...]
    o_ref[...] = (cb_ref[...] * y).astype(MIX_DTYPE)


def _conv(z, w, b, t, nb, row0):
    rb = row0 // t
    return pl.pallas_call(
        functools.partial(_conv_kernel, t=t),
        out_shape=jax.ShapeDtypeStruct((nb * t, GROUP_W), MIX_DTYPE),
        grid=(nb,),
        in_specs=[
            pl.BlockSpec((t, GROUP_W), lambda i: (rb + i, COL_CB // 4)),
            pl.BlockSpec((t, GROUP_W), lambda i: (rb + i, COL_CC // 4)),
            pl.BlockSpec((t, GROUP_W), lambda i: (rb + i, COL_CX // 4)),
            pl.BlockSpec((CONV_K, GROUP_W), lambda i: (0, 0)),
            pl.BlockSpec((1, GROUP_W), lambda i: (0, 0)),
        ],
        out_specs=pl.BlockSpec((t, GROUP_W), lambda i: (i, 0)),
        compiler_params=_params("arbitrary"),
        name=f"conv_t{t}",
    )(z, z, z, w, b.reshape(1, GROUP_W))


def _outproj_kernel(*refs, tm):
    prompt_refs, sample_refs = refs[0:4], refs[4:8]
    w_ref, xp_ref, xs_ref, g1_ref, sh_ref, sc_ref, n2_ref, x1_ref, h2_ref = refs[8:]
    i = pl.program_id(0)
    r = _mod_row(i * tm)

    def run(group_refs, x_ref):
        acc = None
        for g, ref in enumerate(group_refs):
            part = _dot(ref[...], w_ref[g * GROUP_W:(g + 1) * GROUP_W, :])
            acc = part if acc is None else acc + part
        x1 = x_ref[...] + g1_ref[pl.ds(r, 1), :] * acc
        x1_ref[...] = x1
        h2 = _rms(x1, n2_ref[...]) * (1.0 + sc_ref[pl.ds(r, 1), :]) + sh_ref[pl.ds(r, 1), :]
        h2_ref[...] = h2.astype(BF16)

    pl.when(i < N_PROMPT // tm)(lambda: run(prompt_refs, xp_ref))
    pl.when(i >= N_PROMPT // tm)(lambda: run(sample_refs, xs_ref))


def _outproj(prompt_groups, sample_groups, w_out_bf, layer, x_prompt, x_sample, mod, norm2_g):
    tm = 512
    n_pb, n_sb = N_PROMPT // tm, N_SAMPLE // tm
    p_spec = pl.BlockSpec((tm, GROUP_W), lambda i: (jnp.minimum(i, n_pb - 1), 0))
    s_spec = pl.BlockSpec((tm, GROUP_W), lambda i: (jnp.clip(i - n_pb, 0, n_sb - 1), 0))
    modspec = lambda k: pl.BlockSpec((MOD_ROWS, D_MODEL), lambda i: (0, k))
    return pl.pallas_call(
        functools.partial(_outproj_kernel, tm=tm),
        out_shape=(jax.ShapeDtypeStruct((N_TOK, D_MODEL), F32), jax.ShapeDtypeStruct((N_TOK, D_MODEL), BF16)),
        grid=(N_TOK // tm,),
        in_specs=[p_spec] * 4 + [s_spec] * 4 + [
            pl.BlockSpec((None, D_MODEL, D_MODEL), lambda i: (layer, 0, 0), pipeline_mode=pl.Buffered(1)),
            *_token_source_specs(tm, x_prompt, x_sample, 1),
            modspec(2), modspec(3), modspec(4),
            pl.BlockSpec((1, D_MODEL), lambda i: (0, 0))],
        out_specs=(pl.BlockSpec((tm, D_MODEL), lambda i: (i, 0)), pl.BlockSpec((tm, D_MODEL), lambda i: (i, 0))),
        compiler_params=_params("arbitrary"),
        name="outproj",
    )(*prompt_groups, *sample_groups, w_out_bf, x_prompt, x_sample, mod, mod, mod, norm2_g.reshape(1, D_MODEL))


def _peer_query_kernel(h_ref, wq_ref, q_ref):
    q_ref[...] = _dot(h_ref[...], wq_ref[...]).astype(BF16)


def _peer_query(h2, wq_bf, layer):
    tm = 512
    width = PEER_HEADS * PEER_QDIM
    return pl.pallas_call(
        _peer_query_kernel,
        out_shape=jax.ShapeDtypeStruct((N_TOK, width), BF16),
        grid=(N_TOK // tm,),
        in_specs=[pl.BlockSpec((tm, D_MODEL), lambda i: (i, 0)),
                  pl.BlockSpec((None, D_MODEL, width), lambda i: (layer, 0, 0))],
        out_specs=pl.BlockSpec((tm, width), lambda i: (i, 0)),
        compiler_params=_params("arbitrary"),
        name="peer_query",
    )(h2, wq_bf)


_CAND_RUNS = ((2, 5), (3, 4), (4, 3), (5, 2), (6, 2), (7, 2))
_CAND_RUN_ROWS = 24
_CAND_ROWS = 4 * 8 + _CAND_RUN_ROWS
_NO_CAND = 1 << 20


def _candidates(v1, v2, tm):
    sub = lax.broadcasted_iota(jnp.int32, (8, tm), 0)
    vals = [v1[0:1] + v2[0:8], v1[0:1] + v2[8:16], v1[1:2] + v2[0:8], v1[8:16] + v2[0:1]]
    ids = [sub, sub + 8, sub + PEER_TOPK, (sub + 8) * PEER_TOPK]
    row = lax.broadcasted_iota(jnp.int32, (_CAND_RUN_ROWS, tm), 0)
    v2_rep = jnp.concatenate([v2[0:8]] * (_CAND_RUN_ROWS // 8), axis=0)
    run_vals = jnp.full((_CAND_RUN_ROWS, tm), -jnp.inf, F32)
    run_ids = jnp.full((_CAND_RUN_ROWS, tm), _NO_CAND, jnp.int32)
    start = 0
    for r1, count in _CAND_RUNS:
        inside = (row >= start) & (row < start + count)
        shifted = v2_rep if start % 8 == 0 else pltpu.roll(v2_rep, start % 8, axis=0)
        run_vals = jnp.where(inside, v1[r1:r1 + 1] + shifted, run_vals)
        run_ids = jnp.where(inside, row + (r1 * PEER_TOPK - start), run_ids)
        start += count
    return jnp.concatenate(vals + [run_vals], axis=0), jnp.concatenate(ids + [run_ids], axis=0)


def _topk_rows(x_scr, ids, n_rows, v_out, i_out):
    def body(j, carry):
        x = x_scr[0:n_rows, :]
        m = jnp.max(x, axis=0, keepdims=True)
        idx = jnp.min(jnp.where(x == m, ids, _NO_CAND), axis=0, keepdims=True)
        v_out[pl.ds(j, 1), :] = m
        i_out[pl.ds(j, 1), :] = idx
        x_scr[0:n_rows, :] = jnp.where(ids == idx, -jnp.inf, x)
        return carry

    lax.fori_loop(0, PEER_TOPK, body, 0, unroll=2)


def _peer_topk_kernel(q_ref, k1_ref, k2_ref, u_ref, v_ref, e_ref, g_ref, ub_ref, vb_ref, x_scr, v1_scr, i1_scr,
                      v2_scr, i2_scr, ts_scr, tc_scr, *, tm):
    ub_ref[...] = u_ref[...].astype(BF16)
    vb_ref[...] = v_ref[...].astype(BF16)
    key_ids = lax.broadcasted_iota(jnp.int32, (N_KEYS, tm), 0)
    for half, (k_ref, v_scr, i_scr) in enumerate(((k1_ref, v1_scr, i1_scr), (k2_ref, v2_scr, i2_scr))):
        qh = q_ref[:, half * PEER_HALF:(half + 1) * PEER_HALF]
        x_scr[...] = lax.dot_general(k_ref[...], qh, NT_DIMS, preferred_element_type=F32)
        _topk_rows(x_scr, key_ids, N_KEYS, v_scr, i_scr)
    cand_vals, cand_ids = _candidates(v1_scr[...], v2_scr[...], tm)
    x_scr[0:_CAND_ROWS, :] = cand_vals
    _topk_rows(x_scr, cand_ids, _CAND_ROWS, ts_scr, tc_scr)
    tc = tc_scr[...]
    c1, c2 = tc // PEER_TOPK, tc % PEER_TOPK
    i1, i2 = i1_scr[...], i2_scr[...]
    e1 = jnp.zeros_like(tc)
    e2 = jnp.zeros_like(tc)
    for a in range(PEER_TOPK):
        e1 = jnp.where(c1 == a, i1[a:a + 1], e1)
        e2 = jnp.where(c2 == a, i2[a:a + 1], e2)
    e_ref[...] = e1 * N_KEYS + e2
    ts = ts_scr[...]
    p = jnp.exp(ts - jnp.max(ts, axis=0, keepdims=True))
    g_ref[...] = p / jnp.sum(p, axis=0, keepdims=True)


def _peer_topk(q, k1_bf, k2_bf, peer_u, peer_v, layer):
    tm = 512
    slab = LANES
    n_slabs = N_EXPERTS // slab
    assert (N_TOK // tm) * PEER_HEADS >= n_slabs
    row_f = pltpu.VMEM((PEER_TOPK, tm), F32)
    row_i = pltpu.VMEM((PEER_TOPK, tm), jnp.int32)
    slab_of = lambda i, h: jnp.minimum(i * PEER_HEADS + h, n_slabs - 1)
    table_in = pl.BlockSpec((None, slab, D_MODEL), lambda i, h: (layer, slab_of(i, h), 0))
    table_out = pl.BlockSpec((slab, D_MODEL), lambda i, h: (slab_of(i, h), 0))
    table_shape = jax.ShapeDtypeStruct((N_EXPERTS, D_MODEL), BF16)
    return pl.pallas_call(
        functools.partial(_peer_topk_kernel, tm=tm),
        out_shape=(jax.ShapeDtypeStruct((PEER_HEADS * PEER_TOPK, N_TOK), jnp.int32),
                   jax.ShapeDtypeStruct((PEER_HEADS * PEER_TOPK, N_TOK), F32), table_shape, table_shape),
        grid=(N_TOK // tm, PEER_HEADS),
        in_specs=[
            pl.BlockSpec((tm, PEER_QDIM), lambda i, h: (i, h)),
            pl.BlockSpec((None, N_KEYS, PEER_HALF), lambda i, h: (layer, 0, 0)),
            pl.BlockSpec((None, N_KEYS, PEER_HALF), lambda i, h: (layer, 0, 0)),
            table_in, table_in,
        ],
        out_specs=(pl.BlockSpec((PEER_TOPK, tm), lambda i, h: (h, i)),
                   pl.BlockSpec((PEER_TOPK, tm), lambda i, h: (h, i)), table_out, table_out),
        scratch_shapes=[pltpu.VMEM((N_KEYS, tm), F32), row_f, row_i, row_f, row_i, row_f, row_i],
        compiler_params=_params("arbitrary", "arbitrary"),
        name="peer_topk",
    )(q, k1_bf, k2_bf, peer_u, peer_v)


def _peer_mask_kernel(e_ref, g_ref, m_ref, et_scr, gt_scr, slab_scr, *, tb):
    et_scr[...] = e_ref[...].T
    gt_scr[...] = g_ref[...].T
    key = lax.broadcasted_iota(jnp.int32, (N_KEYS, PEER_HEADS * PEER_TOPK), 0)

    def body(t, carry):
        er = et_scr[pl.ds(t, 1), :]
        gr = gt_scr[pl.ds(t, 1), :]
        p1 = jnp.where(key == er // N_KEYS, gr, 0.0).astype(BF16)
        p2 = jnp.where(key == er % N_KEYS, 1.0, 0.0).astype(BF16)
        mt = lax.dot_general(p1, p2, NT_DIMS, preferred_element_type=F32)
        slab_scr[pl.ds(pl.multiple_of(t * MASK_PITCH, 4), N_KEYS), :] = mt
        return carry

    lax.fori_loop(0, tb, body, 0, unroll=64)
    for k1 in range(N_KEYS):
        m_ref[:, k1 * N_KEYS:(k1 + 1) * N_KEYS] = slab_scr[pl.ds(k1, tb, stride=MASK_PITCH), :].astype(BF16)


def _peer_mask(experts_t, gates_t):
    tb = LANES
    pairs = PEER_HEADS * PEER_TOPK
    return pl.pallas_call(
        functools.partial(_peer_mask_kernel, tb=tb),
        out_shape=jax.ShapeDtypeStruct((N_TOK, N_EXPERTS), BF16),
        grid=(N_TOK // tb,),
        in_specs=[pl.BlockSpec((pairs, tb), lambda i: (0, i)), pl.BlockSpec((pairs, tb), lambda i: (0, i))],
        out_specs=pl.BlockSpec((tb, N_EXPERTS), lambda i: (i, 0)),
        scratch_shapes=[pltpu.VMEM((tb, pairs), jnp.int32), pltpu.VMEM((tb, pairs), F32),
                        pltpu.VMEM((tb * MASK_PITCH, N_KEYS), F32)],
        compiler_params=_params("arbitrary"),
        name="peer_mask",
    )(experts_t, gates_t)


def _peer_expert_kernel(h_ref, u_ref, v_ref, m_ref, x_ref, g2_ref, o_ref, *, tm):
    j = pl.program_id(1)

    @pl.when(j == 0)
    def _():
        o_ref[...] = jnp.zeros_like(o_ref)

    s = lax.dot_general(h_ref[...], u_ref[...], NT_DIMS, preferred_element_type=F32)
    a = (m_ref[...].astype(F32) * jax.nn.gelu(s)).astype(BF16)
    o_ref[...] += _dot(a, v_ref[...])

    @pl.when(j == pl.num_programs(1) - 1)
    def _():
        r = _mod_row(pl.program_id(0) * tm)
        o_ref[...] = x_ref[...] + g2_ref[pl.ds(r, 1), :] * o_ref[...]


EXPERT_TM, EXPERT_TE = 1024, 512


def _peer_expert(h2, u_bf, v_bf, mask, x1, mod):
    tm, te = EXPERT_TM, EXPERT_TE
    return pl.pallas_call(
        functools.partial(_peer_expert_kernel, tm=tm),
        out_shape=jax.ShapeDtypeStruct((N_TOK, D_MODEL), F32),
        grid=(N_TOK // tm, N_EXPERTS // te),
        in_specs=[
            pl.BlockSpec((tm, D_MODEL), lambda i, j: (i, 0)),
            pl.BlockSpec((te, D_MODEL), lambda i, j: (j, 0)),
            pl.BlockSpec((te, D_MODEL), lambda i, j: (j, 0)),
            pl.BlockSpec((tm, te), lambda i, j: (i, j)),
            pl.BlockSpec((tm, D_MODEL), lambda i, j: (i, 0)),
            pl.BlockSpec((MOD_ROWS, D_MODEL), lambda i, j: (0, 5)),
        ],
        out_specs=pl.BlockSpec((tm, D_MODEL), lambda i, j: (i, 0)),
        compiler_params=_params("arbitrary", "arbitrary"),
        name="peer_expert",
    )(h2, u_bf, v_bf, mask, x1, mod)


def kernel(x_prompt, x_sample, cache_k, cache_v, state_hgrn, c, c_ctx, ada_w, ada_b, norm1_g, norm2_g, w_in,
           q_norm_g, k_norm_g, hg_lb, hg_onorm_g, conv_w, conv_b, w_out, peer_wq, peer_k1, peer_k2, peer_u, peer_v):
    x_p, x_s = x_prompt.reshape(N_PROMPT, D_MODEL), x_sample.reshape(N_SAMPLE, D_MODEL)
    cvec = jnp.concatenate([c_ctx[None, :], c, jnp.zeros((MOD_ROWS - 1 - DEC_BATCH, D_MODEL), F32)], axis=0)
    mod_all = _ada(cvec, ada_w, ada_b)

    w_in_bf, w_out_bf, wq_bf = w_in.astype(BF16), w_out.astype(BF16), peer_wq.astype(BF16)
    k1_bf, k2_bf = peer_k1.astype(BF16), peer_k2.astype(BF16)

    ks_out, vs_out, ss_out = [], [], []
    for l in range(DEPTH):
        mod = mod_all[l]
        z = _inproj(x_p, x_s, mod, norm1_g[l], w_in_bf, l)

        fn_p = _fourier(z, SEQ, BATCH, 0)
        fn_s = _fourier(z, DEC_SEQ, DEC_BATCH, N_PROMPT)

        at_p, k_l, v_l = _attn_prompt(z, q_norm_g[l], k_norm_g[l])
        k_ctx = cache_k[:, l].reshape(DEC_BATCH, PAST_LEN, N_KV * HEAD_DIM)
        v_ctx = cache_v[:, l].reshape(DEC_BATCH, PAST_LEN, N_KV * HEAD_DIM)
        at_s = _attn_sample(z, q_norm_g[l], k_norm_g[l], k_ctx, v_ctx)

        hg_p, s_l = _hgrn(z, hg_lb, hg_onorm_g[l], l, SEQ, BATCH, 0, None)
        hg_s, _ = _hgrn(z, hg_lb, hg_onorm_g[l], l, DEC_SEQ, DEC_BATCH, N_PROMPT, state_hgrn[:, l])

        cv_p = _conv(z, conv_w[l], conv_b[l], SEQ, BATCH, 0)
        cv_s = _conv(z, conv_w[l], conv_b[l], DEC_SEQ, DEC_BATCH, N_PROMPT)

        x1, h2 = _outproj((fn_p, at_p, hg_p, cv_p), (fn_s, at_s, hg_s, cv_s), w_out_bf, l, x_p, x_s, mod,
                          norm2_g[l])

        q_peer = _peer_query(h2, wq_bf, l)
        experts_t, gates_t, u_bf, v_bf = _peer_topk(q_peer, k1_bf, k2_bf, peer_u, peer_v, l)
        mask = _peer_mask(experts_t, gates_t)
        x = _peer_expert(h2, u_bf, v_bf, mask, x1, mod)
        x_p = x_s = x

        ks_out.append(k_l.reshape(BATCH, SEQ, N_KV, HEAD_DIM))
        vs_out.append(v_l.reshape(BATCH, SEQ, N_KV, HEAD_DIM))
        ss_out.append(s_l)

    y_prompt = x[:N_PROMPT].reshape(BATCH, SEQ, D_MODEL)
    y_sample = x[N_PROMPT:].reshape(DEC_BATCH, DEC_SEQ, D_MODEL)
    return (y_prompt, y_sample, jnp.stack(ks_out, axis=1), jnp.stack(vs_out, axis=1), jnp.stack(ss_out, axis=1))
```

```python
import functools

import numpy as np
import jax
import jax.numpy as jnp
from jax import lax
from jax.experimental import pallas as pl
from jax.experimental.pallas import tpu as pltpu

F32 = jnp.float32
BF16 = jnp.bfloat16
MIX_DTYPE = BF16

D_MODEL = 2048
BATCH = 32
SEQ = 256
DEPTH = 2
DEC_BATCH = 4
DEC_SEQ = 1024
PAST_LEN = 256
GRID_W = 64
GROUP_W = D_MODEL // 4
FN_GROUPS = 4
FN_GW = GROUP_W // FN_GROUPS
HEAD_DIM = 128
N_HEADS = GROUP_W // HEAD_DIM
N_KV = 2
GQA_GROUP = N_HEADS // N_KV
ROPE_THETA = 10000.0
HG_DK = 128
HG_DV = 128
HG_HEADS = GROUP_W // HG_DK
CONV_K = 3
IN_COLS = 11 * GROUP_W
PEER_HEADS = 8
N_KEYS = 128
N_EXPERTS = N_KEYS * N_KEYS
PEER_QDIM = 256
PEER_HALF = PEER_QDIM // 2
PEER_TOPK = 16
EPS = 1e-6
F_FLOOR = 1e-30

N_PROMPT = BATCH * SEQ
N_SAMPLE = DEC_BATCH * DEC_SEQ
N_TOK = N_PROMPT + N_SAMPLE
MOD_ROWS = 8
LANES = 128
NORM_ROWS = 128
HG_CHUNK = 32
HG_GROUP = 8
HG_UNROLL = 16
HG_CUM_BLOCK = 256
MASK_TOKENS = 256
MASK_PITCH = 132
VMEM_LIMIT = 56 * 1024 * 1024

COL_FN, COL_Q, COL_K, COL_V = 0, 4, 8, 10
COL_GQ, COL_GFF, COL_GFB, COL_GI, COL_GO = 12, 16, 20, 24, 28
COL_CB, COL_CC, COL_CX = 32, 36, 40

NT_DIMS = (((1,), (1,)), ((), ()))
TN_DIMS = (((0,), (0,)), ((), ()))


def _params(*sem):
    return pltpu.CompilerParams(dimension_semantics=sem, vmem_limit_bytes=VMEM_LIMIT)


def _dot(a, b):
    return jnp.dot(a, b, preferred_element_type=F32)


def _split3(x):
    h = x.astype(BF16)
    r = x - h.astype(F32)
    m = r.astype(BF16)
    l = (r - m.astype(F32)).astype(BF16)
    return h, m, l


def _split2(x):
    h = x.astype(BF16)
    return h, (x - h.astype(F32)).astype(BF16)


def _mod_row(tok0):
    return jnp.where(tok0 >= N_PROMPT, (tok0 - N_PROMPT) // DEC_SEQ + 1, 0)


def _rms(x, g):
    return x * lax.rsqrt(jnp.mean(x * x, axis=-1, keepdims=True) + EPS) * g


def _ada_kernel(c_ref, w_ref, b_ref, o_ref):
    c = c_ref[...]
    s = (c * jax.nn.sigmoid(c)).astype(BF16)
    o_ref[0] = _dot(s, w_ref[0].astype(BF16)) + b_ref[0]


def _ada(cvec, ada_w, ada_b):
    tn = 1024
    return pl.pallas_call(
        _ada_kernel,
        out_shape=jax.ShapeDtypeStruct((DEPTH, MOD_ROWS, 6 * D_MODEL), F32),
        grid=(DEPTH, 6 * D_MODEL // tn),
        in_specs=[
            pl.BlockSpec((MOD_ROWS, D_MODEL), lambda l, j: (0, 0)),
            pl.BlockSpec((1, D_MODEL, tn), lambda l, j: (l, 0, j)),
            pl.BlockSpec((1, 1, tn), lambda l, j: (l, 0, j)),
        ],
        out_specs=pl.BlockSpec((1, MOD_ROWS, tn), lambda l, j: (l, 0, j)),
        compiler_params=_params("arbitrary", "arbitrary"),
        name="ada",
    )(cvec, ada_w, ada_b.reshape(DEPTH, 1, 6 * D_MODEL))


def _token_source_specs(tm, x_prompt, x_sample, n_grid_axes):
    n_pb, n_sb = N_PROMPT // tm, N_SAMPLE // tm
    s_off = x_sample.shape[0] // tm - n_sb
    if n_grid_axes == 1:
        return (pl.BlockSpec((tm, D_MODEL), lambda i: (jnp.minimum(i, n_pb - 1), 0)),
                pl.BlockSpec((tm, D_MODEL), lambda i: (s_off + jnp.clip(i - n_pb, 0, n_sb - 1), 0)))
    return (pl.BlockSpec((tm, D_MODEL), lambda i, j: (jnp.minimum(i, n_pb - 1), 0)),
            pl.BlockSpec((tm, D_MODEL), lambda i, j: (s_off + jnp.clip(i - n_pb, 0, n_sb - 1), 0)))


def _inproj_kernel(xp_ref, xs_ref, sh_ref, sc_ref, g_ref, w_ref, o_ref, h_ref, *, tm):
    i, j = pl.program_id(0), pl.program_id(1)

    def prologue(x_ref):
        r = _mod_row(i * tm)
        gain = g_ref[...] * (1.0 + sc_ref[pl.ds(r, 1), :])
        shift = sh_ref[pl.ds(r, 1), :]

        def norm_rows(c, carry):
            rows = pl.ds(pl.multiple_of(c * NORM_ROWS, NORM_ROWS), NORM_ROWS)
            x = x_ref[rows, :]
            y = x * lax.rsqrt(jnp.mean(x * x, axis=-1, keepdims=True) + EPS)
            h_ref[rows, :] = (y * gain + shift).astype(BF16)
            return carry

        lax.fori_loop(0, tm // NORM_ROWS, norm_rows, 0)

    pl.when((j == 0) & (i < N_PROMPT // tm))(lambda: prologue(xp_ref))
    pl.when((j == 0) & (i >= N_PROMPT // tm))(lambda: prologue(xs_ref))
    o_ref[...] = _dot(h_ref[...], w_ref[...])


def _inproj(x_prompt, x_sample, mod, norm_g, w_in_bf, layer):
    tm, tn = 1024, 512
    return pl.pallas_call(
        functools.partial(_inproj_kernel, tm=tm),
        out_shape=jax.ShapeDtypeStruct((N_TOK, IN_COLS), F32),
        grid=(N_TOK // tm, IN_COLS // tn),
        in_specs=[
            *_token_source_specs(tm, x_prompt, x_sample, 2),
            pl.BlockSpec((MOD_ROWS, D_MODEL), lambda i, j: (0, 0)),
            pl.BlockSpec((MOD_ROWS, D_MODEL), lambda i, j: (0, 1)),
            pl.BlockSpec((1, D_MODEL), lambda i, j: (0, 0)),
            pl.BlockSpec((None, D_MODEL, tn), lambda i, j: (layer, 0, j)),
        ],
        out_specs=pl.BlockSpec((tm, tn), lambda i, j: (i, j)),
        scratch_shapes=[pltpu.VMEM((tm, D_MODEL), BF16)],
        compiler_params=_params("arbitrary", "arbitrary"),
        name="inproj",
    )(x_prompt, x_sample, mod, mod, norm_g.reshape(1, D_MODEL), w_in_bf)


def _fourier_kernel(u_ref, cch_ref, ccl_ref, cth_ref, ctl_ref, o_ref):
    cch, ccl, cth, ctl = cch_ref[...], ccl_ref[...], cth_ref[...], ctl_ref[...]
    for g in range(0, FN_GROUPS, 2):
        ws = []
        for gg in (g, g + 1):
            uh, ul = _split2(u_ref[:, gg * FN_GW:(gg + 1) * FN_GW])
            ws.append(_dot(uh, cch) + (_dot(ul, cch) + _dot(uh, ccl)))
        st = jnp.concatenate([jnp.concatenate([w[:, :FN_GW] for w in ws], axis=1),
                              jnp.concatenate([w[:, FN_GW:] for w in ws], axis=1)], axis=0)
        sh, sl = _split2(st)
        o_ref[:, g * FN_GW:(g + 2) * FN_GW] = (_dot(cth, sh) + (_dot(cth, sl) + _dot(ctl, sh))).astype(MIX_DTYPE)


def _dft_consts(t):
    def cs(n):
        k = np.arange(n, dtype=np.float64)
        ang = 2.0 * np.pi * np.outer(k, k) / n
        return np.cos(ang) / np.sqrt(n), np.sin(ang) / np.sqrt(n)

    cc, sc = cs(FN_GW)
    ct, st = cs(t)
    right = np.concatenate([cc, sc], axis=1).astype(np.float32)
    left = np.concatenate([ct, -st], axis=1).astype(np.float32)

    def hl(a):
        a = jnp.asarray(a)
        h = a.astype(BF16)
        return h, (a - h.astype(F32)).astype(BF16)

    return hl(right) + hl(left)


def _fourier(z, t, nb, row0):
    cch, ccl, cth, ctl = _dft_consts(t)
    rb = row0 // t
    return pl.pallas_call(
        _fourier_kernel,
        out_shape=jax.ShapeDtypeStruct((nb * t, GROUP_W), MIX_DTYPE),
        grid=(nb,),
        in_specs=[
            pl.BlockSpec((t, GROUP_W), lambda b: (rb + b, COL_FN // 4)),
            pl.BlockSpec((FN_GW, 2 * FN_GW), lambda b: (0, 0)),
            pl.BlockSpec((FN_GW, 2 * FN_GW), lambda b: (0, 0)),
            pl.BlockSpec((t, 2 * t), lambda b: (0, 0)),
            pl.BlockSpec((t, 2 * t), lambda b: (0, 0)),
        ],
        out_specs=pl.BlockSpec((t, GROUP_W), lambda b: (b, 0)),
        compiler_params=_params("arbitrary"),
        name=f"fourier_t{t}",
    )(z, cch, ccl, cth, ctl)


def _softmax_pv(q_bf, k_bf, v_bf):
    s = lax.dot_general(q_bf, k_bf, NT_DIMS, preferred_element_type=F32) * (HEAD_DIM ** -0.5)
    p = jnp.exp(s - jnp.max(s, axis=-1, keepdims=True))
    return _dot(p.astype(BF16), v_bf) / jnp.sum(p, axis=-1, keepdims=True)


def _attn_prompt_kernel(q_ref, k_ref, v_ref, qg_ref, kg_ref, o_ref, kc_ref, vc_ref):
    kn = _rms(k_ref[...], kg_ref[...])
    v = v_ref[...]
    head_rows = pl.ds(pl.program_id(1), SEQ, stride=N_KV)
    kc_ref[0, head_rows, :] = kn
    vc_ref[0, head_rows, :] = v
    kb, vb = kn.astype(BF16), v.astype(BF16)
    for g in range(GQA_GROUP):
        qn = _rms(q_ref[:, g * HEAD_DIM:(g + 1) * HEAD_DIM], qg_ref[...])
        o_ref[:, g * HEAD_DIM:(g + 1) * HEAD_DIM] = _softmax_pv(qn.astype(BF16), kb, vb).astype(MIX_DTYPE)


def _attn_prompt(z, q_g, k_g):
    t = SEQ
    kv_shape = jax.ShapeDtypeStruct((BATCH, SEQ * N_KV, HEAD_DIM), F32)
    kv_spec = pl.BlockSpec((1, SEQ * N_KV, HEAD_DIM), lambda b, h: (b, 0, 0))
    return pl.pallas_call(
        _attn_prompt_kernel,
        out_shape=(jax.ShapeDtypeStruct((N_PROMPT, GROUP_W), MIX_DTYPE), kv_shape, kv_shape),
        grid=(BATCH, N_KV),
        in_specs=[
            pl.BlockSpec((t, GQA_GROUP * HEAD_DIM), lambda b, h: (b, COL_Q // 2 + h)),
            pl.BlockSpec((t, HEAD_DIM), lambda b, h: (b, COL_K + h)),
            pl.BlockSpec((t, HEAD_DIM), lambda b, h: (b, COL_V + h)),
            pl.BlockSpec((1, HEAD_DIM), lambda b, h: (0, 0)),
            pl.BlockSpec((1, HEAD_DIM), lambda b, h: (0, 0)),
        ],
        out_specs=(
            pl.BlockSpec((t, GQA_GROUP * HEAD_DIM), lambda b, h: (b, h)),
            kv_spec, kv_spec,
        ),
        compiler_params=_params("arbitrary", "arbitrary"),
        name="attn_prompt",
    )(z, z, z, q_g.reshape(1, HEAD_DIM), k_g.reshape(1, HEAD_DIM))


def _rope(x, cosf, sinf):
    return x * cosf + pltpu.roll(x, HEAD_DIM // 2, axis=1) * sinf


def _attn_sample_kernel(q_ref, k_ref, v_ref, kctx_ref, vctx_ref, qg_ref, kg_ref, cos_ref, sin_ref,
                        o_ref, kall_ref, vall_ref, *, qb):
    cosf, sinf = cos_ref[...], sin_ref[...]
    kall_ref[0:PAST_LEN, :] = kctx_ref[0].astype(BF16)
    vall_ref[0:PAST_LEN, :] = vctx_ref[0].astype(BF16)
    kall_ref[PAST_LEN:, :] = _rope(_rms(k_ref[...], kg_ref[...]), cosf, sinf).astype(BF16)
    vall_ref[PAST_LEN:, :] = v_ref[...].astype(BF16)
    kb, vb = kall_ref[...], vall_ref[...]
    for i in range(DEC_SEQ // qb):
        rows = slice(i * qb, (i + 1) * qb)
        for g in range(GQA_GROUP):
            cols = slice(g * HEAD_DIM, (g + 1) * HEAD_DIM)
            qn = _rope(_rms(q_ref[rows, cols], qg_ref[...]), cosf[rows], sinf[rows])
            o_ref[rows, cols] = _softmax_pv(qn.astype(BF16), kb, vb).astype(MIX_DTYPE)


def _rope_tables():
    rows = DEC_SEQ // GRID_W
    row = np.repeat(np.arange(rows, dtype=np.float32), GRID_W)
    col = np.tile(np.arange(GRID_W, dtype=np.float32), rows)
    n_freq = HEAD_DIM // 4
    inv = (np.float32(ROPE_THETA) ** (-np.arange(n_freq, dtype=np.float32) / n_freq)).astype(np.float32)
    ang = np.concatenate([row[:, None] * inv, col[:, None] * inv], axis=-1).astype(np.float32)
    cos, sin = np.cos(ang.astype(np.float64)), np.sin(ang.astype(np.float64))
    cosf = np.concatenate([cos, cos], axis=-1).astype(np.float32)
    sinf = np.concatenate([-sin, sin], axis=-1).astype(np.float32)
    return jnp.asarray(cosf), jnp.asarray(sinf)


def _attn_sample(z, q_g, k_g, k_ctx, v_ctx):
    t = DEC_SEQ
    rb = N_PROMPT // t
    cosf, sinf = _rope_tables()
    const = lambda b, h: (0, 0)
    return pl.pallas_call(
        functools.partial(_attn_sample_kernel, qb=256),
        out_shape=jax.ShapeDtypeStruct((N_SAMPLE, GROUP_W), MIX_DTYPE),
        grid=(DEC_BATCH, N_KV),
        in_specs=[
            pl.BlockSpec((t, GQA_GROUP * HEAD_DIM), lambda b, h: (rb + b, COL_Q // 2 + h)),
            pl.BlockSpec((t, HEAD_DIM), lambda b, h: (rb + b, COL_K + h)),
            pl.BlockSpec((t, HEAD_DIM), lambda b, h: (rb + b, COL_V + h)),
            pl.BlockSpec((1, PAST_LEN, HEAD_DIM), lambda b, h: (b, 0, h)),
            pl.BlockSpec((1, PAST_LEN, HEAD_DIM), lambda b, h: (b, 0, h)),
            pl.BlockSpec((1, HEAD_DIM), const),
            pl.BlockSpec((1, HEAD_DIM), const),
            pl.BlockSpec((t, HEAD_DIM), const),
            pl.BlockSpec((t, HEAD_DIM), const),
        ],
        out_specs=pl.BlockSpec((t, GQA_GROUP * HEAD_DIM), lambda b, h: (b, h)),
        scratch_shapes=[pltpu.VMEM((PAST_LEN + t, HEAD_DIM), BF16), pltpu.VMEM((PAST_LEN + t, HEAD_DIM), BF16)],
        compiler_params=_params("arbitrary", "arbitrary"),
        name="attn_sample",
    )(z, z, z, k_ctx, v_ctx, q_g.reshape(1, HEAD_DIM), k_g.reshape(1, HEAD_DIM), cosf, sinf)


def _hgrn_chain(rev, r0, gq_ref, gi_ref, k_scr, cum_scr, st_scr, o_scr):
    c = HG_CHUNK
    d = 1 if rev else 0
    rows = pl.ds(r0, c)
    qc, vc = gq_ref[rows, :], gi_ref[rows, :]
    kc, cc = k_scr[d, rows, :], cum_scr[d, rows, :]
    st = st_scr[d]
    last = cc[0:1] if rev else cc[c - 1:c]
    qs = (qc * jnp.exp(cc)).astype(BF16)
    o_inter = lax.dot_general(qs, st.astype(BF16), NT_DIMS, preferred_element_type=F32)
    ks = (kc * jnp.exp(last - cc)).astype(BF16)
    upd = lax.dot_general(vc.astype(BF16), ks, TN_DIMS, preferred_element_type=F32)
    st_scr[d] = st * jnp.exp(last) + upd
    n_grp = c // HG_GROUP
    row = lax.broadcasted_iota(jnp.int32, (HG_GROUP, 1), 0)
    krow = lax.broadcasted_iota(jnp.int32, (c, 1), 0)
    diag, att = [], []
    for i in range(n_grp):
        lo = i * HG_GROUP
        grp = slice(lo, lo + HG_GROUP)
        q_i, k_i, v_i, c_i = qc[grp], kc[grp], vc[grp], cc[grp]
        acc = jnp.zeros((HG_GROUP, HG_DV), F32)
        for delta in range(HG_GROUP):
            if delta == 0:
                k_sh, c_sh, v_sh = k_i, c_i, v_i
            else:
                sh = (HG_GROUP - delta) if rev else delta
                k_sh, c_sh, v_sh = (pltpu.roll(a, sh, axis=0) for a in (k_i, c_i, v_i))
            a = jnp.sum(q_i * k_sh * jnp.exp(jnp.minimum(c_i - c_sh, 0.0)), axis=-1, keepdims=True)
            valid = (row <= HG_GROUP - 1 - delta) if rev else (row >= delta)
            acc = acc + jnp.where(valid, a, 0.0) * v_sh
        diag.append(acc)
        if (i == n_grp - 1) if rev else (i == 0):
            att.append(jnp.zeros((HG_GROUP, c), F32))
            continue
        if rev:
            b, other = cc[lo + HG_GROUP:lo + HG_GROUP + 1], krow >= lo + HG_GROUP
        else:
            b, other = cc[lo - 1:lo], krow < lo
        qs_i = (q_i * jnp.exp(c_i - b)).astype(BF16)
        ks_i = jnp.where(other, kc * jnp.exp(jnp.minimum(b - cc, 0.0)), 0.0).astype(BF16)
        att.append(lax.dot_general(qs_i, ks_i, NT_DIMS, preferred_element_type=F32))
    o_off = _dot(jnp.concatenate(att, axis=0).astype(BF16), vc.astype(BF16))
    o_scr[d, rows, :] = o_inter + jnp.concatenate(diag, axis=0) + o_off


def _hgrn_kernel(*refs, layer, t, has_ctx):
    gq_ref, gff_ref, gfb_ref, gi_ref, go_ref, lb_ref, on_ref, bdf_ref, bdb_ref = refs[:9]
    n_in = 9
    s0_ref = None
    if has_ctx:
        s0_ref = refs[9]
        n_in = 10
    o_ref, sout_ref = refs[n_in], refs[n_in + 1]
    k_scr, cum_scr, st_scr, o_scr = refs[n_in + 2:]

    raw = lb_ref[...]
    e = jnp.exp(raw - jnp.max(raw, axis=0, keepdims=True))
    soft = e / jnp.sum(e, axis=0, keepdims=True)
    csum = soft[0]
    for i in range(1, layer + 1):
        csum = csum + soft[i]
    lb = csum - soft[0]

    for d, (g_ref, bd_ref) in enumerate(((gff_ref, bdf_ref), (gfb_ref, bdb_ref))):
        lbd = lb[d:d + 1]
        f = jnp.maximum(lbd + (1.0 - lbd) * jax.nn.sigmoid(g_ref[...]), F_FLOOR)
        k_scr[d] = 1.0 - f
        lf = jnp.log(f)
        bd = bd_ref[...]
        for p in range(t // HG_CUM_BLOCK):
            rows = slice(p * HG_CUM_BLOCK, (p + 1) * HG_CUM_BLOCK)
            h, m, l = _split3(lf[rows])
            cum_scr[d, rows, :] = _dot(bd, h) + _dot(bd, m) + _dot(bd, l)
        if has_ctx:
            st_scr[d] = s0_ref[0, d, 0].T
        else:
            st_scr[d] = jnp.zeros((HG_DV, HG_DK), F32)

    n = t // HG_CHUNK

    def body(i, carry):
        _hgrn_chain(False, pl.multiple_of(i * HG_CHUNK, HG_CHUNK), gq_ref, gi_ref, k_scr, cum_scr, st_scr, o_scr)
        _hgrn_chain(True, pl.multiple_of((n - 1 - i) * HG_CHUNK, HG_CHUNK), gq_ref, gi_ref, k_scr, cum_scr,
                    st_scr, o_scr)
        return carry

    lax.fori_loop(0, n, body, 0, unroll=min(n, HG_UNROLL))

    o = _rms(o_scr[0] + o_scr[1], on_ref[...])
    go = go_ref[...]
    o_ref[...] = (o * (go * jax.nn.sigmoid(go))).astype(MIX_DTYPE)
    for d in range(2):
        sout_ref[0, d, 0] = st_scr[d].T


def _cum_consts():
    r = np.arange(HG_CUM_BLOCK)
    same = (r[:, None] // HG_CHUNK) == (r[None, :] // HG_CHUNK)
    fwd = same & (r[None, :] <= r[:, None])
    bwd = same & (r[None, :] >= r[:, None])
    return jnp.asarray(fwd, BF16), jnp.asarray(bwd, BF16)


def _hgrn(z, hg_lb, onorm_g, layer, t, nb, row0, s0):
    rb = row0 // t
    bdf, bdb = _cum_consts()
    has_ctx = s0 is not None
    col = lambda c0: (lambda b, h: (rb + b, c0 + h))
    in_specs = [
        pl.BlockSpec((t, HG_DK), col(COL_GQ)),
        pl.BlockSpec((t, HG_DK), col(COL_GFF)),
        pl.BlockSpec((t, HG_DK), col(COL_GFB)),
        pl.BlockSpec((t, HG_DV), col(COL_GI)),
        pl.BlockSpec((t, HG_DV), col(COL_GO)),
        pl.BlockSpec((DEPTH, 2, HG_DK), lambda b, h: (0, 0, h)),
        pl.BlockSpec((1, HG_DV), lambda b, h: (0, 0)),
        pl.BlockSpec((HG_CUM_BLOCK, HG_CUM_BLOCK), lambda b, h: (0, 0)),
        pl.BlockSpec((HG_CUM_BLOCK, HG_CUM_BLOCK), lambda b, h: (0, 0)),
    ]
    args = [z, z, z, z, z, hg_lb, onorm_g.reshape(1, HG_DV), bdf, bdb]
    if has_ctx:
        in_specs.append(pl.BlockSpec((1, 2, 1, HG_DK, HG_DV), lambda b, h: (b, 0, h, 0, 0)))
        args.append(s0)
    return pl.pallas_call(
        functools.partial(_hgrn_kernel, layer=layer, t=t, has_ctx=has_ctx),
        out_shape=(jax.ShapeDtypeStruct((nb * t, GROUP_W), MIX_DTYPE),
                   jax.ShapeDtypeStruct((nb, 2, HG_HEADS, HG_DK, HG_DV), F32)),
        grid=(nb, HG_HEADS),
        in_specs=in_specs,
        out_specs=(pl.BlockSpec((t, HG_DV), lambda b, h: (b, h)),
                   pl.BlockSpec((1, 2, 1, HG_DK, HG_DV), lambda b, h: (b, 0, h, 0, 0))),
        scratch_shapes=[pltpu.VMEM((2, t, HG_DK), F32), pltpu.VMEM((2, t, HG_DK), F32),
                        pltpu.VMEM((2, HG_DV, HG_DK), F32), pltpu.VMEM((2, t, HG_DV), F32)],
        compiler_params=_params("arbitrary", "arbitrary"),
        name=f"hgrn_t{t}",
    )(*args)


def _conv_kernel(cb_ref, cc_ref, cx_ref, w_ref, b_ref, o_ref, *, t):
    p = cc_ref[...] * cx_ref[...]
    row = lax.broadcasted_iota(jnp.int32, (t, 1), 0)
    prev = jnp.where(row >= 1, pltpu.roll(p, 1, axis=0), 0.0)
    nxt = jnp.where(row <= t - 2, pltpu.roll(p, t - 1, axis=0), 0.0)
    w = w_ref[...]
    y = prev * w[0:1] + p * w[1:2] + nxt * w[2:3] + b_ref[...]
    o_ref[...] = (cb_ref[...] * y).astype(MIX_DTYPE)


def _conv(z, w, b, t, nb, row0):
    rb = row0 // t
    return pl.pallas_call(
        functools.partial(_conv_kernel, t=t),
        out_shape=jax.ShapeDtypeStruct((nb * t, GROUP_W), MIX_DTYPE),
        grid=(nb,),
        in_specs=[
            pl.BlockSpec((t, GROUP_W), lambda i: (rb + i, COL_CB // 4)),
            pl.BlockSpec((t, GROUP_W), lambda i: (rb + i, COL_CC // 4)),
            pl.BlockSpec((t, GROUP_W), lambda i: (rb + i, COL_CX // 4)),
            pl.BlockSpec((CONV_K, GROUP_W), lambda i: (0, 0)),
            pl.BlockSpec((1, GROUP_W), lambda i: (0, 0)),
        ],
        out_specs=pl.BlockSpec((t, GROUP_W), lambda i: (i, 0)),
        compiler_params=_params("arbitrary"),
        name=f"conv_t{t}",
    )(z, z, z, w, b.reshape(1, GROUP_W))


def _outproj_kernel(*refs, tm):
    prompt_refs, sample_refs = refs[0:4], refs[4:8]
    w_ref, xp_ref, xs_ref, g1_ref, sh_ref, sc_ref, n2_ref, x1_ref, h2_ref = refs[8:]
    i = pl.program_id(0)
    r = _mod_row(i * tm)

    def run(group_refs, x_ref):
        acc = None
        for g, ref in enumerate(group_refs):
            part = _dot(ref[...], w_ref[g * GROUP_W:(g + 1) * GROUP_W, :])
            acc = part if acc is None else acc + part
        x1 = x_ref[...] + g1_ref[pl.ds(r, 1), :] * acc
        x1_ref[...] = x1
        h2 = _rms(x1, n2_ref[...]) * (1.0 + sc_ref[pl.ds(r, 1), :]) + sh_ref[pl.ds(r, 1), :]
        h2_ref[...] = h2.astype(BF16)

    pl.when(i < N_PROMPT // tm)(lambda: run(prompt_refs, xp_ref))
    pl.when(i >= N_PROMPT // tm)(lambda: run(sample_refs, xs_ref))


def _outproj(prompt_groups, sample_groups, w_out_bf, layer, x_prompt, x_sample, mod, norm2_g):
    tm = 512
    n_pb, n_sb = N_PROMPT // tm, N_SAMPLE // tm
    p_spec = pl.BlockSpec((tm, GROUP_W), lambda i: (jnp.minimum(i, n_pb - 1), 0))
    s_spec = pl.BlockSpec((tm, GROUP_W), lambda i: (jnp.clip(i - n_pb, 0, n_sb - 1), 0))
    modspec = lambda k: pl.BlockSpec((MOD_ROWS, D_MODEL), lambda i: (0, k))
    return pl.pallas_call(
        functools.partial(_outproj_kernel, tm=tm),
        out_shape=(jax.ShapeDtypeStruct((N_TOK, D_MODEL), F32), jax.ShapeDtypeStruct((N_TOK, D_MODEL), BF16)),
        grid=(N_TOK // tm,),
        in_specs=[p_spec] * 4 + [s_spec] * 4 + [
            pl.BlockSpec((None, D_MODEL, D_MODEL), lambda i: (layer, 0, 0), pipeline_mode=pl.Buffered(1)),
            *_token_source_specs(tm, x_prompt, x_sample, 1),
            modspec(2), modspec(3), modspec(4),
            pl.BlockSpec((1, D_MODEL), lambda i: (0, 0))],
        out_specs=(pl.BlockSpec((tm, D_MODEL), lambda i: (i, 0)), pl.BlockSpec((tm, D_MODEL), lambda i: (i, 0))),
        compiler_params=_params("arbitrary"),
        name="outproj",
    )(*prompt_groups, *sample_groups, w_out_bf, x_prompt, x_sample, mod, mod, mod, norm2_g.reshape(1, D_MODEL))


def _peer_query_kernel(h_ref, wq_ref, q_ref):
    q_ref[...] = _dot(h_ref[...], wq_ref[...]).astype(BF16)


def _peer_query(h2, wq_bf, layer):
    tm = 512
    width = PEER_HEADS * PEER_QDIM
    return pl.pallas_call(
        _peer_query_kernel,
        out_shape=jax.ShapeDtypeStruct((N_TOK, width), BF16),
        grid=(N_TOK // tm,),
        in_specs=[pl.BlockSpec((tm, D_MODEL), lambda i: (i, 0)),
                  pl.BlockSpec((None, D_MODEL, width), lambda i: (layer, 0, 0))],
        out_specs=pl.BlockSpec((tm, width), lambda i: (i, 0)),
        compiler_params=_params("arbitrary"),
        name="peer_query",
    )(h2, wq_bf)


_CAND_RUNS = ((2, 5), (3, 4), (4, 3), (5, 2), (6, 2), (7, 2))
_CAND_RUN_ROWS = 24
_CAND_ROWS = 4 * 8 + _CAND_RUN_ROWS
_NO_CAND = 1 << 20


def _candidates(v1, v2, tm):
    sub = lax.broadcasted_iota(jnp.int32, (8, tm), 0)
    vals = [v1[0:1] + v2[0:8], v1[0:1] + v2[8:16], v1[1:2] + v2[0:8], v1[8:16] + v2[0:1]]
    ids = [sub, sub + 8, sub + PEER_TOPK, (sub + 8) * PEER_TOPK]
    row = lax.broadcasted_iota(jnp.int32, (_CAND_RUN_ROWS, tm), 0)
    v2_rep = jnp.concatenate([v2[0:8]] * (_CAND_RUN_ROWS // 8), axis=0)
    run_vals = jnp.full((_CAND_RUN_ROWS, tm), -jnp.inf, F32)
    run_ids = jnp.full((_CAND_RUN_ROWS, tm), _NO_CAND, jnp.int32)
    start = 0
    for r1, count in _CAND_RUNS:
        inside = (row >= start) & (row < start + count)
        shifted = v2_rep if start % 8 == 0 else pltpu.roll(v2_rep, start % 8, axis=0)
        run_vals = jnp.where(inside, v1[r1:r1 + 1] + shifted, run_vals)
        run_ids = jnp.where(inside, row + (r1 * PEER_TOPK - start), run_ids)
        start += count
    return jnp.concatenate(vals + [run_vals], axis=0), jnp.concatenate(ids + [run_ids], axis=0)


def _topk_rows(x_scr, ids, n_rows, v_out, i_out):
    def body(j, carry):
        x = x_scr[0:n_rows, :]
        m = jnp.max(x, axis=0, keepdims=True)
        idx = jnp.min(jnp.where(x == m, ids, _NO_CAND), axis=0, keepdims=True)
        v_out[pl.ds(j, 1), :] = m
        i_out[pl.ds(j, 1), :] = idx
        x_scr[0:n_rows, :] = jnp.where(ids == idx, -jnp.inf, x)
        return carry

    lax.fori_loop(0, PEER_TOPK, body, 0, unroll=2)


def _peer_topk_kernel(q_ref, k1_ref, k2_ref, u_ref, v_ref, e_ref, g_ref, ub_ref, vb_ref, x_scr, v1_scr, i1_scr,
                      v2_scr, i2_scr, ts_scr, tc_scr, *, tm):
    ub_ref[...] = u_ref[...].astype(BF16)
    vb_ref[...] = v_ref[...].astype(BF16)
    key_ids = lax.broadcasted_iota(jnp.int32, (N_KEYS, tm), 0)
    for half, (k_ref, v_scr, i_scr) in enumerate(((k1_ref, v1_scr, i1_scr), (k2_ref, v2_scr, i2_scr))):
        qh = q_ref[:, half * PEER_HALF:(half + 1) * PEER_HALF]
        x_scr[...] = lax.dot_general(k_ref[...], qh, NT_DIMS, preferred_element_type=F32)
        _topk_rows(x_scr, key_ids, N_KEYS, v_scr, i_scr)
    cand_vals, cand_ids = _candidates(v1_scr[...], v2_scr[...], tm)
    x_scr[0:_CAND_ROWS, :] = cand_vals
    _topk_rows(x_scr, cand_ids, _CAND_ROWS, ts_scr, tc_scr)
    tc = tc_scr[...]
    c1, c2 = tc // PEER_TOPK, tc % PEER_TOPK
    i1, i2 = i1_scr[...], i2_scr[...]
    e1 = jnp.zeros_like(tc)
    e2 = jnp.zeros_like(tc)
    for a in range(PEER_TOPK):
        e1 = jnp.where(c1 == a, i1[a:a + 1], e1)
        e2 = jnp.where(c2 == a, i2[a:a + 1], e2)
    e_ref[...] = e1 * N_KEYS + e2
    ts = ts_scr[...]
    p = jnp.exp(ts - jnp.max(ts, axis=0, keepdims=True))
    g_ref[...] = p / jnp.sum(p, axis=0, keepdims=True)


def _peer_topk(q, k1_bf, k2_bf, peer_u, peer_v, layer):
    tm = 512
    slab = LANES
    n_slabs = N_EXPERTS // slab
    assert (N_TOK // tm) * PEER_HEADS >= n_slabs
    row_f = pltpu.VMEM((PEER_TOPK, tm), F32)
    row_i = pltpu.VMEM((PEER_TOPK, tm), jnp.int32)
    slab_of = lambda i, h: jnp.minimum(i * PEER_HEADS + h, n_slabs - 1)
    table_in = pl.BlockSpec((None, slab, D_MODEL), lambda i, h: (layer, slab_of(i, h), 0))
    table_out = pl.BlockSpec((slab, D_MODEL), lambda i, h: (slab_of(i, h), 0))
    table_shape = jax.ShapeDtypeStruct((N_EXPERTS, D_MODEL), BF16)
    return pl.pallas_call(
        functools.partial(_peer_topk_kernel, tm=tm),
        out_shape=(jax.ShapeDtypeStruct((PEER_HEADS * PEER_TOPK, N_TOK), jnp.int32),
                   jax.ShapeDtypeStruct((PEER_HEADS * PEER_TOPK, N_TOK), F32), table_shape, table_shape),
        grid=(N_TOK // tm, PEER_HEADS),
        in_specs=[
            pl.BlockSpec((tm, PEER_QDIM), lambda i, h: (i, h)),
            pl.BlockSpec((None, N_KEYS, PEER_HALF), lambda i, h: (layer, 0, 0)),
            pl.BlockSpec((None, N_KEYS, PEER_HALF), lambda i, h: (layer, 0, 0)),
            table_in, table_in,
        ],
        out_specs=(pl.BlockSpec((PEER_TOPK, tm), lambda i, h: (h, i)),
                   pl.BlockSpec((PEER_TOPK, tm), lambda i, h: (h, i)), table_out, table_out),
        scratch_shapes=[pltpu.VMEM((N_KEYS, tm), F32), row_f, row_i, row_f, row_i, row_f, row_i],
        compiler_params=_params("arbitrary", "arbitrary"),
        name="peer_topk",
    )(q, k1_bf, k2_bf, peer_u, peer_v)


def _peer_mask_kernel(e_ref, g_ref, m_ref, et_scr, gt_scr, slab_scr, *, tb):
    et_scr[...] = e_ref[...].T
    gt_scr[...] = g_ref[...].T
    key = lax.broadcasted_iota(jnp.int32, (N_KEYS, PEER_HEADS * PEER_TOPK), 0)

    def body(t, carry):
        er = et_scr[pl.ds(t, 1), :]
        gr = gt_scr[pl.ds(t, 1), :]
        p1 = jnp.where(key == er // N_KEYS, gr, 0.0).astype(BF16)
        p2 = jnp.where(key == er % N_KEYS, 1.0, 0.0).astype(BF16)
        mt = lax.dot_general(p1, p2, NT_DIMS, preferred_element_type=F32)
        slab_scr[pl.ds(pl.multiple_of(t * MASK_PITCH, 4), N_KEYS), :] = mt
        return carry

    lax.fori_loop(0, tb, body, 0, unroll=64)
    for k1 in range(N_KEYS):
        m_ref[:, k1 * N_KEYS:(k1 + 1) * N_KEYS] = slab_scr[pl.ds(k1, tb, stride=MASK_PITCH), :].astype(BF16)


def _peer_mask(experts_t, gates_t):
    tb = MASK_TOKENS
    pairs = PEER_HEADS * PEER_TOPK
    return pl.pallas_call(
        functools.partial(_peer_mask_kernel, tb=tb),
        out_shape=jax.ShapeDtypeStruct((N_TOK, N_EXPERTS), BF16),
        grid=(N_TOK // tb,),
        in_specs=[pl.BlockSpec((pairs, tb), lambda i: (0, i)), pl.BlockSpec((pairs, tb), lambda i: (0, i))],
        out_specs=pl.BlockSpec((tb, N_EXPERTS), lambda i: (i, 0)),
        scratch_shapes=[pltpu.VMEM((tb, pairs), jnp.int32), pltpu.VMEM((tb, pairs), F32),
                        pltpu.VMEM((tb * MASK_PITCH, N_KEYS), F32)],
        compiler_params=_params("arbitrary"),
        name="peer_mask",
    )(experts_t, gates_t)


def _peer_expert_kernel(h_ref, u_ref, v_ref, m_ref, x_ref, g2_ref, o_ref, *, tm):
    j = pl.program_id(1)

    @pl.when(j == 0)
    def _():
        o_ref[...] = jnp.zeros_like(o_ref)

    s = lax.dot_general(h_ref[...], u_ref[...], NT_DIMS, preferred_element_type=F32)
    a = (m_ref[...].astype(F32) * jax.nn.gelu(s)).astype(BF16)
    o_ref[...] += _dot(a, v_ref[...])

    @pl.when(j == pl.num_programs(1) - 1)
    def _():
        r = _mod_row(pl.program_id(0) * tm)
        o_ref[...] = x_ref[...] + g2_ref[pl.ds(r, 1), :] * o_ref[...]


EXPERT_TM, EXPERT_TE = 1024, 512


def _peer_expert(h2, u_bf, v_bf, mask, x1, mod):
    tm, te = EXPERT_TM, EXPERT_TE
    return pl.pallas_call(
        functools.partial(_peer_expert_kernel, tm=tm),
        out_shape=jax.ShapeDtypeStruct((N_TOK, D_MODEL), F32),
        grid=(N_TOK // tm, N_EXPERTS // te),
        in_specs=[
            pl.BlockSpec((tm, D_MODEL), lambda i, j: (i, 0)),
            pl.BlockSpec((te, D_MODEL), lambda i, j: (j, 0)),
            pl.BlockSpec((te, D_MODEL), lambda i, j: (j, 0)),
            pl.BlockSpec((tm, te), lambda i, j: (i, j)),
            pl.BlockSpec((tm, D_MODEL), lambda i, j: (i, 0)),
            pl.BlockSpec((MOD_ROWS, D_MODEL), lambda i, j: (0, 5)),
        ],
        out_specs=pl.BlockSpec((tm, D_MODEL), lambda i, j: (i, 0)),
        compiler_params=_params("arbitrary", "arbitrary"),
        name="peer_expert",
    )(h2, u_bf, v_bf, mask, x1, mod)


def kernel(x_prompt, x_sample, cache_k, cache_v, state_hgrn, c, c_ctx, ada_w, ada_b, norm1_g, norm2_g, w_in,
           q_norm_g, k_norm_g, hg_lb, hg_onorm_g, conv_w, conv_b, w_out, peer_wq, peer_k1, peer_k2, peer_u, peer_v):
    x_p, x_s = x_prompt.reshape(N_PROMPT, D_MODEL), x_sample.reshape(N_SAMPLE, D_MODEL)
    cvec = jnp.concatenate([c_ctx[None, :], c, jnp.zeros((MOD_ROWS - 1 - DEC_BATCH, D_MODEL), F32)], axis=0)
    mod_all = _ada(cvec, ada_w, ada_b)

    w_in_bf, w_out_bf, wq_bf = w_in.astype(BF16), w_out.astype(BF16), peer_wq.astype(BF16)
    k1_bf, k2_bf = peer_k1.astype(BF16), peer_k2.astype(BF16)

    ks_out, vs_out, ss_out = [], [], []
    for l in range(DEPTH):
        mod = mod_all[l]
        z = _inproj(x_p, x_s, mod, norm1_g[l], w_in_bf, l)

        fn_p = _fourier(z, SEQ, BATCH, 0)
        fn_s = _fourier(z, DEC_SEQ, DEC_BATCH, N_PROMPT)

        at_p, k_l, v_l = _attn_prompt(z, q_norm_g[l], k_norm_g[l])
        k_ctx = cache_k[:, l].reshape(DEC_BATCH, PAST_LEN, N_KV * HEAD_DIM)
        v_ctx = cache_v[:, l].reshape(DEC_BATCH, PAST_LEN, N_KV * HEAD_DIM)
        at_s = _attn_sample(z, q_norm_g[l], k_norm_g[l], k_ctx, v_ctx)

        hg_p, s_l = _hgrn(z, hg_lb, hg_onorm_g[l], l, SEQ, BATCH, 0, None)
        hg_s, _ = _hgrn(z, hg_lb, hg_onorm_g[l], l, DEC_SEQ, DEC_BATCH, N_PROMPT, state_hgrn[:, l])

        cv_p = _conv(z, conv_w[l], conv_b[l], SEQ, BATCH, 0)
        cv_s = _conv(z, conv_w[l], conv_b[l], DEC_SEQ, DEC_BATCH, N_PROMPT)

        x1, h2 = _outproj((fn_p, at_p, hg_p, cv_p), (fn_s, at_s, hg_s, cv_s), w_out_bf, l, x_p, x_s, mod,
                          norm2_g[l])

        q_peer = _peer_query(h2, wq_bf, l)
        experts_t, gates_t, u_bf, v_bf = _peer_topk(q_peer, k1_bf, k2_bf, peer_u, peer_v, l)
        mask = _peer_mask(experts_t, gates_t)
        x = _peer_expert(h2, u_bf, v_bf, mask, x1, mod)
        x_p = x_s = x

        ks_out.append(k_l.reshape(BATCH, SEQ, N_KV, HEAD_DIM))
        vs_out.append(v_l.reshape(BATCH, SEQ, N_KV, HEAD_DIM))
        ss_out.append(s_l)

    y_prompt = x[:N_PROMPT].reshape(BATCH, SEQ, D_MODEL)
    y_sample = x[N_PROMPT:].reshape(DEC_BATCH, DEC_SEQ, D_MODEL)
    return (y_prompt, y_sample, jnp.stack(ks_out, axis=1), jnp.stack(vs_out, axis=1), jnp.stack(ss_out, axis=1))
```

```python
import functools

import numpy as np
import jax
import jax.numpy as jnp
from jax import lax
from jax.experimental import pallas as pl
from jax.experimental.pallas import tpu as pltpu

F32 = jnp.float32
BF16 = jnp.bfloat16
MIX_DTYPE = BF16

D_MODEL = 2048
BATCH = 32
SEQ = 256
DEPTH = 2
DEC_BATCH = 4
DEC_SEQ = 1024
PAST_LEN = 256
GRID_W = 64
GROUP_W = D_MODEL // 4
FN_GROUPS = 4
FN_GW = GROUP_W // FN_GROUPS
HEAD_DIM = 128
N_HEADS = GROUP_W // HEAD_DIM
N_KV = 2
GQA_GROUP = N_HEADS // N_KV
ROPE_THETA = 10000.0
HG_DK = 128
HG_DV = 128
HG_HEADS = GROUP_W // HG_DK
CONV_K = 3
IN_COLS = 11 * GROUP_W
PEER_HEADS = 8
N_KEYS = 128
N_EXPERTS = N_KEYS * N_KEYS
PEER_QDIM = 256
PEER_HALF = PEER_QDIM // 2
PEER_TOPK = 16
EPS = 1e-6
F_FLOOR = 1e-30

N_PROMPT = BATCH * SEQ
N_SAMPLE = DEC_BATCH * DEC_SEQ
N_TOK = N_PROMPT + N_SAMPLE
MOD_ROWS = 8
LANES = 128
NORM_ROWS = 128
HG_CHUNK = 32
HG_GROUP = 8
HG_UNROLL = 16
HG_CUM_BLOCK = 256
MASK_TOKENS = 128
MASK_PITCH = 132

ADA_TN = 1024
INPROJ_TM, INPROJ_TN = 1024, 512
OUTPROJ_TM = 512
QUERY_TM = 512
TOPK_TM = 512
EXPERT_TM, EXPERT_TE = 1024, 512
ATTN_Q_ROWS = 256

VMEM_SMALL = 32 * 1024 * 1024
VMEM_INPROJ = 48 * 1024 * 1024
VMEM_LARGE = 56 * 1024 * 1024

COL_FN, COL_Q, COL_K, COL_V = 0, 4, 8, 10
COL_GQ, COL_GFF, COL_GFB, COL_GI, COL_GO = 12, 16, 20, 24, 28
COL_CB, COL_CC, COL_CX = 32, 36, 40

NT_DIMS = (((1,), (1,)), ((), ()))
TN_DIMS = (((0,), (0,)), ((), ()))


def _params(*sem, vmem=VMEM_SMALL):
    return pltpu.CompilerParams(dimension_semantics=sem, vmem_limit_bytes=vmem)


def _dot(a, b):
    return jnp.dot(a, b, preferred_element_type=F32)


def _split3(x):
    h = x.astype(BF16)
    r = x - h.astype(F32)
    m = r.astype(BF16)
    l = (r - m.astype(F32)).astype(BF16)
    return h, m, l


def _split2(x):
    h = x.astype(BF16)
    return h, (x - h.astype(F32)).astype(BF16)


def _mod_row(tok0):
    return jnp.where(tok0 >= N_PROMPT, (tok0 - N_PROMPT) // DEC_SEQ + 1, 0)


def _rms(x, g):
    return x * lax.rsqrt(jnp.mean(x * x, axis=-1, keepdims=True) + EPS) * g


def _ada_kernel(c_ref, w_ref, b_ref, o_ref):
    c = c_ref[...]
    s = (c * jax.nn.sigmoid(c)).astype(BF16)
    o_ref[0] = _dot(s, w_ref[0].astype(BF16)) + b_ref[0]


def _ada(cvec, ada_w, ada_b):
    tn = ADA_TN
    return pl.pallas_call(
        _ada_kernel,
        out_shape=jax.ShapeDtypeStruct((DEPTH, MOD_ROWS, 6 * D_MODEL), F32),
        grid=(DEPTH, 6 * D_MODEL // tn),
        in_specs=[
            pl.BlockSpec((MOD_ROWS, D_MODEL), lambda l, j: (0, 0)),
            pl.BlockSpec((1, D_MODEL, tn), lambda l, j: (l, 0, j)),
            pl.BlockSpec((1, 1, tn), lambda l, j: (l, 0, j)),
        ],
        out_specs=pl.BlockSpec((1, MOD_ROWS, tn), lambda l, j: (l, 0, j)),
        compiler_params=_params("arbitrary", "arbitrary"),
        name="ada",
    )(cvec, ada_w, ada_b.reshape(DEPTH, 1, 6 * D_MODEL))


def _token_source_specs(tm, x_prompt, x_sample, n_grid_axes):
    n_pb, n_sb = N_PROMPT // tm, N_SAMPLE // tm
    s_off = x_sample.shape[0] // tm - n_sb
    if n_grid_axes == 1:
        return (pl.BlockSpec((tm, D_MODEL), lambda i: (jnp.minimum(i, n_pb - 1), 0)),
                pl.BlockSpec((tm, D_MODEL), lambda i: (s_off + jnp.clip(i - n_pb, 0, n_sb - 1), 0)))
    return (pl.BlockSpec((tm, D_MODEL), lambda i, j: (jnp.minimum(i, n_pb - 1), 0)),
            pl.BlockSpec((tm, D_MODEL), lambda i, j: (s_off + jnp.clip(i - n_pb, 0, n_sb - 1), 0)))


def _inproj_kernel(xp_ref, xs_ref, sh_ref, sc_ref, g_ref, w_ref, o_ref, h_ref, *, tm):
    i, j = pl.program_id(0), pl.program_id(1)

    def prologue(x_ref):
        r = _mod_row(i * tm)
        gain = g_ref[...] * (1.0 + sc_ref[pl.ds(r, 1), :])
        shift = sh_ref[pl.ds(r, 1), :]

        def norm_rows(c, carry):
            rows = pl.ds(pl.multiple_of(c * NORM_ROWS, NORM_ROWS), NORM_ROWS)
            x = x_ref[rows, :]
            y = x * lax.rsqrt(jnp.mean(x * x, axis=-1, keepdims=True) + EPS)
            h_ref[rows, :] = (y * gain + shift).astype(BF16)
            return carry

        lax.fori_loop(0, tm // NORM_ROWS, norm_rows, 0)

    pl.when((j == 0) & (i < N_PROMPT // tm))(lambda: prologue(xp_ref))
    pl.when((j == 0) & (i >= N_PROMPT // tm))(lambda: prologue(xs_ref))
    o_ref[...] = _dot(h_ref[...], w_ref[...])


def _inproj(x_prompt, x_sample, mod, norm_g, w_in_bf, layer):
    tm, tn = INPROJ_TM, INPROJ_TN
    return pl.pallas_call(
        functools.partial(_inproj_kernel, tm=tm),
        out_shape=jax.ShapeDtypeStruct((N_TOK, IN_COLS), F32),
        grid=(N_TOK // tm, IN_COLS // tn),
        in_specs=[
            *_token_source_specs(tm, x_prompt, x_sample, 2),
            pl.BlockSpec((MOD_ROWS, D_MODEL), lambda i, j: (0, 0)),
            pl.BlockSpec((MOD_ROWS, D_MODEL), lambda i, j: (0, 1)),
            pl.BlockSpec((1, D_MODEL), lambda i, j: (0, 0)),
            pl.BlockSpec((None, D_MODEL, tn), lambda i, j: (layer, 0, j)),
        ],
        out_specs=pl.BlockSpec((tm, tn), lambda i, j: (i, j)),
        scratch_shapes=[pltpu.VMEM((tm, D_MODEL), BF16)],
        compiler_params=_params("arbitrary", "arbitrary", vmem=VMEM_INPROJ),
        name="inproj",
    )(x_prompt, x_sample, mod, mod, norm_g.reshape(1, D_MODEL), w_in_bf)


def _fourier_kernel(u_ref, cch_ref, ccl_ref, cth_ref, ctl_ref, o_ref):
    cch, ccl, cth, ctl = cch_ref[...], ccl_ref[...], cth_ref[...], ctl_ref[...]
    for g in range(0, FN_GROUPS, 2):
        ws = []
        for gg in (g, g + 1):
            uh, ul = _split2(u_ref[:, gg * FN_GW:(gg + 1) * FN_GW])
            ws.append(_dot(uh, cch) + (_dot(ul, cch) + _dot(uh, ccl)))
        st = jnp.concatenate([jnp.concatenate([w[:, :FN_GW] for w in ws], axis=1),
                              jnp.concatenate([w[:, FN_GW:] for w in ws], axis=1)], axis=0)
        sh, sl = _split2(st)
        o_ref[:, g * FN_GW:(g + 2) * FN_GW] = (_dot(cth, sh) + (_dot(cth, sl) + _dot(ctl, sh))).astype(MIX_DTYPE)


def _dft_consts(t):
    def cs(n):
        k = np.arange(n, dtype=np.float64)
        ang = 2.0 * np.pi * np.outer(k, k) / n
        return np.cos(ang) / np.sqrt(n), np.sin(ang) / np.sqrt(n)

    cc, sc = cs(FN_GW)
    ct, st = cs(t)
    right = np.concatenate([cc, sc], axis=1).astype(np.float32)
    left = np.concatenate([ct, -st], axis=1).astype(np.float32)

    def hl(a):
        a = jnp.asarray(a)
        h = a.astype(BF16)
        return h, (a - h.astype(F32)).astype(BF16)

    return hl(right) + hl(left)


def _fourier(z, t, nb, row0):
    cch, ccl, cth, ctl = _dft_consts(t)
    rb = row0 // t
    return pl.pallas_call(
        _fourier_kernel,
        out_shape=jax.ShapeDtypeStruct((nb * t, GROUP_W), MIX_DTYPE),
        grid=(nb,),
        in_specs=[
            pl.BlockSpec((t, GROUP_W), lambda b: (rb + b, COL_FN // 4)),
            pl.BlockSpec((FN_GW, 2 * FN_GW), lambda b: (0, 0)),
            pl.BlockSpec((FN_GW, 2 * FN_GW), lambda b: (0, 0)),
            pl.BlockSpec((t, 2 * t), lambda b: (0, 0)),
            pl.BlockSpec((t, 2 * t), lambda b: (0, 0)),
        ],
        out_specs=pl.BlockSpec((t, GROUP_W), lambda b: (b, 0)),
        compiler_params=_params("arbitrary"),
        name=f"fourier_t{t}",
    )(z, cch, ccl, cth, ctl)


def _softmax_pv(q_bf, k_bf, v_bf):
    s = lax.dot_general(q_bf, k_bf, NT_DIMS, preferred_element_type=F32) * (HEAD_DIM ** -0.5)
    p = jnp.exp(s - jnp.max(s, axis=-1, keepdims=True))
    return _dot(p.astype(BF16), v_bf) / jnp.sum(p, axis=-1, keepdims=True)


def _attn_prompt_kernel(q_ref, k_ref, v_ref, qg_ref, kg_ref, o_ref, kc_ref, vc_ref):
    kn = _rms(k_ref[...], kg_ref[...])
    v = v_ref[...]
    head_rows = pl.ds(pl.program_id(1), SEQ, stride=N_KV)
    kc_ref[0, head_rows, :] = kn
    vc_ref[0, head_rows, :] = v
    kb, vb = kn.astype(BF16), v.astype(BF16)
    for g in range(GQA_GROUP):
        qn = _rms(q_ref[:, g * HEAD_DIM:(g + 1) * HEAD_DIM], qg_ref[...])
        o_ref[:, g * HEAD_DIM:(g + 1) * HEAD_DIM] = _softmax_pv(qn.astype(BF16), kb, vb).astype(MIX_DTYPE)


def _attn_prompt(z, q_g, k_g):
    t = SEQ
    kv_shape = jax.ShapeDtypeStruct((BATCH, SEQ * N_KV, HEAD_DIM), F32)
    kv_spec = pl.BlockSpec((1, SEQ * N_KV, HEAD_DIM), lambda b, h: (b, 0, 0))
    return pl.pallas_call(
        _attn_prompt_kernel,
        out_shape=(jax.ShapeDtypeStruct((N_PROMPT, GROUP_W), MIX_DTYPE), kv_shape, kv_shape),
        grid=(BATCH, N_KV),
        in_specs=[
            pl.BlockSpec((t, GQA_GROUP * HEAD_DIM), lambda b, h: (b, COL_Q // 2 + h)),
            pl.BlockSpec((t, HEAD_DIM), lambda b, h: (b, COL_K + h)),
            pl.BlockSpec((t, HEAD_DIM), lambda b, h: (b, COL_V + h)),
            pl.BlockSpec((1, HEAD_DIM), lambda b, h: (0, 0)),
            pl.BlockSpec((1, HEAD_DIM), lambda b, h: (0, 0)),
        ],
        out_specs=(
            pl.BlockSpec((t, GQA_GROUP * HEAD_DIM), lambda b, h: (b, h)),
            kv_spec, kv_spec,
        ),
        compiler_params=_params("arbitrary", "arbitrary"),
        name="attn_prompt",
    )(z, z, z, q_g.reshape(1, HEAD_DIM), k_g.reshape(1, HEAD_DIM))


def _rope(x, cosf, sinf):
    return x * cosf + pltpu.roll(x, HEAD_DIM // 2, axis=1) * sinf


def _attn_sample_kernel(q_ref, k_ref, v_ref, kctx_ref, vctx_ref, qg_ref, kg_ref, cos_ref, sin_ref,
                        o_ref, kall_ref, vall_ref, *, qb):
    cosf, sinf = cos_ref[...], sin_ref[...]
    kall_ref[0:PAST_LEN, :] = kctx_ref[0].astype(BF16)
    vall_ref[0:PAST_LEN, :] = vctx_ref[0].astype(BF16)
    kall_ref[PAST_LEN:, :] = _rope(_rms(k_ref[...], kg_ref[...]), cosf, sinf).astype(BF16)
    vall_ref[PAST_LEN:, :] = v_ref[...].astype(BF16)
    kb, vb = kall_ref[...], vall_ref[...]
    for i in range(DEC_SEQ // qb):
        rows = slice(i * qb, (i + 1) * qb)
        for g in range(GQA_GROUP):
            cols = slice(g * HEAD_DIM, (g + 1) * HEAD_DIM)
            qn = _rope(_rms(q_ref[rows, cols], qg_ref[...]), cosf[rows], sinf[rows])
            o_ref[rows, cols] = _softmax_pv(qn.astype(BF16), kb, vb).astype(MIX_DTYPE)


def _rope_tables():
    rows = DEC_SEQ // GRID_W
    row = np.repeat(np.arange(rows, dtype=np.float32), GRID_W)
    col = np.tile(np.arange(GRID_W, dtype=np.float32), rows)
    n_freq = HEAD_DIM // 4
    inv = (np.float32(ROPE_THETA) ** (-np.arange(n_freq, dtype=np.float32) / n_freq)).astype(np.float32)
    ang = np.concatenate([row[:, None] * inv, col[:, None] * inv], axis=-1).astype(np.float32)
    cos, sin = np.cos(ang.astype(np.float64)), np.sin(ang.astype(np.float64))
    cosf = np.concatenate([cos, cos], axis=-1).astype(np.float32)
    sinf = np.concatenate([-sin, sin], axis=-1).astype(np.float32)
    return jnp.asarray(cosf), jnp.asarray(sinf)


def _attn_sample(z, q_g, k_g, k_ctx, v_ctx):
    t = DEC_SEQ
    rb = N_PROMPT // t
    cosf, sinf = _rope_tables()
    const = lambda b, h: (0, 0)
    return pl.pallas_call(
        functools.partial(_attn_sample_kernel, qb=ATTN_Q_ROWS),
        out_shape=jax.ShapeDtypeStruct((N_SAMPLE, GROUP_W), MIX_DTYPE),
        grid=(DEC_BATCH, N_KV),
        in_specs=[
            pl.BlockSpec((t, GQA_GROUP * HEAD_DIM), lambda b, h: (rb + b, COL_Q // 2 + h)),
            pl.BlockSpec((t, HEAD_DIM), lambda b, h: (rb + b, COL_K + h)),
            pl.BlockSpec((t, HEAD_DIM), lambda b, h: (rb + b, COL_V + h)),
            pl.BlockSpec((1, PAST_LEN, HEAD_DIM), lambda b, h: (b, 0, h)),
            pl.BlockSpec((1, PAST_LEN, HEAD_DIM), lambda b, h: (b, 0, h)),
            pl.BlockSpec((1, HEAD_DIM), const),
            pl.BlockSpec((1, HEAD_DIM), const),
            pl.BlockSpec((t, HEAD_DIM), const),
            pl.BlockSpec((t, HEAD_DIM), const),
        ],
        out_specs=pl.BlockSpec((t, GQA_GROUP * HEAD_DIM), lambda b, h: (b, h)),
        scratch_shapes=[pltpu.VMEM((PAST_LEN + t, HEAD_DIM), BF16), pltpu.VMEM((PAST_LEN + t, HEAD_DIM), BF16)],
        compiler_params=_params("arbitrary", "arbitrary"),
        name="attn_sample",
    )(z, z, z, k_ctx, v_ctx, q_g.reshape(1, HEAD_DIM), k_g.reshape(1, HEAD_DIM), cosf, sinf)


def _hgrn_chain(rev, r0, gq_ref, gi_ref, k_scr, cum_scr, st_scr, o_scr):
    c = HG_CHUNK
    d = 1 if rev else 0
    rows = pl.ds(r0, c)
    qc, vc = gq_ref[rows, :], gi_ref[rows, :]
    kc, cc = k_scr[d, rows, :], cum_scr[d, rows, :]
    st = st_scr[d]
    last = cc[0:1] if rev else cc[c - 1:c]
    qs = (qc * jnp.exp(cc)).astype(BF16)
    o_inter = lax.dot_general(qs, st.astype(BF16), NT_DIMS, preferred_element_type=F32)
    ks = (kc * jnp.exp(last - cc)).astype(BF16)
    upd = lax.dot_general(vc.astype(BF16), ks, TN_DIMS, preferred_element_type=F32)
    st_scr[d] = st * jnp.exp(last) + upd
    n_grp = c // HG_GROUP
    row = lax.broadcasted_iota(jnp.int32, (HG_GROUP, 1), 0)
    krow = lax.broadcasted_iota(jnp.int32, (c, 1), 0)
    diag, att = [], []
    for i in range(n_grp):
        lo = i * HG_GROUP
        grp = slice(lo, lo + HG_GROUP)
        q_i, k_i, v_i, c_i = qc[grp], kc[grp], vc[grp], cc[grp]
        acc = jnp.zeros((HG_GROUP, HG_DV), F32)
        for delta in range(HG_GROUP):
            if delta == 0:
                k_sh, c_sh, v_sh = k_i, c_i, v_i
            else:
                sh = (HG_GROUP - delta) if rev else delta
                k_sh, c_sh, v_sh = (pltpu.roll(a, sh, axis=0) for a in (k_i, c_i, v_i))
            a = jnp.sum(q_i * k_sh * jnp.exp(jnp.minimum(c_i - c_sh, 0.0)), axis=-1, keepdims=True)
            valid = (row <= HG_GROUP - 1 - delta) if rev else (row >= delta)
            acc = acc + jnp.where(valid, a, 0.0) * v_sh
        diag.append(acc)
        if (i == n_grp - 1) if rev else (i == 0):
            att.append(jnp.zeros((HG_GROUP, c), F32))
            continue
        if rev:
            b, other = cc[lo + HG_GROUP:lo + HG_GROUP + 1], krow >= lo + HG_GROUP
        else:
            b, other = cc[lo - 1:lo], krow < lo
        qs_i = (q_i * jnp.exp(c_i - b)).astype(BF16)
        ks_i = jnp.where(other, kc * jnp.exp(jnp.minimum(b - cc, 0.0)), 0.0).astype(BF16)
        att.append(lax.dot_general(qs_i, ks_i, NT_DIMS, preferred_element_type=F32))
    o_off = _dot(jnp.concatenate(att, axis=0).astype(BF16), vc.astype(BF16))
    o_scr[d, rows, :] = o_inter + jnp.concatenate(diag, axis=0) + o_off


def _hgrn_kernel(*refs, layer, t, has_ctx):
    gq_ref, gff_ref, gfb_ref, gi_ref, go_ref, lb_ref, on_ref, bdf_ref, bdb_ref = refs[:9]
    n_in = 9
    s0_ref = None
    if has_ctx:
        s0_ref = refs[9]
        n_in = 10
    o_ref, sout_ref = refs[n_in], refs[n_in + 1]
    k_scr, cum_scr, st_scr, o_scr = refs[n_in + 2:]

    raw = lb_ref[...]
    e = jnp.exp(raw - jnp.max(raw, axis=0, keepdims=True))
    soft = e / jnp.sum(e, axis=0, keepdims=True)
    csum = soft[0]
    for i in range(1, layer + 1):
        csum = csum + soft[i]
    lb = csum - soft[0]

    for d, (g_ref, bd_ref) in enumerate(((gff_ref, bdf_ref), (gfb_ref, bdb_ref))):
        lbd = lb[d:d + 1]
        f = jnp.maximum(lbd + (1.0 - lbd) * jax.nn.sigmoid(g_ref[...]), F_FLOOR)
        k_scr[d] = 1.0 - f
        lf = jnp.log(f)
        bd = bd_ref[...]
        for p in range(t // HG_CUM_BLOCK):
            rows = slice(p * HG_CUM_BLOCK, (p + 1) * HG_CUM_BLOCK)
            h, m, l = _split3(lf[rows])
            cum_scr[d, rows, :] = _dot(bd, h) + _dot(bd, m) + _dot(bd, l)
        if has_ctx:
            st_scr[d] = s0_ref[0, d, 0].T
        else:
            st_scr[d] = jnp.zeros((HG_DV, HG_DK), F32)

    n = t // HG_CHUNK

    def body(i, carry):
        _hgrn_chain(False, pl.multiple_of(i * HG_CHUNK, HG_CHUNK), gq_ref, gi_ref, k_scr, cum_scr, st_scr, o_scr)
        _hgrn_chain(True, pl.multiple_of((n - 1 - i) * HG_CHUNK, HG_CHUNK), gq_ref, gi_ref, k_scr, cum_scr,
                    st_scr, o_scr)
        return carry

    lax.fori_loop(0, n, body, 0, unroll=min(n, HG_UNROLL))

    o = _rms(o_scr[0] + o_scr[1], on_ref[...])
    go = go_ref[...]
    o_ref[...] = (o * (go * jax.nn.sigmoid(go))).astype(MIX_DTYPE)
    for d in range(2):
        sout_ref[0, d, 0] = st_scr[d].T


def _cum_consts():
    r = np.arange(HG_CUM_BLOCK)
    same = (r[:, None] // HG_CHUNK) == (r[None, :] // HG_CHUNK)
    fwd = same & (r[None, :] <= r[:, None])
    bwd = same & (r[None, :] >= r[:, None])
    return jnp.asarray(fwd, BF16), jnp.asarray(bwd, BF16)


def _hgrn(z, hg_lb, onorm_g, layer, t, nb, row0, s0):
    rb = row0 // t
    bdf, bdb = _cum_consts()
    has_ctx = s0 is not None
    col = lambda c0: (lambda b, h: (rb + b, c0 + h))
    in_specs = [
        pl.BlockSpec((t, HG_DK), col(COL_GQ)),
        pl.BlockSpec((t, HG_DK), col(COL_GFF)),
        pl.BlockSpec((t, HG_DK), col(COL_GFB)),
        pl.BlockSpec((t, HG_DV), col(COL_GI)),
        pl.BlockSpec((t, HG_DV), col(COL_GO)),
        pl.BlockSpec((DEPTH, 2, HG_DK), lambda b, h: (0, 0, h)),
        pl.BlockSpec((1, HG_DV), lambda b, h: (0, 0)),
        pl.BlockSpec((HG_CUM_BLOCK, HG_CUM_BLOCK), lambda b, h: (0, 0)),
        pl.BlockSpec((HG_CUM_BLOCK, HG_CUM_BLOCK), lambda b, h: (0, 0)),
    ]
    args = [z, z, z, z, z, hg_lb, onorm_g.reshape(1, HG_DV), bdf, bdb]
    if has_ctx:
        in_specs.append(pl.BlockSpec((1, 2, 1, HG_DK, HG_DV), lambda b, h: (b, 0, h, 0, 0)))
        args.append(s0)
    return pl.pallas_call(
        functools.partial(_hgrn_kernel, layer=layer, t=t, has_ctx=has_ctx),
        out_shape=(jax.ShapeDtypeStruct((nb * t, GROUP_W), MIX_DTYPE),
                   jax.ShapeDtypeStruct((nb, 2, HG_HEADS, HG_DK, HG_DV), F32)),
        grid=(nb, HG_HEADS),
        in_specs=in_specs,
        out_specs=(pl.BlockSpec((t, HG_DV), lambda b, h: (b, h)),
                   pl.BlockSpec((1, 2, 1, HG_DK, HG_DV), lambda b, h: (b, 0, h, 0, 0))),
        scratch_shapes=[pltpu.VMEM((2, t, HG_DK), F32), pltpu.VMEM((2, t, HG_DK), F32),
                        pltpu.VMEM((2, HG_DV, HG_DK), F32), pltpu.VMEM((2, t, HG_DV), F32)],
        compiler_params=_params("arbitrary", "arbitrary"),
        name=f"hgrn_t{t}",
    )(*args)


def _conv_kernel(cb_ref, cc_ref, cx_ref, w_ref, b_ref, o_ref, *, t):
    p = cc_ref[...] * cx_ref[...]
    row = lax.broadcasted_iota(jnp.int32, (t, 1), 0)
    prev = jnp.where(row >= 1, pltpu.roll(p, 1, axis=0), 0.0)
    nxt = jnp.where(row <= t - 2, pltpu.roll(p, t - 1, axis=0), 0.0)
    w = w_ref[...]
    y = prev * w[0:1] + p * w[1:2] + nxt * w[2:3] + b_ref[...]
    o_ref[...] = (cb_ref[...] * y).astype(MIX_DTYPE)


def _conv(z, w, b, t, nb, row0):
    rb = row0 // t
    return pl.pallas_call(
        functools.partial(_conv_kernel, t=t),
        out_shape=jax.ShapeDtypeStruct((nb * t, GROUP_W), MIX_DTYPE),
        grid=(nb,),
        in_specs=[
            pl.BlockSpec((t, GROUP_W), lambda i: (rb + i, COL_CB // 4)),
            pl.BlockSpec((t, GROUP_W), lambda i: (rb + i, COL_CC // 4)),
            pl.BlockSpec((t, GROUP_W), lambda i: (rb + i, COL_CX // 4)),
            pl.BlockSpec((CONV_K, GROUP_W), lambda i: (0, 0)),
            pl.BlockSpec((1, GROUP_W), lambda i: (0, 0)),
        ],
        out_specs=pl.BlockSpec((t, GROUP_W), lambda i: (i, 0)),
        compiler_params=_params("arbitrary"),
        name=f"conv_t{t}",
    )(z, z, z, w, b.reshape(1, GROUP_W))


def _outproj_kernel(*refs, tm):
    prompt_refs, sample_refs = refs[0:4], refs[4:8]
    w_ref, xp_ref, xs_ref, g1_ref, sh_ref, sc_ref, n2_ref, x1_ref, h2_ref = refs[8:]
    i = pl.program_id(0)
    r = _mod_row(i * tm)

    def run(group_refs, x_ref):
        acc = None
        for g, ref in enumerate(group_refs):
            part = _dot(ref[...], w_ref[g * GROUP_W:(g + 1) * GROUP_W, :])
            acc = part if acc is None else acc + part
        x1 = x_ref[...] + g1_ref[pl.ds(r, 1), :] * acc
        x1_ref[...] = x1
        h2 = _rms(x1, n2_ref[...]) * (1.0 + sc_ref[pl.ds(r, 1), :]) + sh_ref[pl.ds(r, 1), :]
        h2_ref[...] = h2.astype(BF16)

    pl.when(i < N_PROMPT // tm)(lambda: run(prompt_refs, xp_ref))
    pl.when(i >= N_PROMPT // tm)(lambda: run(sample_refs, xs_ref))


def _outproj(prompt_groups, sample_groups, w_out_bf, layer, x_prompt, x_sample, mod, norm2_g):
    tm = OUTPROJ_TM
    n_pb, n_sb = N_PROMPT // tm, N_SAMPLE // tm
    p_spec = pl.BlockSpec((tm, GROUP_W), lambda i: (jnp.minimum(i, n_pb - 1), 0))
    s_spec = pl.BlockSpec((tm, GROUP_W), lambda i: (jnp.clip(i - n_pb, 0, n_sb - 1), 0))
    modspec = lambda k: pl.BlockSpec((MOD_ROWS, D_MODEL), lambda i: (0, k))
    return pl.pallas_call(
        functools.partial(_outproj_kernel, tm=tm),
        out_shape=(jax.ShapeDtypeStruct((N_TOK, D_MODEL), F32), jax.ShapeDtypeStruct((N_TOK, D_MODEL), BF16)),
        grid=(N_TOK // tm,),
        in_specs=[p_spec] * 4 + [s_spec] * 4 + [
            pl.BlockSpec((None, D_MODEL, D_MODEL), lambda i: (layer, 0, 0), pipeline_mode=pl.Buffered(1)),
            *_token_source_specs(tm, x_prompt, x_sample, 1),
            modspec(2), modspec(3), modspec(4),
            pl.BlockSpec((1, D_MODEL), lambda i: (0, 0))],
        out_specs=(pl.BlockSpec((tm, D_MODEL), lambda i: (i, 0)), pl.BlockSpec((tm, D_MODEL), lambda i: (i, 0))),
        compiler_params=_params("arbitrary", vmem=VMEM_LARGE),
        name="outproj",
    )(*prompt_groups, *sample_groups, w_out_bf, x_prompt, x_sample, mod, mod, mod, norm2_g.reshape(1, D_MODEL))


def _peer_query_kernel(h_ref, wq_ref, q_ref):
    q_ref[...] = _dot(h_ref[...], wq_ref[...]).astype(BF16)


def _peer_query(h2, wq_bf, layer):
    tm = QUERY_TM
    width = PEER_HEADS * PEER_QDIM
    return pl.pallas_call(
        _peer_query_kernel,
        out_shape=jax.ShapeDtypeStruct((N_TOK, width), BF16),
        grid=(N_TOK // tm,),
        in_specs=[pl.BlockSpec((tm, D_MODEL), lambda i: (i, 0)),
                  pl.BlockSpec((None, D_MODEL, width), lambda i: (layer, 0, 0))],
        out_specs=pl.BlockSpec((tm, width), lambda i: (i, 0)),
        compiler_params=_params("arbitrary"),
        name="peer_query",
    )(h2, wq_bf)


_CAND_RUNS = ((2, 5), (3, 4), (4, 3), (5, 2), (6, 2), (7, 2))
_CAND_RUN_ROWS = 24
_CAND_ROWS = 4 * 8 + _CAND_RUN_ROWS
_NO_CAND = 1 << 20


def _candidates(v1, v2, tm):
    sub = lax.broadcasted_iota(jnp.int32, (8, tm), 0)
    vals = [v1[0:1] + v2[0:8], v1[0:1] + v2[8:16], v1[1:2] + v2[0:8], v1[8:16] + v2[0:1]]
    ids = [sub, sub + 8, sub + PEER_TOPK, (sub + 8) * PEER_TOPK]
    row = lax.broadcasted_iota(jnp.int32, (_CAND_RUN_ROWS, tm), 0)
    v2_rep = jnp.concatenate([v2[0:8]] * (_CAND_RUN_ROWS // 8), axis=0)
    run_vals = jnp.full((_CAND_RUN_ROWS, tm), -jnp.inf, F32)
    run_ids = jnp.full((_CAND_RUN_ROWS, tm), _NO_CAND, jnp.int32)
    start = 0
    for r1, count in _CAND_RUNS:
        inside = (row >= start) & (row < start + count)
        shifted = v2_rep if start % 8 == 0 else pltpu.roll(v2_rep, start % 8, axis=0)
        run_vals = jnp.where(inside, v1[r1:r1 + 1] + shifted, run_vals)
        run_ids = jnp.where(inside, row + (r1 * PEER_TOPK - start), run_ids)
        start += count
    return jnp.concatenate(vals + [run_vals], axis=0), jnp.concatenate(ids + [run_ids], axis=0)


def _topk_rows(x_scr, ids, n_rows, v_out, i_out):
    def body(j, carry):
        x = x_scr[0:n_rows, :]
        m = jnp.max(x, axis=0, keepdims=True)
        idx = jnp.min(jnp.where(x == m, ids, _NO_CAND), axis=0, keepdims=True)
        v_out[pl.ds(j, 1), :] = m
        i_out[pl.ds(j, 1), :] = idx
        x_scr[0:n_rows, :] = jnp.where(ids == idx, -jnp.inf, x)
        return carry

    lax.fori_loop(0, PEER_TOPK, body, 0, unroll=2)


def _peer_topk_kernel(q_ref, k1_ref, k2_ref, u_ref, v_ref, e_ref, g_ref, ub_ref, vb_ref, x_scr, v1_scr, i1_scr,
                      v2_scr, i2_scr, ts_scr, tc_scr, *, tm):
    ub_ref[...] = u_ref[...].astype(BF16)
    vb_ref[...] = v_ref[...].astype(BF16)
    key_ids = lax.broadcasted_iota(jnp.int32, (N_KEYS, tm), 0)
    for half, (k_ref, v_scr, i_scr) in enumerate(((k1_ref, v1_scr, i1_scr), (k2_ref, v2_scr, i2_scr))):
        qh = q_ref[:, half * PEER_HALF:(half + 1) * PEER_HALF]
        x_scr[...] = lax.dot_general(k_ref[...], qh, NT_DIMS, preferred_element_type=F32)
        _topk_rows(x_scr, key_ids, N_KEYS, v_scr, i_scr)
    cand_vals, cand_ids = _candidates(v1_scr[...], v2_scr[...], tm)
    x_scr[0:_CAND_ROWS, :] = cand_vals
    _topk_rows(x_scr, cand_ids, _CAND_ROWS, ts_scr, tc_scr)
    tc = tc_scr[...]
    c1, c2 = tc // PEER_TOPK, tc % PEER_TOPK
    i1, i2 = i1_scr[...], i2_scr[...]
    e1 = jnp.zeros_like(tc)
    e2 = jnp.zeros_like(tc)
    for a in range(PEER_TOPK):
        e1 = jnp.where(c1 == a, i1[a:a + 1], e1)
        e2 = jnp.where(c2 == a, i2[a:a + 1], e2)
    e_ref[...] = e1 * N_KEYS + e2
    ts = ts_scr[...]
    p = jnp.exp(ts - jnp.max(ts, axis=0, keepdims=True))
    g_ref[...] = p / jnp.sum(p, axis=0, keepdims=True)


def _peer_topk(q, k1_bf, k2_bf, peer_u, peer_v, layer):
    tm = TOPK_TM
    slab = LANES
    n_slabs = N_EXPERTS // slab
    assert (N_TOK // tm) * PEER_HEADS >= n_slabs
    row_f = pltpu.VMEM((PEER_TOPK, tm), F32)
    row_i = pltpu.VMEM((PEER_TOPK, tm), jnp.int32)
    slab_of = lambda i, h: jnp.minimum(i * PEER_HEADS + h, n_slabs - 1)
    table_in = pl.BlockSpec((None, slab, D_MODEL), lambda i, h: (layer, slab_of(i, h), 0))
    table_out = pl.BlockSpec((slab, D_MODEL), lambda i, h: (slab_of(i, h), 0))
    table_shape = jax.ShapeDtypeStruct((N_EXPERTS, D_MODEL), BF16)
    return pl.pallas_call(
        functools.partial(_peer_topk_kernel, tm=tm),
        out_shape=(jax.ShapeDtypeStruct((PEER_HEADS * PEER_TOPK, N_TOK), jnp.int32),
                   jax.ShapeDtypeStruct((PEER_HEADS * PEER_TOPK, N_TOK), F32), table_shape, table_shape),
        grid=(N_TOK // tm, PEER_HEADS),
        in_specs=[
            pl.BlockSpec((tm, PEER_QDIM), lambda i, h: (i, h)),
            pl.BlockSpec((None, N_KEYS, PEER_HALF), lambda i, h: (layer, 0, 0)),
            pl.BlockSpec((None, N_KEYS, PEER_HALF), lambda i, h: (layer, 0, 0)),
            table_in, table_in,
        ],
        out_specs=(pl.BlockSpec((PEER_TOPK, tm), lambda i, h: (h, i)),
                   pl.BlockSpec((PEER_TOPK, tm), lambda i, h: (h, i)), table_out, table_out),
        scratch_shapes=[pltpu.VMEM((N_KEYS, tm), F32), row_f, row_i, row_f, row_i, row_f, row_i],
        compiler_params=_params("arbitrary", "arbitrary"),
        name="peer_topk",
    )(q, k1_bf, k2_bf, peer_u, peer_v)


def _peer_mask_kernel(e_ref, g_ref, m_ref, et_scr, gt_scr, slab_scr, *, tb):
    et_scr[...] = e_ref[...].T
    gt_scr[...] = g_ref[...].T
    key = lax.broadcasted_iota(jnp.int32, (N_KEYS, PEER_HEADS * PEER_TOPK), 0)

    def body(t, carry):
        er = et_scr[pl.ds(t, 1), :]
        gr = gt_scr[pl.ds(t, 1), :]
        p1 = jnp.where(key == er // N_KEYS, gr, 0.0).astype(BF16)
        p2 = jnp.where(key == er % N_KEYS, 1.0, 0.0).astype(BF16)
        mt = lax.dot_general(p1, p2, NT_DIMS, preferred_element_type=F32)
        slab_scr[pl.ds(pl.multiple_of(t * MASK_PITCH, 4), N_KEYS), :] = mt
        return carry

    lax.fori_loop(0, tb, body, 0, unroll=64)
    for k1 in range(N_KEYS):
        m_ref[:, k1 * N_KEYS:(k1 + 1) * N_KEYS] = slab_scr[pl.ds(k1, tb, stride=MASK_PITCH), :].astype(BF16)


def _peer_mask(experts_t, gates_t):
    tb = MASK_TOKENS
    pairs = PEER_HEADS * PEER_TOPK
    return pl.pallas_call(
        functools.partial(_peer_mask_kernel, tb=tb),
        out_shape=jax.ShapeDtypeStruct((N_TOK, N_EXPERTS), BF16),
        grid=(N_TOK // tb,),
        in_specs=[pl.BlockSpec((pairs, tb), lambda i: (0, i)), pl.BlockSpec((pairs, tb), lambda i: (0, i))],
        out_specs=pl.BlockSpec((tb, N_EXPERTS), lambda i: (i, 0)),
        scratch_shapes=[pltpu.VMEM((tb, pairs), jnp.int32), pltpu.VMEM((tb, pairs), F32),
                        pltpu.VMEM((tb * MASK_PITCH, N_KEYS), F32)],
        compiler_params=_params("arbitrary"),
        name="peer_mask",
    )(experts_t, gates_t)


def _peer_expert_kernel(h_ref, u_ref, v_ref, m_ref, x_ref, g2_ref, o_ref, *, tm):
    j = pl.program_id(1)

    @pl.when(j == 0)
    def _():
        o_ref[...] = jnp.zeros_like(o_ref)

    s = lax.dot_general(h_ref[...], u_ref[...], NT_DIMS, preferred_element_type=F32)
    a = (m_ref[...].astype(F32) * jax.nn.gelu(s)).astype(BF16)
    o_ref[...] += _dot(a, v_ref[...])

    @pl.when(j == pl.num_programs(1) - 1)
    def _():
        r = _mod_row(pl.program_id(0) * tm)
        o_ref[...] = x_ref[...] + g2_ref[pl.ds(r, 1), :] * o_ref[...]


def _peer_expert(h2, u_bf, v_bf, mask, x1, mod):
    tm, te = EXPERT_TM, EXPERT_TE
    return pl.pallas_call(
        functools.partial(_peer_expert_kernel, tm=tm),
        out_shape=jax.ShapeDtypeStruct((N_TOK, D_MODEL), F32),
        grid=(N_TOK // tm, N_EXPERTS // te),
        in_specs=[
            pl.BlockSpec((tm, D_MODEL), lambda i, j: (i, 0)),
            pl.BlockSpec((te, D_MODEL), lambda i, j: (j, 0)),
            pl.BlockSpec((te, D_MODEL), lambda i, j: (j, 0)),
            pl.BlockSpec((tm, te), lambda i, j: (i, j)),
            pl.BlockSpec((tm, D_MODEL), lambda i, j: (i, 0)),
            pl.BlockSpec((MOD_ROWS, D_MODEL), lambda i, j: (0, 5)),
        ],
        out_specs=pl.BlockSpec((tm, D_MODEL), lambda i, j: (i, 0)),
        compiler_params=_params("arbitrary", "arbitrary", vmem=VMEM_LARGE),
        name="peer_expert",
    )(h2, u_bf, v_bf, mask, x1, mod)


def kernel(x_prompt, x_sample, cache_k, cache_v, state_hgrn, c, c_ctx, ada_w, ada_b, norm1_g, norm2_g, w_in,
           q_norm_g, k_norm_g, hg_lb, hg_onorm_g, conv_w, conv_b, w_out, peer_wq, peer_k1, peer_k2, peer_u, peer_v):
    x_p, x_s = x_prompt.reshape(N_PROMPT, D_MODEL), x_sample.reshape(N_SAMPLE, D_MODEL)
    cvec = jnp.concatenate([c_ctx[None, :], c, jnp.zeros((MOD_ROWS - 1 - DEC_BATCH, D_MODEL), F32)], axis=0)
    mod_all = _ada(cvec, ada_w, ada_b)

    w_in_bf, w_out_bf, wq_bf = w_in.astype(BF16), w_out.astype(BF16), peer_wq.astype(BF16)
    k1_bf, k2_bf = peer_k1.astype(BF16), peer_k2.astype(BF16)

    ks_out, vs_out, ss_out = [], [], []
    for l in range(DEPTH):
        mod = mod_all[l]
        z = _inproj(x_p, x_s, mod, norm1_g[l], w_in_bf, l)

        fn_p = _fourier(z, SEQ, BATCH, 0)
        fn_s = _fourier(z, DEC_SEQ, DEC_BATCH, N_PROMPT)

        at_p, k_l, v_l = _attn_prompt(z, q_norm_g[l], k_norm_g[l])
        k_ctx = cache_k[:, l].reshape(DEC_BATCH, PAST_LEN, N_KV * HEAD_DIM)
        v_ctx = cache_v[:, l].reshape(DEC_BATCH, PAST_LEN, N_KV * HEAD_DIM)
        at_s = _attn_sample(z, q_norm_g[l], k_norm_g[l], k_ctx, v_ctx)

        hg_p, s_l = _hgrn(z, hg_lb, hg_onorm_g[l], l, SEQ, BATCH, 0, None)
        hg_s, _ = _hgrn(z, hg_lb, hg_onorm_g[l], l, DEC_SEQ, DEC_BATCH, N_PROMPT, state_hgrn[:, l])

        cv_p = _conv(z, conv_w[l], conv_b[l], SEQ, BATCH, 0)
        cv_s = _conv(z, conv_w[l], conv_b[l], DEC_SEQ, DEC_BATCH, N_PROMPT)

        x1, h2 = _outproj((fn_p, at_p, hg_p, cv_p), (fn_s, at_s, hg_s, cv_s), w_out_bf, l, x_p, x_s, mod,
                          norm2_g[l])

        q_peer = _peer_query(h2, wq_bf, l)
        experts_t, gates_t, u_bf, v_bf = _peer_topk(q_peer, k1_bf, k2_bf, peer_u, peer_v, l)
        mask = _peer_mask(experts_t, gates_t)
        x = _peer_expert(h2, u_bf, v_bf, mask, x1, mod)
        x_p = x_s = x

        ks_out.append(k_l.reshape(BATCH, SEQ, N_KV, HEAD_DIM))
        vs_out.append(v_l.reshape(BATCH, SEQ, N_KV, HEAD_DIM))
        ss_out.append(s_l)

    y_prompt = x[:N_PROMPT].reshape(BATCH, SEQ, D_MODEL)
    y_sample = x[N_PROMPT:].reshape(DEC_BATCH, DEC_SEQ, D_MODEL)
    return (y_prompt, y_sample, jnp.stack(ks_out, axis=1), jnp.stack(vs_out, axis=1), jnp.stack(ss_out, axis=1))
```

```python
import functools

import numpy as np
import jax
import jax.numpy as jnp
from jax import lax
from jax.experimental import pallas as pl
from jax.experimental.pallas import tpu as pltpu

F32 = jnp.float32
BF16 = jnp.bfloat16
MIX_DTYPE = BF16

D_MODEL = 2048
BATCH = 32
SEQ = 256
DEPTH = 2
DEC_BATCH = 4
DEC_SEQ = 1024
PAST_LEN = 256
GRID_W = 64
GROUP_W = D_MODEL // 4
FN_GROUPS = 4
FN_GW = GROUP_W // FN_GROUPS
HEAD_DIM = 128
N_HEADS = GROUP_W // HEAD_DIM
N_KV = 2
GQA_GROUP = N_HEADS // N_KV
ROPE_THETA = 10000.0
HG_DK = 128
HG_DV = 128
HG_HEADS = GROUP_W // HG_DK
CONV_K = 3
IN_COLS = 11 * GROUP_W
PEER_HEADS = 8
N_KEYS = 128
N_EXPERTS = N_KEYS * N_KEYS
PEER_QDIM = 256
PEER_HALF = PEER_QDIM // 2
PEER_TOPK = 16
EPS = 1e-6
F_FLOOR = 1e-30

N_PROMPT = BATCH * SEQ
N_SAMPLE = DEC_BATCH * DEC_SEQ
N_TOK = N_PROMPT + N_SAMPLE
MOD_ROWS = 8
LANES = 128
NORM_ROWS = 128
HG_CHUNK = 32
HG_GROUP = 8
HG_UNROLL = 16
HG_CUM_BLOCK = 256
MASK_TOKENS = 128
MASK_PITCH = 132

ADA_TN = 1024
INPROJ_TM, INPROJ_TN = 1024, 512
OUTPROJ_TM = 512
QUERY_TM = 512
TOPK_TM = 512
EXPERT_TM, EXPERT_TE = 1024, 512
ATTN_Q_ROWS = 256

VMEM_SMALL = 32 * 1024 * 1024
VMEM_INPROJ = 52 * 1024 * 1024
VMEM_LARGE = 56 * 1024 * 1024

COL_FN, COL_Q, COL_K, COL_V = 0, 4, 8, 10
COL_GQ, COL_GFF, COL_GFB, COL_GI, COL_GO = 12, 16, 20, 24, 28
COL_CB, COL_CC, COL_CX = 32, 36, 40

NT_DIMS = (((1,), (1,)), ((), ()))
TN_DIMS = (((0,), (0,)), ((), ()))


def _params(*sem, vmem=VMEM_SMALL):
    return pltpu.CompilerParams(dimension_semantics=sem, vmem_limit_bytes=vmem)


def _dot(a, b):
    return jnp.dot(a, b, preferred_element_type=F32)


def _split3(x):
    h = x.astype(BF16)
    r = x - h.astype(F32)
    m = r.astype(BF16)
    l = (r - m.astype(F32)).astype(BF16)
    return h, m, l


def _split2(x):
    h = x.astype(BF16)
    return h, (x - h.astype(F32)).astype(BF16)


def _mod_row(tok0):
    return jnp.where(tok0 >= N_PROMPT, (tok0 - N_PROMPT) // DEC_SEQ + 1, 0)


def _rms(x, g):
    return x * lax.rsqrt(jnp.mean(x * x, axis=-1, keepdims=True) + EPS) * g


def _ada_kernel(c_ref, w_ref, b_ref, o_ref):
    c = c_ref[...]
    s = (c * jax.nn.sigmoid(c)).astype(BF16)
    o_ref[0] = _dot(s, w_ref[0].astype(BF16)) + b_ref[0]


def _ada(cvec, ada_w, ada_b):
    tn = ADA_TN
    return pl.pallas_call(
        _ada_kernel,
        out_shape=jax.ShapeDtypeStruct((DEPTH, MOD_ROWS, 6 * D_MODEL), F32),
        grid=(DEPTH, 6 * D_MODEL // tn),
        in_specs=[
            pl.BlockSpec((MOD_ROWS, D_MODEL), lambda l, j: (0, 0)),
            pl.BlockSpec((1, D_MODEL, tn), lambda l, j: (l, 0, j)),
            pl.BlockSpec((1, 1, tn), lambda l, j: (l, 0, j)),
        ],
        out_specs=pl.BlockSpec((1, MOD_ROWS, tn), lambda l, j: (l, 0, j)),
        compiler_params=_params("arbitrary", "arbitrary"),
        name="ada",
    )(cvec, ada_w, ada_b.reshape(DEPTH, 1, 6 * D_MODEL))


def _token_source_specs(tm, x_prompt, x_sample, n_grid_axes):
    n_pb, n_sb = N_PROMPT // tm, N_SAMPLE // tm
    s_off = x_sample.shape[0] // tm - n_sb
    if n_grid_axes == 1:
        return (pl.BlockSpec((tm, D_MODEL), lambda i: (jnp.minimum(i, n_pb - 1), 0)),
                pl.BlockSpec((tm, D_MODEL), lambda i: (s_off + jnp.clip(i - n_pb, 0, n_sb - 1), 0)))
    return (pl.BlockSpec((tm, D_MODEL), lambda i, j: (jnp.minimum(i, n_pb - 1), 0)),
            pl.BlockSpec((tm, D_MODEL), lambda i, j: (s_off + jnp.clip(i - n_pb, 0, n_sb - 1), 0)))


def _inproj_kernel(xp_ref, xs_ref, sh_ref, sc_ref, g_ref, w_ref, o_ref, h_ref, *, tm):
    i, j = pl.program_id(0), pl.program_id(1)

    def prologue(x_ref):
        r = _mod_row(i * tm)
        gain = g_ref[...] * (1.0 + sc_ref[pl.ds(r, 1), :])
        shift = sh_ref[pl.ds(r, 1), :]

        def norm_rows(c, carry):
            rows = pl.ds(pl.multiple_of(c * NORM_ROWS, NORM_ROWS), NORM_ROWS)
            x = x_ref[rows, :]
            y = x * lax.rsqrt(jnp.mean(x * x, axis=-1, keepdims=True) + EPS)
            h_ref[rows, :] = (y * gain + shift).astype(BF16)
            return carry

        lax.fori_loop(0, tm // NORM_ROWS, norm_rows, 0)

    pl.when((j == 0) & (i < N_PROMPT // tm))(lambda: prologue(xp_ref))
    pl.when((j == 0) & (i >= N_PROMPT // tm))(lambda: prologue(xs_ref))
    o_ref[...] = _dot(h_ref[...], w_ref[...].astype(BF16))


def _inproj(x_prompt, x_sample, mod, norm_g, w_in, layer):
    tm, tn = INPROJ_TM, INPROJ_TN
    return pl.pallas_call(
        functools.partial(_inproj_kernel, tm=tm),
        out_shape=jax.ShapeDtypeStruct((N_TOK, IN_COLS), F32),
        grid=(N_TOK // tm, IN_COLS // tn),
        in_specs=[
            *_token_source_specs(tm, x_prompt, x_sample, 2),
            pl.BlockSpec((MOD_ROWS, D_MODEL), lambda i, j: (0, 0)),
            pl.BlockSpec((MOD_ROWS, D_MODEL), lambda i, j: (0, 1)),
            pl.BlockSpec((1, D_MODEL), lambda i, j: (0, 0)),
            pl.BlockSpec((None, D_MODEL, tn), lambda i, j: (layer, 0, j)),
        ],
        out_specs=pl.BlockSpec((tm, tn), lambda i, j: (i, j)),
        scratch_shapes=[pltpu.VMEM((tm, D_MODEL), BF16)],
        compiler_params=_params("arbitrary", "arbitrary", vmem=VMEM_INPROJ),
        name="inproj",
    )(x_prompt, x_sample, mod, mod, norm_g.reshape(1, D_MODEL), w_in)


def _fourier_kernel(u_ref, cch_ref, ccl_ref, cth_ref, ctl_ref, o_ref):
    cch, ccl, cth, ctl = cch_ref[...], ccl_ref[...], cth_ref[...], ctl_ref[...]
    for g in range(0, FN_GROUPS, 2):
        ws = []
        for gg in (g, g + 1):
            uh, ul = _split2(u_ref[:, gg * FN_GW:(gg + 1) * FN_GW])
            ws.append(_dot(uh, cch) + (_dot(ul, cch) + _dot(uh, ccl)))
        st = jnp.concatenate([jnp.concatenate([w[:, :FN_GW] for w in ws], axis=1),
                              jnp.concatenate([w[:, FN_GW:] for w in ws], axis=1)], axis=0)
        sh, sl = _split2(st)
        o_ref[:, g * FN_GW:(g + 2) * FN_GW] = (_dot(cth, sh) + (_dot(cth, sl) + _dot(ctl, sh))).astype(MIX_DTYPE)


def _dft_consts(t):
    def cs(n):
        k = np.arange(n, dtype=np.float64)
        ang = 2.0 * np.pi * np.outer(k, k) / n
        return np.cos(ang) / np.sqrt(n), np.sin(ang) / np.sqrt(n)

    cc, sc = cs(FN_GW)
    ct, st = cs(t)
    right = np.concatenate([cc, sc], axis=1).astype(np.float32)
    left = np.concatenate([ct, -st], axis=1).astype(np.float32)

    def hl(a):
        a = jnp.asarray(a)
        h = a.astype(BF16)
        return h, (a - h.astype(F32)).astype(BF16)

    return hl(right) + hl(left)


def _fourier(z, t, nb, row0):
    cch, ccl, cth, ctl = _dft_consts(t)
    rb = row0 // t
    return pl.pallas_call(
        _fourier_kernel,
        out_shape=jax.ShapeDtypeStruct((nb * t, GROUP_W), MIX_DTYPE),
        grid=(nb,),
        in_specs=[
            pl.BlockSpec((t, GROUP_W), lambda b: (rb + b, COL_FN // 4)),
            pl.BlockSpec((FN_GW, 2 * FN_GW), lambda b: (0, 0)),
            pl.BlockSpec((FN_GW, 2 * FN_GW), lambda b: (0, 0)),
            pl.BlockSpec((t, 2 * t), lambda b: (0, 0)),
            pl.BlockSpec((t, 2 * t), lambda b: (0, 0)),
        ],
        out_specs=pl.BlockSpec((t, GROUP_W), lambda b: (b, 0)),
        compiler_params=_params("arbitrary"),
        name=f"fourier_t{t}",
    )(z, cch, ccl, cth, ctl)


def _softmax_pv(q_bf, k_bf, v_bf):
    s = lax.dot_general(q_bf, k_bf, NT_DIMS, preferred_element_type=F32) * (HEAD_DIM ** -0.5)
    p = jnp.exp(s - jnp.max(s, axis=-1, keepdims=True))
    return _dot(p.astype(BF16), v_bf) / jnp.sum(p, axis=-1, keepdims=True)


def _attn_prompt_kernel(q_ref, k_ref, v_ref, qg_ref, kg_ref, o_ref, kc_ref, vc_ref):
    kn = _rms(k_ref[...], kg_ref[...])
    v = v_ref[...]
    head_rows = pl.ds(pl.program_id(1), SEQ, stride=N_KV)
    kc_ref[0, head_rows, :] = kn
    vc_ref[0, head_rows, :] = v
    kb, vb = kn.astype(BF16), v.astype(BF16)
    for g in range(GQA_GROUP):
        qn = _rms(q_ref[:, g * HEAD_DIM:(g + 1) * HEAD_DIM], qg_ref[...])
        o_ref[:, g * HEAD_DIM:(g + 1) * HEAD_DIM] = _softmax_pv(qn.astype(BF16), kb, vb).astype(MIX_DTYPE)


def _attn_prompt(z, q_g, k_g):
    t = SEQ
    kv_shape = jax.ShapeDtypeStruct((BATCH, SEQ * N_KV, HEAD_DIM), F32)
    kv_spec = pl.BlockSpec((1, SEQ * N_KV, HEAD_DIM), lambda b, h: (b, 0, 0))
    return pl.pallas_call(
        _attn_prompt_kernel,
        out_shape=(jax.ShapeDtypeStruct((N_PROMPT, GROUP_W), MIX_DTYPE), kv_shape, kv_shape),
        grid=(BATCH, N_KV),
        in_specs=[
            pl.BlockSpec((t, GQA_GROUP * HEAD_DIM), lambda b, h: (b, COL_Q // 2 + h)),
            pl.BlockSpec((t, HEAD_DIM), lambda b, h: (b, COL_K + h)),
            pl.BlockSpec((t, HEAD_DIM), lambda b, h: (b, COL_V + h)),
            pl.BlockSpec((1, HEAD_DIM), lambda b, h: (0, 0)),
            pl.BlockSpec((1, HEAD_DIM), lambda b, h: (0, 0)),
        ],
        out_specs=(
            pl.BlockSpec((t, GQA_GROUP * HEAD_DIM), lambda b, h: (b, h)),
            kv_spec, kv_spec,
        ),
        compiler_params=_params("arbitrary", "arbitrary"),
        name="attn_prompt",
    )(z, z, z, q_g.reshape(1, HEAD_DIM), k_g.reshape(1, HEAD_DIM))


def _rope(x, cosf, sinf):
    return x * cosf + pltpu.roll(x, HEAD_DIM // 2, axis=1) * sinf


def _attn_sample_kernel(q_ref, k_ref, v_ref, kctx_ref, vctx_ref, qg_ref, kg_ref, cos_ref, sin_ref,
                        o_ref, kall_ref, vall_ref, *, qb):
    cosf, sinf = cos_ref[...], sin_ref[...]
    kall_ref[0:PAST_LEN, :] = kctx_ref[0].astype(BF16)
    vall_ref[0:PAST_LEN, :] = vctx_ref[0].astype(BF16)
    kall_ref[PAST_LEN:, :] = _rope(_rms(k_ref[...], kg_ref[...]), cosf, sinf).astype(BF16)
    vall_ref[PAST_LEN:, :] = v_ref[...].astype(BF16)
    kb, vb = kall_ref[...], vall_ref[...]
    for i in range(DEC_SEQ // qb):
        rows = slice(i * qb, (i + 1) * qb)
        for g in range(GQA_GROUP):
            cols = slice(g * HEAD_DIM, (g + 1) * HEAD_DIM)
            qn = _rope(_rms(q_ref[rows, cols], qg_ref[...]), cosf[rows], sinf[rows])
            o_ref[rows, cols] = _softmax_pv(qn.astype(BF16), kb, vb).astype(MIX_DTYPE)


def _rope_tables():
    rows = DEC_SEQ // GRID_W
    row = np.repeat(np.arange(rows, dtype=np.float32), GRID_W)
    col = np.tile(np.arange(GRID_W, dtype=np.float32), rows)
    n_freq = HEAD_DIM // 4
    inv = (np.float32(ROPE_THETA) ** (-np.arange(n_freq, dtype=np.float32) / n_freq)).astype(np.float32)
    ang = np.concatenate([row[:, None] * inv, col[:, None] * inv], axis=-1).astype(np.float32)
    cos, sin = np.cos(ang.astype(np.float64)), np.sin(ang.astype(np.float64))
    cosf = np.concatenate([cos, cos], axis=-1).astype(np.float32)
    sinf = np.concatenate([-sin, sin], axis=-1).astype(np.float32)
    return jnp.asarray(cosf), jnp.asarray(sinf)


def _attn_sample(z, q_g, k_g, k_ctx, v_ctx):
    t = DEC_SEQ
    rb = N_PROMPT // t
    cosf, sinf = _rope_tables()
    const = lambda b, h: (0, 0)
    return pl.pallas_call(
        functools.partial(_attn_sample_kernel, qb=ATTN_Q_ROWS),
        out_shape=jax.ShapeDtypeStruct((N_SAMPLE, GROUP_W), MIX_DTYPE),
        grid=(DEC_BATCH, N_KV),
        in_specs=[
            pl.BlockSpec((t, GQA_GROUP * HEAD_DIM), lambda b, h: (rb + b, COL_Q // 2 + h)),
            pl.BlockSpec((t, HEAD_DIM), lambda b, h: (rb + b, COL_K + h)),
            pl.BlockSpec((t, HEAD_DIM), lambda b, h: (rb + b, COL_V + h)),
            pl.BlockSpec((1, PAST_LEN, HEAD_DIM), lambda b, h: (b, 0, h)),
            pl.BlockSpec((1, PAST_LEN, HEAD_DIM), lambda b, h: (b, 0, h)),
            pl.BlockSpec((1, HEAD_DIM), const),
            pl.BlockSpec((1, HEAD_DIM), const),
            pl.BlockSpec((t, HEAD_DIM), const),
            pl.BlockSpec((t, HEAD_DIM), const),
        ],
        out_specs=pl.BlockSpec((t, GQA_GROUP * HEAD_DIM), lambda b, h: (b, h)),
        scratch_shapes=[pltpu.VMEM((PAST_LEN + t, HEAD_DIM), BF16), pltpu.VMEM((PAST_LEN + t, HEAD_DIM), BF16)],
        compiler_params=_params("arbitrary", "arbitrary"),
        name="attn_sample",
    )(z, z, z, k_ctx, v_ctx, q_g.reshape(1, HEAD_DIM), k_g.reshape(1, HEAD_DIM), cosf, sinf)


def _hgrn_chain(rev, r0, gq_ref, gi_ref, k_scr, cum_scr, st_scr, o_scr):
    c = HG_CHUNK
    d = 1 if rev else 0
    rows = pl.ds(r0, c)
    qc, vc = gq_ref[rows, :], gi_ref[rows, :]
    kc, cc = k_scr[d, rows, :], cum_scr[d, rows, :]
    st = st_scr[d]
    last = cc[0:1] if rev else cc[c - 1:c]
    qs = (qc * jnp.exp(cc)).astype(BF16)
    o_inter = lax.dot_general(qs, st.astype(BF16), NT_DIMS, preferred_element_type=F32)
    ks = (kc * jnp.exp(last - cc)).astype(BF16)
    upd = lax.dot_general(vc.astype(BF16), ks, TN_DIMS, preferred_element_type=F32)
    st_scr[d] = st * jnp.exp(last) + upd
    n_grp = c // HG_GROUP
    row = lax.broadcasted_iota(jnp.int32, (HG_GROUP, 1), 0)
    krow = lax.broadcasted_iota(jnp.int32, (c, 1), 0)
    diag, att = [], []
    for i in range(n_grp):
        lo = i * HG_GROUP
        grp = slice(lo, lo + HG_GROUP)
        q_i, k_i, v_i, c_i = qc[grp], kc[grp], vc[grp], cc[grp]
        acc = jnp.zeros((HG_GROUP, HG_DV), F32)
        for delta in range(HG_GROUP):
            if delta == 0:
                k_sh, c_sh, v_sh = k_i, c_i, v_i
            else:
                sh = (HG_GROUP - delta) if rev else delta
                k_sh, c_sh, v_sh = (pltpu.roll(a, sh, axis=0) for a in (k_i, c_i, v_i))
            a = jnp.sum(q_i * k_sh * jnp.exp(jnp.minimum(c_i - c_sh, 0.0)), axis=-1, keepdims=True)
            valid = (row <= HG_GROUP - 1 - delta) if rev else (row >= delta)
            acc = acc + jnp.where(valid, a, 0.0) * v_sh
        diag.append(acc)
        if (i == n_grp - 1) if rev else (i == 0):
            att.append(jnp.zeros((HG_GROUP, c), F32))
            continue
        if rev:
            b, other = cc[lo + HG_GROUP:lo + HG_GROUP + 1], krow >= lo + HG_GROUP
        else:
            b, other = cc[lo - 1:lo], krow < lo
        qs_i = (q_i * jnp.exp(c_i - b)).astype(BF16)
        ks_i = jnp.where(other, kc * jnp.exp(jnp.minimum(b - cc, 0.0)), 0.0).astype(BF16)
        att.append(lax.dot_general(qs_i, ks_i, NT_DIMS, preferred_element_type=F32))
    o_off = _dot(jnp.concatenate(att, axis=0).astype(BF16), vc.astype(BF16))
    o_scr[d, rows, :] = o_inter + jnp.concatenate(diag, axis=0) + o_off


def _hgrn_kernel(*refs, layer, t, has_ctx):
    gq_ref, gff_ref, gfb_ref, gi_ref, go_ref, lb_ref, on_ref, bdf_ref, bdb_ref = refs[:9]
    n_in = 9
    s0_ref = None
    if has_ctx:
        s0_ref = refs[9]
        n_in = 10
    o_ref, sout_ref = refs[n_in], refs[n_in + 1]
    k_scr, cum_scr, st_scr, o_scr = refs[n_in + 2:]

    raw = lb_ref[...]
    e = jnp.exp(raw - jnp.max(raw, axis=0, keepdims=True))
    soft = e / jnp.sum(e, axis=0, keepdims=True)
    csum = soft[0]
    for i in range(1, layer + 1):
        csum = csum + soft[i]
    lb = csum - soft[0]

    for d, (g_ref, bd_ref) in enumerate(((gff_ref, bdf_ref), (gfb_ref, bdb_ref))):
        lbd = lb[d:d + 1]
        f = jnp.maximum(lbd + (1.0 - lbd) * jax.nn.sigmoid(g_ref[...]), F_FLOOR)
        k_scr[d] = 1.0 - f
        lf = jnp.log(f)
        bd = bd_ref[...]
        for p in range(t // HG_CUM_BLOCK):
            rows = slice(p * HG_CUM_BLOCK, (p + 1) * HG_CUM_BLOCK)
            h, m, l = _split3(lf[rows])
            cum_scr[d, rows, :] = _dot(bd, h) + _dot(bd, m) + _dot(bd, l)
        if has_ctx:
            st_scr[d] = s0_ref[0, d, 0].T
        else:
            st_scr[d] = jnp.zeros((HG_DV, HG_DK), F32)

    n = t // HG_CHUNK

    def body(i, carry):
        _hgrn_chain(False, pl.multiple_of(i * HG_CHUNK, HG_CHUNK), gq_ref, gi_ref, k_scr, cum_scr, st_scr, o_scr)
        _hgrn_chain(True, pl.multiple_of((n - 1 - i) * HG_CHUNK, HG_CHUNK), gq_ref, gi_ref, k_scr, cum_scr,
                    st_scr, o_scr)
        return carry

    lax.fori_loop(0, n, body, 0, unroll=min(n, HG_UNROLL))

    o = _rms(o_scr[0] + o_scr[1], on_ref[...])
    go = go_ref[...]
    o_ref[...] = (o * (go * jax.nn.sigmoid(go))).astype(MIX_DTYPE)
    for d in range(2):
        sout_ref[0, d, 0] = st_scr[d].T


def _cum_consts():
    r = np.arange(HG_CUM_BLOCK)
    same = (r[:, None] // HG_CHUNK) == (r[None, :] // HG_CHUNK)
    fwd = same & (r[None, :] <= r[:, None])
    bwd = same & (r[None, :] >= r[:, None])
    return jnp.asarray(fwd, BF16), jnp.asarray(bwd, BF16)


def _hgrn(z, hg_lb, onorm_g, layer, t, nb, row0, s0):
    rb = row0 // t
    bdf, bdb = _cum_consts()
    has_ctx = s0 is not None
    col = lambda c0: (lambda b, h: (rb + b, c0 + h))
    in_specs = [
        pl.BlockSpec((t, HG_DK), col(COL_GQ)),
        pl.BlockSpec((t, HG_DK), col(COL_GFF)),
        pl.BlockSpec((t, HG_DK), col(COL_GFB)),
        pl.BlockSpec((t, HG_DV), col(COL_GI)),
        pl.BlockSpec((t, HG_DV), col(COL_GO)),
        pl.BlockSpec((DEPTH, 2, HG_DK), lambda b, h: (0, 0, h)),
        pl.BlockSpec((1, HG_DV), lambda b, h: (0, 0)),
        pl.BlockSpec((HG_CUM_BLOCK, HG_CUM_BLOCK), lambda b, h: (0, 0)),
        pl.BlockSpec((HG_CUM_BLOCK, HG_CUM_BLOCK), lambda b, h: (0, 0)),
    ]
    args = [z, z, z, z, z, hg_lb, onorm_g.reshape(1, HG_DV), bdf, bdb]
    if has_ctx:
        in_specs.append(pl.BlockSpec((1, 2, 1, HG_DK, HG_DV), lambda b, h: (b, 0, h, 0, 0)))
        args.append(s0)
    return pl.pallas_call(
        functools.partial(_hgrn_kernel, layer=layer, t=t, has_ctx=has_ctx),
        out_shape=(jax.ShapeDtypeStruct((nb * t, GROUP_W), MIX_DTYPE),
                   jax.ShapeDtypeStruct((nb, 2, HG_HEADS, HG_DK, HG_DV), F32)),
        grid=(nb, HG_HEADS),
        in_specs=in_specs,
        out_specs=(pl.BlockSpec((t, HG_DV), lambda b, h: (b, h)),
                   pl.BlockSpec((1, 2, 1, HG_DK, HG_DV), lambda b, h: (b, 0, h, 0, 0))),
        scratch_shapes=[pltpu.VMEM((2, t, HG_DK), F32), pltpu.VMEM((2, t, HG_DK), F32),
                        pltpu.VMEM((2, HG_DV, HG_DK), F32), pltpu.VMEM((2, t, HG_DV), F32)],
        compiler_params=_params("arbitrary", "arbitrary"),
        name=f"hgrn_t{t}",
    )(*args)


def _conv_kernel(cb_ref, cc_ref, cx_ref, w_ref, b_ref, o_ref, *, t):
    p = cc_ref[...] * cx_ref[...]
    row = lax.broadcasted_iota(jnp.int32, (t, 1), 0)
    prev = jnp.where(row >= 1, pltpu.roll(p, 1, axis=0), 0.0)
    nxt = jnp.where(row <= t - 2, pltpu.roll(p, t - 1, axis=0), 0.0)
    w = w_ref[...]
    y = prev * w[0:1] + p * w[1:2] + nxt * w[2:3] + b_ref[...]
    o_ref[...] = (cb_ref[...] * y).astype(MIX_DTYPE)


def _conv(z, w, b, t, nb, row0):
    rb = row0 // t
    return pl.pallas_call(
        functools.partial(_conv_kernel, t=t),
        out_shape=jax.ShapeDtypeStruct((nb * t, GROUP_W), MIX_DTYPE),
        grid=(nb,),
        in_specs=[
            pl.BlockSpec((t, GROUP_W), lambda i: (rb + i, COL_CB // 4)),
            pl.BlockSpec((t, GROUP_W), lambda i: (rb + i, COL_CC // 4)),
            pl.BlockSpec((t, GROUP_W), lambda i: (rb + i, COL_CX // 4)),
            pl.BlockSpec((CONV_K, GROUP_W), lambda i: (0, 0)),
            pl.BlockSpec((1, GROUP_W), lambda i: (0, 0)),
        ],
        out_specs=pl.BlockSpec((t, GROUP_W), lambda i: (i, 0)),
        compiler_params=_params("arbitrary"),
        name=f"conv_t{t}",
    )(z, z, z, w, b.reshape(1, GROUP_W))


def _outproj_kernel(*refs, tm):
    prompt_refs, sample_refs = refs[0:4], refs[4:8]
    w_ref, xp_ref, xs_ref, g1_ref, sh_ref, sc_ref, n2_ref, x1_ref, h2_ref = refs[8:]
    i = pl.program_id(0)
    r = _mod_row(i * tm)

    def run(group_refs, x_ref):
        acc = None
        for g, ref in enumerate(group_refs):
            part = _dot(ref[...], w_ref[g * GROUP_W:(g + 1) * GROUP_W, :])
            acc = part if acc is None else acc + part
        x1 = x_ref[...] + g1_ref[pl.ds(r, 1), :] * acc
        x1_ref[...] = x1
        h2 = _rms(x1, n2_ref[...]) * (1.0 + sc_ref[pl.ds(r, 1), :]) + sh_ref[pl.ds(r, 1), :]
        h2_ref[...] = h2.astype(BF16)

    pl.when(i < N_PROMPT // tm)(lambda: run(prompt_refs, xp_ref))
    pl.when(i >= N_PROMPT // tm)(lambda: run(sample_refs, xs_ref))


def _outproj(prompt_groups, sample_groups, w_out_bf, layer, x_prompt, x_sample, mod, norm2_g):
    tm = OUTPROJ_TM
    n_pb, n_sb = N_PROMPT // tm, N_SAMPLE // tm
    p_spec = pl.BlockSpec((tm, GROUP_W), lambda i: (jnp.minimum(i, n_pb - 1), 0))
    s_spec = pl.BlockSpec((tm, GROUP_W), lambda i: (jnp.clip(i - n_pb, 0, n_sb - 1), 0))
    modspec = lambda k: pl.BlockSpec((MOD_ROWS, D_MODEL), lambda i: (0, k))
    return pl.pallas_call(
        functools.partial(_outproj_kernel, tm=tm),
        out_shape=(jax.ShapeDtypeStruct((N_TOK, D_MODEL), F32), jax.ShapeDtypeStruct((N_TOK, D_MODEL), BF16)),
        grid=(N_TOK // tm,),
        in_specs=[p_spec] * 4 + [s_spec] * 4 + [
            pl.BlockSpec((None, D_MODEL, D_MODEL), lambda i: (layer, 0, 0), pipeline_mode=pl.Buffered(1)),
            *_token_source_specs(tm, x_prompt, x_sample, 1),
            modspec(2), modspec(3), modspec(4),
            pl.BlockSpec((1, D_MODEL), lambda i: (0, 0))],
        out_specs=(pl.BlockSpec((tm, D_MODEL), lambda i: (i, 0)), pl.BlockSpec((tm, D_MODEL), lambda i: (i, 0))),
        compiler_params=_params("arbitrary", vmem=VMEM_LARGE),
        name="outproj",
    )(*prompt_groups, *sample_groups, w_out_bf, x_prompt, x_sample, mod, mod, mod, norm2_g.reshape(1, D_MODEL))


def _peer_query_kernel(h_ref, wq_ref, q_ref):
    q_ref[...] = _dot(h_ref[...], wq_ref[...]).astype(BF16)


def _peer_query(h2, wq_bf, layer):
    tm = QUERY_TM
    width = PEER_HEADS * PEER_QDIM
    return pl.pallas_call(
        _peer_query_kernel,
        out_shape=jax.ShapeDtypeStruct((N_TOK, width), BF16),
        grid=(N_TOK // tm,),
        in_specs=[pl.BlockSpec((tm, D_MODEL), lambda i: (i, 0)),
                  pl.BlockSpec((None, D_MODEL, width), lambda i: (layer, 0, 0))],
        out_specs=pl.BlockSpec((tm, width), lambda i: (i, 0)),
        compiler_params=_params("arbitrary"),
        name="peer_query",
    )(h2, wq_bf)


_CAND_RUNS = ((2, 5), (3, 4), (4, 3), (5, 2), (6, 2), (7, 2))
_CAND_RUN_ROWS = 24
_CAND_ROWS = 4 * 8 + _CAND_RUN_ROWS
_NO_CAND = 1 << 20


def _candidates(v1, v2, tm):
    sub = lax.broadcasted_iota(jnp.int32, (8, tm), 0)
    vals = [v1[0:1] + v2[0:8], v1[0:1] + v2[8:16], v1[1:2] + v2[0:8], v1[8:16] + v2[0:1]]
    ids = [sub, sub + 8, sub + PEER_TOPK, (sub + 8) * PEER_TOPK]
    row = lax.broadcasted_iota(jnp.int32, (_CAND_RUN_ROWS, tm), 0)
    v2_rep = jnp.concatenate([v2[0:8]] * (_CAND_RUN_ROWS // 8), axis=0)
    run_vals = jnp.full((_CAND_RUN_ROWS, tm), -jnp.inf, F32)
    run_ids = jnp.full((_CAND_RUN_ROWS, tm), _NO_CAND, jnp.int32)
    start = 0
    for r1, count in _CAND_RUNS:
        inside = (row >= start) & (row < start + count)
        shifted = v2_rep if start % 8 == 0 else pltpu.roll(v2_rep, start % 8, axis=0)
        run_vals = jnp.where(inside, v1[r1:r1 + 1] + shifted, run_vals)
        run_ids = jnp.where(inside, row + (r1 * PEER_TOPK - start), run_ids)
        start += count
    return jnp.concatenate(vals + [run_vals], axis=0), jnp.concatenate(ids + [run_ids], axis=0)


def _topk_rows(x_scr, ids, n_rows, v_out, i_out):
    def body(j, carry):
        x = x_scr[0:n_rows, :]
        m = jnp.max(x, axis=0, keepdims=True)
        idx = jnp.min(jnp.where(x == m, ids, _NO_CAND), axis=0, keepdims=True)
        v_out[pl.ds(j, 1), :] = m
        i_out[pl.ds(j, 1), :] = idx
        x_scr[0:n_rows, :] = jnp.where(ids == idx, -jnp.inf, x)
        return carry

    lax.fori_loop(0, PEER_TOPK, body, 0, unroll=2)


def _peer_topk_kernel(q_ref, k1_ref, k2_ref, u_ref, v_ref, e_ref, g_ref, ub_ref, vb_ref, x_scr, v1_scr, i1_scr,
                      v2_scr, i2_scr, ts_scr, tc_scr, *, tm):
    ub_ref[...] = u_ref[...].astype(BF16)
    vb_ref[...] = v_ref[...].astype(BF16)
    key_ids = lax.broadcasted_iota(jnp.int32, (N_KEYS, tm), 0)
    for half, (k_ref, v_scr, i_scr) in enumerate(((k1_ref, v1_scr, i1_scr), (k2_ref, v2_scr, i2_scr))):
        qh = q_ref[:, half * PEER_HALF:(half + 1) * PEER_HALF]
        x_scr[...] = lax.dot_general(k_ref[...], qh, NT_DIMS, preferred_element_type=F32)
        _topk_rows(x_scr, key_ids, N_KEYS, v_scr, i_scr)
    cand_vals, cand_ids = _candidates(v1_scr[...], v2_scr[...], tm)
    x_scr[0:_CAND_ROWS, :] = cand_vals
    _topk_rows(x_scr, cand_ids, _CAND_ROWS, ts_scr, tc_scr)
    tc = tc_scr[...]
    c1, c2 = tc // PEER_TOPK, tc % PEER_TOPK
    i1, i2 = i1_scr[...], i2_scr[...]
    e1 = jnp.zeros_like(tc)
    e2 = jnp.zeros_like(tc)
    for a in range(PEER_TOPK):
        e1 = jnp.where(c1 == a, i1[a:a + 1], e1)
        e2 = jnp.where(c2 == a, i2[a:a + 1], e2)
    e_ref[...] = e1 * N_KEYS + e2
    ts = ts_scr[...]
    p = jnp.exp(ts - jnp.max(ts, axis=0, keepdims=True))
    g_ref[...] = p / jnp.sum(p, axis=0, keepdims=True)


def _peer_topk(q, k1_bf, k2_bf, peer_u, peer_v, layer):
    tm = TOPK_TM
    slab = LANES
    n_slabs = N_EXPERTS // slab
    assert (N_TOK // tm) * PEER_HEADS >= n_slabs
    row_f = pltpu.VMEM((PEER_TOPK, tm), F32)
    row_i = pltpu.VMEM((PEER_TOPK, tm), jnp.int32)
    slab_of = lambda i, h: jnp.minimum(i * PEER_HEADS + h, n_slabs - 1)
    table_in = pl.BlockSpec((None, slab, D_MODEL), lambda i, h: (layer, slab_of(i, h), 0))
    table_out = pl.BlockSpec((slab, D_MODEL), lambda i, h: (slab_of(i, h), 0))
    table_shape = jax.ShapeDtypeStruct((N_EXPERTS, D_MODEL), BF16)
    return pl.pallas_call(
        functools.partial(_peer_topk_kernel, tm=tm),
        out_shape=(jax.ShapeDtypeStruct((PEER_HEADS * PEER_TOPK, N_TOK), jnp.int32),
                   jax.ShapeDtypeStruct((PEER_HEADS * PEER_TOPK, N_TOK), F32), table_shape, table_shape),
        grid=(N_TOK // tm, PEER_HEADS),
        in_specs=[
            pl.BlockSpec((tm, PEER_QDIM), lambda i, h: (i, h)),
            pl.BlockSpec((None, N_KEYS, PEER_HALF), lambda i, h: (layer, 0, 0)),
            pl.BlockSpec((None, N_KEYS, PEER_HALF), lambda i, h: (layer, 0, 0)),
            table_in, table_in,
        ],
        out_specs=(pl.BlockSpec((PEER_TOPK, tm), lambda i, h: (h, i)),
                   pl.BlockSpec((PEER_TOPK, tm), lambda i, h: (h, i)), table_out, table_out),
        scratch_shapes=[pltpu.VMEM((N_KEYS, tm), F32), row_f, row_i, row_f, row_i, row_f, row_i],
        compiler_params=_params("arbitrary", "arbitrary"),
        name="peer_topk",
    )(q, k1_bf, k2_bf, peer_u, peer_v)


def _peer_mask_kernel(e_ref, g_ref, m_ref, et_scr, gt_scr, slab_scr, *, tb):
    et_scr[...] = e_ref[...].T
    gt_scr[...] = g_ref[...].T
    key = lax.broadcasted_iota(jnp.int32, (N_KEYS, PEER_HEADS * PEER_TOPK), 0)

    def body(t, carry):
        er = et_scr[pl.ds(t, 1), :]
        gr = gt_scr[pl.ds(t, 1), :]
        p1 = jnp.where(key == er // N_KEYS, gr, 0.0).astype(BF16)
        p2 = jnp.where(key == er % N_KEYS, 1.0, 0.0).astype(BF16)
        mt = lax.dot_general(p1, p2, NT_DIMS, preferred_element_type=F32)
        slab_scr[pl.ds(pl.multiple_of(t * MASK_PITCH, 4), N_KEYS), :] = mt
        return carry

    lax.fori_loop(0, tb, body, 0, unroll=64)
    for k1 in range(N_KEYS):
        m_ref[:, k1 * N_KEYS:(k1 + 1) * N_KEYS] = slab_scr[pl.ds(k1, tb, stride=MASK_PITCH), :].astype(BF16)


def _peer_mask(experts_t, gates_t):
    tb = MASK_TOKENS
    pairs = PEER_HEADS * PEER_TOPK
    return pl.pallas_call(
        functools.partial(_peer_mask_kernel, tb=tb),
        out_shape=jax.ShapeDtypeStruct((N_TOK, N_EXPERTS), BF16),
        grid=(N_TOK // tb,),
        in_specs=[pl.BlockSpec((pairs, tb), lambda i: (0, i)), pl.BlockSpec((pairs, tb), lambda i: (0, i))],
        out_specs=pl.BlockSpec((tb, N_EXPERTS), lambda i: (i, 0)),
        scratch_shapes=[pltpu.VMEM((tb, pairs), jnp.int32), pltpu.VMEM((tb, pairs), F32),
                        pltpu.VMEM((tb * MASK_PITCH, N_KEYS), F32)],
        compiler_params=_params("arbitrary"),
        name="peer_mask",
    )(experts_t, gates_t)


def _peer_expert_kernel(h_ref, u_ref, v_ref, m_ref, x_ref, g2_ref, o_ref, *, tm):
    j = pl.program_id(1)

    @pl.when(j == 0)
    def _():
        o_ref[...] = jnp.zeros_like(o_ref)

    s = lax.dot_general(h_ref[...], u_ref[...], NT_DIMS, preferred_element_type=F32)
    a = (m_ref[...].astype(F32) * jax.nn.gelu(s)).astype(BF16)
    o_ref[...] += _dot(a, v_ref[...])

    @pl.when(j == pl.num_programs(1) - 1)
    def _():
        r = _mod_row(pl.program_id(0) * tm)
        o_ref[...] = x_ref[...] + g2_ref[pl.ds(r, 1), :] * o_ref[...]


def _peer_expert(h2, u_bf, v_bf, mask, x1, mod):
    tm, te = EXPERT_TM, EXPERT_TE
    return pl.pallas_call(
        functools.partial(_peer_expert_kernel, tm=tm),
        out_shape=jax.ShapeDtypeStruct((N_TOK, D_MODEL), F32),
        grid=(N_TOK // tm, N_EXPERTS // te),
        in_specs=[
            pl.BlockSpec((tm, D_MODEL), lambda i, j: (i, 0)),
            pl.BlockSpec((te, D_MODEL), lambda i, j: (j, 0)),
            pl.BlockSpec((te, D_MODEL), lambda i, j: (j, 0)),
            pl.BlockSpec((tm, te), lambda i, j: (i, j)),
            pl.BlockSpec((tm, D_MODEL), lambda i, j: (i, 0)),
            pl.BlockSpec((MOD_ROWS, D_MODEL), lambda i, j: (0, 5)),
        ],
        out_specs=pl.BlockSpec((tm, D_MODEL), lambda i, j: (i, 0)),
        compiler_params=_params("arbitrary", "arbitrary", vmem=VMEM_LARGE),
        name="peer_expert",
    )(h2, u_bf, v_bf, mask, x1, mod)


def kernel(x_prompt, x_sample, cache_k, cache_v, state_hgrn, c, c_ctx, ada_w, ada_b, norm1_g, norm2_g, w_in,
           q_norm_g, k_norm_g, hg_lb, hg_onorm_g, conv_w, conv_b, w_out, peer_wq, peer_k1, peer_k2, peer_u, peer_v):
    x_p, x_s = x_prompt.reshape(N_PROMPT, D_MODEL), x_sample.reshape(N_SAMPLE, D_MODEL)
    cvec = jnp.concatenate([c_ctx[None, :], c, jnp.zeros((MOD_ROWS - 1 - DEC_BATCH, D_MODEL), F32)], axis=0)
    mod_all = _ada(cvec, ada_w, ada_b)

    w_out_bf, wq_bf = w_out.astype(BF16), peer_wq.astype(BF16)
    k1_bf, k2_bf = peer_k1.astype(BF16), peer_k2.astype(BF16)

    ks_out, vs_out, ss_out = [], [], []
    for l in range(DEPTH):
        mod = mod_all[l]
        z = _inproj(x_p, x_s, mod, norm1_g[l], w_in, l)

        fn_p = _fourier(z, SEQ, BATCH, 0)
        fn_s = _fourier(z, DEC_SEQ, DEC_BATCH, N_PROMPT)

        at_p, k_l, v_l = _attn_prompt(z, q_norm_g[l], k_norm_g[l])
        k_ctx = cache_k[:, l].reshape(DEC_BATCH, PAST_LEN, N_KV * HEAD_DIM)
        v_ctx = cache_v[:, l].reshape(DEC_BATCH, PAST_LEN, N_KV * HEAD_DIM)
        at_s = _attn_sample(z, q_norm_g[l], k_norm_g[l], k_ctx, v_ctx)

        hg_p, s_l = _hgrn(z, hg_lb, hg_onorm_g[l], l, SEQ, BATCH, 0, None)
        hg_s, _ = _hgrn(z, hg_lb, hg_onorm_g[l], l, DEC_SEQ, DEC_BATCH, N_PROMPT, state_hgrn[:, l])

        cv_p = _conv(z, conv_w[l], conv_b[l], SEQ, BATCH, 0)
        cv_s = _conv(z, conv_w[l], conv_b[l], DEC_SEQ, DEC_BATCH, N_PROMPT)

        x1, h2 = _outproj((fn_p, at_p, hg_p, cv_p), (fn_s, at_s, hg_s, cv_s), w_out_bf, l, x_p, x_s, mod,
                          norm2_g[l])

        q_peer = _peer_query(h2, wq_bf, l)
        experts_t, gates_t, u_bf, v_bf = _peer_topk(q_peer, k1_bf, k2_bf, peer_u, peer_v, l)
        mask = _peer_mask(experts_t, gates_t)
        x = _peer_expert(h2, u_bf, v_bf, mask, x1, mod)
        x_p = x_s = x

        ks_out.append(k_l.reshape(BATCH, SEQ, N_KV, HEAD_DIM))
        vs_out.append(v_l.reshape(BATCH, SEQ, N_KV, HEAD_DIM))
        ss_out.append(s_l)

    y_prompt = x[:N_PROMPT].reshape(BATCH, SEQ, D_MODEL)
    y_sample = x[N_PROMPT:].reshape(DEC_BATCH, DEC_SEQ, D_MODEL)
    return (y_prompt, y_sample, jnp.stack(ks_out, axis=1), jnp.stack(vs_out, axis=1), jnp.stack(ss_out, axis=1))
```

```python
import functools

import numpy as np
import jax
import jax.numpy as jnp
from jax import lax
from jax.experimental import pallas as pl
from jax.experimental.pallas import tpu as pltpu

F32 = jnp.float32
BF16 = jnp.bfloat16
MIX_DTYPE = BF16

D_MODEL = 2048
BATCH = 32
SEQ = 256
DEPTH = 2
DEC_BATCH = 4
DEC_SEQ = 1024
PAST_LEN = 256
GRID_W = 64
GROUP_W = D_MODEL // 4
FN_GROUPS = 4
FN_GW = GROUP_W // FN_GROUPS
HEAD_DIM = 128
N_HEADS = GROUP_W // HEAD_DIM
N_KV = 2
GQA_GROUP = N_HEADS // N_KV
ROPE_THETA = 10000.0
HG_DK = 128
HG_DV = 128
HG_HEADS = GROUP_W // HG_DK
CONV_K = 3
IN_COLS = 11 * GROUP_W
PEER_HEADS = 8
N_KEYS = 128
N_EXPERTS = N_KEYS * N_KEYS
PEER_QDIM = 256
PEER_HALF = PEER_QDIM // 2
PEER_TOPK = 16
EPS = 1e-6
F_FLOOR = 1e-30

N_PROMPT = BATCH * SEQ
N_SAMPLE = DEC_BATCH * DEC_SEQ
N_TOK = N_PROMPT + N_SAMPLE
MOD_ROWS = 8
LANES = 128
NORM_ROWS = 128
HG_CHUNK = 32
HG_GROUP = 8
HG_UNROLL = 16
HG_CUM_BLOCK = 256
MASK_TOKENS = 128
MASK_PITCH = 132

ADA_TN = 1024
INPROJ_TM, INPROJ_TN = 1024, 512
OUTPROJ_TM = 512
QUERY_TM = 512
TOPK_TM = 512
EXPERT_TM, EXPERT_TE = 1024, 512
ATTN_Q_ROWS = 256

VMEM_SMALL = 32 * 1024 * 1024
VMEM_INPROJ = 48 * 1024 * 1024
VMEM_LARGE = 56 * 1024 * 1024

COL_FN, COL_Q, COL_K, COL_V = 0, 4, 8, 10
COL_GQ, COL_GFF, COL_GFB, COL_GI, COL_GO = 12, 16, 20, 24, 28
COL_CB, COL_CC, COL_CX = 32, 36, 40

NT_DIMS = (((1,), (1,)), ((), ()))
TN_DIMS = (((0,), (0,)), ((), ()))


def _params(*sem, vmem=VMEM_SMALL):
    return pltpu.CompilerParams(dimension_semantics=sem, vmem_limit_bytes=vmem)


def _dot(a, b):
    return jnp.dot(a, b, preferred_element_type=F32)


def _split3(x):
    h = x.astype(BF16)
    r = x - h.astype(F32)
    m = r.astype(BF16)
    l = (r - m.astype(F32)).astype(BF16)
    return h, m, l


def _split2(x):
    h = x.astype(BF16)
    return h, (x - h.astype(F32)).astype(BF16)


def _mod_row(tok0):
    return jnp.where(tok0 >= N_PROMPT, (tok0 - N_PROMPT) // DEC_SEQ + 1, 0)


def _rms(x, g):
    return x * lax.rsqrt(jnp.mean(x * x, axis=-1, keepdims=True) + EPS) * g


def _ada_kernel(c_ref, w_ref, b_ref, o_ref):
    c = c_ref[...]
    s = (c * jax.nn.sigmoid(c)).astype(BF16)
    o_ref[0] = _dot(s, w_ref[0].astype(BF16)) + b_ref[0]


def _ada(cvec, ada_w, ada_b):
    tn = ADA_TN
    return pl.pallas_call(
        _ada_kernel,
        out_shape=jax.ShapeDtypeStruct((DEPTH, MOD_ROWS, 6 * D_MODEL), F32),
        grid=(DEPTH, 6 * D_MODEL // tn),
        in_specs=[
            pl.BlockSpec((MOD_ROWS, D_MODEL), lambda l, j: (0, 0)),
            pl.BlockSpec((1, D_MODEL, tn), lambda l, j: (l, 0, j)),
            pl.BlockSpec((1, 1, tn), lambda l, j: (l, 0, j)),
        ],
        out_specs=pl.BlockSpec((1, MOD_ROWS, tn), lambda l, j: (l, 0, j)),
        compiler_params=_params("arbitrary", "arbitrary"),
        name="ada",
    )(cvec, ada_w, ada_b.reshape(DEPTH, 1, 6 * D_MODEL))


def _token_source_specs(tm, x_prompt, x_sample, n_grid_axes):
    n_pb, n_sb = N_PROMPT // tm, N_SAMPLE // tm
    s_off = x_sample.shape[0] // tm - n_sb
    if n_grid_axes == 1:
        return (pl.BlockSpec((tm, D_MODEL), lambda i: (jnp.minimum(i, n_pb - 1), 0)),
                pl.BlockSpec((tm, D_MODEL), lambda i: (s_off + jnp.clip(i - n_pb, 0, n_sb - 1), 0)))
    return (pl.BlockSpec((tm, D_MODEL), lambda i, j: (jnp.minimum(i, n_pb - 1), 0)),
            pl.BlockSpec((tm, D_MODEL), lambda i, j: (s_off + jnp.clip(i - n_pb, 0, n_sb - 1), 0)))


def _inproj_kernel(xp_ref, xs_ref, sh_ref, sc_ref, g_ref, w_ref, o_ref, h_ref, *, tm):
    i, j = pl.program_id(0), pl.program_id(1)

    def prologue(x_ref):
        r = _mod_row(i * tm)
        gain = g_ref[...] * (1.0 + sc_ref[pl.ds(r, 1), :])
        shift = sh_ref[pl.ds(r, 1), :]

        def norm_rows(c, carry):
            rows = pl.ds(pl.multiple_of(c * NORM_ROWS, NORM_ROWS), NORM_ROWS)
            x = x_ref[rows, :]
            y = x * lax.rsqrt(jnp.mean(x * x, axis=-1, keepdims=True) + EPS)
            h_ref[rows, :] = (y * gain + shift).astype(BF16)
            return carry

        lax.fori_loop(0, tm // NORM_ROWS, norm_rows, 0)

    pl.when((j == 0) & (i < N_PROMPT // tm))(lambda: prologue(xp_ref))
    pl.when((j == 0) & (i >= N_PROMPT // tm))(lambda: prologue(xs_ref))
    o_ref[...] = _dot(h_ref[...], w_ref[...])


def _inproj(x_prompt, x_sample, mod, norm_g, w_in_bf, layer):
    tm, tn = INPROJ_TM, INPROJ_TN
    return pl.pallas_call(
        functools.partial(_inproj_kernel, tm=tm),
        out_shape=jax.ShapeDtypeStruct((N_TOK, IN_COLS), F32),
        grid=(N_TOK // tm, IN_COLS // tn),
        in_specs=[
            *_token_source_specs(tm, x_prompt, x_sample, 2),
            pl.BlockSpec((MOD_ROWS, D_MODEL), lambda i, j: (0, 0)),
            pl.BlockSpec((MOD_ROWS, D_MODEL), lambda i, j: (0, 1)),
            pl.BlockSpec((1, D_MODEL), lambda i, j: (0, 0)),
            pl.BlockSpec((None, D_MODEL, tn), lambda i, j: (layer, 0, j)),
        ],
        out_specs=pl.BlockSpec((tm, tn), lambda i, j: (i, j)),
        scratch_shapes=[pltpu.VMEM((tm, D_MODEL), BF16)],
        compiler_params=_params("arbitrary", "arbitrary", vmem=VMEM_INPROJ),
        name="inproj",
    )(x_prompt, x_sample, mod, mod, norm_g.reshape(1, D_MODEL), w_in_bf)


def _fourier_kernel(u_ref, cch_ref, ccl_ref, cth_ref, ctl_ref, o_ref):
    cch, ccl, cth, ctl = cch_ref[...], ccl_ref[...], cth_ref[...], ctl_ref[...]
    for g in range(0, FN_GROUPS, 2):
        ws = []
        for gg in (g, g + 1):
            uh, ul = _split2(u_ref[:, gg * FN_GW:(gg + 1) * FN_GW])
            ws.append(_dot(uh, cch) + (_dot(ul, cch) + _dot(uh, ccl)))
        st = jnp.concatenate([jnp.concatenate([w[:, :FN_GW] for w in ws], axis=1),
                              jnp.concatenate([w[:, FN_GW:] for w in ws], axis=1)], axis=0)
        sh, sl = _split2(st)
        o_ref[:, g * FN_GW:(g + 2) * FN_GW] = (_dot(cth, sh) + (_dot(cth, sl) + _dot(ctl, sh))).astype(MIX_DTYPE)


def _dft_consts(t):
    def cs(n):
        k = np.arange(n, dtype=np.float64)
        ang = 2.0 * np.pi * np.outer(k, k) / n
        return np.cos(ang) / np.sqrt(n), np.sin(ang) / np.sqrt(n)

    cc, sc = cs(FN_GW)
    ct, st = cs(t)
    right = np.concatenate([cc, sc], axis=1).astype(np.float32)
    left = np.concatenate([ct, -st], axis=1).astype(np.float32)

    def hl(a):
        a = jnp.asarray(a)
        h = a.astype(BF16)
        return h, (a - h.astype(F32)).astype(BF16)

    return hl(right) + hl(left)


def _fourier(z, t, nb, row0):
    cch, ccl, cth, ctl = _dft_consts(t)
    rb = row0 // t
    return pl.pallas_call(
        _fourier_kernel,
        out_shape=jax.ShapeDtypeStruct((nb * t, GROUP_W), MIX_DTYPE),
        grid=(nb,),
        in_specs=[
            pl.BlockSpec((t, GROUP_W), lambda b: (rb + b, COL_FN // 4)),
            pl.BlockSpec((FN_GW, 2 * FN_GW), lambda b: (0, 0)),
            pl.BlockSpec((FN_GW, 2 * FN_GW), lambda b: (0, 0)),
            pl.BlockSpec((t, 2 * t), lambda b: (0, 0)),
            pl.BlockSpec((t, 2 * t), lambda b: (0, 0)),
        ],
        out_specs=pl.BlockSpec((t, GROUP_W), lambda b: (b, 0)),
        compiler_params=_params("arbitrary"),
        name=f"fourier_t{t}",
    )(z, cch, ccl, cth, ctl)


def _softmax_pv(q_bf, k_bf, v_bf):
    s = lax.dot_general(q_bf, k_bf, NT_DIMS, preferred_element_type=F32) * (HEAD_DIM ** -0.5)
    p = jnp.exp(s - jnp.max(s, axis=-1, keepdims=True))
    return _dot(p.astype(BF16), v_bf) / jnp.sum(p, axis=-1, keepdims=True)


def _attn_prompt_kernel(*refs, layer):
    q_ref, k_ref, v_ref, qg_ref, kg_ref = refs[:5]
    o_ref, kc_ref, vc_ref = refs[-3:]
    kn = _rms(k_ref[...], kg_ref[...])
    v = v_ref[...]
    if layer:
        kprev_ref, vprev_ref = refs[5:7]
        kc_ref[0, 0:layer] = kprev_ref[0]
        vc_ref[0, 0:layer] = vprev_ref[0]
    head_rows = pl.ds(pl.program_id(1), SEQ, stride=N_KV)
    kc_ref[0, layer, head_rows, :] = kn
    vc_ref[0, layer, head_rows, :] = v
    kb, vb = kn.astype(BF16), v.astype(BF16)
    for g in range(GQA_GROUP):
        qn = _rms(q_ref[:, g * HEAD_DIM:(g + 1) * HEAD_DIM], qg_ref[...])
        o_ref[:, g * HEAD_DIM:(g + 1) * HEAD_DIM] = _softmax_pv(qn.astype(BF16), kb, vb).astype(MIX_DTYPE)


def _attn_prompt(z, q_g, k_g, layer, k_prev, v_prev):
    t = SEQ
    kv_shape = jax.ShapeDtypeStruct((BATCH, layer + 1, SEQ * N_KV, HEAD_DIM), F32)
    kv_spec = pl.BlockSpec((1, layer + 1, SEQ * N_KV, HEAD_DIM), lambda b, h: (b, 0, 0, 0))
    prev_spec = pl.BlockSpec((1, layer, SEQ * N_KV, HEAD_DIM), lambda b, h: (b, 0, 0, 0))
    prev = [k_prev, v_prev] if layer else []
    return pl.pallas_call(
        functools.partial(_attn_prompt_kernel, layer=layer),
        out_shape=(jax.ShapeDtypeStruct((N_PROMPT, GROUP_W), MIX_DTYPE), kv_shape, kv_shape),
        grid=(BATCH, N_KV),
        in_specs=[
            pl.BlockSpec((t, GQA_GROUP * HEAD_DIM), lambda b, h: (b, COL_Q // 2 + h)),
            pl.BlockSpec((t, HEAD_DIM), lambda b, h: (b, COL_K + h)),
            pl.BlockSpec((t, HEAD_DIM), lambda b, h: (b, COL_V + h)),
            pl.BlockSpec((1, HEAD_DIM), lambda b, h: (0, 0)),
            pl.BlockSpec((1, HEAD_DIM), lambda b, h: (0, 0)),
        ] + [prev_spec] * len(prev),
        out_specs=(
            pl.BlockSpec((t, GQA_GROUP * HEAD_DIM), lambda b, h: (b, h)),
            kv_spec, kv_spec,
        ),
        compiler_params=_params("arbitrary", "arbitrary"),
        name="attn_prompt",
    )(z, z, z, q_g.reshape(1, HEAD_DIM), k_g.reshape(1, HEAD_DIM), *prev)


def _rope(x, cosf, sinf):
    return x * cosf + pltpu.roll(x, HEAD_DIM // 2, axis=1) * sinf


def _attn_sample_kernel(q_ref, k_ref, v_ref, kctx_ref, vctx_ref, qg_ref, kg_ref, cos_ref, sin_ref,
                        o_ref, kall_ref, vall_ref, *, qb):
    cosf, sinf = cos_ref[...], sin_ref[...]
    kall_ref[0:PAST_LEN, :] = kctx_ref[0].astype(BF16)
    vall_ref[0:PAST_LEN, :] = vctx_ref[0].astype(BF16)
    kall_ref[PAST_LEN:, :] = _rope(_rms(k_ref[...], kg_ref[...]), cosf, sinf).astype(BF16)
    vall_ref[PAST_LEN:, :] = v_ref[...].astype(BF16)
    kb, vb = kall_ref[...], vall_ref[...]
    for i in range(DEC_SEQ // qb):
        rows = slice(i * qb, (i + 1) * qb)
        for g in range(GQA_GROUP):
            cols = slice(g * HEAD_DIM, (g + 1) * HEAD_DIM)
            qn = _rope(_rms(q_ref[rows, cols], qg_ref[...]), cosf[rows], sinf[rows])
            o_ref[rows, cols] = _softmax_pv(qn.astype(BF16), kb, vb).astype(MIX_DTYPE)


def _rope_tables():
    rows = DEC_SEQ // GRID_W
    row = np.repeat(np.arange(rows, dtype=np.float32), GRID_W)
    col = np.tile(np.arange(GRID_W, dtype=np.float32), rows)
    n_freq = HEAD_DIM // 4
    inv = (np.float32(ROPE_THETA) ** (-np.arange(n_freq, dtype=np.float32) / n_freq)).astype(np.float32)
    ang = np.concatenate([row[:, None] * inv, col[:, None] * inv], axis=-1).astype(np.float32)
    cos, sin = np.cos(ang.astype(np.float64)), np.sin(ang.astype(np.float64))
    cosf = np.concatenate([cos, cos], axis=-1).astype(np.float32)
    sinf = np.concatenate([-sin, sin], axis=-1).astype(np.float32)
    return jnp.asarray(cosf), jnp.asarray(sinf)


def _attn_sample(z, q_g, k_g, k_ctx, v_ctx):
    t = DEC_SEQ
    rb = N_PROMPT // t
    cosf, sinf = _rope_tables()
    const = lambda b, h: (0, 0)
    return pl.pallas_call(
        functools.partial(_attn_sample_kernel, qb=ATTN_Q_ROWS),
        out_shape=jax.ShapeDtypeStruct((N_SAMPLE, GROUP_W), MIX_DTYPE),
        grid=(DEC_BATCH, N_KV),
        in_specs=[
            pl.BlockSpec((t, GQA_GROUP * HEAD_DIM), lambda b, h: (rb + b, COL_Q // 2 + h)),
            pl.BlockSpec((t, HEAD_DIM), lambda b, h: (rb + b, COL_K + h)),
            pl.BlockSpec((t, HEAD_DIM), lambda b, h: (rb + b, COL_V + h)),
            pl.BlockSpec((1, PAST_LEN, HEAD_DIM), lambda b, h: (b, 0, h)),
            pl.BlockSpec((1, PAST_LEN, HEAD_DIM), lambda b, h: (b, 0, h)),
            pl.BlockSpec((1, HEAD_DIM), const),
            pl.BlockSpec((1, HEAD_DIM), const),
            pl.BlockSpec((t, HEAD_DIM), const),
            pl.BlockSpec((t, HEAD_DIM), const),
        ],
        out_specs=pl.BlockSpec((t, GQA_GROUP * HEAD_DIM), lambda b, h: (b, h)),
        scratch_shapes=[pltpu.VMEM((PAST_LEN + t, HEAD_DIM), BF16), pltpu.VMEM((PAST_LEN + t, HEAD_DIM), BF16)],
        compiler_params=_params("arbitrary", "arbitrary"),
        name="attn_sample",
    )(z, z, z, k_ctx, v_ctx, q_g.reshape(1, HEAD_DIM), k_g.reshape(1, HEAD_DIM), cosf, sinf)


def _hgrn_chain(rev, r0, gq_ref, gi_ref, k_scr, cum_scr, st_scr, o_scr):
    c = HG_CHUNK
    d = 1 if rev else 0
    rows = pl.ds(r0, c)
    qc, vc = gq_ref[rows, :], gi_ref[rows, :]
    kc, cc = k_scr[d, rows, :], cum_scr[d, rows, :]
    st = st_scr[d]
    last = cc[0:1] if rev else cc[c - 1:c]
    qs = (qc * jnp.exp(cc)).astype(BF16)
    o_inter = lax.dot_general(qs, st.astype(BF16), NT_DIMS, preferred_element_type=F32)
    ks = (kc * jnp.exp(last - cc)).astype(BF16)
    upd = lax.dot_general(vc.astype(BF16), ks, TN_DIMS, preferred_element_type=F32)
    st_scr[d] = st * jnp.exp(last) + upd
    n_grp = c // HG_GROUP
    row = lax.broadcasted_iota(jnp.int32, (HG_GROUP, 1), 0)
    krow = lax.broadcasted_iota(jnp.int32, (c, 1), 0)
    diag, att = [], []
    for i in range(n_grp):
        lo = i * HG_GROUP
        grp = slice(lo, lo + HG_GROUP)
        q_i, k_i, v_i, c_i = qc[grp], kc[grp], vc[grp], cc[grp]
        acc = jnp.zeros((HG_GROUP, HG_DV), F32)
        for delta in range(HG_GROUP):
            if delta == 0:
                k_sh, c_sh, v_sh = k_i, c_i, v_i
            else:
                sh = (HG_GROUP - delta) if rev else delta
                k_sh, c_sh, v_sh = (pltpu.roll(a, sh, axis=0) for a in (k_i, c_i, v_i))
            a = jnp.sum(q_i * k_sh * jnp.exp(jnp.minimum(c_i - c_sh, 0.0)), axis=-1, keepdims=True)
            valid = (row <= HG_GROUP - 1 - delta) if rev else (row >= delta)
            acc = acc + jnp.where(valid, a, 0.0) * v_sh
        diag.append(acc)
        if (i == n_grp - 1) if rev else (i == 0):
            att.append(jnp.zeros((HG_GROUP, c), F32))
            continue
        if rev:
            b, other = cc[lo + HG_GROUP:lo + HG_GROUP + 1], krow >= lo + HG_GROUP
        else:
            b, other = cc[lo - 1:lo], krow < lo
        qs_i = (q_i * jnp.exp(c_i - b)).astype(BF16)
        ks_i = jnp.where(other, kc * jnp.exp(jnp.minimum(b - cc, 0.0)), 0.0).astype(BF16)
        att.append(lax.dot_general(qs_i, ks_i, NT_DIMS, preferred_element_type=F32))
    o_off = _dot(jnp.concatenate(att, axis=0).astype(BF16), vc.astype(BF16))
    o_scr[d, rows, :] = o_inter + jnp.concatenate(diag, axis=0) + o_off


def _hgrn_kernel(*refs, layer, t, has_ctx):
    gq_ref, gff_ref, gfb_ref, gi_ref, go_ref, lb_ref, on_ref, bdf_ref, bdb_ref = refs[:9]
    n_in = 9
    s0_ref = None
    if has_ctx:
        s0_ref = refs[9]
        n_in = 10
    o_ref, sout_ref = refs[n_in], refs[n_in + 1]
    k_scr, cum_scr, st_scr, o_scr = refs[n_in + 2:]

    raw = lb_ref[...]
    e = jnp.exp(raw - jnp.max(raw, axis=0, keepdims=True))
    soft = e / jnp.sum(e, axis=0, keepdims=True)
    csum = soft[0]
    for i in range(1, layer + 1):
        csum = csum + soft[i]
    lb = csum - soft[0]

    for d, (g_ref, bd_ref) in enumerate(((gff_ref, bdf_ref), (gfb_ref, bdb_ref))):
        lbd = lb[d:d + 1]
        f = jnp.maximum(lbd + (1.0 - lbd) * jax.nn.sigmoid(g_ref[...]), F_FLOOR)
        k_scr[d] = 1.0 - f
        lf = jnp.log(f)
        bd = bd_ref[...]
        for p in range(t // HG_CUM_BLOCK):
            rows = slice(p * HG_CUM_BLOCK, (p + 1) * HG_CUM_BLOCK)
            h, m, l = _split3(lf[rows])
            cum_scr[d, rows, :] = _dot(bd, h) + _dot(bd, m) + _dot(bd, l)
        if has_ctx:
            st_scr[d] = s0_ref[0, d, 0].T
        else:
            st_scr[d] = jnp.zeros((HG_DV, HG_DK), F32)

    n = t // HG_CHUNK

    def body(i, carry):
        _hgrn_chain(False, pl.multiple_of(i * HG_CHUNK, HG_CHUNK), gq_ref, gi_ref, k_scr, cum_scr, st_scr, o_scr)
        _hgrn_chain(True, pl.multiple_of((n - 1 - i) * HG_CHUNK, HG_CHUNK), gq_ref, gi_ref, k_scr, cum_scr,
                    st_scr, o_scr)
        return carry

    lax.fori_loop(0, n, body, 0, unroll=min(n, HG_UNROLL))

    o = _rms(o_scr[0] + o_scr[1], on_ref[...])
    go = go_ref[...]
    o_ref[...] = (o * (go * jax.nn.sigmoid(go))).astype(MIX_DTYPE)
    for d in range(2):
        sout_ref[0, d, 0] = st_scr[d].T


def _cum_consts():
    r = np.arange(HG_CUM_BLOCK)
    same = (r[:, None] // HG_CHUNK) == (r[None, :] // HG_CHUNK)
    fwd = same & (r[None, :] <= r[:, None])
    bwd = same & (r[None, :] >= r[:, None])
    return jnp.asarray(fwd, BF16), jnp.asarray(bwd, BF16)


def _hgrn(z, hg_lb, onorm_g, layer, t, nb, row0, s0):
    rb = row0 // t
    bdf, bdb = _cum_consts()
    has_ctx = s0 is not None
    col = lambda c0: (lambda b, h: (rb + b, c0 + h))
    in_specs = [
        pl.BlockSpec((t, HG_DK), col(COL_GQ)),
        pl.BlockSpec((t, HG_DK), col(COL_GFF)),
        pl.BlockSpec((t, HG_DK), col(COL_GFB)),
        pl.BlockSpec((t, HG_DV), col(COL_GI)),
        pl.BlockSpec((t, HG_DV), col(COL_GO)),
        pl.BlockSpec((DEPTH, 2, HG_DK), lambda b, h: (0, 0, h)),
        pl.BlockSpec((1, HG_DV), lambda b, h: (0, 0)),
        pl.BlockSpec((HG_CUM_BLOCK, HG_CUM_BLOCK), lambda b, h: (0, 0)),
        pl.BlockSpec((HG_CUM_BLOCK, HG_CUM_BLOCK), lambda b, h: (0, 0)),
    ]
    args = [z, z, z, z, z, hg_lb, onorm_g.reshape(1, HG_DV), bdf, bdb]
    if has_ctx:
        in_specs.append(pl.BlockSpec((1, 2, 1, HG_DK, HG_DV), lambda b, h: (b, 0, h, 0, 0)))
        args.append(s0)
    return pl.pallas_call(
        functools.partial(_hgrn_kernel, layer=layer, t=t, has_ctx=has_ctx),
        out_shape=(jax.ShapeDtypeStruct((nb * t, GROUP_W), MIX_DTYPE),
                   jax.ShapeDtypeStruct((nb, 2, HG_HEADS, HG_DK, HG_DV), F32)),
        grid=(nb, HG_HEADS),
        in_specs=in_specs,
        out_specs=(pl.BlockSpec((t, HG_DV), lambda b, h: (b, h)),
                   pl.BlockSpec((1, 2, 1, HG_DK, HG_DV), lambda b, h: (b, 0, h, 0, 0))),
        scratch_shapes=[pltpu.VMEM((2, t, HG_DK), F32), pltpu.VMEM((2, t, HG_DK), F32),
                        pltpu.VMEM((2, HG_DV, HG_DK), F32), pltpu.VMEM((2, t, HG_DV), F32)],
        compiler_params=_params("arbitrary", "arbitrary"),
        name=f"hgrn_t{t}",
    )(*args)


def _conv_kernel(cb_ref, cc_ref, cx_ref, w_ref, b_ref, o_ref, *, t):
    p = cc_ref[...] * cx_ref[...]
    row = lax.broadcasted_iota(jnp.int32, (t, 1), 0)
    prev = jnp.where(row >= 1, pltpu.roll(p, 1, axis=0), 0.0)
    nxt = jnp.where(row <= t - 2, pltpu.roll(p, t - 1, axis=0), 0.0)
    w = w_ref[...]
    y = prev * w[0:1] + p * w[1:2] + nxt * w[2:3] + b_ref[...]
    o_ref[...] = (cb_ref[...] * y).astype(MIX_DTYPE)


def _conv(z, w, b, t, nb, row0):
    rb = row0 // t
    return pl.pallas_call(
        functools.partial(_conv_kernel, t=t),
        out_shape=jax.ShapeDtypeStruct((nb * t, GROUP_W), MIX_DTYPE),
        grid=(nb,),
        in_specs=[
            pl.BlockSpec((t, GROUP_W), lambda i: (rb + i, COL_CB // 4)),
            pl.BlockSpec((t, GROUP_W), lambda i: (rb + i, COL_CC // 4)),
            pl.BlockSpec((t, GROUP_W), lambda i: (rb + i, COL_CX // 4)),
            pl.BlockSpec((CONV_K, GROUP_W), lambda i: (0, 0)),
            pl.BlockSpec((1, GROUP_W), lambda i: (0, 0)),
        ],
        out_specs=pl.BlockSpec((t, GROUP_W), lambda i: (i, 0)),
        compiler_params=_params("arbitrary"),
        name=f"conv_t{t}",
    )(z, z, z, w, b.reshape(1, GROUP_W))


def _outproj_kernel(*refs, tm):
    prompt_refs, sample_refs = refs[0:4], refs[4:8]
    w_ref, xp_ref, xs_ref, g1_ref, sh_ref, sc_ref, n2_ref, x1_ref, h2_ref = refs[8:]
    i = pl.program_id(0)
    r = _mod_row(i * tm)

    def run(group_refs, x_ref):
        acc = None
        for g, ref in enumerate(group_refs):
            part = _dot(ref[...], w_ref[g * GROUP_W:(g + 1) * GROUP_W, :])
            acc = part if acc is None else acc + part
        x1 = x_ref[...] + g1_ref[pl.ds(r, 1), :] * acc
        x1_ref[...] = x1
        h2 = _rms(x1, n2_ref[...]) * (1.0 + sc_ref[pl.ds(r, 1), :]) + sh_ref[pl.ds(r, 1), :]
        h2_ref[...] = h2.astype(BF16)

    pl.when(i < N_PROMPT // tm)(lambda: run(prompt_refs, xp_ref))
    pl.when(i >= N_PROMPT // tm)(lambda: run(sample_refs, xs_ref))


def _outproj(prompt_groups, sample_groups, w_out_bf, layer, x_prompt, x_sample, mod, norm2_g):
    tm = OUTPROJ_TM
    n_pb, n_sb = N_PROMPT // tm, N_SAMPLE // tm
    p_spec = pl.BlockSpec((tm, GROUP_W), lambda i: (jnp.minimum(i, n_pb - 1), 0))
    s_spec = pl.BlockSpec((tm, GROUP_W), lambda i: (jnp.clip(i - n_pb, 0, n_sb - 1), 0))
    modspec = lambda k: pl.BlockSpec((MOD_ROWS, D_MODEL), lambda i: (0, k))
    return pl.pallas_call(
        functools.partial(_outproj_kernel, tm=tm),
        out_shape=(jax.ShapeDtypeStruct((N_TOK, D_MODEL), F32), jax.ShapeDtypeStruct((N_TOK, D_MODEL), BF16)),
        grid=(N_TOK // tm,),
        in_specs=[p_spec] * 4 + [s_spec] * 4 + [
            pl.BlockSpec((None, D_MODEL, D_MODEL), lambda i: (layer, 0, 0), pipeline_mode=pl.Buffered(1)),
            *_token_source_specs(tm, x_prompt, x_sample, 1),
            modspec(2), modspec(3), modspec(4),
            pl.BlockSpec((1, D_MODEL), lambda i: (0, 0))],
        out_specs=(pl.BlockSpec((tm, D_MODEL), lambda i: (i, 0)), pl.BlockSpec((tm, D_MODEL), lambda i: (i, 0))),
        compiler_params=_params("arbitrary", vmem=VMEM_LARGE),
        name="outproj",
    )(*prompt_groups, *sample_groups, w_out_bf, x_prompt, x_sample, mod, mod, mod, norm2_g.reshape(1, D_MODEL))


def _peer_query_kernel(h_ref, wq_ref, q_ref):
    q_ref[...] = _dot(h_ref[...], wq_ref[...]).astype(BF16)


def _peer_query(h2, wq_bf, layer):
    tm = QUERY_TM
    width = PEER_HEADS * PEER_QDIM
    return pl.pallas_call(
        _peer_query_kernel,
        out_shape=jax.ShapeDtypeStruct((N_TOK, width), BF16),
        grid=(N_TOK // tm,),
        in_specs=[pl.BlockSpec((tm, D_MODEL), lambda i: (i, 0)),
                  pl.BlockSpec((None, D_MODEL, width), lambda i: (layer, 0, 0))],
        out_specs=pl.BlockSpec((tm, width), lambda i: (i, 0)),
        compiler_params=_params("arbitrary"),
        name="peer_query",
    )(h2, wq_bf)


_CAND_RUNS = ((2, 5), (3, 4), (4, 3), (5, 2), (6, 2), (7, 2))
_CAND_RUN_ROWS = 24
_CAND_ROWS = 4 * 8 + _CAND_RUN_ROWS
_NO_CAND = 1 << 20


def _candidates(v1, v2, tm):
    sub = lax.broadcasted_iota(jnp.int32, (8, tm), 0)
    vals = [v1[0:1] + v2[0:8], v1[0:1] + v2[8:16], v1[1:2] + v2[0:8], v1[8:16] + v2[0:1]]
    ids = [sub, sub + 8, sub + PEER_TOPK, (sub + 8) * PEER_TOPK]
    row = lax.broadcasted_iota(jnp.int32, (_CAND_RUN_ROWS, tm), 0)
    v2_rep = jnp.concatenate([v2[0:8]] * (_CAND_RUN_ROWS // 8), axis=0)
    run_vals = jnp.full((_CAND_RUN_ROWS, tm), -jnp.inf, F32)
    run_ids = jnp.full((_CAND_RUN_ROWS, tm), _NO_CAND, jnp.int32)
    start = 0
    for r1, count in _CAND_RUNS:
        inside = (row >= start) & (row < start + count)
        shifted = v2_rep if start % 8 == 0 else pltpu.roll(v2_rep, start % 8, axis=0)
        run_vals = jnp.where(inside, v1[r1:r1 + 1] + shifted, run_vals)
        run_ids = jnp.where(inside, row + (r1 * PEER_TOPK - start), run_ids)
        start += count
    return jnp.concatenate(vals + [run_vals], axis=0), jnp.concatenate(ids + [run_ids], axis=0)


def _topk_rows(x_scr, ids, n_rows, v_out, i_out):
    def body(j, carry):
        x = x_scr[0:n_rows, :]
        m = jnp.max(x, axis=0, keepdims=True)
        idx = jnp.min(jnp.where(x == m, ids, _NO_CAND), axis=0, keepdims=True)
        v_out[pl.ds(j, 1), :] = m
        i_out[pl.ds(j, 1), :] = idx
        x_scr[0:n_rows, :] = jnp.where(ids == idx, -jnp.inf, x)
        return carry

    lax.fori_loop(0, PEER_TOPK, body, 0, unroll=2)


def _peer_topk_kernel(q_ref, k1_ref, k2_ref, u_ref, v_ref, e_ref, g_ref, ub_ref, vb_ref, x_scr, v1_scr, i1_scr,
                      v2_scr, i2_scr, ts_scr, tc_scr, *, tm):
    ub_ref[...] = u_ref[...].astype(BF16)
    vb_ref[...] = v_ref[...].astype(BF16)
    key_ids = lax.broadcasted_iota(jnp.int32, (N_KEYS, tm), 0)
    for half, (k_ref, v_scr, i_scr) in enumerate(((k1_ref, v1_scr, i1_scr), (k2_ref, v2_scr, i2_scr))):
        qh = q_ref[:, half * PEER_HALF:(half + 1) * PEER_HALF]
        x_scr[...] = lax.dot_general(k_ref[...], qh, NT_DIMS, preferred_element_type=F32)
        _topk_rows(x_scr, key_ids, N_KEYS, v_scr, i_scr)
    cand_vals, cand_ids = _candidates(v1_scr[...], v2_scr[...], tm)
    x_scr[0:_CAND_ROWS, :] = cand_vals
    _topk_rows(x_scr, cand_ids, _CAND_ROWS, ts_scr, tc_scr)
    tc = tc_scr[...]
    c1, c2 = tc // PEER_TOPK, tc % PEER_TOPK
    i1, i2 = i1_scr[...], i2_scr[...]
    e1 = jnp.zeros_like(tc)
    e2 = jnp.zeros_like(tc)
    for a in range(PEER_TOPK):
        e1 = jnp.where(c1 == a, i1[a:a + 1], e1)
        e2 = jnp.where(c2 == a, i2[a:a + 1], e2)
    e_ref[...] = e1 * N_KEYS + e2
    ts = ts_scr[...]
    p = jnp.exp(ts - jnp.max(ts, axis=0, keepdims=True))
    g_ref[...] = p / jnp.sum(p, axis=0, keepdims=True)


def _peer_topk(q, k1_bf, k2_bf, peer_u, peer_v, layer):
    tm = TOPK_TM
    slab = LANES
    n_slabs = N_EXPERTS // slab
    assert (N_TOK // tm) * PEER_HEADS >= n_slabs
    row_f = pltpu.VMEM((PEER_TOPK, tm), F32)
    row_i = pltpu.VMEM((PEER_TOPK, tm), jnp.int32)
    slab_of = lambda i, h: jnp.minimum(i * PEER_HEADS + h, n_slabs - 1)
    table_in = pl.BlockSpec((None, slab, D_MODEL), lambda i, h: (layer, slab_of(i, h), 0))
    table_out = pl.BlockSpec((slab, D_MODEL), lambda i, h: (slab_of(i, h), 0))
    table_shape = jax.ShapeDtypeStruct((N_EXPERTS, D_MODEL), BF16)
    return pl.pallas_call(
        functools.partial(_peer_topk_kernel, tm=tm),
        out_shape=(jax.ShapeDtypeStruct((PEER_HEADS * PEER_TOPK, N_TOK), jnp.int32),
                   jax.ShapeDtypeStruct((PEER_HEADS * PEER_TOPK, N_TOK), F32), table_shape, table_shape),
        grid=(N_TOK // tm, PEER_HEADS),
        in_specs=[
            pl.BlockSpec((tm, PEER_QDIM), lambda i, h: (i, h)),
            pl.BlockSpec((None, N_KEYS, PEER_HALF), lambda i, h: (layer, 0, 0)),
            pl.BlockSpec((None, N_KEYS, PEER_HALF), lambda i, h: (layer, 0, 0)),
            table_in, table_in,
        ],
        out_specs=(pl.BlockSpec((PEER_TOPK, tm), lambda i, h: (h, i)),
                   pl.BlockSpec((PEER_TOPK, tm), lambda i, h: (h, i)), table_out, table_out),
        scratch_shapes=[pltpu.VMEM((N_KEYS, tm), F32), row_f, row_i, row_f, row_i, row_f, row_i],
        compiler_params=_params("arbitrary", "arbitrary"),
        name="peer_topk",
    )(q, k1_bf, k2_bf, peer_u, peer_v)


def _peer_mask_kernel(e_ref, g_ref, m_ref, et_scr, gt_scr, slab_scr, *, tb):
    et_scr[...] = e_ref[...].T
    gt_scr[...] = g_ref[...].T
    key = lax.broadcasted_iota(jnp.int32, (N_KEYS, PEER_HEADS * PEER_TOPK), 0)

    def body(t, carry):
        er = et_scr[pl.ds(t, 1), :]
        gr = gt_scr[pl.ds(t, 1), :]
        p1 = jnp.where(key == er // N_KEYS, gr, 0.0).astype(BF16)
        p2 = jnp.where(key == er % N_KEYS, 1.0, 0.0).astype(BF16)
        mt = lax.dot_general(p1, p2, NT_DIMS, preferred_element_type=F32)
        slab_scr[pl.ds(pl.multiple_of(t * MASK_PITCH, 4), N_KEYS), :] = mt
        return carry

    lax.fori_loop(0, tb, body, 0, unroll=64)
    for k1 in range(N_KEYS):
        m_ref[:, k1 * N_KEYS:(k1 + 1) * N_KEYS] = slab_scr[pl.ds(k1, tb, stride=MASK_PITCH), :].astype(BF16)


def _peer_mask(experts_t, gates_t):
    tb = MASK_TOKENS
    pairs = PEER_HEADS * PEER_TOPK
    return pl.pallas_call(
        functools.partial(_peer_mask_kernel, tb=tb),
        out_shape=jax.ShapeDtypeStruct((N_TOK, N_EXPERTS), BF16),
        grid=(N_TOK // tb,),
        in_specs=[pl.BlockSpec((pairs, tb), lambda i: (0, i)), pl.BlockSpec((pairs, tb), lambda i: (0, i))],
        out_specs=pl.BlockSpec((tb, N_EXPERTS), lambda i: (i, 0)),
        scratch_shapes=[pltpu.VMEM((tb, pairs), jnp.int32), pltpu.VMEM((tb, pairs), F32),
                        pltpu.VMEM((tb * MASK_PITCH, N_KEYS), F32)],
        compiler_params=_params("arbitrary"),
        name="peer_mask",
    )(experts_t, gates_t)


def _peer_expert_kernel(h_ref, u_ref, v_ref, m_ref, x_ref, g2_ref, o_ref, *, tm):
    j = pl.program_id(1)

    @pl.when(j == 0)
    def _():
        o_ref[...] = jnp.zeros_like(o_ref)

    s = lax.dot_general(h_ref[...], u_ref[...], NT_DIMS, preferred_element_type=F32)
    a = (m_ref[...].astype(F32) * jax.nn.gelu(s)).astype(BF16)
    o_ref[...] += _dot(a, v_ref[...])

    @pl.when(j == pl.num_programs(1) - 1)
    def _():
        r = _mod_row(pl.program_id(0) * tm)
        o_ref[...] = x_ref[...] + g2_ref[pl.ds(r, 1), :] * o_ref[...]


def _peer_expert(h2, u_bf, v_bf, mask, x1, mod):
    tm, te = EXPERT_TM, EXPERT_TE
    return pl.pallas_call(
        functools.partial(_peer_expert_kernel, tm=tm),
        out_shape=jax.ShapeDtypeStruct((N_TOK, D_MODEL), F32),
        grid=(N_TOK // tm, N_EXPERTS // te),
        in_specs=[
            pl.BlockSpec((tm, D_MODEL), lambda i, j: (i, 0)),
            pl.BlockSpec((te, D_MODEL), lambda i, j: (j, 0)),
            pl.BlockSpec((te, D_MODEL), lambda i, j: (j, 0)),
            pl.BlockSpec((tm, te), lambda i, j: (i, j)),
            pl.BlockSpec((tm, D_MODEL), lambda i, j: (i, 0)),
            pl.BlockSpec((MOD_ROWS, D_MODEL), lambda i, j: (0, 5)),
        ],
        out_specs=pl.BlockSpec((tm, D_MODEL), lambda i, j: (i, 0)),
        compiler_params=_params("arbitrary", "arbitrary", vmem=VMEM_LARGE),
        name="peer_expert",
    )(h2, u_bf, v_bf, mask, x1, mod)


def kernel(x_prompt, x_sample, cache_k, cache_v, state_hgrn, c, c_ctx, ada_w, ada_b, norm1_g, norm2_g, w_in,
           q_norm_g, k_norm_g, hg_lb, hg_onorm_g, conv_w, conv_b, w_out, peer_wq, peer_k1, peer_k2, peer_u, peer_v):
    x_p, x_s = x_prompt.reshape(N_PROMPT, D_MODEL), x_sample.reshape(N_SAMPLE, D_MODEL)
    cvec = jnp.concatenate([c_ctx[None, :], c, jnp.zeros((MOD_ROWS - 1 - DEC_BATCH, D_MODEL), F32)], axis=0)
    mod_all = _ada(cvec, ada_w, ada_b)

    w_in_bf, w_out_bf, wq_bf = w_in.astype(BF16), w_out.astype(BF16), peer_wq.astype(BF16)
    k1_bf, k2_bf = peer_k1.astype(BF16), peer_k2.astype(BF16)

    ss_out = []
    k_new = v_new = None
    for l in range(DEPTH):
        mod = mod_all[l]
        z = _inproj(x_p, x_s, mod, norm1_g[l], w_in_bf, l)

        fn_p = _fourier(z, SEQ, BATCH, 0)
        fn_s = _fourier(z, DEC_SEQ, DEC_BATCH, N_PROMPT)

        at_p, k_new, v_new = _attn_prompt(z, q_norm_g[l], k_norm_g[l], l, k_new, v_new)
        k_ctx = cache_k[:, l].reshape(DEC_BATCH, PAST_LEN, N_KV * HEAD_DIM)
        v_ctx = cache_v[:, l].reshape(DEC_BATCH, PAST_LEN, N_KV * HEAD_DIM)
        at_s = _attn_sample(z, q_norm_g[l], k_norm_g[l], k_ctx, v_ctx)

        hg_p, s_l = _hgrn(z, hg_lb, hg_onorm_g[l], l, SEQ, BATCH, 0, None)
        hg_s, _ = _hgrn(z, hg_lb, hg_onorm_g[l], l, DEC_SEQ, DEC_BATCH, N_PROMPT, state_hgrn[:, l])

        cv_p = _conv(z, conv_w[l], conv_b[l], SEQ, BATCH, 0)
        cv_s = _conv(z, conv_w[l], conv_b[l], DEC_SEQ, DEC_BATCH, N_PROMPT)

        x1, h2 = _outproj((fn_p, at_p, hg_p, cv_p), (fn_s, at_s, hg_s, cv_s), w_out_bf, l, x_p, x_s, mod,
                          norm2_g[l])

        q_peer = _peer_query(h2, wq_bf, l)
        experts_t, gates_t, u_bf, v_bf = _peer_topk(q_peer, k1_bf, k2_bf, peer_u, peer_v, l)
        mask = _peer_mask(experts_t, gates_t)
        x = _peer_expert(h2, u_bf, v_bf, mask, x1, mod)
        x_p = x_s = x

        ss_out.append(s_l)

    y_prompt = x[:N_PROMPT].reshape(BATCH, SEQ, D_MODEL)
    y_sample = x[N_PROMPT:].reshape(DEC_BATCH, DEC_SEQ, D_MODEL)
    cache_shape = (BATCH, DEPTH, SEQ, N_KV, HEAD_DIM)
    return (y_prompt, y_sample, k_new.reshape(cache_shape), v_new.reshape(cache_shape), jnp.stack(ss_out, axis=1))
```

```python
import functools

import numpy as np
import jax
import jax.numpy as jnp
from jax import lax
from jax.experimental import pallas as pl
from jax.experimental.pallas import tpu as pltpu

F32 = jnp.float32
BF16 = jnp.bfloat16
MIX_DTYPE = BF16

D_MODEL = 2048
BATCH = 32
SEQ = 256
DEPTH = 2
DEC_BATCH = 4
DEC_SEQ = 1024
PAST_LEN = 256
GRID_W = 64
GROUP_W = D_MODEL // 4
FN_GROUPS = 4
FN_GW = GROUP_W // FN_GROUPS
HEAD_DIM = 128
N_HEADS = GROUP_W // HEAD_DIM
N_KV = 2
GQA_GROUP = N_HEADS // N_KV
ROPE_THETA = 10000.0
HG_DK = 128
HG_DV = 128
HG_HEADS = GROUP_W // HG_DK
CONV_K = 3
IN_COLS = 11 * GROUP_W
PEER_HEADS = 8
N_KEYS = 128
N_EXPERTS = N_KEYS * N_KEYS
PEER_QDIM = 256
PEER_HALF = PEER_QDIM // 2
PEER_TOPK = 16
EPS = 1e-6
F_FLOOR = 1e-30

N_PROMPT = BATCH * SEQ
N_SAMPLE = DEC_BATCH * DEC_SEQ
N_TOK = N_PROMPT + N_SAMPLE
MOD_ROWS = 8
LANES = 128
NORM_ROWS = 128
HG_CHUNK = 32
HG_GROUP = 8
HG_UNROLL = 16
HG_CUM_BLOCK = 256
MASK_TOKENS = 128
MASK_PITCH = 132

ADA_TN = 1024
INPROJ_TM, INPROJ_TN = 1024, 512
OUTPROJ_TM = 512
QUERY_TM = 512
TOPK_TM = 512
EXPERT_TM, EXPERT_TE = 1024, 512
ATTN_Q_ROWS = 256

VMEM_SMALL = 32 * 1024 * 1024
VMEM_INPROJ = 48 * 1024 * 1024
VMEM_LARGE = 56 * 1024 * 1024

COL_FN, COL_Q, COL_K, COL_V = 0, 4, 8, 10
COL_GQ, COL_GFF, COL_GFB, COL_GI, COL_GO = 12, 16, 20, 24, 28
COL_CB, COL_CC, COL_CX = 32, 36, 40

NT_DIMS = (((1,), (1,)), ((), ()))
TN_DIMS = (((0,), (0,)), ((), ()))


def _params(*sem, vmem=VMEM_SMALL):
    return pltpu.CompilerParams(dimension_semantics=sem, vmem_limit_bytes=vmem)


def _dot(a, b):
    return jnp.dot(a, b, preferred_element_type=F32)


def _split3(x):
    h = x.astype(BF16)
    r = x - h.astype(F32)
    m = r.astype(BF16)
    l = (r - m.astype(F32)).astype(BF16)
    return h, m, l


def _split2(x):
    h = x.astype(BF16)
    return h, (x - h.astype(F32)).astype(BF16)


def _mod_row(tok0):
    return jnp.where(tok0 >= N_PROMPT, (tok0 - N_PROMPT) // DEC_SEQ + 1, 0)


def _rms(x, g):
    return x * lax.rsqrt(jnp.mean(x * x, axis=-1, keepdims=True) + EPS) * g


def _ada_kernel(c_ref, w_ref, b_ref, o_ref):
    c = c_ref[...]
    s = (c * jax.nn.sigmoid(c)).astype(BF16)
    o_ref[0] = _dot(s, w_ref[0].astype(BF16)) + b_ref[0]


def _ada(cvec, ada_w, ada_b):
    tn = ADA_TN
    return pl.pallas_call(
        _ada_kernel,
        out_shape=jax.ShapeDtypeStruct((DEPTH, MOD_ROWS, 6 * D_MODEL), F32),
        grid=(DEPTH, 6 * D_MODEL // tn),
        in_specs=[
            pl.BlockSpec((MOD_ROWS, D_MODEL), lambda l, j: (0, 0)),
            pl.BlockSpec((1, D_MODEL, tn), lambda l, j: (l, 0, j)),
            pl.BlockSpec((1, 1, tn), lambda l, j: (l, 0, j)),
        ],
        out_specs=pl.BlockSpec((1, MOD_ROWS, tn), lambda l, j: (l, 0, j)),
        compiler_params=_params("arbitrary", "arbitrary"),
        name="ada",
    )(cvec, ada_w, ada_b.reshape(DEPTH, 1, 6 * D_MODEL))


def _token_source_specs(tm, x_prompt, x_sample, n_grid_axes):
    n_pb, n_sb = N_PROMPT // tm, N_SAMPLE // tm
    s_off = x_sample.shape[0] // tm - n_sb
    if n_grid_axes == 1:
        return (pl.BlockSpec((tm, D_MODEL), lambda i: (jnp.minimum(i, n_pb - 1), 0)),
                pl.BlockSpec((tm, D_MODEL), lambda i: (s_off + jnp.clip(i - n_pb, 0, n_sb - 1), 0)))
    return (pl.BlockSpec((tm, D_MODEL), lambda i, j: (jnp.minimum(i, n_pb - 1), 0)),
            pl.BlockSpec((tm, D_MODEL), lambda i, j: (s_off + jnp.clip(i - n_pb, 0, n_sb - 1), 0)))


def _inproj_kernel(xp_ref, xs_ref, sh_ref, sc_ref, g_ref, w_ref, o_ref, h_ref, *, tm):
    i, j = pl.program_id(0), pl.program_id(1)

    def prologue(x_ref):
        r = _mod_row(i * tm)
        gain = g_ref[...] * (1.0 + sc_ref[pl.ds(r, 1), :])
        shift = sh_ref[pl.ds(r, 1), :]

        def norm_rows(c, carry):
            rows = pl.ds(pl.multiple_of(c * NORM_ROWS, NORM_ROWS), NORM_ROWS)
            x = x_ref[rows, :]
            y = x * lax.rsqrt(jnp.mean(x * x, axis=-1, keepdims=True) + EPS)
            h_ref[rows, :] = (y * gain + shift).astype(BF16)
            return carry

        lax.fori_loop(0, tm // NORM_ROWS, norm_rows, 0)

    pl.when((j == 0) & (i < N_PROMPT // tm))(lambda: prologue(xp_ref))
    pl.when((j == 0) & (i >= N_PROMPT // tm))(lambda: prologue(xs_ref))
    o_ref[...] = _dot(h_ref[...], w_ref[...])


def _inproj(x_prompt, x_sample, mod, norm_g, w_in_bf, layer):
    tm, tn = INPROJ_TM, INPROJ_TN
    return pl.pallas_call(
        functools.partial(_inproj_kernel, tm=tm),
        out_shape=jax.ShapeDtypeStruct((N_TOK, IN_COLS), F32),
        grid=(N_TOK // tm, IN_COLS // tn),
        in_specs=[
            *_token_source_specs(tm, x_prompt, x_sample, 2),
            pl.BlockSpec((MOD_ROWS, D_MODEL), lambda i, j: (0, 0)),
            pl.BlockSpec((MOD_ROWS, D_MODEL), lambda i, j: (0, 1)),
            pl.BlockSpec((1, D_MODEL), lambda i, j: (0, 0)),
            pl.BlockSpec((None, D_MODEL, tn), lambda i, j: (layer, 0, j)),
        ],
        out_specs=pl.BlockSpec((tm, tn), lambda i, j: (i, j)),
        scratch_shapes=[pltpu.VMEM((tm, D_MODEL), BF16)],
        compiler_params=_params("arbitrary", "arbitrary", vmem=VMEM_INPROJ),
        name="inproj",
    )(x_prompt, x_sample, mod, mod, norm_g.reshape(1, D_MODEL), w_in_bf)


def _fourier_kernel(u_ref, cch_ref, ccl_ref, cth_ref, ctl_ref, o_ref):
    cch, ccl, cth, ctl = cch_ref[...], ccl_ref[...], cth_ref[...], ctl_ref[...]
    for g in range(0, FN_GROUPS, 2):
        ws = []
        for gg in (g, g + 1):
            uh, ul = _split2(u_ref[:, gg * FN_GW:(gg + 1) * FN_GW])
            ws.append(_dot(uh, cch) + (_dot(ul, cch) + _dot(uh, ccl)))
        st = jnp.concatenate([jnp.concatenate([w[:, :FN_GW] for w in ws], axis=1),
                              jnp.concatenate([w[:, FN_GW:] for w in ws], axis=1)], axis=0)
        sh, sl = _split2(st)
        o_ref[:, g * FN_GW:(g + 2) * FN_GW] = (_dot(cth, sh) + (_dot(cth, sl) + _dot(ctl, sh))).astype(MIX_DTYPE)


def _dft_consts(t):
    def cs(n):
        k = np.arange(n, dtype=np.float64)
        ang = 2.0 * np.pi * np.outer(k, k) / n
        return np.cos(ang) / np.sqrt(n), np.sin(ang) / np.sqrt(n)

    cc, sc = cs(FN_GW)
    ct, st = cs(t)
    right = np.concatenate([cc, sc], axis=1).astype(np.float32)
    left = np.concatenate([ct, -st], axis=1).astype(np.float32)

    def hl(a):
        a = jnp.asarray(a)
        h = a.astype(BF16)
        return h, (a - h.astype(F32)).astype(BF16)

    return hl(right) + hl(left)


def _fourier(z, t, nb, row0):
    cch, ccl, cth, ctl = _dft_consts(t)
    rb = row0 // t
    return pl.pallas_call(
        _fourier_kernel,
        out_shape=jax.ShapeDtypeStruct((nb * t, GROUP_W), MIX_DTYPE),
        grid=(nb,),
        in_specs=[
            pl.BlockSpec((t, GROUP_W), lambda b: (rb + b, COL_FN // 4)),
            pl.BlockSpec((FN_GW, 2 * FN_GW), lambda b: (0, 0)),
            pl.BlockSpec((FN_GW, 2 * FN_GW), lambda b: (0, 0)),
            pl.BlockSpec((t, 2 * t), lambda b: (0, 0)),
            pl.BlockSpec((t, 2 * t), lambda b: (0, 0)),
        ],
        out_specs=pl.BlockSpec((t, GROUP_W), lambda b: (b, 0)),
        compiler_params=_params("arbitrary"),
        name=f"fourier_t{t}",
    )(z, cch, ccl, cth, ctl)


def _softmax_pv(q_bf, k_bf, v_bf):
    s = lax.dot_general(q_bf, k_bf, NT_DIMS, preferred_element_type=F32) * (HEAD_DIM ** -0.5)
    p = jnp.exp(s - jnp.max(s, axis=-1, keepdims=True))
    return _dot(p.astype(BF16), v_bf) / jnp.sum(p, axis=-1, keepdims=True)


def _attn_prompt_kernel(q_ref, k_ref, v_ref, qg_ref, kg_ref, o_ref, kc_ref, vc_ref):
    kn = _rms(k_ref[...], kg_ref[...])
    v = v_ref[...]
    head_rows = pl.ds(pl.program_id(1), SEQ, stride=N_KV)
    kc_ref[0, head_rows, :] = kn
    vc_ref[0, head_rows, :] = v
    kb, vb = kn.astype(BF16), v.astype(BF16)
    for g in range(GQA_GROUP):
        qn = _rms(q_ref[:, g * HEAD_DIM:(g + 1) * HEAD_DIM], qg_ref[...])
        o_ref[:, g * HEAD_DIM:(g + 1) * HEAD_DIM] = _softmax_pv(qn.astype(BF16), kb, vb).astype(MIX_DTYPE)


def _attn_prompt(z, q_g, k_g):
    t = SEQ
    kv_shape = jax.ShapeDtypeStruct((BATCH, SEQ * N_KV, HEAD_DIM), F32)
    kv_spec = pl.BlockSpec((1, SEQ * N_KV, HEAD_DIM), lambda b, h: (b, 0, 0))
    return pl.pallas_call(
        _attn_prompt_kernel,
        out_shape=(jax.ShapeDtypeStruct((N_PROMPT, GROUP_W), MIX_DTYPE), kv_shape, kv_shape),
        grid=(BATCH, N_KV),
        in_specs=[
            pl.BlockSpec((t, GQA_GROUP * HEAD_DIM), lambda b, h: (b, COL_Q // 2 + h)),
            pl.BlockSpec((t, HEAD_DIM), lambda b, h: (b, COL_K + h)),
            pl.BlockSpec((t, HEAD_DIM), lambda b, h: (b, COL_V + h)),
            pl.BlockSpec((1, HEAD_DIM), lambda b, h: (0, 0)),
            pl.BlockSpec((1, HEAD_DIM), lambda b, h: (0, 0)),
        ],
        out_specs=(
            pl.BlockSpec((t, GQA_GROUP * HEAD_DIM), lambda b, h: (b, h)),
            kv_spec, kv_spec,
        ),
        compiler_params=_params("arbitrary", "arbitrary"),
        name="attn_prompt",
    )(z, z, z, q_g.reshape(1, HEAD_DIM), k_g.reshape(1, HEAD_DIM))


def _rope(x, cosf, sinf):
    return x * cosf + pltpu.roll(x, HEAD_DIM // 2, axis=1) * sinf


def _attn_sample_kernel(q_ref, k_ref, v_ref, kctx_ref, vctx_ref, qg_ref, kg_ref, cos_ref, sin_ref,
                        o_ref, kall_ref, vall_ref, *, qb):
    cosf, sinf = cos_ref[...], sin_ref[...]
    kall_ref[0:PAST_LEN, :] = kctx_ref[0].astype(BF16)
    vall_ref[0:PAST_LEN, :] = vctx_ref[0].astype(BF16)
    kall_ref[PAST_LEN:, :] = _rope(_rms(k_ref[...], kg_ref[...]), cosf, sinf).astype(BF16)
    vall_ref[PAST_LEN:, :] = v_ref[...].astype(BF16)
    kb, vb = kall_ref[...], vall_ref[...]
    for i in range(DEC_SEQ // qb):
        rows = slice(i * qb, (i + 1) * qb)
        for g in range(GQA_GROUP):
            cols = slice(g * HEAD_DIM, (g + 1) * HEAD_DIM)
            qn = _rope(_rms(q_ref[rows, cols], qg_ref[...]), cosf[rows], sinf[rows])
            o_ref[rows, cols] = _softmax_pv(qn.astype(BF16), kb, vb).astype(MIX_DTYPE)


def _rope_tables():
    rows = DEC_SEQ // GRID_W
    row = np.repeat(np.arange(rows, dtype=np.float32), GRID_W)
    col = np.tile(np.arange(GRID_W, dtype=np.float32), rows)
    n_freq = HEAD_DIM // 4
    inv = (np.float32(ROPE_THETA) ** (-np.arange(n_freq, dtype=np.float32) / n_freq)).astype(np.float32)
    ang = np.concatenate([row[:, None] * inv, col[:, None] * inv], axis=-1).astype(np.float32)
    cos, sin = np.cos(ang.astype(np.float64)), np.sin(ang.astype(np.float64))
    cosf = np.concatenate([cos, cos], axis=-1).astype(np.float32)
    sinf = np.concatenate([-sin, sin], axis=-1).astype(np.float32)
    return jnp.asarray(cosf), jnp.asarray(sinf)


def _attn_sample(z, q_g, k_g, k_ctx, v_ctx):
    t = DEC_SEQ
    rb = N_PROMPT // t
    cosf, sinf = _rope_tables()
    const = lambda b, h: (0, 0)
    return pl.pallas_call(
        functools.partial(_attn_sample_kernel, qb=ATTN_Q_ROWS),
        out_shape=jax.ShapeDtypeStruct((N_SAMPLE, GROUP_W), MIX_DTYPE),
        grid=(DEC_BATCH, N_KV),
        in_specs=[
            pl.BlockSpec((t, GQA_GROUP * HEAD_DIM), lambda b, h: (rb + b, COL_Q // 2 + h)),
            pl.BlockSpec((t, HEAD_DIM), lambda b, h: (rb + b, COL_K + h)),
            pl.BlockSpec((t, HEAD_DIM), lambda b, h: (rb + b, COL_V + h)),
            pl.BlockSpec((1, PAST_LEN, HEAD_DIM), lambda b, h: (b, 0, h)),
            pl.BlockSpec((1, PAST_LEN, HEAD_DIM), lambda b, h: (b, 0, h)),
            pl.BlockSpec((1, HEAD_DIM), const),
            pl.BlockSpec((1, HEAD_DIM), const),
            pl.BlockSpec((t, HEAD_DIM), const),
            pl.BlockSpec((t, HEAD_DIM), const),
        ],
        out_specs=pl.BlockSpec((t, GQA_GROUP * HEAD_DIM), lambda b, h: (b, h)),
        scratch_shapes=[pltpu.VMEM((PAST_LEN + t, HEAD_DIM), BF16), pltpu.VMEM((PAST_LEN + t, HEAD_DIM), BF16)],
        compiler_params=_params("arbitrary", "arbitrary"),
        name="attn_sample",
    )(z, z, z, k_ctx, v_ctx, q_g.reshape(1, HEAD_DIM), k_g.reshape(1, HEAD_DIM), cosf, sinf)


def _hgrn_chain(rev, r0, gq_ref, gi_ref, k_scr, cum_scr, st_scr, o_scr):
    c = HG_CHUNK
    d = 1 if rev else 0
    rows = pl.ds(r0, c)
    qc, vc = gq_ref[rows, :], gi_ref[rows, :]
    kc, cc = k_scr[d, rows, :], cum_scr[d, rows, :]
    st = st_scr[d]
    last = cc[0:1] if rev else cc[c - 1:c]
    qs = (qc * jnp.exp(cc)).astype(BF16)
    o_inter = lax.dot_general(qs, st.astype(BF16), NT_DIMS, preferred_element_type=F32)
    ks = (kc * jnp.exp(last - cc)).astype(BF16)
    upd = lax.dot_general(vc.astype(BF16), ks, TN_DIMS, preferred_element_type=F32)
    st_scr[d] = st * jnp.exp(last) + upd
    n_grp = c // HG_GROUP
    row = lax.broadcasted_iota(jnp.int32, (HG_GROUP, 1), 0)
    krow = lax.broadcasted_iota(jnp.int32, (c, 1), 0)
    diag, att = [], []
    for i in range(n_grp):
        lo = i * HG_GROUP
        grp = slice(lo, lo + HG_GROUP)
        q_i, k_i, v_i, c_i = qc[grp], kc[grp], vc[grp], cc[grp]
        acc = jnp.zeros((HG_GROUP, HG_DV), F32)
        for delta in range(HG_GROUP):
            if delta == 0:
                k_sh, c_sh, v_sh = k_i, c_i, v_i
            else:
                sh = (HG_GROUP - delta) if rev else delta
                k_sh, c_sh, v_sh = (pltpu.roll(a, sh, axis=0) for a in (k_i, c_i, v_i))
            a = jnp.sum(q_i * k_sh * jnp.exp(jnp.minimum(c_i - c_sh, 0.0)), axis=-1, keepdims=True)
            valid = (row <= HG_GROUP - 1 - delta) if rev else (row >= delta)
            acc = acc + jnp.where(valid, a, 0.0) * v_sh
        diag.append(acc)
        if (i == n_grp - 1) if rev else (i == 0):
            att.append(jnp.zeros((HG_GROUP, c), F32))
            continue
        if rev:
            b, other = cc[lo + HG_GROUP:lo + HG_GROUP + 1], krow >= lo + HG_GROUP
        else:
            b, other = cc[lo - 1:lo], krow < lo
        qs_i = (q_i * jnp.exp(c_i - b)).astype(BF16)
        ks_i = jnp.where(other, kc * jnp.exp(jnp.minimum(b - cc, 0.0)), 0.0).astype(BF16)
        att.append(lax.dot_general(qs_i, ks_i, NT_DIMS, preferred_element_type=F32))
    o_off = _dot(jnp.concatenate(att, axis=0).astype(BF16), vc.astype(BF16))
    o_scr[d, rows, :] = o_inter + jnp.concatenate(diag, axis=0) + o_off


def _hgrn_kernel(*refs, layer, t, has_ctx):
    gq_ref, gff_ref, gfb_ref, gi_ref, go_ref, lb_ref, on_ref, bdf_ref, bdb_ref = refs[:9]
    n_in = 9
    s0_ref = None
    if has_ctx:
        s0_ref = refs[9]
        n_in = 10
    o_ref, sout_ref = refs[n_in], refs[n_in + 1]
    k_scr, cum_scr, st_scr, o_scr = refs[n_in + 2:]

    raw = lb_ref[...]
    e = jnp.exp(raw - jnp.max(raw, axis=0, keepdims=True))
    soft = e / jnp.sum(e, axis=0, keepdims=True)
    csum = soft[0]
    for i in range(1, layer + 1):
        csum = csum + soft[i]
    lb = csum - soft[0]

    for d, (g_ref, bd_ref) in enumerate(((gff_ref, bdf_ref), (gfb_ref, bdb_ref))):
        lbd = lb[d:d + 1]
        f = jnp.maximum(lbd + (1.0 - lbd) * jax.nn.sigmoid(g_ref[...]), F_FLOOR)
        k_scr[d] = 1.0 - f
        lf = jnp.log(f)
        bd = bd_ref[...]
        for p in range(t // HG_CUM_BLOCK):
            rows = slice(p * HG_CUM_BLOCK, (p + 1) * HG_CUM_BLOCK)
            h, m, l = _split3(lf[rows])
            cum_scr[d, rows, :] = _dot(bd, h) + _dot(bd, m) + _dot(bd, l)
        if has_ctx:
            st_scr[d] = s0_ref[0, d, 0].T
        else:
            st_scr[d] = jnp.zeros((HG_DV, HG_DK), F32)

    n = t // HG_CHUNK

    def body(i, carry):
        _hgrn_chain(False, pl.multiple_of(i * HG_CHUNK, HG_CHUNK), gq_ref, gi_ref, k_scr, cum_scr, st_scr, o_scr)
        _hgrn_chain(True, pl.multiple_of((n - 1 - i) * HG_CHUNK, HG_CHUNK), gq_ref, gi_ref, k_scr, cum_scr,
                    st_scr, o_scr)
        return carry

    lax.fori_loop(0, n, body, 0, unroll=min(n, HG_UNROLL))

    o = _rms(o_scr[0] + o_scr[1], on_ref[...])
    go = go_ref[...]
    o_ref[...] = (o * (go * jax.nn.sigmoid(go))).astype(MIX_DTYPE)
    for d in range(2):
        sout_ref[0, d, 0] = st_scr[d].T


def _cum_consts():
    r = np.arange(HG_CUM_BLOCK)
    same = (r[:, None] // HG_CHUNK) == (r[None, :] // HG_CHUNK)
    fwd = same & (r[None, :] <= r[:, None])
    bwd = same & (r[None, :] >= r[:, None])
    return jnp.asarray(fwd, BF16), jnp.asarray(bwd, BF16)


def _hgrn(z, hg_lb, onorm_g, layer, t, nb, row0, s0):
    rb = row0 // t
    bdf, bdb = _cum_consts()
    has_ctx = s0 is not None
    col = lambda c0: (lambda b, h: (rb + b, c0 + h))
    in_specs = [
        pl.BlockSpec((t, HG_DK), col(COL_GQ)),
        pl.BlockSpec((t, HG_DK), col(COL_GFF)),
        pl.BlockSpec((t, HG_DK), col(COL_GFB)),
        pl.BlockSpec((t, HG_DV), col(COL_GI)),
        pl.BlockSpec((t, HG_DV), col(COL_GO)),
        pl.BlockSpec((DEPTH, 2, HG_DK), lambda b, h: (0, 0, h)),
        pl.BlockSpec((1, HG_DV), lambda b, h: (0, 0)),
        pl.BlockSpec((HG_CUM_BLOCK, HG_CUM_BLOCK), lambda b, h: (0, 0)),
        pl.BlockSpec((HG_CUM_BLOCK, HG_CUM_BLOCK), lambda b, h: (0, 0)),
    ]
    args = [z, z, z, z, z, hg_lb, onorm_g.reshape(1, HG_DV), bdf, bdb]
    if has_ctx:
        in_specs.append(pl.BlockSpec((1, 2, 1, HG_DK, HG_DV), lambda b, h: (b, 0, h, 0, 0)))
        args.append(s0)
    return pl.pallas_call(
        functools.partial(_hgrn_kernel, layer=layer, t=t, has_ctx=has_ctx),
        out_shape=(jax.ShapeDtypeStruct((nb * t, GROUP_W), MIX_DTYPE),
                   jax.ShapeDtypeStruct((nb, 2, HG_HEADS, HG_DK, HG_DV), F32)),
        grid=(nb, HG_HEADS),
        in_specs=in_specs,
        out_specs=(pl.BlockSpec((t, HG_DV), lambda b, h: (b, h)),
                   pl.BlockSpec((1, 2, 1, HG_DK, HG_DV), lambda b, h: (b, 0, h, 0, 0))),
        scratch_shapes=[pltpu.VMEM((2, t, HG_DK), F32), pltpu.VMEM((2, t, HG_DK), F32),
                        pltpu.VMEM((2, HG_DV, HG_DK), F32), pltpu.VMEM((2, t, HG_DV), F32)],
        compiler_params=_params("arbitrary", "arbitrary"),
        name=f"hgrn_t{t}",
    )(*args)


def _conv_kernel(cb_ref, cc_ref, cx_ref, w_ref, b_ref, o_ref, *, t):
    p = cc_ref[...] * cx_ref[...]
    row = lax.broadcasted_iota(jnp.int32, (t, 1), 0)
    prev = jnp.where(row >= 1, pltpu.roll(p, 1, axis=0), 0.0)
    nxt = jnp.where(row <= t - 2, pltpu.roll(p, t - 1, axis=0), 0.0)
    w = w_ref[...]
    y = prev * w[0:1] + p * w[1:2] + nxt * w[2:3] + b_ref[...]
    o_ref[...] = (cb_ref[...] * y).astype(MIX_DTYPE)


def _conv(z, w, b, t, nb, row0):
    rb = row0 // t
    return pl.pallas_call(
        functools.partial(_conv_kernel, t=t),
        out_shape=jax.ShapeDtypeStruct((nb * t, GROUP_W), MIX_DTYPE),
        grid=(nb,),
        in_specs=[
            pl.BlockSpec((t, GROUP_W), lambda i: (rb + i, COL_CB // 4)),
            pl.BlockSpec((t, GROUP_W), lambda i: (rb + i, COL_CC // 4)),
            pl.BlockSpec((t, GROUP_W), lambda i: (rb + i, COL_CX // 4)),
            pl.BlockSpec((CONV_K, GROUP_W), lambda i: (0, 0)),
            pl.BlockSpec((1, GROUP_W), lambda i: (0, 0)),
        ],
        out_specs=pl.BlockSpec((t, GROUP_W), lambda i: (i, 0)),
        compiler_params=_params("arbitrary"),
        name=f"conv_t{t}",
    )(z, z, z, w, b.reshape(1, GROUP_W))


def _outproj_kernel(*refs, tm):
    prompt_refs, sample_refs = refs[0:4], refs[4:8]
    w_ref, xp_ref, xs_ref, g1_ref, sh_ref, sc_ref, n2_ref, x1_ref, h2_ref = refs[8:]
    i = pl.program_id(0)
    r = _mod_row(i * tm)

    def run(group_refs, x_ref):
        acc = None
        for g, ref in enumerate(group_refs):
            part = _dot(ref[...], w_ref[g * GROUP_W:(g + 1) * GROUP_W, :])
            acc = part if acc is None else acc + part
        x1 = x_ref[...] + g1_ref[pl.ds(r, 1), :] * acc
        x1_ref[...] = x1
        h2 = _rms(x1, n2_ref[...]) * (1.0 + sc_ref[pl.ds(r, 1), :]) + sh_ref[pl.ds(r, 1), :]
        h2_ref[...] = h2.astype(BF16)

    pl.when(i < N_PROMPT // tm)(lambda: run(prompt_refs, xp_ref))
    pl.when(i >= N_PROMPT // tm)(lambda: run(sample_refs, xs_ref))


def _outproj(prompt_groups, sample_groups, w_out_bf, layer, x_prompt, x_sample, mod, norm2_g):
    tm = OUTPROJ_TM
    n_pb, n_sb = N_PROMPT // tm, N_SAMPLE // tm
    p_spec = pl.BlockSpec((tm, GROUP_W), lambda i: (jnp.minimum(i, n_pb - 1), 0))
    s_spec = pl.BlockSpec((tm, GROUP_W), lambda i: (jnp.clip(i - n_pb, 0, n_sb - 1), 0))
    modspec = lambda k: pl.BlockSpec((MOD_ROWS, D_MODEL), lambda i: (0, k))
    return pl.pallas_call(
        functools.partial(_outproj_kernel, tm=tm),
        out_shape=(jax.ShapeDtypeStruct((N_TOK, D_MODEL), F32), jax.ShapeDtypeStruct((N_TOK, D_MODEL), BF16)),
        grid=(N_TOK // tm,),
        in_specs=[p_spec] * 4 + [s_spec] * 4 + [
            pl.BlockSpec((None, D_MODEL, D_MODEL), lambda i: (layer, 0, 0), pipeline_mode=pl.Buffered(1)),
            *_token_source_specs(tm, x_prompt, x_sample, 1),
            modspec(2), modspec(3), modspec(4),
            pl.BlockSpec((1, D_MODEL), lambda i: (0, 0))],
        out_specs=(pl.BlockSpec((tm, D_MODEL), lambda i: (i, 0)), pl.BlockSpec((tm, D_MODEL), lambda i: (i, 0))),
        compiler_params=_params("arbitrary", vmem=VMEM_LARGE),
        name="outproj",
    )(*prompt_groups, *sample_groups, w_out_bf, x_prompt, x_sample, mod, mod, mod, norm2_g.reshape(1, D_MODEL))


def _peer_query_kernel(h_ref, wq_ref, q_ref):
    q_ref[...] = _dot(h_ref[...], wq_ref[...]).astype(BF16)


def _peer_query(h2, wq_bf, layer):
    tm = QUERY_TM
    width = PEER_HEADS * PEER_QDIM
    return pl.pallas_call(
        _peer_query_kernel,
        out_shape=jax.ShapeDtypeStruct((N_TOK, width), BF16),
        grid=(N_TOK // tm,),
        in_specs=[pl.BlockSpec((tm, D_MODEL), lambda i: (i, 0)),
                  pl.BlockSpec((None, D_MODEL, width), lambda i: (layer, 0, 0))],
        out_specs=pl.BlockSpec((tm, width), lambda i: (i, 0)),
        compiler_params=_params("arbitrary"),
        name="peer_query",
    )(h2, wq_bf)


_CAND_RUNS = ((2, 5), (3, 4), (4, 3), (5, 2), (6, 2), (7, 2))
_CAND_RUN_ROWS = 24
_CAND_ROWS = 4 * 8 + _CAND_RUN_ROWS
_NO_CAND = 1 << 20


def _candidates(v1, v2, tm):
    sub = lax.broadcasted_iota(jnp.int32, (8, tm), 0)
    vals = [v1[0:1] + v2[0:8], v1[0:1] + v2[8:16], v1[1:2] + v2[0:8], v1[8:16] + v2[0:1]]
    ids = [sub, sub + 8, sub + PEER_TOPK, (sub + 8) * PEER_TOPK]
    row = lax.broadcasted_iota(jnp.int32, (_CAND_RUN_ROWS, tm), 0)
    v2_rep = jnp.concatenate([v2[0:8]] * (_CAND_RUN_ROWS // 8), axis=0)
    run_vals = jnp.full((_CAND_RUN_ROWS, tm), -jnp.inf, F32)
    run_ids = jnp.full((_CAND_RUN_ROWS, tm), _NO_CAND, jnp.int32)
    start = 0
    for r1, count in _CAND_RUNS:
        inside = (row >= start) & (row < start + count)
        shifted = v2_rep if start % 8 == 0 else pltpu.roll(v2_rep, start % 8, axis=0)
        run_vals = jnp.where(inside, v1[r1:r1 + 1] + shifted, run_vals)
        run_ids = jnp.where(inside, row + (r1 * PEER_TOPK - start), run_ids)
        start += count
    return jnp.concatenate(vals + [run_vals], axis=0), jnp.concatenate(ids + [run_ids], axis=0)


def _topk_rows(x_scr, ids, n_rows, v_out, i_out):
    def body(j, carry):
        x = x_scr[0:n_rows, :]
        m = jnp.max(x, axis=0, keepdims=True)
        idx = jnp.min(jnp.where(x == m, ids, _NO_CAND), axis=0, keepdims=True)
        v_out[pl.ds(j, 1), :] = m
        i_out[pl.ds(j, 1), :] = idx
        x_scr[0:n_rows, :] = jnp.where(ids == idx, -jnp.inf, x)
        return carry

    lax.fori_loop(0, PEER_TOPK, body, 0, unroll=2)


def _peer_topk_kernel(q_ref, k1_ref, k2_ref, u_ref, v_ref, e_ref, g_ref, ub_ref, vb_ref, x_scr, v1_scr, i1_scr,
                      v2_scr, i2_scr, ts_scr, tc_scr, *, tm):
    ub_ref[...] = u_ref[...].astype(BF16)
    vb_ref[...] = v_ref[...].astype(BF16)
    key_ids = lax.broadcasted_iota(jnp.int32, (N_KEYS, tm), 0)
    for half, (k_ref, v_scr, i_scr) in enumerate(((k1_ref, v1_scr, i1_scr), (k2_ref, v2_scr, i2_scr))):
        qh = q_ref[:, half * PEER_HALF:(half + 1) * PEER_HALF]
        x_scr[...] = lax.dot_general(k_ref[...], qh, NT_DIMS, preferred_element_type=F32)
        _topk_rows(x_scr, key_ids, N_KEYS, v_scr, i_scr)
    cand_vals, cand_ids = _candidates(v1_scr[...], v2_scr[...], tm)
    x_scr[0:_CAND_ROWS, :] = cand_vals
    _topk_rows(x_scr, cand_ids, _CAND_ROWS, ts_scr, tc_scr)
    tc = tc_scr[...]
    c1, c2 = tc // PEER_TOPK, tc % PEER_TOPK
    i1, i2 = i1_scr[...], i2_scr[...]
    e1 = jnp.zeros_like(tc)
    e2 = jnp.zeros_like(tc)
    for a in range(PEER_TOPK):
        e1 = jnp.where(c1 == a, i1[a:a + 1], e1)
        e2 = jnp.where(c2 == a, i2[a:a + 1], e2)
    e_ref[...] = e1 * N_KEYS + e2
    ts = ts_scr[...]
    p = jnp.exp(ts - jnp.max(ts, axis=0, keepdims=True))
    g_ref[...] = p / jnp.sum(p, axis=0, keepdims=True)


def _peer_topk(q, k1_bf, k2_bf, peer_u, peer_v, layer):
    tm = TOPK_TM
    slab = LANES
    n_slabs = N_EXPERTS // slab
    assert (N_TOK // tm) * PEER_HEADS >= n_slabs
    row_f = pltpu.VMEM((PEER_TOPK, tm), F32)
    row_i = pltpu.VMEM((PEER_TOPK, tm), jnp.int32)
    slab_of = lambda i, h: jnp.minimum(i * PEER_HEADS + h, n_slabs - 1)
    table_in = pl.BlockSpec((None, slab, D_MODEL), lambda i, h: (layer, slab_of(i, h), 0))
    table_out = pl.BlockSpec((slab, D_MODEL), lambda i, h: (slab_of(i, h), 0))
    table_shape = jax.ShapeDtypeStruct((N_EXPERTS, D_MODEL), BF16)
    return pl.pallas_call(
        functools.partial(_peer_topk_kernel, tm=tm),
        out_shape=(jax.ShapeDtypeStruct((PEER_HEADS * PEER_TOPK, N_TOK), jnp.int32),
                   jax.ShapeDtypeStruct((PEER_HEADS * PEER_TOPK, N_TOK), F32), table_shape, table_shape),
        grid=(N_TOK // tm, PEER_HEADS),
        in_specs=[
            pl.BlockSpec((tm, PEER_QDIM), lambda i, h: (i, h)),
            pl.BlockSpec((None, N_KEYS, PEER_HALF), lambda i, h: (layer, 0, 0)),
            pl.BlockSpec((None, N_KEYS, PEER_HALF), lambda i, h: (layer, 0, 0)),
            table_in, table_in,
        ],
        out_specs=(pl.BlockSpec((PEER_TOPK, tm), lambda i, h: (h, i)),
                   pl.BlockSpec((PEER_TOPK, tm), lambda i, h: (h, i)), table_out, table_out),
        scratch_shapes=[pltpu.VMEM((N_KEYS, tm), F32), row_f, row_i, row_f, row_i, row_f, row_i],
        compiler_params=_params("arbitrary", "arbitrary"),
        name="peer_topk",
    )(q, k1_bf, k2_bf, peer_u, peer_v)


def _peer_mask_kernel(e_ref, g_ref, m_ref, et_scr, gt_scr, slab_scr, *, tb):
    et_scr[...] = e_ref[...].T
    gt_scr[...] = g_ref[...].T
    key = lax.broadcasted_iota(jnp.int32, (N_KEYS, PEER_HEADS * PEER_TOPK), 0)

    def body(t, carry):
        er = et_scr[pl.ds(t, 1), :]
        gr = gt_scr[pl.ds(t, 1), :]
        p1 = jnp.where(key == er // N_KEYS, gr, 0.0).astype(BF16)
        p2 = jnp.where(key == er % N_KEYS, 1.0, 0.0).astype(BF16)
        mt = lax.dot_general(p1, p2, NT_DIMS, preferred_element_type=F32)
        slab_scr[pl.ds(pl.multiple_of(t * MASK_PITCH, 4), N_KEYS), :] = mt
        return carry

    lax.fori_loop(0, tb, body, 0, unroll=64)
    for k1 in range(N_KEYS):
        m_ref[:, k1 * N_KEYS:(k1 + 1) * N_KEYS] = slab_scr[pl.ds(k1, tb, stride=MASK_PITCH), :].astype(BF16)


def _peer_mask(experts_t, gates_t):
    tb = MASK_TOKENS
    pairs = PEER_HEADS * PEER_TOPK
    return pl.pallas_call(
        functools.partial(_peer_mask_kernel, tb=tb),
        out_shape=jax.ShapeDtypeStruct((N_TOK, N_EXPERTS), BF16),
        grid=(N_TOK // tb,),
        in_specs=[pl.BlockSpec((pairs, tb), lambda i: (0, i)), pl.BlockSpec((pairs, tb), lambda i: (0, i))],
        out_specs=pl.BlockSpec((tb, N_EXPERTS), lambda i: (i, 0)),
        scratch_shapes=[pltpu.VMEM((tb, pairs), jnp.int32), pltpu.VMEM((tb, pairs), F32),
                        pltpu.VMEM((tb * MASK_PITCH, N_KEYS), F32)],
        compiler_params=_params("arbitrary"),
        name="peer_mask",
    )(experts_t, gates_t)


def _peer_expert_kernel(h_ref, u_ref, v_ref, m_ref, x_ref, g2_ref, o_ref, *, tm):
    j = pl.program_id(1)

    @pl.when(j == 0)
    def _():
        o_ref[...] = jnp.zeros_like(o_ref)

    s = lax.dot_general(h_ref[...], u_ref[...], NT_DIMS, preferred_element_type=F32)
    a = m_ref[...] * jax.nn.gelu(s.astype(BF16))
    o_ref[...] += _dot(a, v_ref[...])

    @pl.when(j == pl.num_programs(1) - 1)
    def _():
        r = _mod_row(pl.program_id(0) * tm)
        o_ref[...] = x_ref[...] + g2_ref[pl.ds(r, 1), :] * o_ref[...]


def _peer_expert(h2, u_bf, v_bf, mask, x1, mod):
    tm, te = EXPERT_TM, EXPERT_TE
    return pl.pallas_call(
        functools.partial(_peer_expert_kernel, tm=tm),
        out_shape=jax.ShapeDtypeStruct((N_TOK, D_MODEL), F32),
        grid=(N_TOK // tm, N_EXPERTS // te),
        in_specs=[
            pl.BlockSpec((tm, D_MODEL), lambda i, j: (i, 0)),
            pl.BlockSpec((te, D_MODEL), lambda i, j: (j, 0)),
            pl.BlockSpec((te, D_MODEL), lambda i, j: (j, 0)),
            pl.BlockSpec((tm, te), lambda i, j: (i, j)),
            pl.BlockSpec((tm, D_MODEL), lambda i, j: (i, 0)),
            pl.BlockSpec((MOD_ROWS, D_MODEL), lambda i, j: (0, 5)),
        ],
        out_specs=pl.BlockSpec((tm, D_MODEL), lambda i, j: (i, 0)),
        compiler_params=_params("arbitrary", "arbitrary", vmem=VMEM_LARGE),
        name="peer_expert",
    )(h2, u_bf, v_bf, mask, x1, mod)


def kernel(x_prompt, x_sample, cache_k, cache_v, state_hgrn, c, c_ctx, ada_w, ada_b, norm1_g, norm2_g, w_in,
           q_norm_g, k_norm_g, hg_lb, hg_onorm_g, conv_w, conv_b, w_out, peer_wq, peer_k1, peer_k2, peer_u, peer_v):
    x_p, x_s = x_prompt.reshape(N_PROMPT, D_MODEL), x_sample.reshape(N_SAMPLE, D_MODEL)
    cvec = jnp.concatenate([c_ctx[None, :], c, jnp.zeros((MOD_ROWS - 1 - DEC_BATCH, D_MODEL), F32)], axis=0)
    mod_all = _ada(cvec, ada_w, ada_b)

    w_in_bf, w_out_bf, wq_bf = w_in.astype(BF16), w_out.astype(BF16), peer_wq.astype(BF16)
    k1_bf, k2_bf = peer_k1.astype(BF16), peer_k2.astype(BF16)

    ks_out, vs_out, ss_out = [], [], []
    for l in range(DEPTH):
        mod = mod_all[l]
        z = _inproj(x_p, x_s, mod, norm1_g[l], w_in_bf, l)

        fn_p = _fourier(z, SEQ, BATCH, 0)
        fn_s = _fourier(z, DEC_SEQ, DEC_BATCH, N_PROMPT)

        at_p, k_l, v_l = _attn_prompt(z, q_norm_g[l], k_norm_g[l])
        k_ctx = cache_k[:, l].reshape(DEC_BATCH, PAST_LEN, N_KV * HEAD_DIM)
        v_ctx = cache_v[:, l].reshape(DEC_BATCH, PAST_LEN, N_KV * HEAD_DIM)
        at_s = _attn_sample(z, q_norm_g[l], k_norm_g[l], k_ctx, v_ctx)

        hg_p, s_l = _hgrn(z, hg_lb, hg_onorm_g[l], l, SEQ, BATCH, 0, None)
        hg_s, _ = _hgrn(z, hg_lb, hg_onorm_g[l], l, DEC_SEQ, DEC_BATCH, N_PROMPT, state_hgrn[:, l])

        cv_p = _conv(z, conv_w[l], conv_b[l], SEQ, BATCH, 0)
        cv_s = _conv(z, conv_w[l], conv_b[l], DEC_SEQ, DEC_BATCH, N_PROMPT)

        x1, h2 = _outproj((fn_p, at_p, hg_p, cv_p), (fn_s, at_s, hg_s, cv_s), w_out_bf, l, x_p, x_s, mod,
                          norm2_g[l])

        q_peer = _peer_query(h2, wq_bf, l)
        experts_t, gates_t, u_bf, v_bf = _peer_topk(q_peer, k1_bf, k2_bf, peer_u, peer_v, l)
        mask = _peer_mask(experts_t, gates_t)
        x = _peer_expert(h2, u_bf, v_bf, mask, x1, mod)
        x_p = x_s = x

        ks_out.append(k_l.reshape(BATCH, SEQ, N_KV, HEAD_DIM))
        vs_out.append(v_l.reshape(BATCH, SEQ, N_KV, HEAD_DIM))
        ss_out.append(s_l)

    y_prompt = x[:N_PROMPT].reshape(BATCH, SEQ, D_MODEL)
    y_sample = x[N_PROMPT:].reshape(DEC_BATCH, DEC_SEQ, D_MODEL)
    return (y_prompt, y_sample, jnp.stack(ks_out, axis=1), jnp.stack(vs_out, axis=1), jnp.stack(ss_out, axis=1))
```

```python
import functools

import numpy as np
import jax
import jax.numpy as jnp
from jax import lax
from jax.experimental import pallas as pl
from jax.experimental.pallas import tpu as pltpu

F32 = jnp.float32
BF16 = jnp.bfloat16
MIX_DTYPE = BF16

D_MODEL = 2048
BATCH = 32
SEQ = 256
DEPTH = 2
DEC_BATCH = 4
DEC_SEQ = 1024
PAST_LEN = 256
GRID_W = 64
GROUP_W = D_MODEL // 4
FN_GROUPS = 4
FN_GW = GROUP_W // FN_GROUPS
HEAD_DIM = 128
N_HEADS = GROUP_W // HEAD_DIM
N_KV = 2
GQA_GROUP = N_HEADS // N_KV
ROPE_THETA = 10000.0
HG_DK = 128
HG_DV = 128
HG_HEADS = GROUP_W // HG_DK
CONV_K = 3
IN_COLS = 11 * GROUP_W
PEER_HEADS = 8
N_KEYS = 128
N_EXPERTS = N_KEYS * N_KEYS
PEER_QDIM = 256
PEER_HALF = PEER_QDIM // 2
PEER_TOPK = 16
EPS = 1e-6
F_FLOOR = 1e-30

N_PROMPT = BATCH * SEQ
N_SAMPLE = DEC_BATCH * DEC_SEQ
N_TOK = N_PROMPT + N_SAMPLE
MOD_ROWS = 8
LANES = 128
NORM_ROWS = 128
HG_CHUNK = 32
HG_GROUP = 8
HG_UNROLL = 16
HG_CUM_BLOCK = 256
MASK_TOKENS = 128
MASK_PITCH = 132

ADA_TN = 1024
INPROJ_TM, INPROJ_TN = 1024, 512
OUTPROJ_TM = 512
QUERY_TM = 512
TOPK_TM = 512
EXPERT_TM, EXPERT_TE = 1024, 512
ATTN_Q_ROWS = 256

VMEM_SMALL = 32 * 1024 * 1024
VMEM_INPROJ = 48 * 1024 * 1024
VMEM_LARGE = 56 * 1024 * 1024

COL_FN, COL_Q, COL_K, COL_V = 0, 4, 8, 10
COL_GQ, COL_GFF, COL_GFB, COL_GI, COL_GO = 12, 16, 20, 24, 28
COL_CB, COL_CC, COL_CX = 32, 36, 40

NT_DIMS = (((1,), (1,)), ((), ()))
TN_DIMS = (((0,), (0,)), ((), ()))


def _params(*sem, vmem=VMEM_SMALL):
    return pltpu.CompilerParams(dimension_semantics=sem, vmem_limit_bytes=vmem)


def _dot(a, b):
    return jnp.dot(a, b, preferred_element_type=F32)


def _split3(x):
    h = x.astype(BF16)
    r = x - h.astype(F32)
    m = r.astype(BF16)
    l = (r - m.astype(F32)).astype(BF16)
    return h, m, l


def _split2(x):
    h = x.astype(BF16)
    return h, (x - h.astype(F32)).astype(BF16)


def _mod_row(tok0):
    return jnp.where(tok0 >= N_PROMPT, (tok0 - N_PROMPT) // DEC_SEQ + 1, 0)


def _rms(x, g):
    return x * lax.rsqrt(jnp.mean(x * x, axis=-1, keepdims=True) + EPS) * g


def _ada_kernel(c_ref, w_ref, b_ref, o_ref):
    c = c_ref[...]
    s = (c * jax.nn.sigmoid(c)).astype(BF16)
    o_ref[0] = _dot(s, w_ref[0].astype(BF16)) + b_ref[0]


def _ada(cvec, ada_w, ada_b):
    tn = ADA_TN
    return pl.pallas_call(
        _ada_kernel,
        out_shape=jax.ShapeDtypeStruct((DEPTH, MOD_ROWS, 6 * D_MODEL), F32),
        grid=(DEPTH, 6 * D_MODEL // tn),
        in_specs=[
            pl.BlockSpec((MOD_ROWS, D_MODEL), lambda l, j: (0, 0)),
            pl.BlockSpec((1, D_MODEL, tn), lambda l, j: (l, 0, j)),
            pl.BlockSpec((1, 1, tn), lambda l, j: (l, 0, j)),
        ],
        out_specs=pl.BlockSpec((1, MOD_ROWS, tn), lambda l, j: (l, 0, j)),
        compiler_params=_params("arbitrary", "arbitrary"),
        name="ada",
    )(cvec, ada_w, ada_b.reshape(DEPTH, 1, 6 * D_MODEL))


def _token_source_specs(tm, x_prompt, x_sample, n_grid_axes):
    n_pb, n_sb = N_PROMPT // tm, N_SAMPLE // tm
    s_off = x_sample.shape[0] // tm - n_sb
    if n_grid_axes == 1:
        return (pl.BlockSpec((tm, D_MODEL), lambda i: (jnp.minimum(i, n_pb - 1), 0)),
                pl.BlockSpec((tm, D_MODEL), lambda i: (s_off + jnp.clip(i - n_pb, 0, n_sb - 1), 0)))
    return (pl.BlockSpec((tm, D_MODEL), lambda i, j: (jnp.minimum(i, n_pb - 1), 0)),
            pl.BlockSpec((tm, D_MODEL), lambda i, j: (s_off + jnp.clip(i - n_pb, 0, n_sb - 1), 0)))


def _inproj_kernel(xp_ref, xs_ref, sh_ref, sc_ref, g_ref, w_ref, o_ref, h_ref, *, tm):
    i, j = pl.program_id(0), pl.program_id(1)

    def prologue(x_ref):
        r = _mod_row(i * tm)
        gain = g_ref[...] * (1.0 + sc_ref[pl.ds(r, 1), :])
        shift = sh_ref[pl.ds(r, 1), :]

        def norm_rows(c, carry):
            rows = pl.ds(pl.multiple_of(c * NORM_ROWS, NORM_ROWS), NORM_ROWS)
            x = x_ref[rows, :]
            y = x * lax.rsqrt(jnp.mean(x * x, axis=-1, keepdims=True) + EPS)
            h_ref[rows, :] = (y * gain + shift).astype(BF16)
            return carry

        lax.fori_loop(0, tm // NORM_ROWS, norm_rows, 0)

    pl.when((j == 0) & (i < N_PROMPT // tm))(lambda: prologue(xp_ref))
    pl.when((j == 0) & (i >= N_PROMPT // tm))(lambda: prologue(xs_ref))
    o_ref[...] = _dot(h_ref[...], w_ref[...])


def _inproj(x_prompt, x_sample, mod, norm_g, w_in_bf, layer):
    tm, tn = INPROJ_TM, INPROJ_TN
    return pl.pallas_call(
        functools.partial(_inproj_kernel, tm=tm),
        out_shape=jax.ShapeDtypeStruct((N_TOK, IN_COLS), F32),
        grid=(N_TOK // tm, IN_COLS // tn),
        in_specs=[
            *_token_source_specs(tm, x_prompt, x_sample, 2),
            pl.BlockSpec((MOD_ROWS, D_MODEL), lambda i, j: (0, 0)),
            pl.BlockSpec((MOD_ROWS, D_MODEL), lambda i, j: (0, 1)),
            pl.BlockSpec((1, D_MODEL), lambda i, j: (0, 0)),
            pl.BlockSpec((None, D_MODEL, tn), lambda i, j: (layer, 0, j)),
        ],
        out_specs=pl.BlockSpec((tm, tn), lambda i, j: (i, j)),
        scratch_shapes=[pltpu.VMEM((tm, D_MODEL), BF16)],
        compiler_params=_params("arbitrary", "arbitrary", vmem=VMEM_INPROJ),
        name="inproj",
    )(x_prompt, x_sample, mod, mod, norm_g.reshape(1, D_MODEL), w_in_bf)


def _fourier_kernel(u_ref, cch_ref, ccl_ref, cth_ref, ctl_ref, o_ref):
    cch, ccl, cth, ctl = cch_ref[...], ccl_ref[...], cth_ref[...], ctl_ref[...]
    for g in range(0, FN_GROUPS, 2):
        ws = []
        for gg in (g, g + 1):
            uh, ul = _split2(u_ref[:, gg * FN_GW:(gg + 1) * FN_GW])
            ws.append(_dot(uh, cch) + (_dot(ul, cch) + _dot(uh, ccl)))
        st = jnp.concatenate([jnp.concatenate([w[:, :FN_GW] for w in ws], axis=1),
                              jnp.concatenate([w[:, FN_GW:] for w in ws], axis=1)], axis=0)
        sh, sl = _split2(st)
        o_ref[:, g * FN_GW:(g + 2) * FN_GW] = (_dot(cth, sh) + (_dot(cth, sl) + _dot(ctl, sh))).astype(MIX_DTYPE)


def _dft_consts(t):
    def cs(n):
        k = np.arange(n, dtype=np.float64)
        ang = 2.0 * np.pi * np.outer(k, k) / n
        return np.cos(ang) / np.sqrt(n), np.sin(ang) / np.sqrt(n)

    cc, sc = cs(FN_GW)
    ct, st = cs(t)
    right = np.concatenate([cc, sc], axis=1).astype(np.float32)
    left = np.concatenate([ct, -st], axis=1).astype(np.float32)

    def hl(a):
        a = jnp.asarray(a)
        h = a.astype(BF16)
        return h, (a - h.astype(F32)).astype(BF16)

    return hl(right) + hl(left)


def _fourier(z, t, nb, row0):
    cch, ccl, cth, ctl = _dft_consts(t)
    rb = row0 // t
    return pl.pallas_call(
        _fourier_kernel,
        out_shape=jax.ShapeDtypeStruct((nb * t, GROUP_W), MIX_DTYPE),
        grid=(nb,),
        in_specs=[
            pl.BlockSpec((t, GROUP_W), lambda b: (rb + b, COL_FN // 4)),
            pl.BlockSpec((FN_GW, 2 * FN_GW), lambda b: (0, 0)),
            pl.BlockSpec((FN_GW, 2 * FN_GW), lambda b: (0, 0)),
            pl.BlockSpec((t, 2 * t), lambda b: (0, 0)),
            pl.BlockSpec((t, 2 * t), lambda b: (0, 0)),
        ],
        out_specs=pl.BlockSpec((t, GROUP_W), lambda b: (b, 0)),
        compiler_params=_params("arbitrary"),
        name=f"fourier_t{t}",
    )(z, cch, ccl, cth, ctl)


def _softmax_pv(q_bf, k_bf, v_bf):
    s = lax.dot_general(q_bf, k_bf, NT_DIMS, preferred_element_type=F32) * (HEAD_DIM ** -0.5)
    p = jnp.exp(s - jnp.max(s, axis=-1, keepdims=True))
    return _dot(p.astype(BF16), v_bf) / jnp.sum(p, axis=-1, keepdims=True)


def _attn_prompt_kernel(q_ref, k_ref, v_ref, qg_ref, kg_ref, o_ref, kc_ref, vc_ref):
    kn = _rms(k_ref[...], kg_ref[...])
    v = v_ref[...]
    head_rows = pl.ds(pl.program_id(1), SEQ, stride=N_KV)
    kc_ref[0, head_rows, :] = kn
    vc_ref[0, head_rows, :] = v
    kb, vb = kn.astype(BF16), v.astype(BF16)
    for g in range(GQA_GROUP):
        qn = _rms(q_ref[:, g * HEAD_DIM:(g + 1) * HEAD_DIM], qg_ref[...])
        o_ref[:, g * HEAD_DIM:(g + 1) * HEAD_DIM] = _softmax_pv(qn.astype(BF16), kb, vb).astype(MIX_DTYPE)


def _attn_prompt(z, q_g, k_g):
    t = SEQ
    kv_shape = jax.ShapeDtypeStruct((BATCH, SEQ * N_KV, HEAD_DIM), F32)
    kv_spec = pl.BlockSpec((1, SEQ * N_KV, HEAD_DIM), lambda b, h: (b, 0, 0))
    return pl.pallas_call(
        _attn_prompt_kernel,
        out_shape=(jax.ShapeDtypeStruct((N_PROMPT, GROUP_W), MIX_DTYPE), kv_shape, kv_shape),
        grid=(BATCH, N_KV),
        in_specs=[
            pl.BlockSpec((t, GQA_GROUP * HEAD_DIM), lambda b, h: (b, COL_Q // 2 + h)),
            pl.BlockSpec((t, HEAD_DIM), lambda b, h: (b, COL_K + h)),
            pl.BlockSpec((t, HEAD_DIM), lambda b, h: (b, COL_V + h)),
            pl.BlockSpec((1, HEAD_DIM), lambda b, h: (0, 0)),
            pl.BlockSpec((1, HEAD_DIM), lambda b, h: (0, 0)),
        ],
        out_specs=(
            pl.BlockSpec((t, GQA_GROUP * HEAD_DIM), lambda b, h: (b, h)),
            kv_spec, kv_spec,
        ),
        compiler_params=_params("arbitrary", "arbitrary"),
        name="attn_prompt",
    )(z, z, z, q_g.reshape(1, HEAD_DIM), k_g.reshape(1, HEAD_DIM))


def _rope(x, cosf, sinf):
    return x * cosf + pltpu.roll(x, HEAD_DIM // 2, axis=1) * sinf


def _attn_sample_kernel(q_ref, k_ref, v_ref, kctx_ref, vctx_ref, qg_ref, kg_ref, cos_ref, sin_ref,
                        o_ref, kall_ref, vall_ref, *, qb):
    cosf, sinf = cos_ref[...], sin_ref[...]
    kall_ref[0:PAST_LEN, :] = kctx_ref[0].astype(BF16)
    vall_ref[0:PAST_LEN, :] = vctx_ref[0].astype(BF16)
    kall_ref[PAST_LEN:, :] = _rope(_rms(k_ref[...], kg_ref[...]), cosf, sinf).astype(BF16)
    vall_ref[PAST_LEN:, :] = v_ref[...].astype(BF16)
    kb, vb = kall_ref[...], vall_ref[...]
    for i in range(DEC_SEQ // qb):
        rows = slice(i * qb, (i + 1) * qb)
        for g in range(GQA_GROUP):
            cols = slice(g * HEAD_DIM, (g + 1) * HEAD_DIM)
            qn = _rope(_rms(q_ref[rows, cols], qg_ref[...]), cosf[rows], sinf[rows])
            o_ref[rows, cols] = _softmax_pv(qn.astype(BF16), kb, vb).astype(MIX_DTYPE)


def _rope_tables():
    rows = DEC_SEQ // GRID_W
    row = np.repeat(np.arange(rows, dtype=np.float32), GRID_W)
    col = np.tile(np.arange(GRID_W, dtype=np.float32), rows)
    n_freq = HEAD_DIM // 4
    inv = (np.float32(ROPE_THETA) ** (-np.arange(n_freq, dtype=np.float32) / n_freq)).astype(np.float32)
    ang = np.concatenate([row[:, None] * inv, col[:, None] * inv], axis=-1).astype(np.float32)
    cos, sin = np.cos(ang.astype(np.float64)), np.sin(ang.astype(np.float64))
    cosf = np.concatenate([cos, cos], axis=-1).astype(np.float32)
    sinf = np.concatenate([-sin, sin], axis=-1).astype(np.float32)
    return jnp.asarray(cosf), jnp.asarray(sinf)


def _attn_sample(z, q_g, k_g, k_ctx, v_ctx):
    t = DEC_SEQ
    rb = N_PROMPT // t
    cosf, sinf = _rope_tables()
    const = lambda b, h: (0, 0)
    return pl.pallas_call(
        functools.partial(_attn_sample_kernel, qb=ATTN_Q_ROWS),
        out_shape=jax.ShapeDtypeStruct((N_SAMPLE, GROUP_W), MIX_DTYPE),
        grid=(DEC_BATCH, N_KV),
        in_specs=[
            pl.BlockSpec((t, GQA_GROUP * HEAD_DIM), lambda b, h: (rb + b, COL_Q // 2 + h)),
            pl.BlockSpec((t, HEAD_DIM), lambda b, h: (rb + b, COL_K + h)),
            pl.BlockSpec((t, HEAD_DIM), lambda b, h: (rb + b, COL_V + h)),
            pl.BlockSpec((1, PAST_LEN, HEAD_DIM), lambda b, h: (b, 0, h)),
            pl.BlockSpec((1, PAST_LEN, HEAD_DIM), lambda b, h: (b, 0, h)),
            pl.BlockSpec((1, HEAD_DIM), const),
            pl.BlockSpec((1, HEAD_DIM), const),
            pl.BlockSpec((t, HEAD_DIM), const),
            pl.BlockSpec((t, HEAD_DIM), const),
        ],
        out_specs=pl.BlockSpec((t, GQA_GROUP * HEAD_DIM), lambda b, h: (b, h)),
        scratch_shapes=[pltpu.VMEM((PAST_LEN + t, HEAD_DIM), BF16), pltpu.VMEM((PAST_LEN + t, HEAD_DIM), BF16)],
        compiler_params=_params("arbitrary", "arbitrary"),
        name="attn_sample",
    )(z, z, z, k_ctx, v_ctx, q_g.reshape(1, HEAD_DIM), k_g.reshape(1, HEAD_DIM), cosf, sinf)


def _hgrn_chain(rev, r0, gq_ref, gi_ref, k_scr, cum_scr, st_scr, o_scr):
    c = HG_CHUNK
    d = 1 if rev else 0
    rows = pl.ds(r0, c)
    qc, vc = gq_ref[rows, :], gi_ref[rows, :]
    kc, cc = k_scr[d, rows, :], cum_scr[d, rows, :]
    st = st_scr[d]
    last = cc[0:1] if rev else cc[c - 1:c]
    qs = (qc * jnp.exp(cc)).astype(BF16)
    o_inter = lax.dot_general(qs, st.astype(BF16), NT_DIMS, preferred_element_type=F32)
    ks = (kc * jnp.exp(last - cc)).astype(BF16)
    upd = lax.dot_general(vc.astype(BF16), ks, TN_DIMS, preferred_element_type=F32)
    st_scr[d] = st * jnp.exp(last) + upd
    n_grp = c // HG_GROUP
    row = lax.broadcasted_iota(jnp.int32, (HG_GROUP, 1), 0)
    krow = lax.broadcasted_iota(jnp.int32, (c, 1), 0)
    diag, att = [], []
    for i in range(n_grp):
        lo = i * HG_GROUP
        grp = slice(lo, lo + HG_GROUP)
        q_i, k_i, v_i, c_i = qc[grp], kc[grp], vc[grp], cc[grp]
        acc = jnp.zeros((HG_GROUP, HG_DV), F32)
        for delta in range(HG_GROUP):
            if delta == 0:
                k_sh, c_sh, v_sh = k_i, c_i, v_i
            else:
                sh = (HG_GROUP - delta) if rev else delta
                k_sh, c_sh, v_sh = (pltpu.roll(a, sh, axis=0) for a in (k_i, c_i, v_i))
            a = jnp.sum(q_i * k_sh * jnp.exp(jnp.minimum(c_i - c_sh, 0.0)), axis=-1, keepdims=True)
            valid = (row <= HG_GROUP - 1 - delta) if rev else (row >= delta)
            acc = acc + jnp.where(valid, a, 0.0) * v_sh
        diag.append(acc)
        if (i == n_grp - 1) if rev else (i == 0):
            att.append(jnp.zeros((HG_GROUP, c), F32))
            continue
        if rev:
            b, other = cc[lo + HG_GROUP:lo + HG_GROUP + 1], krow >= lo + HG_GROUP
        else:
            b, other = cc[lo - 1:lo], krow < lo
        qs_i = (q_i * jnp.exp(c_i - b)).astype(BF16)
        ks_i = jnp.where(other, kc * jnp.exp(jnp.minimum(b - cc, 0.0)), 0.0).astype(BF16)
        att.append(lax.dot_general(qs_i, ks_i, NT_DIMS, preferred_element_type=F32))
    o_off = _dot(jnp.concatenate(att, axis=0).astype(BF16), vc.astype(BF16))
    o_scr[d, rows, :] = o_inter + jnp.concatenate(diag, axis=0) + o_off


def _hgrn_kernel(*refs, layer, t, has_ctx):
    gq_ref, gff_ref, gfb_ref, gi_ref, go_ref, lb_ref, on_ref, bdf_ref, bdb_ref = refs[:9]
    n_in = 9
    s0_ref = None
    if has_ctx:
        s0_ref = refs[9]
        n_in = 10
    o_ref, sout_ref = refs[n_in], refs[n_in + 1]
    k_scr, cum_scr, st_scr, o_scr = refs[n_in + 2:]

    raw = lb_ref[...]
    e = jnp.exp(raw - jnp.max(raw, axis=0, keepdims=True))
    soft = e / jnp.sum(e, axis=0, keepdims=True)
    csum = soft[0]
    for i in range(1, layer + 1):
        csum = csum + soft[i]
    lb = csum - soft[0]

    for d, (g_ref, bd_ref) in enumerate(((gff_ref, bdf_ref), (gfb_ref, bdb_ref))):
        lbd = lb[d:d + 1]
        f = jnp.maximum(lbd + (1.0 - lbd) * jax.nn.sigmoid(g_ref[...]), F_FLOOR)
        k_scr[d] = 1.0 - f
        lf = jnp.log(f)
        bd = bd_ref[...]
        for p in range(t // HG_CUM_BLOCK):
            rows = slice(p * HG_CUM_BLOCK, (p + 1) * HG_CUM_BLOCK)
            h, m, l = _split3(lf[rows])
            cum_scr[d, rows, :] = _dot(bd, h) + _dot(bd, m) + _dot(bd, l)
        if has_ctx:
            st_scr[d] = s0_ref[0, d, 0].T
        else:
            st_scr[d] = jnp.zeros((HG_DV, HG_DK), F32)

    n = t // HG_CHUNK

    def body(i, carry):
        _hgrn_chain(False, pl.multiple_of(i * HG_CHUNK, HG_CHUNK), gq_ref, gi_ref, k_scr, cum_scr, st_scr, o_scr)
        _hgrn_chain(True, pl.multiple_of((n - 1 - i) * HG_CHUNK, HG_CHUNK), gq_ref, gi_ref, k_scr, cum_scr,
                    st_scr, o_scr)
        return carry

    lax.fori_loop(0, n, body, 0, unroll=min(n, HG_UNROLL))

    o = _rms(o_scr[0] + o_scr[1], on_ref[...])
    go = go_ref[...]
    o_ref[...] = (o * (go * jax.nn.sigmoid(go))).astype(MIX_DTYPE)
    for d in range(2):
        sout_ref[0, d, 0] = st_scr[d].T


def _cum_consts():
    r = np.arange(HG_CUM_BLOCK)
    same = (r[:, None] // HG_CHUNK) == (r[None, :] // HG_CHUNK)
    fwd = same & (r[None, :] <= r[:, None])
    bwd = same & (r[None, :] >= r[:, None])
    return jnp.asarray(fwd, BF16), jnp.asarray(bwd, BF16)


def _hgrn(z, hg_lb, onorm_g, layer, t, nb, row0, s0):
    rb = row0 // t
    bdf, bdb = _cum_consts()
    has_ctx = s0 is not None
    col = lambda c0: (lambda b, h: (rb + b, c0 + h))
    in_specs = [
        pl.BlockSpec((t, HG_DK), col(COL_GQ)),
        pl.BlockSpec((t, HG_DK), col(COL_GFF)),
        pl.BlockSpec((t, HG_DK), col(COL_GFB)),
        pl.BlockSpec((t, HG_DV), col(COL_GI)),
        pl.BlockSpec((t, HG_DV), col(COL_GO)),
        pl.BlockSpec((DEPTH, 2, HG_DK), lambda b, h: (0, 0, h)),
        pl.BlockSpec((1, HG_DV), lambda b, h: (0, 0)),
        pl.BlockSpec((HG_CUM_BLOCK, HG_CUM_BLOCK), lambda b, h: (0, 0)),
        pl.BlockSpec((HG_CUM_BLOCK, HG_CUM_BLOCK), lambda b, h: (0, 0)),
    ]
    args = [z, z, z, z, z, hg_lb, onorm_g.reshape(1, HG_DV), bdf, bdb]
    if has_ctx:
        in_specs.append(pl.BlockSpec((1, 2, 1, HG_DK, HG_DV), lambda b, h: (b, 0, h, 0, 0)))
        args.append(s0)
    return pl.pallas_call(
        functools.partial(_hgrn_kernel, layer=layer, t=t, has_ctx=has_ctx),
        out_shape=(jax.ShapeDtypeStruct((nb * t, GROUP_W), MIX_DTYPE),
                   jax.ShapeDtypeStruct((nb, 2, HG_HEADS, HG_DK, HG_DV), F32)),
        grid=(nb, HG_HEADS),
        in_specs=in_specs,
        out_specs=(pl.BlockSpec((t, HG_DV), lambda b, h: (b, h)),
                   pl.BlockSpec((1, 2, 1, HG_DK, HG_DV), lambda b, h: (b, 0, h, 0, 0))),
        scratch_shapes=[pltpu.VMEM((2, t, HG_DK), F32), pltpu.VMEM((2, t, HG_DK), F32),
                        pltpu.VMEM((2, HG_DV, HG_DK), F32), pltpu.VMEM((2, t, HG_DV), F32)],
        compiler_params=_params("arbitrary", "arbitrary"),
        name=f"hgrn_t{t}",
    )(*args)


def _conv_kernel(cb_ref, cc_ref, cx_ref, w_ref, b_ref, o_ref, *, t):
    p = cc_ref[...] * cx_ref[...]
    row = lax.broadcasted_iota(jnp.int32, (t, 1), 0)
    prev = jnp.where(row >= 1, pltpu.roll(p, 1, axis=0), 0.0)
    nxt = jnp.where(row <= t - 2, pltpu.roll(p, t - 1, axis=0), 0.0)
    w = w_ref[...]
    y = prev * w[0:1] + p * w[1:2] + nxt * w[2:3] + b_ref[...]
    o_ref[...] = (cb_ref[...] * y).astype(MIX_DTYPE)


def _conv(z, w, b, t, nb, row0):
    rb = row0 // t
    return pl.pallas_call(
        functools.partial(_conv_kernel, t=t),
        out_shape=jax.ShapeDtypeStruct((nb * t, GROUP_W), MIX_DTYPE),
        grid=(nb,),
        in_specs=[
            pl.BlockSpec((t, GROUP_W), lambda i: (rb + i, COL_CB // 4)),
            pl.BlockSpec((t, GROUP_W), lambda i: (rb + i, COL_CC // 4)),
            pl.BlockSpec((t, GROUP_W), lambda i: (rb + i, COL_CX // 4)),
            pl.BlockSpec((CONV_K, GROUP_W), lambda i: (0, 0)),
            pl.BlockSpec((1, GROUP_W), lambda i: (0, 0)),
        ],
        out_specs=pl.BlockSpec((t, GROUP_W), lambda i: (i, 0)),
        compiler_params=_params("arbitrary"),
        name=f"conv_t{t}",
    )(z, z, z, w, b.reshape(1, GROUP_W))


def _outproj_kernel(*refs, tm):
    prompt_refs, sample_refs = refs[0:4], refs[4:8]
    w_ref, xp_ref, xs_ref, g1_ref, sh_ref, sc_ref, n2_ref, x1_ref, h2_ref = refs[8:]
    i = pl.program_id(0)
    r = _mod_row(i * tm)

    def run(group_refs, x_ref):
        acc = None
        for g, ref in enumerate(group_refs):
            part = _dot(ref[...], w_ref[g * GROUP_W:(g + 1) * GROUP_W, :])
            acc = part if acc is None else acc + part
        x1 = x_ref[...] + g1_ref[pl.ds(r, 1), :] * acc
        x1_ref[...] = x1
        h2 = _rms(x1, n2_ref[...]) * (1.0 + sc_ref[pl.ds(r, 1), :]) + sh_ref[pl.ds(r, 1), :]
        h2_ref[...] = h2.astype(BF16)

    pl.when(i < N_PROMPT // tm)(lambda: run(prompt_refs, xp_ref))
    pl.when(i >= N_PROMPT // tm)(lambda: run(sample_refs, xs_ref))


def _outproj(prompt_groups, sample_groups, w_out_bf, layer, x_prompt, x_sample, mod, norm2_g):
    tm = OUTPROJ_TM
    n_pb, n_sb = N_PROMPT // tm, N_SAMPLE // tm
    p_spec = pl.BlockSpec((tm, GROUP_W), lambda i: (jnp.minimum(i, n_pb - 1), 0))
    s_spec = pl.BlockSpec((tm, GROUP_W), lambda i: (jnp.clip(i - n_pb, 0, n_sb - 1), 0))
    modspec = lambda k: pl.BlockSpec((MOD_ROWS, D_MODEL), lambda i: (0, k))
    return pl.pallas_call(
        functools.partial(_outproj_kernel, tm=tm),
        out_shape=(jax.ShapeDtypeStruct((N_TOK, D_MODEL), F32), jax.ShapeDtypeStruct((N_TOK, D_MODEL), BF16)),
        grid=(N_TOK // tm,),
        in_specs=[p_spec] * 4 + [s_spec] * 4 + [
            pl.BlockSpec((None, D_MODEL, D_MODEL), lambda i: (layer, 0, 0), pipeline_mode=pl.Buffered(1)),
            *_token_source_specs(tm, x_prompt, x_sample, 1),
            modspec(2), modspec(3), modspec(4),
            pl.BlockSpec((1, D_MODEL), lambda i: (0, 0))],
        out_specs=(pl.BlockSpec((tm, D_MODEL), lambda i: (i, 0)), pl.BlockSpec((tm, D_MODEL), lambda i: (i, 0))),
        compiler_params=_params("arbitrary", vmem=VMEM_LARGE),
        name="outproj",
    )(*prompt_groups, *sample_groups, w_out_bf, x_prompt, x_sample, mod, mod, mod, norm2_g.reshape(1, D_MODEL))


def _peer_query_kernel(h_ref, wq_ref, q_ref):
    q_ref[...] = _dot(h_ref[...], wq_ref[...]).astype(BF16)


def _peer_query(h2, wq_bf, layer):
    tm = QUERY_TM
    width = PEER_HEADS * PEER_QDIM
    return pl.pallas_call(
        _peer_query_kernel,
        out_shape=jax.ShapeDtypeStruct((N_TOK, width), BF16),
        grid=(N_TOK // tm,),
        in_specs=[pl.BlockSpec((tm, D_MODEL), lambda i: (i, 0)),
                  pl.BlockSpec((None, D_MODEL, width), lambda i: (layer, 0, 0))],
        out_specs=pl.BlockSpec((tm, width), lambda i: (i, 0)),
        compiler_params=_params("arbitrary"),
        name="peer_query",
    )(h2, wq_bf)


_CAND_RUNS = ((2, 5), (3, 4), (4, 3), (5, 2), (6, 2), (7, 2))
_CAND_RUN_ROWS = 24
_CAND_ROWS = 4 * 8 + _CAND_RUN_ROWS
_NO_CAND = 1 << 20


def _candidates(v1, v2, tm):
    sub = lax.broadcasted_iota(jnp.int32, (8, tm), 0)
    vals = [v1[0:1] + v2[0:8], v1[0:1] + v2[8:16], v1[1:2] + v2[0:8], v1[8:16] + v2[0:1]]
    ids = [sub, sub + 8, sub + PEER_TOPK, (sub + 8) * PEER_TOPK]
    row = lax.broadcasted_iota(jnp.int32, (_CAND_RUN_ROWS, tm), 0)
    v2_rep = jnp.concatenate([v2[0:8]] * (_CAND_RUN_ROWS // 8), axis=0)
    run_vals = jnp.full((_CAND_RUN_ROWS, tm), -jnp.inf, F32)
    run_ids = jnp.full((_CAND_RUN_ROWS, tm), _NO_CAND, jnp.int32)
    start = 0
    for r1, count in _CAND_RUNS:
        inside = (row >= start) & (row < start + count)
        shifted = v2_rep if start % 8 == 0 else pltpu.roll(v2_rep, start % 8, axis=0)
        run_vals = jnp.where(inside, v1[r1:r1 + 1] + shifted, run_vals)
        run_ids = jnp.where(inside, row + (r1 * PEER_TOPK - start), run_ids)
        start += count
    return jnp.concatenate(vals + [run_vals], axis=0), jnp.concatenate(ids + [run_ids], axis=0)


def _topk_rows(x_scr, ids, n_rows, v_out, i_out):
    def body(j, carry):
        x = x_scr[0:n_rows, :]
        m = jnp.max(x, axis=0, keepdims=True)
        idx = jnp.min(jnp.where(x == m, ids, _NO_CAND), axis=0, keepdims=True)
        v_out[pl.ds(j, 1), :] = m
        i_out[pl.ds(j, 1), :] = idx
        x_scr[0:n_rows, :] = jnp.where(ids == idx, -jnp.inf, x)
        return carry

    lax.fori_loop(0, PEER_TOPK, body, 0, unroll=8)


def _peer_topk_kernel(q_ref, k1_ref, k2_ref, u_ref, v_ref, e_ref, g_ref, ub_ref, vb_ref, x_scr, v1_scr, i1_scr,
                      v2_scr, i2_scr, ts_scr, tc_scr, *, tm):
    ub_ref[...] = u_ref[...].astype(BF16)
    vb_ref[...] = v_ref[...].astype(BF16)
    key_ids = lax.broadcasted_iota(jnp.int32, (N_KEYS, tm), 0)
    for half, (k_ref, v_scr, i_scr) in enumerate(((k1_ref, v1_scr, i1_scr), (k2_ref, v2_scr, i2_scr))):
        qh = q_ref[:, half * PEER_HALF:(half + 1) * PEER_HALF]
        x_scr[...] = lax.dot_general(k_ref[...], qh, NT_DIMS, preferred_element_type=F32)
        _topk_rows(x_scr, key_ids, N_KEYS, v_scr, i_scr)
    cand_vals, cand_ids = _candidates(v1_scr[...], v2_scr[...], tm)
    x_scr[0:_CAND_ROWS, :] = cand_vals
    _topk_rows(x_scr, cand_ids, _CAND_ROWS, ts_scr, tc_scr)
    tc = tc_scr[...]
    c1, c2 = tc // PEER_TOPK, tc % PEER_TOPK
    i1, i2 = i1_scr[...], i2_scr[...]
    e1 = jnp.zeros_like(tc)
    e2 = jnp.zeros_like(tc)
    for a in range(PEER_TOPK):
        e1 = jnp.where(c1 == a, i1[a:a + 1], e1)
        e2 = jnp.where(c2 == a, i2[a:a + 1], e2)
    e_ref[...] = e1 * N_KEYS + e2
    ts = ts_scr[...]
    p = jnp.exp(ts - jnp.max(ts, axis=0, keepdims=True))
    g_ref[...] = p / jnp.sum(p, axis=0, keepdims=True)


def _peer_topk(q, k1_bf, k2_bf, peer_u, peer_v, layer):
    tm = TOPK_TM
    slab = LANES
    n_slabs = N_EXPERTS // slab
    assert (N_TOK // tm) * PEER_HEADS >= n_slabs
    row_f = pltpu.VMEM((PEER_TOPK, tm), F32)
    row_i = pltpu.VMEM((PEER_TOPK, tm), jnp.int32)
    slab_of = lambda i, h: jnp.minimum(i * PEER_HEADS + h, n_slabs - 1)
    table_in = pl.BlockSpec((None, slab, D_MODEL), lambda i, h: (layer, slab_of(i, h), 0))
    table_out = pl.BlockSpec((slab, D_MODEL), lambda i, h: (slab_of(i, h), 0))
    table_shape = jax.ShapeDtypeStruct((N_EXPERTS, D_MODEL), BF16)
    return pl.pallas_call(
        functools.partial(_peer_topk_kernel, tm=tm),
        out_shape=(jax.ShapeDtypeStruct((PEER_HEADS * PEER_TOPK, N_TOK), jnp.int32),
                   jax.ShapeDtypeStruct((PEER_HEADS * PEER_TOPK, N_TOK), F32), table_shape, table_shape),
        grid=(N_TOK // tm, PEER_HEADS),
        in_specs=[
            pl.BlockSpec((tm, PEER_QDIM), lambda i, h: (i, h)),
            pl.BlockSpec((None, N_KEYS, PEER_HALF), lambda i, h: (layer, 0, 0)),
            pl.BlockSpec((None, N_KEYS, PEER_HALF), lambda i, h: (layer, 0, 0)),
            table_in, table_in,
        ],
        out_specs=(pl.BlockSpec((PEER_TOPK, tm), lambda i, h: (h, i)),
                   pl.BlockSpec((PEER_TOPK, tm), lambda i, h: (h, i)), table_out, table_out),
        scratch_shapes=[pltpu.VMEM((N_KEYS, tm), F32), row_f, row_i, row_f, row_i, row_f, row_i],
        compiler_params=_params("arbitrary", "arbitrary"),
        name="peer_topk",
    )(q, k1_bf, k2_bf, peer_u, peer_v)


def _peer_mask_kernel(e_ref, g_ref, m_ref, et_scr, gt_scr, slab_scr, *, tb):
    et_scr[...] = e_ref[...].T
    gt_scr[...] = g_ref[...].T
    key = lax.broadcasted_iota(jnp.int32, (N_KEYS, PEER_HEADS * PEER_TOPK), 0)

    def body(t, carry):
        er = et_scr[pl.ds(t, 1), :]
        gr = gt_scr[pl.ds(t, 1), :]
        p1 = jnp.where(key == er // N_KEYS, gr, 0.0).astype(BF16)
        p2 = jnp.where(key == er % N_KEYS, 1.0, 0.0).astype(BF16)
        mt = lax.dot_general(p1, p2, NT_DIMS, preferred_element_type=F32)
        slab_scr[pl.ds(pl.multiple_of(t * MASK_PITCH, 4), N_KEYS), :] = mt
        return carry

    lax.fori_loop(0, tb, body, 0, unroll=64)
    for k1 in range(N_KEYS):
        m_ref[:, k1 * N_KEYS:(k1 + 1) * N_KEYS] = slab_scr[pl.ds(k1, tb, stride=MASK_PITCH), :].astype(BF16)


def _peer_mask(experts_t, gates_t):
    tb = MASK_TOKENS
    pairs = PEER_HEADS * PEER_TOPK
    return pl.pallas_call(
        functools.partial(_peer_mask_kernel, tb=tb),
        out_shape=jax.ShapeDtypeStruct((N_TOK, N_EXPERTS), BF16),
        grid=(N_TOK // tb,),
        in_specs=[pl.BlockSpec((pairs, tb), lambda i: (0, i)), pl.BlockSpec((pairs, tb), lambda i: (0, i))],
        out_specs=pl.BlockSpec((tb, N_EXPERTS), lambda i: (i, 0)),
        scratch_shapes=[pltpu.VMEM((tb, pairs), jnp.int32), pltpu.VMEM((tb, pairs), F32),
                        pltpu.VMEM((tb * MASK_PITCH, N_KEYS), F32)],
        compiler_params=_params("arbitrary"),
        name="peer_mask",
    )(experts_t, gates_t)


def _peer_expert_kernel(h_ref, u_ref, v_ref, m_ref, x_ref, g2_ref, o_ref, *, tm):
    j = pl.program_id(1)

    @pl.when(j == 0)
    def _():
        o_ref[...] = jnp.zeros_like(o_ref)

    s = lax.dot_general(h_ref[...], u_ref[...], NT_DIMS, preferred_element_type=F32)
    a = m_ref[...] * jax.nn.gelu(s.astype(BF16))
    o_ref[...] += _dot(a, v_ref[...])

    @pl.when(j == pl.num_programs(1) - 1)
    def _():
        r = _mod_row(pl.program_id(0) * tm)
        o_ref[...] = x_ref[...] + g2_ref[pl.ds(r, 1), :] * o_ref[...]


def _peer_expert(h2, u_bf, v_bf, mask, x1, mod):
    tm, te = EXPERT_TM, EXPERT_TE
    return pl.pallas_call(
        functools.partial(_peer_expert_kernel, tm=tm),
        out_shape=jax.ShapeDtypeStruct((N_TOK, D_MODEL), F32),
        grid=(N_TOK // tm, N_EXPERTS // te),
        in_specs=[
            pl.BlockSpec((tm, D_MODEL), lambda i, j: (i, 0)),
            pl.BlockSpec((te, D_MODEL), lambda i, j: (j, 0)),
            pl.BlockSpec((te, D_MODEL), lambda i, j: (j, 0)),
            pl.BlockSpec((tm, te), lambda i, j: (i, j)),
            pl.BlockSpec((tm, D_MODEL), lambda i, j: (i, 0)),
            pl.BlockSpec((MOD_ROWS, D_MODEL), lambda i, j: (0, 5)),
        ],
        out_specs=pl.BlockSpec((tm, D_MODEL), lambda i, j: (i, 0)),
        compiler_params=_params("arbitrary", "arbitrary", vmem=VMEM_LARGE),
        name="peer_expert",
    )(h2, u_bf, v_bf, mask, x1, mod)


def kernel(x_prompt, x_sample, cache_k, cache_v, state_hgrn, c, c_ctx, ada_w, ada_b, norm1_g, norm2_g, w_in,
           q_norm_g, k_norm_g, hg_lb, hg_onorm_g, conv_w, conv_b, w_out, peer_wq, peer_k1, peer_k2, peer_u, peer_v):
    x_p, x_s = x_prompt.reshape(N_PROMPT, D_MODEL), x_sample.reshape(N_SAMPLE, D_MODEL)
    cvec = jnp.concatenate([c_ctx[None, :], c, jnp.zeros((MOD_ROWS - 1 - DEC_BATCH, D_MODEL), F32)], axis=0)
    mod_all = _ada(cvec, ada_w, ada_b)

    w_in_bf, w_out_bf, wq_bf = w_in.astype(BF16), w_out.astype(BF16), peer_wq.astype(BF16)
    k1_bf, k2_bf = peer_k1.astype(BF16), peer_k2.astype(BF16)

    ks_out, vs_out, ss_out = [], [], []
    for l in range(DEPTH):
        mod = mod_all[l]
        z = _inproj(x_p, x_s, mod, norm1_g[l], w_in_bf, l)

        fn_p = _fourier(z, SEQ, BATCH, 0)
        fn_s = _fourier(z, DEC_SEQ, DEC_BATCH, N_PROMPT)

        at_p, k_l, v_l = _attn_prompt(z, q_norm_g[l], k_norm_g[l])
        k_ctx = cache_k[:, l].reshape(DEC_BATCH, PAST_LEN, N_KV * HEAD_DIM)
        v_ctx = cache_v[:, l].reshape(DEC_BATCH, PAST_LEN, N_KV * HEAD_DIM)
        at_s = _attn_sample(z, q_norm_g[l], k_norm_g[l], k_ctx, v_ctx)

        hg_p, s_l = _hgrn(z, hg_lb, hg_onorm_g[l], l, SEQ, BATCH, 0, None)
        hg_s, _ = _hgrn(z, hg_lb, hg_onorm_g[l], l, DEC_SEQ, DEC_BATCH, N_PROMPT, state_hgrn[:, l])

        cv_p = _conv(z, conv_w[l], conv_b[l], SEQ, BATCH, 0)
        cv_s = _conv(z, conv_w[l], conv_b[l], DEC_SEQ, DEC_BATCH, N_PROMPT)

        x1, h2 = _outproj((fn_p, at_p, hg_p, cv_p), (fn_s, at_s, hg_s, cv_s), w_out_bf, l, x_p, x_s, mod,
                          norm2_g[l])

        q_peer = _peer_query(h2, wq_bf, l)
        experts_t, gates_t, u_bf, v_bf = _peer_topk(q_peer, k1_bf, k2_bf, peer_u, peer_v, l)
        mask = _peer_mask(experts_t, gates_t)
        x = _peer_expert(h2, u_bf, v_bf, mask, x1, mod)
        x_p = x_s = x

        ks_out.append(k_l.reshape(BATCH, SEQ, N_KV, HEAD_DIM))
        vs_out.append(v_l.reshape(BATCH, SEQ, N_KV, HEAD_DIM))
        ss_out.append(s_l)

    y_prompt = x[:N_PROMPT].reshape(BATCH, SEQ, D_MODEL)
    y_sample = x[N_PROMPT:].reshape(DEC_BATCH, DEC_SEQ, D_MODEL)
    return (y_prompt, y_sample, jnp.stack(ks_out, axis=1), jnp.stack(vs_out, axis=1), jnp.stack(ss_out, axis=1))
```
